```python
import math
import jax, jax.numpy as jnp
from jax import lax
import numpy as np

D_MODEL = 4096
BATCH = 2
SEQ = 8192
DEPTH = 2

N_MIXERS = 2
N_META = 16
BLOCK_Q = 128
DA_HEADS = 16
DA_HEAD_DIM = D_MODEL // (2 * DA_HEADS)
ROPE_THETA = 10000.0
SSM_GROUP = 16
SSM_GROUPS = D_MODEL // SSM_GROUP
SSM_STATE = 64
DT_MIN = 0.001
DT_MAX = 0.1
D_FF = 4 * D_MODEL
EPS = 1e-6
N_ATTN_LAYERS = (DEPTH + N_MIXERS - 1) // N_MIXERS
N_SSM_LAYERS = DEPTH // N_MIXERS

kernel_name = "hybrid_diffattn_s5_sqrelu"


def lambda_init(layer_idx):
    return 0.8 - 0.6 * math.exp(-0.3 * layer_idx)


def rmsnorm(x, g):
    xf = x.astype(jnp.float32)
    y = xf * lax.rsqrt(jnp.mean(xf * xf, axis=-1, keepdims=True) + EPS)
    return (y * g.astype(jnp.float32)).astype(x.dtype)


def rope_tables(length):
    inv = ROPE_THETA ** (-jnp.arange(0, DA_HEAD_DIM, 2, dtype=jnp.float32) / DA_HEAD_DIM)
    ang = jnp.arange(length, dtype=jnp.float32)[:, None] * inv[None, :]
    return jnp.cos(ang), jnp.sin(ang)


def apply_rope(x, cos, sin):
    xf = x.astype(jnp.float32)
    x1, x2 = jnp.split(xf, 2, axis=-1)
    c = cos[None, :, None, :]
    s = sin[None, :, None, :]
    return jnp.concatenate([x1 * c - x2 * s, x2 * c + x1 * s], axis=-1).astype(x.dtype)


def diff_attention(u, w_qkv, q_g, k_g, lam_vecs, subln_g, w_o, cos, sin, lam_init):
    bsz, length, _ = u.shape
    H, dh = DA_HEADS, DA_HEAD_DIM
    nb = length // BLOCK_Q
    qkv = u @ w_qkv
    q, k, v = jnp.split(qkv, 3, axis=-1)
    q = q.reshape(bsz, length, 2 * H, dh)
    k = k.reshape(bsz, length, 2 * H, dh)
    v = v.reshape(bsz, length, H, 2 * dh)
    q = apply_rope(rmsnorm(q, q_g), cos, sin) * (dh ** -0.5)
    k = apply_rope(rmsnorm(k, k_g), cos, sin)
    q = q.reshape(bsz, nb, BLOCK_Q, H, 2, dh).transpose(1, 0, 2, 3, 4, 5)
    k = k.reshape(bsz, length, H, 2, dh)
    lv = lam_vecs.astype(jnp.float32)
    lam = jnp.exp(jnp.sum(lv[0] * lv[1])) - jnp.exp(jnp.sum(lv[2] * lv[3])) + lam_init
    kpos = jnp.arange(length)

    def block(args):
        qb, bi = args
        s = jnp.einsum('bqhcd,bkhcd->bhcqk', qb, k, preferred_element_type=jnp.float32)
        qpos = bi * BLOCK_Q + jnp.arange(BLOCK_Q)
        mask = kpos[None, :] <= qpos[:, None]
        p = jax.nn.softmax(jnp.where(mask, s, -jnp.inf), axis=-1)
        a = p[:, :, 0] - lam * p[:, :, 1]
        return jnp.einsum('bhqk,bkhe->bqhe', a.astype(v.dtype), v)

    o = lax.map(block, (q, jnp.arange(nb)))
    o = o.transpose(1, 0, 2, 3, 4).reshape(bsz, length, H, 2 * dh)
    o = rmsnorm(o, subln_g) * (1.0 - lam_init)
    return o.reshape(bsz, length, D_MODEL) @ w_o


def s5_mixer(u, a_re, a_im, log_dt, b_re, b_im, c_re, c_im, d_skip, w_glu, b_glu):
    bsz, length, _ = u.shape
    G, P = SSM_GROUPS, SSM_STATE
    uf = u.astype(jnp.float32).reshape(bsz, length, G, SSM_GROUP)
    dt = jnp.exp(log_dt.astype(jnp.float32))[:, None]
    ar = a_re.astype(jnp.float32)
    ai = a_im.astype(jnp.float32)
    mag = jnp.exp(dt * ar)
    ang = dt * ai
    abar_re = mag * jnp.cos(ang)
    abar_im = mag * jnp.sin(ang)
    nr = abar_re - 1.0
    ni = abar_im
    den = ar * ar + ai * ai
    f_re = (nr * ar + ni * ai) / den
    f_im = (ni * ar - nr * ai) / den
    br = b_re.astype(jnp.float32)
    bi = b_im.astype(jnp.float32)
    bb_re = f_re[..., None] * br - f_im[..., None] * bi
    bb_im = f_re[..., None] * bi + f_im[..., None] * br
    bu_re = jnp.einsum('blgh,gph->blgp', uf, bb_re)
    bu_im = jnp.einsum('blgh,gph->blgp', uf, bb_im)
    a_r = jnp.broadcast_to(abar_re, (1, length, G, P))
    a_i = jnp.broadcast_to(abar_im, (1, length, G, P))

    def combine(e1, e2):
        a1r, a1i, b1r, b1i = e1
        a2r, a2i, b2r, b2i = e2
        return (a1r * a2r - a1i * a2i,
                a1r * a2i + a1i * a2r,
                a2r * b1r - a2i * b1i + b2r,
                a2r * b1i + a2i * b1r + b2i)

    _, _, x_re, x_im = lax.associative_scan(combine, (a_r, a_i, bu_re, bu_im), axis=1)
    y = (jnp.einsum('blgp,ghp->blgh', x_re, c_re.astype(jnp.float32))
         - jnp.einsum('blgp,ghp->blgh', x_im, c_im.astype(jnp.float32)))
    y = y.reshape(bsz, length, D_MODEL) + d_skip.astype(jnp.float32) * uf.reshape(bsz, length, D_MODEL)
    z = jax.nn.gelu(y).astype(u.dtype)
    gl = z @ w_glu + b_glu
    val, gate = jnp.split(gl, 2, axis=-1)
    return val * jax.nn.sigmoid(gate)


def sqrelu_mlp(u, w_up, w_down):
    return jnp.square(jax.nn.relu(u @ w_up)) @ w_down


def setup_inputs(seed: int = 0) -> dict:
    key = jax.random.key(seed)
    ks = jax.random.split(key, 24)
    D, NA, NS = D_MODEL, N_ATTN_LAYERS, N_SSM_LAYERS
    G, P, GS, dh = SSM_GROUPS, SSM_STATE, SSM_GROUP, DA_HEAD_DIM
    f32 = jnp.float32
    nrm = lambda k, s: jax.random.normal(k, s, f32)
    return {
        "x": nrm(ks[0], (BATCH, SEQ, D)),
        "meta_tokens": nrm(ks[1], (N_META, D)),
        "norm_mix_g": 1.0 + 0.02 * nrm(ks[2], (DEPTH, D)),
        "norm_mlp_g": 1.0 + 0.02 * nrm(ks[3], (DEPTH, D)),
        "da_w_qkv": nrm(ks[4], (NA, D, 3 * D)) * D ** -0.5,
        "da_q_norm_g": 1.0 + 0.02 * nrm(ks[5], (NA, dh)),
        "da_k_norm_g": 1.0 + 0.02 * nrm(ks[6], (NA, dh)),
        "da_lambda": 0.1 * nrm(ks[7], (NA, 4, dh)),
        "da_subln_g": 1.0 + 0.02 * nrm(ks[8], (NA, 2 * dh)),
        "da_w_o": nrm(ks[9], (NA, D, D)) * D ** -0.5,
        "ssm_a_re": -0.5 + 0.01 * nrm(ks[10], (NS, G, P)),
        "ssm_a_im": math.pi * jnp.arange(P, dtype=f32)[None, None, :] + 0.01 * nrm(ks[11], (NS, G, P)),
        "ssm_log_dt": jax.random.uniform(ks[12], (NS, G), f32, math.log(DT_MIN), math.log(DT_MAX)),
        "ssm_b_re": nrm(ks[13], (NS, G, P, GS)) * (2 * GS) ** -0.5,
        "ssm_b_im": nrm(ks[14], (NS, G, P, GS)) * (2 * GS) ** -0.5,
        "ssm_c_re": nrm(ks[15], (NS, G, GS, P)) * P ** -0.5,
        "ssm_c_im": nrm(ks[16], (NS, G, GS, P)) * P ** -0.5,
        "ssm_d": nrm(ks[17], (NS, D)),
        "ssm_w_glu": nrm(ks[18], (NS, D, 2 * D)) * D ** -0.5,
        "ssm_b_glu": 0.01 * nrm(ks[19], (NS, 2 * D)),
        "mlp_w_up": nrm(ks[20], (DEPTH, D, D_FF)) * D ** -0.5,
        "mlp_w_down": nrm(ks[21], (DEPTH, D_FF, D)) * D_FF ** -0.5,
    }


def reference(x, meta_tokens, norm_mix_g, norm_mlp_g, da_w_qkv, da_q_norm_g, da_k_norm_g,
              da_lambda, da_subln_g, da_w_o, ssm_a_re, ssm_a_im, ssm_log_dt, ssm_b_re,
              ssm_b_im, ssm_c_re, ssm_c_im, ssm_d, ssm_w_glu, ssm_b_glu, mlp_w_up, mlp_w_down):
    bsz, seq, _ = x.shape
    length = N_META + seq
    padded = ((length + BLOCK_Q - 1) // BLOCK_Q) * BLOCK_Q
    meta = jnp.broadcast_to(meta_tokens.astype(x.dtype)[None], (bsz, N_META, D_MODEL))
    pad = jnp.zeros((bsz, padded - length, D_MODEL), x.dtype)
    h = jnp.concatenate([meta, x, pad], axis=1)
    cos, sin = rope_tables(padded)
    for i in range(DEPTH):
        j = i // N_MIXERS
        hn = rmsnorm(h, norm_mix_g[i])
        if i % N_MIXERS == 0:
            h = h + diff_attention(hn, da_w_qkv[j], da_q_norm_g[j], da_k_norm_g[j], da_lambda[j],
                                   da_subln_g[j], da_w_o[j], cos, sin, lambda_init(i))
        else:
            h = h + s5_mixer(hn, ssm_a_re[j], ssm_a_im[j], ssm_log_dt[j], ssm_b_re[j], ssm_b_im[j],
                             ssm_c_re[j], ssm_c_im[j], ssm_d[j], ssm_w_glu[j], ssm_b_glu[j])
        h = h + sqrelu_mlp(rmsnorm(h, norm_mlp_g[i]), mlp_w_up[i], mlp_w_down[i])
    return h[:, N_META:length]
```

```python
import functools
import math

import jax
import jax.numpy as jnp
from jax import lax
from jax.experimental import pallas as pl
from jax.experimental.pallas import tpu as pltpu

N_META = 16
BLOCK_Q = 128
HEAD_DIM = 128
ROPE_THETA = 10000.0
SSM_GROUP = 16
SSM_CHUNK = 16
SSM_SEGMENTS = 8
LANES = 128
SUBLANES = 8
EPS = 1e-6
VMEM_LIMIT_BYTES = 56 * 1024 * 1024

F32 = jnp.float32
BF16 = jnp.bfloat16


def _pick_tile(n, target, mult):
    best = None
    for t in range(mult, min(n, target) + 1, mult):
        if n % t == 0:
            best = t
    assert best is not None, (n, target, mult)
    return best


def _params(*sem):
    return pltpu.CompilerParams(dimension_semantics=sem, vmem_limit_bytes=VMEM_LIMIT_BYTES)


def _rmsnorm_kernel(x_ref, g_ref, o_ref):
    x = x_ref[...]
    ms = jnp.mean(x * x, axis=-1, keepdims=True)
    o_ref[...] = (x * lax.rsqrt(ms + EPS) * g_ref[...]).astype(o_ref.dtype)


def _rmsnorm(x, g):
    t, d = x.shape
    tr = _pick_tile(t, 256, 16)
    return pl.pallas_call(
        _rmsnorm_kernel,
        grid=(t // tr,),
        in_specs=[pl.BlockSpec((tr, d), lambda i: (i, 0)),
                  pl.BlockSpec((1, d), lambda i: (0, 0))],
        out_specs=pl.BlockSpec((tr, d), lambda i: (i, 0)),
        out_shape=jax.ShapeDtypeStruct((t, d), BF16),
        compiler_params=_params("parallel"),
        name="rmsnorm",
    )(x, g.reshape(1, d).astype(F32))


def _qkv_kernel(a_ref, w_ref, cos_ref, sin_ref, qg_ref, kg_ref, o_ref, *, n_q_blocks, q_scale):
    acc = jnp.dot(a_ref[...], w_ref[...], preferred_element_type=F32)
    j = pl.program_id(1)
    tn = acc.shape[1]

    def norm_rope(g, scale):
        cos = cos_ref[...]
        sin = sin_ref[...]
        for u in range(tn // HEAD_DIM):
            x = acc[:, u * HEAD_DIM:(u + 1) * HEAD_DIM]
            ms = jnp.mean(x * x, axis=-1, keepdims=True)
            y = x * lax.rsqrt(ms + EPS) * g
            y = y * cos + pltpu.roll(y, HEAD_DIM // 2, 1) * sin
            if scale != 1.0:
                y = y * scale
            o_ref[:, u * HEAD_DIM:(u + 1) * HEAD_DIM] = y.astype(o_ref.dtype)

    @pl.when(j < n_q_blocks)
    def _():
        norm_rope(qg_ref[...], q_scale)

    @pl.when(jnp.logical_and(j >= n_q_blocks, j < 2 * n_q_blocks))
    def _():
        norm_rope(kg_ref[...], 1.0)

    @pl.when(j >= 2 * n_q_blocks)
    def _():
        o_ref[...] = acc.astype(o_ref.dtype)


def _qkv_proj(a, w, cos, sin, q_g, k_g, lp):
    t, d = a.shape
    n = w.shape[1]
    tm = _pick_tile(lp, 1280, LANES)
    tn = _pick_tile(d, 512, HEAD_DIM)
    n_pos_blocks = lp // tm
    kern = functools.partial(_qkv_kernel, n_q_blocks=d // tn, q_scale=HEAD_DIM ** -0.5)
    return pl.pallas_call(
        kern,
        grid=(t // tm, n // tn),
        in_specs=[pl.BlockSpec((tm, d), lambda i, j: (i, 0)),
                  pl.BlockSpec((d, tn), lambda i, j: (0, j)),
                  pl.BlockSpec((tm, HEAD_DIM), lambda i, j: (i % n_pos_blocks, 0)),
                  pl.BlockSpec((tm, HEAD_DIM), lambda i, j: (i % n_pos_blocks, 0)),
                  pl.BlockSpec((1, HEAD_DIM), lambda i, j: (0, 0)),
                  pl.BlockSpec((1, HEAD_DIM), lambda i, j: (0, 0))],
        out_specs=pl.BlockSpec((tm, tn), lambda i, j: (i, j)),
        out_shape=jax.ShapeDtypeStruct((t, n), BF16),
        compiler_params=_params("parallel", "arbitrary"),
        name="qkv_proj",
    )(a, w, cos, sin, q_g.reshape(1, HEAD_DIM).astype(F32), k_g.reshape(1, HEAD_DIM).astype(F32))


def _attn_kernel(q_ref, k_ref, v_ref, lam_ref, g_ref, o_ref,
                 m1_ref, l1_ref, a1_ref, m2_ref, l2_ref, a2_ref, *, blk, lam_init):
    qi = pl.program_id(2)
    q = q_ref[...]
    qs = (q[:, :HEAD_DIM], q[:, HEAD_DIM:])
    stats = ((m1_ref, l1_ref, a1_ref), (m2_ref, l2_ref, a2_ref))
    for m_ref, l_ref, a_ref in stats:
        m_ref[...] = jnp.full(m_ref.shape, -jnp.inf, F32)
        l_ref[...] = jnp.zeros(l_ref.shape, F32)
        a_ref[...] = jnp.zeros(a_ref.shape, F32)

    def chunk(j, masked):
        start = pl.multiple_of(j * blk, blk)
        kk = k_ref[pl.ds(start, blk), :]
        vv = v_ref[pl.ds(start, blk), :]
        for c, (m_ref, l_ref, a_ref) in enumerate(stats):
            kc = kk[:, c * HEAD_DIM:(c + 1) * HEAD_DIM]
            s = lax.dot_general(qs[c], kc, (((1,), (1,)), ((), ())), preferred_element_type=F32)
            if masked:
                row = lax.broadcasted_iota(jnp.int32, s.shape, 0)
                col = lax.broadcasted_iota(jnp.int32, s.shape, 1)
                s = jnp.where(col <= row, s, -jnp.inf)
            m_old = m_ref[...]
            m_new = jnp.maximum(m_old, jnp.max(s, axis=-1, keepdims=True))
            alpha = jnp.exp(m_old - m_new)
            p = jnp.exp(s - m_new)
            l_ref[...] = alpha * l_ref[...] + jnp.sum(p, axis=-1, keepdims=True)
            a_ref[...] = alpha * a_ref[...] + jnp.dot(p.astype(BF16), vv, preferred_element_type=F32)
            m_ref[...] = m_new

    def body(j, carry):
        chunk(j, False)
        return carry

    lax.fori_loop(0, qi, body, 0)
    chunk(qi, True)

    lv = lam_ref[...]
    lam = (jnp.exp(jnp.sum(lv[0:1] * lv[1:2], axis=-1, keepdims=True))
           - jnp.exp(jnp.sum(lv[2:3] * lv[3:4], axis=-1, keepdims=True)) + lam_init)
    o = a1_ref[...] / l1_ref[...] - lam * (a2_ref[...] / l2_ref[...])
    ms = jnp.mean(o * o, axis=-1, keepdims=True)
    y = o * lax.rsqrt(ms + EPS) * g_ref[...] * (1.0 - lam_init)
    o_ref[...] = y.astype(o_ref.dtype)


def _diff_attention(qkv, lam_vecs, subln_g, bsz, lp, d, lam_init):
    t = qkv.shape[0]
    n_heads = d // (2 * HEAD_DIM)
    hw = 2 * HEAD_DIM
    blk = _pick_tile(lp, 640, LANES)
    nq = lp // blk
    kern = functools.partial(_attn_kernel, blk=blk, lam_init=lam_init)
    return pl.pallas_call(
        kern,
        grid=(bsz, n_heads, nq),
        in_specs=[pl.BlockSpec((blk, hw), lambda b, h, i: (b * nq + i, h)),
                  pl.BlockSpec((lp, hw), lambda b, h, i: (b, n_heads + h)),
                  pl.BlockSpec((lp, hw), lambda b, h, i: (b, 2 * n_heads + h)),
                  pl.BlockSpec((4, HEAD_DIM), lambda b, h, i: (0, 0)),
                  pl.BlockSpec((1, hw), lambda b, h, i: (0, 0))],
        out_specs=pl.BlockSpec((blk, hw), lambda b, h, i: (b * nq + i, h)),
        out_shape=jax.ShapeDtypeStruct((t, d), BF16),
        scratch_shapes=[pltpu.VMEM((blk, 1), F32), pltpu.VMEM((blk, 1), F32), pltpu.VMEM((blk, hw), F32),
                        pltpu.VMEM((blk, 1), F32), pltpu.VMEM((blk, 1), F32), pltpu.VMEM((blk, hw), F32)],
        compiler_params=_params("parallel", "parallel", "arbitrary"),
        name="diff_attention",
    )(qkv, qkv, qkv, lam_vecs.astype(F32), subln_g.reshape(1, hw).astype(F32))


def _proj_residual_kernel(a_ref, w_ref, r_ref, o_ref):
    o_ref[...] = r_ref[...] + jnp.dot(a_ref[...], w_ref[...], preferred_element_type=F32)


def _proj_residual(a, w, res):
    t, k = a.shape
    n = w.shape[1]
    tm = _pick_tile(t, 1280, LANES)
    tn = _pick_tile(n, 512, LANES)
    return pl.pallas_call(
        _proj_residual_kernel,
        grid=(t // tm, n // tn),
        in_specs=[pl.BlockSpec((tm, k), lambda i, j: (i, 0)),
                  pl.BlockSpec((k, tn), lambda i, j: (0, j)),
                  pl.BlockSpec((tm, tn), lambda i, j: (i, j))],
        out_specs=pl.BlockSpec((tm, tn), lambda i, j: (i, j)),
        out_shape=jax.ShapeDtypeStruct((t, n), F32),
        compiler_params=_params("parallel", "arbitrary"),
        name="proj_residual",
    )(a, w, res)


def _mlp_up_kernel(a_ref, w_ref, o_ref):
    u = jnp.dot(a_ref[...], w_ref[...], preferred_element_type=F32)
    o_ref[...] = jnp.square(jnp.maximum(u, 0.0)).astype(o_ref.dtype)


def _mlp_up(a, w):
    t, k = a.shape
    n = w.shape[1]
    tm = _pick_tile(t, 1280, LANES)
    tn = _pick_tile(n, 512, LANES)
    return pl.pallas_call(
        _mlp_up_kernel,
        grid=(t // tm, n // tn),
        in_specs=[pl.BlockSpec((tm, k), lambda i, j: (i, 0)),
                  pl.BlockSpec((k, tn), lambda i, j: (0, j))],
        out_specs=pl.BlockSpec((tm, tn), lambda i, j: (i, j)),
        out_shape=jax.ShapeDtypeStruct((t, n), BF16),
        compiler_params=_params("parallel", "arbitrary"),
        name="mlp_up",
    )(a, w)


def _mlp_down_kernel(a_ref, w_ref, r_ref, o_ref):
    k = pl.program_id(2)
    part = jnp.dot(a_ref[...], w_ref[...], preferred_element_type=F32)

    @pl.when(k == 0)
    def _():
        o_ref[...] = r_ref[...] + part

    @pl.when(k > 0)
    def _():
        o_ref[...] += part


def _mlp_down(a, w, res):
    t, kdim = a.shape
    n = w.shape[1]
    tm = _pick_tile(t, 1280, LANES)
    tn = _pick_tile(n, 1024, LANES)
    tk = _pick_tile(kdim, 1024, LANES)
    return pl.pallas_call(
        _mlp_down_kernel,
        grid=(t // tm, n // tn, kdim // tk),
        in_specs=[pl.BlockSpec((tm, tk), lambda i, j, k: (i, k)),
                  pl.BlockSpec((tk, tn), lambda i, j, k: (k, j)),
                  pl.BlockSpec((tm, tn), lambda i, j, k: (i, j))],
        out_specs=pl.BlockSpec((tm, tn), lambda i, j, k: (i, j)),
        out_shape=jax.ShapeDtypeStruct((t, n), F32),
        compiler_params=_params("parallel", "parallel", "arbitrary"),
        name="mlp_down",
    )(a, w, res)


def _glu_kernel(a_ref, wv_ref, wg_ref, bv_ref, bg_ref, r_ref, o_ref):
    a = a_ref[...]
    val = jnp.dot(a, wv_ref[...], preferred_element_type=F32) + bv_ref[...]
    gate = jnp.dot(a, wg_ref[...], preferred_element_type=F32) + bg_ref[...]
    o_ref[...] = r_ref[...] + val * jax.nn.sigmoid(gate)


def _glu_residual(a, w, bias, res):
    t, k = a.shape
    n = w.shape[1] // 2
    tm = _pick_tile(t, 1280, LANES)
    tn = _pick_tile(n, 256, LANES)
    nb = n // tn
    b2 = bias.reshape(1, 2 * n).astype(F32)
    return pl.pallas_call(
        _glu_kernel,
        grid=(t // tm, nb),
        in_specs=[pl.BlockSpec((tm, k), lambda i, j: (i, 0)),
                  pl.BlockSpec((k, tn), lambda i, j: (0, j)),
                  pl.BlockSpec((k, tn), lambda i, j: (0, nb + j)),
                  pl.BlockSpec((1, tn), lambda i, j: (0, j)),
                  pl.BlockSpec((1, tn), lambda i, j: (0, nb + j)),
                  pl.BlockSpec((tm, tn), lambda i, j: (i, j))],
        out_specs=pl.BlockSpec((tm, tn), lambda i, j: (i, j)),
        out_shape=jax.ShapeDtypeStruct((t, n), F32),
        compiler_params=_params("parallel", "arbitrary"),
        name="glu_residual",
    )(a, w, w, b2, b2, res)


def _s5_kernel(x_ref, w0_ref, cm_ref, pr_ref, pi_ref, prc_ref, pic_ref, d_ref, o_ref,
               m_scr, wst_scr, wout_scr, s_scr, xin_scr, *, n_steps):
    q = SSM_CHUNK
    half = w0_ref.shape[2] // 2

    @pl.when(pl.program_id(1) == 0)
    def _build_weights():
        w0 = w0_ref[0]
        w0re, w0im = w0[:, :half], w0[:, half:]
        cm = cm_ref[0]
        cm_bf = cm.astype(BF16)
        row = lax.broadcasted_iota(jnp.int32, (LANES, LANES), 0)
        col = lax.broadcasted_iota(jnp.int32, (LANES, LANES), 1)
        skip = jnp.where(row == col, jnp.broadcast_to(d_ref[...], (LANES, LANES)), 0.0)
        zero_blk = jnp.zeros((LANES, LANES), BF16)
        for tau in range(q):
            ar = pr_ref[0, tau:tau + 1, :]
            ai = pi_ref[0, tau:tau + 1, :]
            w_tau = jnp.concatenate([w0re * ar - w0im * ai, w0im * ar + w0re * ai], axis=1).astype(BF16)
            t_st = q - 1 - tau
            wst_scr[t_st * LANES:(t_st + 1) * LANES, :] = w_tau
            k_tau = jnp.dot(w_tau, cm_bf, preferred_element_type=F32)
            if tau == 0:
                k_tau = k_tau + skip
            k_bf = k_tau.astype(BF16)
            for t in range(q - tau):
                m_scr[t * LANES:(t + 1) * LANES, (t + tau) * LANES:(t + tau + 1) * LANES] = k_bf
        for t in range(q):
            for t2 in range(t):
                m_scr[t * LANES:(t + 1) * LANES, t2 * LANES:(t2 + 1) * LANES] = zero_blk
        cre, cimn = cm[:half], cm[half:]
        for t in range(q):
            arc = prc_ref[0, :, t + 1:t + 2]
            aic = pic_ref[0, :, t + 1:t + 2]
            wout_scr[:half, t * LANES:(t + 1) * LANES] = (arc * cre + aic * cimn).astype(BF16)
            wout_scr[half:, t * LANES:(t + 1) * LANES] = (arc * cimn - aic * cre).astype(BF16)

    u = jnp.concatenate([x_ref[t] for t in range(q)], axis=1)
    s_scr[...] = jnp.dot(u, wst_scr[...], preferred_element_type=F32)

    shp = (SUBLANES, half)
    ar = jnp.broadcast_to(pr_ref[0, q:q + 1, :], shp)
    ai = jnp.broadcast_to(pi_ref[0, q:q + 1, :], shp)
    anr = jnp.broadcast_to(pr_ref[0, q + 1:q + 2, :], shp)
    ani = jnp.broadcast_to(pi_ref[0, q + 1:q + 2, :], shp)
    seg = lax.broadcasted_iota(jnp.int32, shp, 0)
    zeros = jnp.zeros(shp, F32)

    def shift_down(x):
        return jnp.where(seg == 0, 0.0, pltpu.roll(x, 1, 0))

    def advance(j, cr, ci):
        sj = s_scr[pl.ds(pl.multiple_of(j * SUBLANES, SUBLANES), SUBLANES), :]
        return ar * cr - ai * ci + sj[:, :half], ar * ci + ai * cr + sj[:, half:]

    er, ei = lax.fori_loop(0, n_steps, lambda j, c: advance(j, *c), (zeros, zeros))
    tr, ti = er, ei
    for _ in range(SSM_SEGMENTS - 1):
        sr, si = shift_down(tr), shift_down(ti)
        tr, ti = er + anr * sr - ani * si, ei + anr * si + ani * sr
    cr0, ci0 = shift_down(tr), shift_down(ti)

    def scan_store(j, c):
        cr, ci = c
        xin_scr[pl.ds(pl.multiple_of(j * SUBLANES, SUBLANES), SUBLANES), :] = jnp.concatenate([cr, ci], axis=1)
        return advance(j, cr, ci)

    lax.fori_loop(0, n_steps, scan_store, (cr0, ci0))

    y = (jnp.dot(u, m_scr[...], preferred_element_type=F32)
         + jnp.dot(xin_scr[...].astype(BF16), wout_scr[...], preferred_element_type=F32))
    z = jax.nn.gelu(y)
    for t in range(q):
        o_ref[t] = z[:, t * LANES:(t + 1) * LANES].astype(o_ref.dtype)


def _complex_pow(zr, zi, n):
    rr, ri = None, None
    br, bi = zr, zi
    while n:
        if n & 1:
            rr, ri = (br, bi) if rr is None else (rr * br - ri * bi, rr * bi + ri * br)
        n >>= 1
        if n:
            br, bi = br * br - bi * bi, 2.0 * br * bi
    return rr, ri


def _s5_tables(a_re, a_im, log_dt, b_re, b_im, c_re, c_im, n_steps):
    g, p = a_re.shape
    gpb = LANES // SSM_GROUP
    nblk = g // gpb
    dt = jnp.exp(log_dt.astype(F32))[:, None]
    ar = a_re.astype(F32)
    ai = a_im.astype(F32)
    mag = jnp.exp(dt * ar)
    ang = dt * ai
    abar_re = mag * jnp.cos(ang)
    abar_im = mag * jnp.sin(ang)
    nr = abar_re - 1.0
    ni = abar_im
    den = ar * ar + ai * ai
    f_re = (nr * ar + ni * ai) / den
    f_im = (ni * ar - nr * ai) / den
    br = b_re.astype(F32)
    bi = b_im.astype(F32)
    bb_re = f_re[..., None] * br - f_im[..., None] * bi
    bb_im = f_re[..., None] * bi + f_im[..., None] * br

    pw_r, pw_i = [jnp.ones_like(abar_re)], [jnp.zeros_like(abar_re)]
    for _ in range(SSM_CHUNK):
        pw_r.append(pw_r[-1] * abar_re - pw_i[-1] * abar_im)
        pw_i.append(pw_r[-2] * abar_im + pw_i[-1] * abar_re)
    seg_r, seg_i = _complex_pow(pw_r[-1], pw_i[-1], n_steps)
    pw_r.append(seg_r)
    pw_i.append(seg_i)
    n_rows = 24
    pr = jnp.stack(pw_r, 0).reshape(len(pw_r), nblk, gpb * p).transpose(1, 0, 2)
    pi = jnp.stack(pw_i, 0).reshape(len(pw_i), nblk, gpb * p).transpose(1, 0, 2)
    pad = ((0, 0), (0, n_rows - pr.shape[1]), (0, 0))
    pr = jnp.pad(pr, pad)
    pi = jnp.pad(pi, pad)
    prc = pr.transpose(0, 2, 1)
    pic = pi.transpose(0, 2, 1)

    eye = jnp.eye(gpb, dtype=F32)

    def in_to_state(bb):
        v = bb.reshape(nblk, gpb, p, SSM_GROUP).transpose(0, 1, 3, 2)
        e = v[:, :, :, None, :] * eye[None, :, None, :, None]
        return e.reshape(nblk, LANES, gpb * p)

    def state_to_out(c):
        v = c.astype(F32).reshape(nblk, gpb, SSM_GROUP, p).transpose(0, 1, 3, 2)
        e = v[:, :, :, None, :] * eye[None, :, None, :, None]
        return e.reshape(nblk, gpb * p, LANES)

    w0 = jnp.concatenate([in_to_state(bb_re), in_to_state(bb_im)], axis=2)
    cm = jnp.concatenate([state_to_out(c_re), -state_to_out(c_im)], axis=1)
    return w0, cm, pr, pi, prc, pic


def _s5_mixer(hn, bsz, lp, a_re, a_im, log_dt, b_re, b_im, c_re, c_im, d_skip):
    t, d = hn.shape
    q, nseg = SSM_CHUNK, SSM_SEGMENTS
    span = q * nseg * 2
    lp5 = ((lp + span - 1) // span) * span
    n_steps = lp5 // (q * nseg)
    rows = n_steps * nseg
    x = hn.reshape(bsz, lp, d)
    x = jnp.pad(x, ((0, 0), (0, lp5 - lp), (0, 0)))
    x = x.reshape(bsz, nseg, n_steps, q, d).transpose(3, 0, 2, 1, 4).reshape(q, bsz * rows, d)

    w0, cm, pr, pi, prc, pic = _s5_tables(a_re, a_im, log_dt, b_re, b_im, c_re, c_im, n_steps)
    nblk = d // LANES
    n_state = w0.shape[2]
    kern = functools.partial(_s5_kernel, n_steps=n_steps)
    z = pl.pallas_call(
        kern,
        grid=(nblk, bsz),
        in_specs=[pl.BlockSpec((q, rows, LANES), lambda k, b: (0, b, k)),
                  pl.BlockSpec((1, LANES, n_state), lambda k, b: (k, 0, 0)),
                  pl.BlockSpec((1, n_state, LANES), lambda k, b: (k, 0, 0)),
                  pl.BlockSpec((1,) + pr.shape[1:], lambda k, b: (k, 0, 0)),
                  pl.BlockSpec((1,) + pi.shape[1:], lambda k, b: (k, 0, 0)),
                  pl.BlockSpec((1,) + prc.shape[1:], lambda k, b: (k, 0, 0)),
                  pl.BlockSpec((1,) + pic.shape[1:], lambda k, b: (k, 0, 0)),
                  pl.BlockSpec((1, LANES), lambda k, b: (0, k))],
        out_specs=pl.BlockSpec((q, rows, LANES), lambda k, b: (0, b, k)),
        out_shape=jax.ShapeDtypeStruct((q, bsz * rows, d), BF16),
        scratch_shapes=[pltpu.VMEM((q * LANES, q * LANES), BF16),
                        pltpu.VMEM((q * LANES, n_state), BF16),
                        pltpu.VMEM((n_state, q * LANES), BF16),
                        pltpu.VMEM((rows, n_state), F32),
                        pltpu.VMEM((rows, n_state), F32)],
        compiler_params=_params("arbitrary", "arbitrary"),
        name="s5_mixer",
    )(x, w0, cm, pr, pi, prc, pic, d_skip.reshape(1, d).astype(F32))
    z = z.reshape(q, bsz, n_steps, nseg, d).transpose(1, 3, 2, 0, 4).reshape(bsz, lp5, d)
    return z[:, :lp].reshape(t, d)


def _rope_tables(length):
    inv = ROPE_THETA ** (-jnp.arange(0, HEAD_DIM, 2, dtype=F32) / HEAD_DIM)
    ang = jnp.arange(length, dtype=F32)[:, None] * inv[None, :]
    cos, sin = jnp.cos(ang), jnp.sin(ang)
    return jnp.concatenate([cos, cos], axis=1), jnp.concatenate([-sin, sin], axis=1)


def _lambda_init(layer_idx):
    return 0.8 - 0.6 * math.exp(-0.3 * layer_idx)


def kernel(x, meta_tokens, norm_mix_g, norm_mlp_g, da_w_qkv, da_q_norm_g, da_k_norm_g, da_lambda, da_subln_g, da_w_o, ssm_a_re, ssm_a_im, ssm_log_dt, ssm_b_re, ssm_b_im, ssm_c_re, ssm_c_im, ssm_d, ssm_w_glu, ssm_b_glu, mlp_w_up, mlp_w_down):
    bsz, seq, d = x.shape
    depth = norm_mix_g.shape[0]
    n_mixers = 2
    length = N_META + seq
    lp = ((length + BLOCK_Q - 1) // BLOCK_Q) * BLOCK_Q
    meta = jnp.broadcast_to(meta_tokens.astype(x.dtype)[None], (bsz, N_META, d))
    pad = jnp.zeros((bsz, lp - length, d), x.dtype)
    h = jnp.concatenate([meta, x, pad], axis=1).reshape(bsz * lp, d)
    cos, sin = _rope_tables(lp)
    for i in range(depth):
        j = i // n_mixers
        hn = _rmsnorm(h, norm_mix_g[i])
        if i % n_mixers == 0:
            qkv = _qkv_proj(hn, da_w_qkv[j].astype(BF16), cos, sin, da_q_norm_g[j], da_k_norm_g[j], lp)
            att = _diff_attention(qkv, da_lambda[j], da_subln_g[j], bsz, lp, d, _lambda_init(i))
            h = _proj_residual(att, da_w_o[j].astype(BF16), h)
        else:
            z = _s5_mixer(hn, bsz, lp, ssm_a_re[j], ssm_a_im[j], ssm_log_dt[j], ssm_b_re[j], ssm_b_im[j],
                          ssm_c_re[j], ssm_c_im[j], ssm_d[j])
            h = _glu_residual(z, ssm_w_glu[j].astype(BF16), ssm_b_glu[j], h)
        hm = _rmsnorm(h, norm_mlp_g[i])
        f = _mlp_up(hm, mlp_w_up[i].astype(BF16))
        h = _mlp_down(f, mlp_w_down[i].astype(BF16), h)
    return h.reshape(bsz, lp, d)[:, N_META:length]
```

```python
import functools
import math

import jax
import jax.numpy as jnp
from jax import lax
from jax.experimental import pallas as pl
from jax.experimental.pallas import tpu as pltpu

N_META = 16
SEQ_ALIGN = 256
HEAD_DIM = 128
LOG2E = 1.4426950408889634
ROPE_THETA = 10000.0
SSM_GROUP = 16
SSM_CHUNK = 16
SSM_SEGMENTS = 8
LANES = 128
SUBLANES = 8
EPS = 1e-6
VMEM_LIMIT_BYTES = 56 * 1024 * 1024

F32 = jnp.float32
BF16 = jnp.bfloat16


def _pick_tile(n, target, mult):
    best = None
    for t in range(mult, min(n, target) + 1, mult):
        if n % t == 0:
            best = t
    assert best is not None, (n, target, mult)
    return best


def _row_tile(t):
    return _pick_tile(t, 1536, SEQ_ALIGN)


def _params(*sem):
    return pltpu.CompilerParams(dimension_semantics=sem, vmem_limit_bytes=VMEM_LIMIT_BYTES)


def _rmsnorm_kernel(x_ref, g_ref, o_ref):
    x = x_ref[...]
    ms = jnp.mean(x * x, axis=-1, keepdims=True)
    o_ref[...] = (x * lax.rsqrt(ms + EPS) * g_ref[...]).astype(o_ref.dtype)


def _rmsnorm(x, g):
    t, d = x.shape
    tr = _pick_tile(t, 256, 16)
    return pl.pallas_call(
        _rmsnorm_kernel,
        grid=(t // tr,),
        in_specs=[pl.BlockSpec((tr, d), lambda i: (i, 0)),
                  pl.BlockSpec((1, d), lambda i: (0, 0))],
        out_specs=pl.BlockSpec((tr, d), lambda i: (i, 0)),
        out_shape=jax.ShapeDtypeStruct((t, d), BF16),
        compiler_params=_params("parallel"),
        name="rmsnorm",
    )(x, g.reshape(1, d).astype(F32))


def _qkv_kernel(a_ref, w_ref, cos_ref, sin_ref, qg_ref, kg_ref, o_ref, *, n_q_blocks, q_scale):
    acc = jnp.dot(a_ref[...], w_ref[...], preferred_element_type=F32)
    j = pl.program_id(1)
    tn = acc.shape[1]

    def norm_rope(g, scale):
        cos = cos_ref[...]
        sin = sin_ref[...]
        for u in range(tn // HEAD_DIM):
            x = acc[:, u * HEAD_DIM:(u + 1) * HEAD_DIM]
            ms = jnp.mean(x * x, axis=-1, keepdims=True)
            y = x * lax.rsqrt(ms + EPS) * g
            y = y * cos + pltpu.roll(y, HEAD_DIM // 2, 1) * sin
            if scale != 1.0:
                y = y * scale
            o_ref[:, u * HEAD_DIM:(u + 1) * HEAD_DIM] = y.astype(o_ref.dtype)

    @pl.when(j < n_q_blocks)
    def _():
        norm_rope(qg_ref[...], q_scale)

    @pl.when(jnp.logical_and(j >= n_q_blocks, j < 2 * n_q_blocks))
    def _():
        norm_rope(kg_ref[...], 1.0)

    @pl.when(j >= 2 * n_q_blocks)
    def _():
        o_ref[...] = acc.astype(o_ref.dtype)


def _qkv_proj(a, w, cos, sin, q_g, k_g):
    t, d = a.shape
    n = w.shape[1]
    tm = _row_tile(t)
    tn = _pick_tile(d, 512, HEAD_DIM)
    kern = functools.partial(_qkv_kernel, n_q_blocks=d // tn, q_scale=HEAD_DIM ** -0.5 * LOG2E)
    return pl.pallas_call(
        kern,
        grid=(t // tm, n // tn),
        in_specs=[pl.BlockSpec((tm, d), lambda i, j: (i, 0)),
                  pl.BlockSpec((d, tn), lambda i, j: (0, j)),
                  pl.BlockSpec((tm, HEAD_DIM), lambda i, j: (i, 0)),
                  pl.BlockSpec((tm, HEAD_DIM), lambda i, j: (i, 0)),
                  pl.BlockSpec((1, HEAD_DIM), lambda i, j: (0, 0)),
                  pl.BlockSpec((1, HEAD_DIM), lambda i, j: (0, 0))],
        out_specs=pl.BlockSpec((tm, tn), lambda i, j: (i, j)),
        out_shape=jax.ShapeDtypeStruct((t, n), BF16),
        compiler_params=_params("parallel", "arbitrary"),
        name="qkv_proj",
    )(a, w, cos, sin, q_g.reshape(1, HEAD_DIM).astype(F32), k_g.reshape(1, HEAD_DIM).astype(F32))


def _attn_kernel(q_ref, k_ref, v_ref, lam_ref, g_ref, o_ref,
                 m1_ref, l1_ref, a1_ref, m2_ref, l2_ref, a2_ref, sa_ref, sb_ref, *, blk, lam_init):
    qi = pl.program_id(2)
    q = q_ref[...]
    qs = (q[:, :HEAD_DIM], q[:, HEAD_DIM:])
    stats = ((m1_ref, l1_ref, a1_ref), (m2_ref, l2_ref, a2_ref))
    for m_ref, l_ref, a_ref in stats:
        m_ref[...] = jnp.full(m_ref.shape, -jnp.inf, F32)
        l_ref[...] = jnp.zeros(l_ref.shape, F32)
        a_ref[...] = jnp.zeros(a_ref.shape, F32)

    def scores(j, s_ref):
        kk = k_ref[pl.ds(pl.multiple_of(j * blk, blk), blk), :]
        for c in range(2):
            kc = kk[:, c * HEAD_DIM:(c + 1) * HEAD_DIM]
            s_ref[c] = lax.dot_general(qs[c], kc, (((1,), (1,)), ((), ())), preferred_element_type=F32)

    def softmax_pv(j, s_ref, masked):
        vv = v_ref[pl.ds(pl.multiple_of(j * blk, blk), blk), :]
        for c, (m_ref, l_ref, a_ref) in enumerate(stats):
            s = s_ref[c]
            if masked:
                row = lax.broadcasted_iota(jnp.int32, s.shape, 0)
                col = lax.broadcasted_iota(jnp.int32, s.shape, 1)
                s = jnp.where(col <= row, s, -jnp.inf)
            m_old = m_ref[...]
            m_new = jnp.maximum(m_old, jnp.max(s, axis=-1, keepdims=True))
            alpha = jnp.exp2(m_old - m_new)
            p = jnp.exp2(s - m_new)
            l_ref[...] = alpha * l_ref[...] + jnp.sum(p, axis=-1, keepdims=True)
            a_ref[...] = alpha * a_ref[...] + jnp.dot(p.astype(BF16), vv, preferred_element_type=F32)
            m_ref[...] = m_new

    scores(0, sa_ref)

    def pair(i, carry):
        scores(2 * i + 1, sb_ref)
        softmax_pv(2 * i, sa_ref, False)
        scores(2 * i + 2, sa_ref)
        softmax_pv(2 * i + 1, sb_ref, False)
        return carry

    lax.fori_loop(0, qi // 2, pair, 0)

    @pl.when(qi % 2 == 0)
    def _():
        softmax_pv(qi, sa_ref, True)

    @pl.when(qi % 2 == 1)
    def _():
        scores(qi, sb_ref)
        softmax_pv(qi - 1, sa_ref, False)
        softmax_pv(qi, sb_ref, True)

    lv = lam_ref[...]
    lam = (jnp.exp(jnp.sum(lv[0:1] * lv[1:2], axis=-1, keepdims=True))
           - jnp.exp(jnp.sum(lv[2:3] * lv[3:4], axis=-1, keepdims=True)) + lam_init)
    o = a1_ref[...] / l1_ref[...] - lam * (a2_ref[...] / l2_ref[...])
    ms = jnp.mean(o * o, axis=-1, keepdims=True)
    y = o * lax.rsqrt(ms + EPS) * g_ref[...] * (1.0 - lam_init)
    o_ref[...] = y.astype(o_ref.dtype)


def _diff_attention(qkv, lam_vecs, subln_g, bsz, lp, d, lam_init):
    t = qkv.shape[0]
    n_heads = d // (2 * HEAD_DIM)
    hw = 2 * HEAD_DIM
    blk = _pick_tile(lp, 768, SEQ_ALIGN)
    nq = lp // blk
    kern = functools.partial(_attn_kernel, blk=blk, lam_init=lam_init)
    return pl.pallas_call(
        kern,
        grid=(bsz, n_heads, nq),
        in_specs=[pl.BlockSpec((blk, hw), lambda b, h, i: (b * nq + i, h)),
                  pl.BlockSpec((lp, hw), lambda b, h, i: (b, n_heads + h)),
                  pl.BlockSpec((lp, hw), lambda b, h, i: (b, 2 * n_heads + h)),
                  pl.BlockSpec((4, HEAD_DIM), lambda b, h, i: (0, 0)),
                  pl.BlockSpec((1, hw), lambda b, h, i: (0, 0))],
        out_specs=pl.BlockSpec((blk, hw), lambda b, h, i: (b * nq + i, h)),
        out_shape=jax.ShapeDtypeStruct((t, d), BF16),
        scratch_shapes=[pltpu.VMEM((blk, 1), F32), pltpu.VMEM((blk, 1), F32), pltpu.VMEM((blk, hw), F32),
                        pltpu.VMEM((blk, 1), F32), pltpu.VMEM((blk, 1), F32), pltpu.VMEM((blk, hw), F32),
                        pltpu.VMEM((2, blk, blk), F32), pltpu.VMEM((2, blk, blk), F32)],
        compiler_params=_params("parallel", "parallel", "arbitrary"),
        name="diff_attention",
    )(qkv, qkv, qkv, lam_vecs.astype(F32), subln_g.reshape(1, hw).astype(F32))


def _proj_residual_kernel(a_ref, w_ref, r_ref, o_ref):
    o_ref[...] = r_ref[...] + jnp.dot(a_ref[...], w_ref[...], preferred_element_type=F32)


def _proj_residual(a, w, res):
    t, k = a.shape
    n = w.shape[1]
    tm = _row_tile(t)
    tn = _pick_tile(n, 512, LANES)
    return pl.pallas_call(
        _proj_residual_kernel,
        grid=(t // tm, n // tn),
        in_specs=[pl.BlockSpec((tm, k), lambda i, j: (i, 0)),
                  pl.BlockSpec((k, tn), lambda i, j: (0, j)),
                  pl.BlockSpec((tm, tn), lambda i, j: (i, j))],
        out_specs=pl.BlockSpec((tm, tn), lambda i, j: (i, j)),
        out_shape=jax.ShapeDtypeStruct((t, n), F32),
        compiler_params=_params("parallel", "arbitrary"),
        name="proj_residual",
    )(a, w, res)


def _mlp_up_kernel(a_ref, w_ref, o_ref):
    u = jnp.dot(a_ref[...], w_ref[...], preferred_element_type=F32)
    o_ref[...] = jnp.square(jnp.maximum(u, 0.0)).astype(o_ref.dtype)


def _mlp_up(a, w):
    t, k = a.shape
    n = w.shape[1]
    tm = _row_tile(t)
    tn = _pick_tile(n, 512, LANES)
    return pl.pallas_call(
        _mlp_up_kernel,
        grid=(t // tm, n // tn),
        in_specs=[pl.BlockSpec((tm, k), lambda i, j: (i, 0)),
                  pl.BlockSpec((k, tn), lambda i, j: (0, j))],
        out_specs=pl.BlockSpec((tm, tn), lambda i, j: (i, j)),
        out_shape=jax.ShapeDtypeStruct((t, n), BF16),
        compiler_params=_params("parallel", "arbitrary"),
        name="mlp_up",
    )(a, w)


def _mlp_down_kernel(a_ref, w_ref, r_ref, o_ref):
    k = pl.program_id(2)
    part = jnp.dot(a_ref[...], w_ref[...], preferred_element_type=F32)

    @pl.when(k == 0)
    def _():
        o_ref[...] = r_ref[...] + part

    @pl.when(k > 0)
    def _():
        o_ref[...] += part


def _mlp_down(a, w, res):
    t, kdim = a.shape
    n = w.shape[1]
    tm = _row_tile(t)
    tn = _pick_tile(n, 1024, LANES)
    tk = _pick_tile(kdim, 2048, LANES)
    return pl.pallas_call(
        _mlp_down_kernel,
        grid=(t // tm, n // tn, kdim // tk),
        in_specs=[pl.BlockSpec((tm, tk), lambda i, j, k: (i, k)),
                  pl.BlockSpec((tk, tn), lambda i, j, k: (k, j)),
                  pl.BlockSpec((tm, tn), lambda i, j, k: (i, j))],
        out_specs=pl.BlockSpec((tm, tn), lambda i, j, k: (i, j)),
        out_shape=jax.ShapeDtypeStruct((t, n), F32),
        compiler_params=_params("parallel", "parallel", "arbitrary"),
        name="mlp_down",
    )(a, w, res)


def _glu_kernel(a_ref, wv_ref, wg_ref, bv_ref, bg_ref, r_ref, o_ref):
    a = a_ref[...]
    val = jnp.dot(a, wv_ref[...], preferred_element_type=F32) + bv_ref[...]
    gate = jnp.dot(a, wg_ref[...], preferred_element_type=F32) + bg_ref[...]
    o_ref[...] = r_ref[...] + val * jax.nn.sigmoid(gate)


def _glu_residual(a, w, bias, res):
    t, k = a.shape
    n = w.shape[1] // 2
    tm = _row_tile(t)
    tn = _pick_tile(n, 256, LANES)
    nb = n // tn
    b2 = bias.reshape(1, 2 * n).astype(F32)
    return pl.pallas_call(
        _glu_kernel,
        grid=(t // tm, nb),
        in_specs=[pl.BlockSpec((tm, k), lambda i, j: (i, 0)),
                  pl.BlockSpec((k, tn), lambda i, j: (0, j)),
                  pl.BlockSpec((k, tn), lambda i, j: (0, nb + j)),
                  pl.BlockSpec((1, tn), lambda i, j: (0, j)),
                  pl.BlockSpec((1, tn), lambda i, j: (0, nb + j)),
                  pl.BlockSpec((tm, tn), lambda i, j: (i, j))],
        out_specs=pl.BlockSpec((tm, tn), lambda i, j: (i, j)),
        out_shape=jax.ShapeDtypeStruct((t, n), F32),
        compiler_params=_params("parallel", "arbitrary"),
        name="glu_residual",
    )(a, w, w, b2, b2, res)


def _s5_kernel(x_ref, w0_ref, cm_ref, pr_ref, pi_ref, prc_ref, pic_ref, d_ref, o_ref,
               m_scr, wst_scr, wout_scr, s_scr, xin_scr, *, n_steps):
    q = SSM_CHUNK
    half = w0_ref.shape[2] // 2

    @pl.when(pl.program_id(1) == 0)
    def _build_weights():
        w0 = w0_ref[0]
        w0re, w0im = w0[:, :half], w0[:, half:]
        cm = cm_ref[0]
        cm_bf = cm.astype(BF16)
        row = lax.broadcasted_iota(jnp.int32, (LANES, LANES), 0)
        col = lax.broadcasted_iota(jnp.int32, (LANES, LANES), 1)
        skip = jnp.where(row == col, jnp.broadcast_to(d_ref[...], (LANES, LANES)), 0.0)
        zero_blk = jnp.zeros((LANES, LANES), BF16)
        for tau in range(q):
            ar = pr_ref[0, tau:tau + 1, :]
            ai = pi_ref[0, tau:tau + 1, :]
            w_tau = jnp.concatenate([w0re * ar - w0im * ai, w0im * ar + w0re * ai], axis=1).astype(BF16)
            t_st = q - 1 - tau
            wst_scr[t_st * LANES:(t_st + 1) * LANES, :] = w_tau
            k_tau = jnp.dot(w_tau, cm_bf, preferred_element_type=F32)
            if tau == 0:
                k_tau = k_tau + skip
            k_bf = k_tau.astype(BF16)
            for t in range(q - tau):
                m_scr[t * LANES:(t + 1) * LANES, (t + tau) * LANES:(t + tau + 1) * LANES] = k_bf
        for t in range(q):
            for t2 in range(t):
                m_scr[t * LANES:(t + 1) * LANES, t2 * LANES:(t2 + 1) * LANES] = zero_blk
        cre, cimn = cm[:half], cm[half:]
        for t in range(q):
            arc = prc_ref[0, :, t + 1:t + 2]
            aic = pic_ref[0, :, t + 1:t + 2]
            wout_scr[:half, t * LANES:(t + 1) * LANES] = (arc * cre + aic * cimn).astype(BF16)
            wout_scr[half:, t * LANES:(t + 1) * LANES] = (arc * cimn - aic * cre).astype(BF16)

    u = jnp.concatenate([x_ref[t] for t in range(q)], axis=1)
    s_scr[...] = jnp.dot(u, wst_scr[...], preferred_element_type=F32)

    shp = (SUBLANES, half)
    ar = jnp.broadcast_to(pr_ref[0, q:q + 1, :], shp)
    ai = jnp.broadcast_to(pi_ref[0, q:q + 1, :], shp)
    anr = jnp.broadcast_to(pr_ref[0, q + 1:q + 2, :], shp)
    ani = jnp.broadcast_to(pi_ref[0, q + 1:q + 2, :], shp)
    seg = lax.broadcasted_iota(jnp.int32, shp, 0)
    zeros = jnp.zeros(shp, F32)

    def shift_down(x):
        return jnp.where(seg == 0, 0.0, pltpu.roll(x, 1, 0))

    def advance(j, cr, ci):
        sj = s_scr[pl.ds(pl.multiple_of(j * SUBLANES, SUBLANES), SUBLANES), :]
        return ar * cr - ai * ci + sj[:, :half], ar * ci + ai * cr + sj[:, half:]

    er, ei = lax.fori_loop(0, n_steps, lambda j, c: advance(j, *c), (zeros, zeros))
    tr, ti = er, ei
    for _ in range(SSM_SEGMENTS - 1):
        sr, si = shift_down(tr), shift_down(ti)
        tr, ti = er + anr * sr - ani * si, ei + anr * si + ani * sr
    cr0, ci0 = shift_down(tr), shift_down(ti)

    def scan_store(j, c):
        cr, ci = c
        xin_scr[pl.ds(pl.multiple_of(j * SUBLANES, SUBLANES), SUBLANES), :] = jnp.concatenate([cr, ci], axis=1)
        return advance(j, cr, ci)

    lax.fori_loop(0, n_steps, scan_store, (cr0, ci0))

    y = (jnp.dot(u, m_scr[...], preferred_element_type=F32)
         + jnp.dot(xin_scr[...].astype(BF16), wout_scr[...], preferred_element_type=F32))
    z = jax.nn.gelu(y)
    for t in range(q):
        o_ref[t] = z[:, t * LANES:(t + 1) * LANES].astype(o_ref.dtype)


def _complex_pow(zr, zi, n):
    rr, ri = None, None
    br, bi = zr, zi
    while n:
        if n & 1:
            rr, ri = (br, bi) if rr is None else (rr * br - ri * bi, rr * bi + ri * br)
        n >>= 1
        if n:
            br, bi = br * br - bi * bi, 2.0 * br * bi
    return rr, ri


def _s5_tables(a_re, a_im, log_dt, b_re, b_im, c_re, c_im, n_steps):
    g, p = a_re.shape
    gpb = LANES // SSM_GROUP
    nblk = g // gpb
    dt = jnp.exp(log_dt.astype(F32))[:, None]
    ar = a_re.astype(F32)
    ai = a_im.astype(F32)
    mag = jnp.exp(dt * ar)
    ang = dt * ai
    abar_re = mag * jnp.cos(ang)
    abar_im = mag * jnp.sin(ang)
    nr = abar_re - 1.0
    ni = abar_im
    den = ar * ar + ai * ai
    f_re = (nr * ar + ni * ai) / den
    f_im = (ni * ar - nr * ai) / den
    br = b_re.astype(F32)
    bi = b_im.astype(F32)
    bb_re = f_re[..., None] * br - f_im[..., None] * bi
    bb_im = f_re[..., None] * bi + f_im[..., None] * br

    pw_r, pw_i = [jnp.ones_like(abar_re)], [jnp.zeros_like(abar_re)]
    for _ in range(SSM_CHUNK):
        pw_r.append(pw_r[-1] * abar_re - pw_i[-1] * abar_im)
        pw_i.append(pw_r[-2] * abar_im + pw_i[-1] * abar_re)
    seg_r, seg_i = _complex_pow(pw_r[-1], pw_i[-1], n_steps)
    pw_r.append(seg_r)
    pw_i.append(seg_i)
    n_rows = 24
    pr = jnp.stack(pw_r, 0).reshape(len(pw_r), nblk, gpb * p).transpose(1, 0, 2)
    pi = jnp.stack(pw_i, 0).reshape(len(pw_i), nblk, gpb * p).transpose(1, 0, 2)
    pad = ((0, 0), (0, n_rows - pr.shape[1]), (0, 0))
    pr = jnp.pad(pr, pad)
    pi = jnp.pad(pi, pad)
    prc = pr.transpose(0, 2, 1)
    pic = pi.transpose(0, 2, 1)

    eye = jnp.eye(gpb, dtype=F32)

    def in_to_state(bb):
        v = bb.reshape(nblk, gpb, p, SSM_GROUP).transpose(0, 1, 3, 2)
        e = v[:, :, :, None, :] * eye[None, :, None, :, None]
        return e.reshape(nblk, LANES, gpb * p)

    def state_to_out(c):
        v = c.astype(F32).reshape(nblk, gpb, SSM_GROUP, p).transpose(0, 1, 3, 2)
        e = v[:, :, :, None, :] * eye[None, :, None, :, None]
        return e.reshape(nblk, gpb * p, LANES)

    w0 = jnp.concatenate([in_to_state(bb_re), in_to_state(bb_im)], axis=2)
    cm = jnp.concatenate([state_to_out(c_re), -state_to_out(c_im)], axis=1)
    return w0, cm, pr, pi, prc, pic


def _s5_mixer(hn, bsz, lp, a_re, a_im, log_dt, b_re, b_im, c_re, c_im, d_skip):
    t, d = hn.shape
    q, nseg = SSM_CHUNK, SSM_SEGMENTS
    assert lp % (q * nseg * 2) == 0
    n_steps = lp // (q * nseg)
    rows = n_steps * nseg
    x = hn.reshape(bsz, nseg, n_steps, q, d).transpose(3, 0, 2, 1, 4).reshape(q, bsz * rows, d)

    w0, cm, pr, pi, prc, pic = _s5_tables(a_re, a_im, log_dt, b_re, b_im, c_re, c_im, n_steps)
    nblk = d // LANES
    n_state = w0.shape[2]
    kern = functools.partial(_s5_kernel, n_steps=n_steps)
    z = pl.pallas_call(
        kern,
        grid=(nblk, bsz),
        in_specs=[pl.BlockSpec((q, rows, LANES), lambda k, b: (0, b, k)),
                  pl.BlockSpec((1, LANES, n_state), lambda k, b: (k, 0, 0)),
                  pl.BlockSpec((1, n_state, LANES), lambda k, b: (k, 0, 0)),
                  pl.BlockSpec((1,) + pr.shape[1:], lambda k, b: (k, 0, 0)),
                  pl.BlockSpec((1,) + pi.shape[1:], lambda k, b: (k, 0, 0)),
                  pl.BlockSpec((1,) + prc.shape[1:], lambda k, b: (k, 0, 0)),
                  pl.BlockSpec((1,) + pic.shape[1:], lambda k, b: (k, 0, 0)),
                  pl.BlockSpec((1, LANES), lambda k, b: (0, k))],
        out_specs=pl.BlockSpec((q, rows, LANES), lambda k, b: (0, b, k)),
        out_shape=jax.ShapeDtypeStruct((q, bsz * rows, d), BF16),
        scratch_shapes=[pltpu.VMEM((q * LANES, q * LANES), BF16),
                        pltpu.VMEM((q * LANES, n_state), BF16),
                        pltpu.VMEM((n_state, q * LANES), BF16),
                        pltpu.VMEM((rows, n_state), F32),
                        pltpu.VMEM((rows, n_state), F32)],
        compiler_params=_params("arbitrary", "arbitrary"),
        name="s5_mixer",
    )(x, w0, cm, pr, pi, prc, pic, d_skip.reshape(1, d).astype(F32))
    return z.reshape(q, bsz, n_steps, nseg, d).transpose(1, 3, 2, 0, 4).reshape(t, d)


def _rope_tables(length):
    inv = ROPE_THETA ** (-jnp.arange(0, HEAD_DIM, 2, dtype=F32) / HEAD_DIM)
    ang = jnp.arange(length, dtype=F32)[:, None] * inv[None, :]
    cos, sin = jnp.cos(ang), jnp.sin(ang)
    return jnp.concatenate([cos, cos], axis=1), jnp.concatenate([-sin, sin], axis=1)


def _lambda_init(layer_idx):
    return 0.8 - 0.6 * math.exp(-0.3 * layer_idx)


def kernel(x, meta_tokens, norm_mix_g, norm_mlp_g, da_w_qkv, da_q_norm_g, da_k_norm_g, da_lambda, da_subln_g, da_w_o, ssm_a_re, ssm_a_im, ssm_log_dt, ssm_b_re, ssm_b_im, ssm_c_re, ssm_c_im, ssm_d, ssm_w_glu, ssm_b_glu, mlp_w_up, mlp_w_down):
    bsz, seq, d = x.shape
    depth = norm_mix_g.shape[0]
    n_mixers = 2
    length = N_META + seq
    lp = ((length + SEQ_ALIGN - 1) // SEQ_ALIGN) * SEQ_ALIGN
    meta = jnp.broadcast_to(meta_tokens.astype(x.dtype)[None], (bsz, N_META, d))
    pad = jnp.zeros((bsz, lp - length, d), x.dtype)
    h = jnp.concatenate([meta, x, pad], axis=1).reshape(bsz * lp, d)
    cos, sin = (jnp.tile(tab, (bsz, 1)) for tab in _rope_tables(lp))
    for i in range(depth):
        j = i // n_mixers
        hn = _rmsnorm(h, norm_mix_g[i])
        if i % n_mixers == 0:
            qkv = _qkv_proj(hn, da_w_qkv[j].astype(BF16), cos, sin, da_q_norm_g[j], da_k_norm_g[j])
            att = _diff_attention(qkv, da_lambda[j], da_subln_g[j], bsz, lp, d, _lambda_init(i))
            h = _proj_residual(att, da_w_o[j].astype(BF16), h)
        else:
            z = _s5_mixer(hn, bsz, lp, ssm_a_re[j], ssm_a_im[j], ssm_log_dt[j], ssm_b_re[j], ssm_b_im[j],
                          ssm_c_re[j], ssm_c_im[j], ssm_d[j])
            h = _glu_residual(z, ssm_w_glu[j].astype(BF16), ssm_b_glu[j], h)
        hm = _rmsnorm(h, norm_mlp_g[i])
        f = _mlp_up(hm, mlp_w_up[i].astype(BF16))
        h = _mlp_down(f, mlp_w_down[i].astype(BF16), h)
    return h.reshape(bsz, lp, d)[:, N_META:length]
```

```python
import functools
import math

import jax
import jax.numpy as jnp
from jax import lax
from jax.experimental import pallas as pl
from jax.experimental.pallas import tpu as pltpu

N_META = 16
SEQ_ALIGN = 256
HEAD_DIM = 128
LOG2E = 1.4426950408889634
ATTN_ROW_BLOCK = 64
ROPE_THETA = 10000.0
SSM_GROUP = 16
SSM_CHUNK = 16
SSM_SEGMENTS = 8
LANES = 128
SUBLANES = 8
EPS = 1e-6
VMEM_LIMIT_BYTES = 56 * 1024 * 1024

F32 = jnp.float32
BF16 = jnp.bfloat16


def _pick_tile(n, target, mult):
    best = None
    for t in range(mult, min(n, target) + 1, mult):
        if n % t == 0:
            best = t
    assert best is not None, (n, target, mult)
    return best


def _row_tile(t):
    return _pick_tile(t, 1536, SEQ_ALIGN)


def _params(*sem):
    return pltpu.CompilerParams(dimension_semantics=sem, vmem_limit_bytes=VMEM_LIMIT_BYTES)


def _rmsnorm_kernel(x_ref, g_ref, o_ref):
    x = x_ref[...]
    ms = jnp.mean(x * x, axis=-1, keepdims=True)
    o_ref[...] = (x * lax.rsqrt(ms + EPS) * g_ref[...]).astype(o_ref.dtype)


def _rmsnorm(x, g):
    t, d = x.shape
    tr = _pick_tile(t, 256, 16)
    return pl.pallas_call(
        _rmsnorm_kernel,
        grid=(t // tr,),
        in_specs=[pl.BlockSpec((tr, d), lambda i: (i, 0)),
                  pl.BlockSpec((1, d), lambda i: (0, 0))],
        out_specs=pl.BlockSpec((tr, d), lambda i: (i, 0)),
        out_shape=jax.ShapeDtypeStruct((t, d), BF16),
        compiler_params=_params("parallel"),
        name="rmsnorm",
    )(x, g.reshape(1, d).astype(F32))


def _qkv_kernel(a_ref, w_ref, cos_ref, sin_ref, qg_ref, kg_ref, o_ref, *, n_q_blocks, q_scale):
    acc = jnp.dot(a_ref[...], w_ref[...], preferred_element_type=F32)
    j = pl.program_id(1)
    tn = acc.shape[1]

    def norm_rope(g, scale):
        cos = cos_ref[...]
        sin = sin_ref[...]
        for u in range(tn // HEAD_DIM):
            x = acc[:, u * HEAD_DIM:(u + 1) * HEAD_DIM]
            ms = jnp.mean(x * x, axis=-1, keepdims=True)
            y = x * lax.rsqrt(ms + EPS) * g
            y = y * cos + pltpu.roll(y, HEAD_DIM // 2, 1) * sin
            if scale != 1.0:
                y = y * scale
            o_ref[:, u * HEAD_DIM:(u + 1) * HEAD_DIM] = y.astype(o_ref.dtype)

    @pl.when(j < n_q_blocks)
    def _():
        norm_rope(qg_ref[...], q_scale)

    @pl.when(jnp.logical_and(j >= n_q_blocks, j < 2 * n_q_blocks))
    def _():
        norm_rope(kg_ref[...], 1.0)

    @pl.when(j >= 2 * n_q_blocks)
    def _():
        o_ref[...] = acc.astype(o_ref.dtype)


def _qkv_proj(a, w, cos, sin, q_g, k_g):
    t, d = a.shape
    n = w.shape[1]
    tm = _row_tile(t)
    tn = _pick_tile(d, 512, HEAD_DIM)
    kern = functools.partial(_qkv_kernel, n_q_blocks=d // tn, q_scale=HEAD_DIM ** -0.5 * LOG2E)
    return pl.pallas_call(
        kern,
        grid=(t // tm, n // tn),
        in_specs=[pl.BlockSpec((tm, d), lambda i, j: (i, 0)),
                  pl.BlockSpec((d, tn), lambda i, j: (0, j)),
                  pl.BlockSpec((tm, HEAD_DIM), lambda i, j: (i, 0)),
                  pl.BlockSpec((tm, HEAD_DIM), lambda i, j: (i, 0)),
                  pl.BlockSpec((1, HEAD_DIM), lambda i, j: (0, 0)),
                  pl.BlockSpec((1, HEAD_DIM), lambda i, j: (0, 0))],
        out_specs=pl.BlockSpec((tm, tn), lambda i, j: (i, j)),
        out_shape=jax.ShapeDtypeStruct((t, n), BF16),
        compiler_params=_params("parallel", "arbitrary"),
        name="qkv_proj",
    )(a, w, cos, sin, q_g.reshape(1, HEAD_DIM).astype(F32), k_g.reshape(1, HEAD_DIM).astype(F32))


def _attn_kernel(q_ref, k_ref, v_ref, lam_ref, g_ref, o_ref,
                 m1_ref, l1_ref, a1_ref, m2_ref, l2_ref, a2_ref, s_even_ref, s_odd_ref, p_ref, alpha_ref,
                 *, blk, lam_init):
    s_bufs = (s_even_ref, s_odd_ref)
    nq = q_ref.shape[0] // blk
    stats = ((m1_ref, l1_ref, a1_ref), (m2_ref, l2_ref, a2_ref))
    lv = lam_ref[...]
    lam = (jnp.exp(jnp.sum(lv[0:1] * lv[1:2], axis=-1, keepdims=True))
           - jnp.exp(jnp.sum(lv[2:3] * lv[3:4], axis=-1, keepdims=True)) + lam_init)

    def rows(i):
        return pl.ds(pl.multiple_of(i * blk, blk), blk)

    def init_stats():
        for m_ref, l_ref, a_ref in stats:
            m_ref[...] = jnp.full(m_ref.shape, -jnp.inf, F32)
            l_ref[...] = jnp.zeros(l_ref.shape, F32)
            a_ref[...] = jnp.zeros(a_ref.shape, F32)

    def scores(qi, j, s_ref):
        q = q_ref[rows(qi), :]
        kk = k_ref[rows(j), :]
        for c in range(2):
            s_ref[c] = lax.dot_general(
                q[:, c * HEAD_DIM:(c + 1) * HEAD_DIM], kk[:, c * HEAD_DIM:(c + 1) * HEAD_DIM],
                (((1,), (1,)), ((), ())), preferred_element_type=F32)

    def softmax_pv(j, s_ref, masked):
        vv = v_ref[rows(j), :]
        for c, (m_ref, l_ref, a_ref) in enumerate(stats):
            for r in range(0, blk, ATTN_ROW_BLOCK):
                rb = slice(r, r + ATTN_ROW_BLOCK)
                s = s_ref[c, rb, :]
                if masked:
                    row = r + lax.broadcasted_iota(jnp.int32, s.shape, 0)
                    col = lax.broadcasted_iota(jnp.int32, s.shape, 1)
                    s = jnp.where(col <= row, s, -jnp.inf)
                m_old = m_ref[rb, :]
                m_new = jnp.maximum(m_old, jnp.max(s, axis=-1, keepdims=True))
                alpha = jnp.exp2(m_old - m_new)
                p = jnp.exp2(s - m_new)
                part = p[:, :LANES]
                for u in range(1, blk // LANES):
                    part = part + p[:, u * LANES:(u + 1) * LANES]
                l_ref[rb, :] = alpha * l_ref[rb, :] + part
                p_ref[c, rb, :] = p.astype(BF16)
                alpha_ref[c, rb, :] = alpha
                m_ref[rb, :] = m_new
            a_ref[...] = alpha_ref[c] * a_ref[...] + jnp.dot(p_ref[c], vv, preferred_element_type=F32)

    def finalize(qi):
        l1 = jnp.sum(l1_ref[...], axis=-1, keepdims=True)
        l2 = jnp.sum(l2_ref[...], axis=-1, keepdims=True)
        o = a1_ref[...] / l1 - lam * (a2_ref[...] / l2)
        ms = jnp.mean(o * o, axis=-1, keepdims=True)
        y = o * lax.rsqrt(ms + EPS) * g_ref[...] * (1.0 - lam_init)
        o_ref[rows(qi), :] = y.astype(o_ref.dtype)

    def item(n, next_qi, next_j, j, masked):
        for parity in range(2):
            @pl.when(n % 2 == parity)
            def _():
                scores(next_qi, next_j, s_bufs[1 - parity])
                softmax_pv(j, s_bufs[parity], masked)

    init_stats()
    scores(0, 0, s_bufs[0])

    def q_block(qi, carry):
        base = (qi * (qi + 1)) // 2

        def full_chunk(j, c):
            item(base + j, qi, j + 1, j, False)
            return c

        lax.fori_loop(0, qi, full_chunk, 0)
        item(base + qi, jnp.minimum(qi + 1, nq - 1), 0, qi, True)
        finalize(qi)
        init_stats()
        return carry

    lax.fori_loop(0, nq, q_block, 0)


def _diff_attention(qkv, lam_vecs, subln_g, bsz, lp, d, lam_init):
    t = qkv.shape[0]
    n_heads = d // (2 * HEAD_DIM)
    hw = 2 * HEAD_DIM
    blk = _pick_tile(lp, 768, SEQ_ALIGN)
    kern = functools.partial(_attn_kernel, blk=blk, lam_init=lam_init)
    return pl.pallas_call(
        kern,
        grid=(bsz, n_heads),
        in_specs=[pl.BlockSpec((lp, hw), lambda b, h: (b, h)),
                  pl.BlockSpec((lp, hw), lambda b, h: (b, n_heads + h)),
                  pl.BlockSpec((lp, hw), lambda b, h: (b, 2 * n_heads + h)),
                  pl.BlockSpec((4, HEAD_DIM), lambda b, h: (0, 0)),
                  pl.BlockSpec((1, hw), lambda b, h: (0, 0))],
        out_specs=pl.BlockSpec((lp, hw), lambda b, h: (b, h)),
        out_shape=jax.ShapeDtypeStruct((t, d), BF16),
        scratch_shapes=[pltpu.VMEM((blk, 1), F32), pltpu.VMEM((blk, LANES), F32), pltpu.VMEM((blk, hw), F32),
                        pltpu.VMEM((blk, 1), F32), pltpu.VMEM((blk, LANES), F32), pltpu.VMEM((blk, hw), F32),
                        pltpu.VMEM((2, blk, blk), F32), pltpu.VMEM((2, blk, blk), F32),
                        pltpu.VMEM((2, blk, blk), BF16), pltpu.VMEM((2, blk, 1), F32)],
        compiler_params=_params("parallel", "parallel"),
        name="diff_attention",
    )(qkv, qkv, qkv, lam_vecs.astype(F32), subln_g.reshape(1, hw).astype(F32))


def _proj_residual_kernel(a_ref, w_ref, r_ref, o_ref):
    o_ref[...] = r_ref[...] + jnp.dot(a_ref[...], w_ref[...], preferred_element_type=F32)


def _proj_residual(a, w, res):
    t, k = a.shape
    n = w.shape[1]
    tm = _row_tile(t)
    tn = _pick_tile(n, 512, LANES)
    return pl.pallas_call(
        _proj_residual_kernel,
        grid=(t // tm, n // tn),
        in_specs=[pl.BlockSpec((tm, k), lambda i, j: (i, 0)),
                  pl.BlockSpec((k, tn), lambda i, j: (0, j)),
                  pl.BlockSpec((tm, tn), lambda i, j: (i, j))],
        out_specs=pl.BlockSpec((tm, tn), lambda i, j: (i, j)),
        out_shape=jax.ShapeDtypeStruct((t, n), F32),
        compiler_params=_params("parallel", "arbitrary"),
        name="proj_residual",
    )(a, w, res)


def _mlp_up_kernel(a_ref, w_ref, o_ref):
    u = jnp.dot(a_ref[...], w_ref[...], preferred_element_type=F32)
    o_ref[...] = jnp.square(jnp.maximum(u, 0.0)).astype(o_ref.dtype)


def _mlp_up(a, w):
    t, k = a.shape
    n = w.shape[1]
    tm = _row_tile(t)
    tn = _pick_tile(n, 512, LANES)
    return pl.pallas_call(
        _mlp_up_kernel,
        grid=(t // tm, n // tn),
        in_specs=[pl.BlockSpec((tm, k), lambda i, j: (i, 0)),
                  pl.BlockSpec((k, tn), lambda i, j: (0, j))],
        out_specs=pl.BlockSpec((tm, tn), lambda i, j: (i, j)),
        out_shape=jax.ShapeDtypeStruct((t, n), BF16),
        compiler_params=_params("parallel", "arbitrary"),
        name="mlp_up",
    )(a, w)


def _mlp_down_kernel(a_ref, w_ref, r_ref, o_ref):
    k = pl.program_id(2)
    part = jnp.dot(a_ref[...], w_ref[...], preferred_element_type=F32)

    @pl.when(k == 0)
    def _():
        o_ref[...] = r_ref[...] + part

    @pl.when(k > 0)
    def _():
        o_ref[...] += part


def _mlp_down(a, w, res):
    t, kdim = a.shape
    n = w.shape[1]
    tm = _row_tile(t)
    tn = _pick_tile(n, 1024, LANES)
    tk = _pick_tile(kdim, 2048, LANES)
    return pl.pallas_call(
        _mlp_down_kernel,
        grid=(t // tm, n // tn, kdim // tk),
        in_specs=[pl.BlockSpec((tm, tk), lambda i, j, k: (i, k)),
                  pl.BlockSpec((tk, tn), lambda i, j, k: (k, j)),
                  pl.BlockSpec((tm, tn), lambda i, j, k: (i, j))],
        out_specs=pl.BlockSpec((tm, tn), lambda i, j, k: (i, j)),
        out_shape=jax.ShapeDtypeStruct((t, n), F32),
        compiler_params=_params("parallel", "parallel", "arbitrary"),
        name="mlp_down",
    )(a, w, res)


def _glu_kernel(a_ref, wv_ref, wg_ref, bv_ref, bg_ref, r_ref, o_ref):
    a = a_ref[...]
    val = jnp.dot(a, wv_ref[...], preferred_element_type=F32) + bv_ref[...]
    gate = jnp.dot(a, wg_ref[...], preferred_element_type=F32) + bg_ref[...]
    o_ref[...] = r_ref[...] + val * jax.nn.sigmoid(gate)


def _glu_residual(a, w, bias, res):
    t, k = a.shape
    n = w.shape[1] // 2
    tm = _row_tile(t)
    tn = _pick_tile(n, 256, LANES)
    nb = n // tn
    b2 = bias.reshape(1, 2 * n).astype(F32)
    return pl.pallas_call(
        _glu_kernel,
        grid=(t // tm, nb),
        in_specs=[pl.BlockSpec((tm, k), lambda i, j: (i, 0)),
                  pl.BlockSpec((k, tn), lambda i, j: (0, j)),
                  pl.BlockSpec((k, tn), lambda i, j: (0, nb + j)),
                  pl.BlockSpec((1, tn), lambda i, j: (0, j)),
                  pl.BlockSpec((1, tn), lambda i, j: (0, nb + j)),
                  pl.BlockSpec((tm, tn), lambda i, j: (i, j))],
        out_specs=pl.BlockSpec((tm, tn), lambda i, j: (i, j)),
        out_shape=jax.ShapeDtypeStruct((t, n), F32),
        compiler_params=_params("parallel", "arbitrary"),
        name="glu_residual",
    )(a, w, w, b2, b2, res)


def _s5_kernel(x_ref, w0_ref, cm_ref, pr_ref, pi_ref, prc_ref, pic_ref, d_ref, o_ref,
               m_scr, wst_scr, wout_scr, s_scr, xin_scr, *, n_steps):
    q = SSM_CHUNK
    half = w0_ref.shape[2] // 2

    @pl.when(pl.program_id(1) == 0)
    def _build_weights():
        w0 = w0_ref[0]
        w0re, w0im = w0[:, :half], w0[:, half:]
        cm = cm_ref[0]
        cm_bf = cm.astype(BF16)
        row = lax.broadcasted_iota(jnp.int32, (LANES, LANES), 0)
        col = lax.broadcasted_iota(jnp.int32, (LANES, LANES), 1)
        skip = jnp.where(row == col, jnp.broadcast_to(d_ref[...], (LANES, LANES)), 0.0)
        zero_blk = jnp.zeros((LANES, LANES), BF16)
        for tau in range(q):
            ar = pr_ref[0, tau:tau + 1, :]
            ai = pi_ref[0, tau:tau + 1, :]
            w_tau = jnp.concatenate([w0re * ar - w0im * ai, w0im * ar + w0re * ai], axis=1).astype(BF16)
            t_st = q - 1 - tau
            wst_scr[t_st * LANES:(t_st + 1) * LANES, :] = w_tau
            k_tau = jnp.dot(w_tau, cm_bf, preferred_element_type=F32)
            if tau == 0:
                k_tau = k_tau + skip
            k_bf = k_tau.astype(BF16)
            for t in range(q - tau):
                m_scr[t * LANES:(t + 1) * LANES, (t + tau) * LANES:(t + tau + 1) * LANES] = k_bf
        for t in range(q):
            for t2 in range(t):
                m_scr[t * LANES:(t + 1) * LANES, t2 * LANES:(t2 + 1) * LANES] = zero_blk
        cre, cimn = cm[:half], cm[half:]
        for t in range(q):
            arc = prc_ref[0, :, t + 1:t + 2]
            aic = pic_ref[0, :, t + 1:t + 2]
            wout_scr[:half, t * LANES:(t + 1) * LANES] = (arc * cre + aic * cimn).astype(BF16)
            wout_scr[half:, t * LANES:(t + 1) * LANES] = (arc * cimn - aic * cre).astype(BF16)

    u = jnp.concatenate([x_ref[t] for t in range(q)], axis=1)
    s_scr[...] = jnp.dot(u, wst_scr[...], preferred_element_type=F32)

    shp = (SUBLANES, half)
    ar = jnp.broadcast_to(pr_ref[0, q:q + 1, :], shp)
    ai = jnp.broadcast_to(pi_ref[0, q:q + 1, :], shp)
    anr = jnp.broadcast_to(pr_ref[0, q + 1:q + 2, :], shp)
    ani = jnp.broadcast_to(pi_ref[0, q + 1:q + 2, :], shp)
    seg = lax.broadcasted_iota(jnp.int32, shp, 0)
    zeros = jnp.zeros(shp, F32)

    def shift_down(x):
        return jnp.where(seg == 0, 0.0, pltpu.roll(x, 1, 0))

    def advance(j, cr, ci):
        sj = s_scr[pl.ds(pl.multiple_of(j * SUBLANES, SUBLANES), SUBLANES), :]
        return ar * cr - ai * ci + sj[:, :half], ar * ci + ai * cr + sj[:, half:]

    er, ei = lax.fori_loop(0, n_steps, lambda j, c: advance(j, *c), (zeros, zeros))
    tr, ti = er, ei
    for _ in range(SSM_SEGMENTS - 1):
        sr, si = shift_down(tr), shift_down(ti)
        tr, ti = er + anr * sr - ani * si, ei + anr * si + ani * sr
    cr0, ci0 = shift_down(tr), shift_down(ti)

    def scan_store(j, c):
        cr, ci = c
        xin_scr[pl.ds(pl.multiple_of(j * SUBLANES, SUBLANES), SUBLANES), :] = jnp.concatenate([cr, ci], axis=1)
        return advance(j, cr, ci)

    lax.fori_loop(0, n_steps, scan_store, (cr0, ci0))

    y = (jnp.dot(u, m_scr[...], preferred_element_type=F32)
         + jnp.dot(xin_scr[...].astype(BF16), wout_scr[...], preferred_element_type=F32))
    z = jax.nn.gelu(y)
    for t in range(q):
        o_ref[t] = z[:, t * LANES:(t + 1) * LANES].astype(o_ref.dtype)


def _complex_pow(zr, zi, n):
    rr, ri = None, None
    br, bi = zr, zi
    while n:
        if n & 1:
            rr, ri = (br, bi) if rr is None else (rr * br - ri * bi, rr * bi + ri * br)
        n >>= 1
        if n:
            br, bi = br * br - bi * bi, 2.0 * br * bi
    return rr, ri


def _s5_tables(a_re, a_im, log_dt, b_re, b_im, c_re, c_im, n_steps):
    g, p = a_re.shape
    gpb = LANES // SSM_GROUP
    nblk = g // gpb
    dt = jnp.exp(log_dt.astype(F32))[:, None]
    ar = a_re.astype(F32)
    ai = a_im.astype(F32)
    mag = jnp.exp(dt * ar)
    ang = dt * ai
    abar_re = mag * jnp.cos(ang)
    abar_im = mag * jnp.sin(ang)
    nr = abar_re - 1.0
    ni = abar_im
    den = ar * ar + ai * ai
    f_re = (nr * ar + ni * ai) / den
    f_im = (ni * ar - nr * ai) / den
    br = b_re.astype(F32)
    bi = b_im.astype(F32)
    bb_re = f_re[..., None] * br - f_im[..., None] * bi
    bb_im = f_re[..., None] * bi + f_im[..., None] * br

    pw_r, pw_i = [jnp.ones_like(abar_re)], [jnp.zeros_like(abar_re)]
    for _ in range(SSM_CHUNK):
        pw_r.append(pw_r[-1] * abar_re - pw_i[-1] * abar_im)
        pw_i.append(pw_r[-2] * abar_im + pw_i[-1] * abar_re)
    seg_r, seg_i = _complex_pow(pw_r[-1], pw_i[-1], n_steps)
    pw_r.append(seg_r)
    pw_i.append(seg_i)
    n_rows = 24
    pr = jnp.stack(pw_r, 0).reshape(len(pw_r), nblk, gpb * p).transpose(1, 0, 2)
    pi = jnp.stack(pw_i, 0).reshape(len(pw_i), nblk, gpb * p).transpose(1, 0, 2)
    pad = ((0, 0), (0, n_rows - pr.shape[1]), (0, 0))
    pr = jnp.pad(pr, pad)
    pi = jnp.pad(pi, pad)
    prc = pr.transpose(0, 2, 1)
    pic = pi.transpose(0, 2, 1)

    eye = jnp.eye(gpb, dtype=F32)

    def in_to_state(bb):
        v = bb.reshape(nblk, gpb, p, SSM_GROUP).transpose(0, 1, 3, 2)
        e = v[:, :, :, None, :] * eye[None, :, None, :, None]
        return e.reshape(nblk, LANES, gpb * p)

    def state_to_out(c):
        v = c.astype(F32).reshape(nblk, gpb, SSM_GROUP, p).transpose(0, 1, 3, 2)
        e = v[:, :, :, None, :] * eye[None, :, None, :, None]
        return e.reshape(nblk, gpb * p, LANES)

    w0 = jnp.concatenate([in_to_state(bb_re), in_to_state(bb_im)], axis=2)
    cm = jnp.concatenate([state_to_out(c_re), -state_to_out(c_im)], axis=1)
    return w0, cm, pr, pi, prc, pic


def _s5_mixer(hn, bsz, lp, a_re, a_im, log_dt, b_re, b_im, c_re, c_im, d_skip):
    t, d = hn.shape
    q, nseg = SSM_CHUNK, SSM_SEGMENTS
    assert lp % (q * nseg * 2) == 0
    n_steps = lp // (q * nseg)
    rows = n_steps * nseg
    x = hn.reshape(bsz, nseg, n_steps, q, d).transpose(3, 0, 2, 1, 4).reshape(q, bsz * rows, d)

    w0, cm, pr, pi, prc, pic = _s5_tables(a_re, a_im, log_dt, b_re, b_im, c_re, c_im, n_steps)
    nblk = d // LANES
    n_state = w0.shape[2]
    kern = functools.partial(_s5_kernel, n_steps=n_steps)
    z = pl.pallas_call(
        kern,
        grid=(nblk, bsz),
        in_specs=[pl.BlockSpec((q, rows, LANES), lambda k, b: (0, b, k)),
                  pl.BlockSpec((1, LANES, n_state), lambda k, b: (k, 0, 0)),
                  pl.BlockSpec((1, n_state, LANES), lambda k, b: (k, 0, 0)),
                  pl.BlockSpec((1,) + pr.shape[1:], lambda k, b: (k, 0, 0)),
                  pl.BlockSpec((1,) + pi.shape[1:], lambda k, b: (k, 0, 0)),
                  pl.BlockSpec((1,) + prc.shape[1:], lambda k, b: (k, 0, 0)),
                  pl.BlockSpec((1,) + pic.shape[1:], lambda k, b: (k, 0, 0)),
                  pl.BlockSpec((1, LANES), lambda k, b: (0, k))],
        out_specs=pl.BlockSpec((q, rows, LANES), lambda k, b: (0, b, k)),
        out_shape=jax.ShapeDtypeStruct((q, bsz * rows, d), BF16),
        scratch_shapes=[pltpu.VMEM((q * LANES, q * LANES), BF16),
                        pltpu.VMEM((q * LANES, n_state), BF16),
                        pltpu.VMEM((n_state, q * LANES), BF16),
                        pltpu.VMEM((rows, n_state), F32),
                        pltpu.VMEM((rows, n_state), F32)],
        compiler_params=_params("arbitrary", "arbitrary"),
        name="s5_mixer",
    )(x, w0, cm, pr, pi, prc, pic, d_skip.reshape(1, d).astype(F32))
    return z.reshape(q, bsz, n_steps, nseg, d).transpose(1, 3, 2, 0, 4).reshape(t, d)


def _rope_tables(length):
    inv = ROPE_THETA ** (-jnp.arange(0, HEAD_DIM, 2, dtype=F32) / HEAD_DIM)
    ang = jnp.arange(length, dtype=F32)[:, None] * inv[None, :]
    cos, sin = jnp.cos(ang), jnp.sin(ang)
    return jnp.concatenate([cos, cos], axis=1), jnp.concatenate([-sin, sin], axis=1)


def _lambda_init(layer_idx):
    return 0.8 - 0.6 * math.exp(-0.3 * layer_idx)


def kernel(x, meta_tokens, norm_mix_g, norm_mlp_g, da_w_qkv, da_q_norm_g, da_k_norm_g, da_lambda, da_subln_g, da_w_o, ssm_a_re, ssm_a_im, ssm_log_dt, ssm_b_re, ssm_b_im, ssm_c_re, ssm_c_im, ssm_d, ssm_w_glu, ssm_b_glu, mlp_w_up, mlp_w_down):
    bsz, seq, d = x.shape
    depth = norm_mix_g.shape[0]
    n_mixers = 2
    length = N_META + seq
    lp = ((length + SEQ_ALIGN - 1) // SEQ_ALIGN) * SEQ_ALIGN
    meta = jnp.broadcast_to(meta_tokens.astype(x.dtype)[None], (bsz, N_META, d))
    pad = jnp.zeros((bsz, lp - length, d), x.dtype)
    h = jnp.concatenate([meta, x, pad], axis=1).reshape(bsz * lp, d)
    cos, sin = (jnp.tile(tab, (bsz, 1)) for tab in _rope_tables(lp))
    for i in range(depth):
        j = i // n_mixers
        hn = _rmsnorm(h, norm_mix_g[i])
        if i % n_mixers == 0:
            qkv = _qkv_proj(hn, da_w_qkv[j].astype(BF16), cos, sin, da_q_norm_g[j], da_k_norm_g[j])
            att = _diff_attention(qkv, da_lambda[j], da_subln_g[j], bsz, lp, d, _lambda_init(i))
            h = _proj_residual(att, da_w_o[j].astype(BF16), h)
        else:
            z = _s5_mixer(hn, bsz, lp, ssm_a_re[j], ssm_a_im[j], ssm_log_dt[j], ssm_b_re[j], ssm_b_im[j],
                          ssm_c_re[j], ssm_c_im[j], ssm_d[j])
            h = _glu_residual(z, ssm_w_glu[j].astype(BF16), ssm_b_glu[j], h)
        hm = _rmsnorm(h, norm_mlp_g[i])
        f = _mlp_up(hm, mlp_w_up[i].astype(BF16))
        h = _mlp_down(f, mlp_w_down[i].astype(BF16), h)
    return h.reshape(bsz, lp, d)[:, N_META:length]
```

```python
import functools
import math

import jax
import jax.numpy as jnp
from jax import lax
from jax.experimental import pallas as pl
from jax.experimental.pallas import tpu as pltpu

N_META = 16
SEQ_ALIGN = 256
HEAD_DIM = 128
LOG2E = 1.4426950408889634
ATTN_ROW_BLOCK = 64
ROPE_THETA = 10000.0
SSM_GROUP = 16
SSM_CHUNK = 16
SSM_SEGMENTS = 8
MXU_COLS = 256
LANES = 128
SUBLANES = 8
EPS = 1e-6
VMEM_LIMIT_BYTES = 56 * 1024 * 1024

F32 = jnp.float32
BF16 = jnp.bfloat16


def _pick_tile(n, target, mult):
    best = None
    for t in range(mult, min(n, target) + 1, mult):
        if n % t == 0:
            best = t
    assert best is not None, (n, target, mult)
    return best


def _row_tile(t):
    return _pick_tile(t, 1536, SEQ_ALIGN)


def _params(*sem):
    return pltpu.CompilerParams(dimension_semantics=sem, vmem_limit_bytes=VMEM_LIMIT_BYTES)


def _rmsnorm_kernel(x_ref, g_ref, o_ref):
    x = x_ref[...]
    ms = jnp.mean(x * x, axis=-1, keepdims=True)
    o_ref[...] = (x * lax.rsqrt(ms + EPS) * g_ref[...]).astype(o_ref.dtype)


def _rmsnorm(x, g):
    t, d = x.shape
    tr = _pick_tile(t, 256, 16)
    return pl.pallas_call(
        _rmsnorm_kernel,
        grid=(t // tr,),
        in_specs=[pl.BlockSpec((tr, d), lambda i: (i, 0)),
                  pl.BlockSpec((1, d), lambda i: (0, 0))],
        out_specs=pl.BlockSpec((tr, d), lambda i: (i, 0)),
        out_shape=jax.ShapeDtypeStruct((t, d), BF16),
        compiler_params=_params("parallel"),
        name="rmsnorm",
    )(x, g.reshape(1, d).astype(F32))


def _qkv_kernel(a_ref, w_ref, cos_ref, sin_ref, qg_ref, kg_ref, o_ref, *, n_q_blocks, q_scale):
    acc = jnp.dot(a_ref[...], w_ref[...], preferred_element_type=F32)
    j = pl.program_id(1)
    tn = acc.shape[1]

    def norm_rope(g, scale):
        cos = cos_ref[...]
        sin = sin_ref[...]
        for u in range(tn // HEAD_DIM):
            x = acc[:, u * HEAD_DIM:(u + 1) * HEAD_DIM]
            ms = jnp.mean(x * x, axis=-1, keepdims=True)
            y = x * lax.rsqrt(ms + EPS) * g
            y = y * cos + pltpu.roll(y, HEAD_DIM // 2, 1) * sin
            if scale != 1.0:
                y = y * scale
            o_ref[:, u * HEAD_DIM:(u + 1) * HEAD_DIM] = y.astype(o_ref.dtype)

    @pl.when(j < n_q_blocks)
    def _():
        norm_rope(qg_ref[...], q_scale)

    @pl.when(jnp.logical_and(j >= n_q_blocks, j < 2 * n_q_blocks))
    def _():
        norm_rope(kg_ref[...], 1.0)

    @pl.when(j >= 2 * n_q_blocks)
    def _():
        o_ref[...] = acc.astype(o_ref.dtype)


def _qkv_proj(a, w, cos, sin, q_g, k_g):
    t, d = a.shape
    n = w.shape[1]
    tm = _row_tile(t)
    tn = _pick_tile(d, 512, HEAD_DIM)
    kern = functools.partial(_qkv_kernel, n_q_blocks=d // tn, q_scale=HEAD_DIM ** -0.5 * LOG2E)
    return pl.pallas_call(
        kern,
        grid=(t // tm, n // tn),
        in_specs=[pl.BlockSpec((tm, d), lambda i, j: (i, 0)),
                  pl.BlockSpec((d, tn), lambda i, j: (0, j)),
                  pl.BlockSpec((tm, HEAD_DIM), lambda i, j: (i, 0)),
                  pl.BlockSpec((tm, HEAD_DIM), lambda i, j: (i, 0)),
                  pl.BlockSpec((1, HEAD_DIM), lambda i, j: (0, 0)),
                  pl.BlockSpec((1, HEAD_DIM), lambda i, j: (0, 0))],
        out_specs=pl.BlockSpec((tm, tn), lambda i, j: (i, j)),
        out_shape=jax.ShapeDtypeStruct((t, n), BF16),
        compiler_params=_params("parallel", "arbitrary"),
        name="qkv_proj",
    )(a, w, cos, sin, q_g.reshape(1, HEAD_DIM).astype(F32), k_g.reshape(1, HEAD_DIM).astype(F32))


def _attn_kernel(q_ref, k_ref, v_ref, lam_ref, g_ref, o_ref,
                 m1_ref, l1_ref, a1_ref, m2_ref, l2_ref, a2_ref, s_even_ref, s_odd_ref, p_ref, alpha_ref,
                 *, blk, lam_init):
    s_bufs = (s_even_ref, s_odd_ref)
    nq = q_ref.shape[0] // blk
    stats = ((m1_ref, l1_ref, a1_ref), (m2_ref, l2_ref, a2_ref))
    lv = lam_ref[...]
    lam = (jnp.exp(jnp.sum(lv[0:1] * lv[1:2], axis=-1, keepdims=True))
           - jnp.exp(jnp.sum(lv[2:3] * lv[3:4], axis=-1, keepdims=True)) + lam_init)

    def rows(i):
        return pl.ds(pl.multiple_of(i * blk, blk), blk)

    def init_stats():
        for m_ref, l_ref, a_ref in stats:
            m_ref[...] = jnp.full(m_ref.shape, -jnp.inf, F32)
            l_ref[...] = jnp.zeros(l_ref.shape, F32)
            a_ref[...] = jnp.zeros(a_ref.shape, F32)

    def scores(qi, j, s_ref):
        q = q_ref[rows(qi), :]
        kk = k_ref[rows(j), :]
        for c in range(2):
            s_ref[c] = lax.dot_general(
                q[:, c * HEAD_DIM:(c + 1) * HEAD_DIM], kk[:, c * HEAD_DIM:(c + 1) * HEAD_DIM],
                (((1,), (1,)), ((), ())), preferred_element_type=F32)

    def softmax_pv(j, s_ref, masked):
        vv = v_ref[rows(j), :]
        for c, (m_ref, l_ref, a_ref) in enumerate(stats):
            for r in range(0, blk, ATTN_ROW_BLOCK):
                rb = slice(r, r + ATTN_ROW_BLOCK)
                s = s_ref[c, rb, :]
                if masked:
                    row = r + lax.broadcasted_iota(jnp.int32, s.shape, 0)
                    col = lax.broadcasted_iota(jnp.int32, s.shape, 1)
                    s = jnp.where(col <= row, s, -jnp.inf)
                m_old = m_ref[rb, :]
                m_new = jnp.maximum(m_old, jnp.max(s, axis=-1, keepdims=True))
                alpha = jnp.exp2(m_old - m_new)
                p = jnp.exp2(s - m_new)
                part = p[:, :LANES]
                for u in range(1, blk // LANES):
                    part = part + p[:, u * LANES:(u + 1) * LANES]
                l_ref[rb, :] = alpha * l_ref[rb, :] + part
                p_ref[c, rb, :] = p.astype(BF16)
                alpha_ref[c, rb, :] = alpha
                m_ref[rb, :] = m_new
            a_ref[...] = alpha_ref[c] * a_ref[...] + jnp.dot(p_ref[c], vv, preferred_element_type=F32)

    def finalize(qi):
        l1 = jnp.sum(l1_ref[...], axis=-1, keepdims=True)
        l2 = jnp.sum(l2_ref[...], axis=-1, keepdims=True)
        o = a1_ref[...] / l1 - lam * (a2_ref[...] / l2)
        ms = jnp.mean(o * o, axis=-1, keepdims=True)
        y = o * lax.rsqrt(ms + EPS) * g_ref[...] * (1.0 - lam_init)
        o_ref[rows(qi), :] = y.astype(o_ref.dtype)

    def item(n, next_qi, next_j, j, masked):
        for parity in range(2):
            @pl.when(n % 2 == parity)
            def _():
                scores(next_qi, next_j, s_bufs[1 - parity])
                softmax_pv(j, s_bufs[parity], masked)

    init_stats()
    scores(0, 0, s_bufs[0])

    def q_block(qi, carry):
        base = (qi * (qi + 1)) // 2

        def full_chunk(j, c):
            item(base + j, qi, j + 1, j, False)
            return c

        lax.fori_loop(0, qi, full_chunk, 0)
        item(base + qi, jnp.minimum(qi + 1, nq - 1), 0, qi, True)
        finalize(qi)
        init_stats()
        return carry

    lax.fori_loop(0, nq, q_block, 0)


def _diff_attention(qkv, lam_vecs, subln_g, bsz, lp, d, lam_init):
    t = qkv.shape[0]
    n_heads = d // (2 * HEAD_DIM)
    hw = 2 * HEAD_DIM
    blk = _pick_tile(lp, 768, SEQ_ALIGN)
    kern = functools.partial(_attn_kernel, blk=blk, lam_init=lam_init)
    return pl.pallas_call(
        kern,
        grid=(bsz, n_heads),
        in_specs=[pl.BlockSpec((lp, hw), lambda b, h: (b, h)),
                  pl.BlockSpec((lp, hw), lambda b, h: (b, n_heads + h)),
                  pl.BlockSpec((lp, hw), lambda b, h: (b, 2 * n_heads + h)),
                  pl.BlockSpec((4, HEAD_DIM), lambda b, h: (0, 0)),
                  pl.BlockSpec((1, hw), lambda b, h: (0, 0))],
        out_specs=pl.BlockSpec((lp, hw), lambda b, h: (b, h)),
        out_shape=jax.ShapeDtypeStruct((t, d), BF16),
        scratch_shapes=[pltpu.VMEM((blk, 1), F32), pltpu.VMEM((blk, LANES), F32), pltpu.VMEM((blk, hw), F32),
                        pltpu.VMEM((blk, 1), F32), pltpu.VMEM((blk, LANES), F32), pltpu.VMEM((blk, hw), F32),
                        pltpu.VMEM((2, blk, blk), F32), pltpu.VMEM((2, blk, blk), F32),
                        pltpu.VMEM((2, blk, blk), BF16), pltpu.VMEM((2, blk, 1), F32)],
        compiler_params=_params("parallel", "parallel"),
        name="diff_attention",
    )(qkv, qkv, qkv, lam_vecs.astype(F32), subln_g.reshape(1, hw).astype(F32))


def _col_panels(n):
    return [slice(u, u + MXU_COLS) for u in range(0, n, MXU_COLS)]


def _emit_norm_inputs(h, cols, hb_ref, ssq_ref):
    hb_ref[:, cols] = h.astype(hb_ref.dtype)
    ssq_ref[...] += jnp.sum(h * h, axis=-1, keepdims=True)


def _proj_residual_kernel(a_ref, w_ref, r_ref, o_ref, hb_ref, ssq_ref):
    @pl.when(pl.program_id(1) == 0)
    def _():
        ssq_ref[...] = jnp.zeros(ssq_ref.shape, F32)

    for cols in _col_panels(o_ref.shape[1]):
        h = r_ref[:, cols] + jnp.dot(a_ref[...], w_ref[:, cols], preferred_element_type=F32)
        o_ref[:, cols] = h
        _emit_norm_inputs(h, cols, hb_ref, ssq_ref)


def _norm_out_shapes(t, n):
    return (jax.ShapeDtypeStruct((t, n), F32), jax.ShapeDtypeStruct((t, n), BF16),
            jax.ShapeDtypeStruct((t, 1), F32))


def _proj_residual(a, w, res):
    t, k = a.shape
    n = w.shape[1]
    tm = _row_tile(t)
    tn = _pick_tile(n, 512, LANES)
    return pl.pallas_call(
        _proj_residual_kernel,
        grid=(t // tm, n // tn),
        in_specs=[pl.BlockSpec((tm, k), lambda i, j: (i, 0)),
                  pl.BlockSpec((k, tn), lambda i, j: (0, j)),
                  pl.BlockSpec((tm, tn), lambda i, j: (i, j))],
        out_specs=(pl.BlockSpec((tm, tn), lambda i, j: (i, j)),
                   pl.BlockSpec((tm, tn), lambda i, j: (i, j)),
                   pl.BlockSpec((tm, 1), lambda i, j: (i, 0))),
        out_shape=_norm_out_shapes(t, n),
        compiler_params=_params("parallel", "arbitrary"),
        name="proj_residual",
    )(a, w, res)


def _mlp_up_kernel(a_ref, w_ref, ssq_ref, o_ref, *, inv_d):
    r2 = 1.0 / (ssq_ref[...] * inv_d + EPS)
    for cols in _col_panels(o_ref.shape[1]):
        u = jnp.dot(a_ref[...], w_ref[:, cols], preferred_element_type=F32)
        o_ref[:, cols] = (jnp.square(jnp.maximum(u, 0.0)) * r2).astype(o_ref.dtype)


def _mlp_up(hb, w, ssq):
    t, k = hb.shape
    n = w.shape[1]
    tm = _row_tile(t)
    tn = _pick_tile(n, 512, LANES)
    return pl.pallas_call(
        functools.partial(_mlp_up_kernel, inv_d=1.0 / k),
        grid=(t // tm, n // tn),
        in_specs=[pl.BlockSpec((tm, k), lambda i, j: (i, 0)),
                  pl.BlockSpec((k, tn), lambda i, j: (0, j)),
                  pl.BlockSpec((tm, 1), lambda i, j: (i, 0))],
        out_specs=pl.BlockSpec((tm, tn), lambda i, j: (i, j)),
        out_shape=jax.ShapeDtypeStruct((t, n), BF16),
        compiler_params=_params("parallel", "arbitrary"),
        name="mlp_up",
    )(hb, w, ssq)


def _mlp_down_kernel(a_ref, w_ref, r_ref, o_ref, *norm_refs, nk):
    j = pl.program_id(1)
    k = pl.program_id(2)

    def accumulate(base_ref, last):
        for cols in _col_panels(o_ref.shape[1]):
            h = base_ref[:, cols] + jnp.dot(a_ref[...], w_ref[:, cols], preferred_element_type=F32)
            o_ref[:, cols] = h
            if last and norm_refs:
                _emit_norm_inputs(h, cols, *norm_refs)

    if norm_refs:
        @pl.when(jnp.logical_and(j == 0, k == 0))
        def _():
            norm_refs[1][...] = jnp.zeros(norm_refs[1].shape, F32)

    if nk == 1:
        accumulate(r_ref, True)
        return

    @pl.when(k == 0)
    def _():
        accumulate(r_ref, False)

    @pl.when(jnp.logical_and(k > 0, k < nk - 1))
    def _():
        accumulate(o_ref, False)

    @pl.when(k == nk - 1)
    def _():
        accumulate(o_ref, True)


def _mlp_down(a, w, res, emit_norm):
    t, kdim = a.shape
    n = w.shape[1]
    tm = _row_tile(t)
    tn = _pick_tile(n, 1024, LANES)
    tk = _pick_tile(kdim, 2048, LANES)
    nk = kdim // tk
    tile = pl.BlockSpec((tm, tn), lambda i, j, k: (i, j))
    if emit_norm:
        out_specs = (tile, tile, pl.BlockSpec((tm, 1), lambda i, j, k: (i, 0)))
        out_shape = _norm_out_shapes(t, n)
    else:
        out_specs = tile
        out_shape = jax.ShapeDtypeStruct((t, n), F32)
    return pl.pallas_call(
        functools.partial(_mlp_down_kernel, nk=nk),
        grid=(t // tm, n // tn, nk),
        in_specs=[pl.BlockSpec((tm, tk), lambda i, j, k: (i, k)),
                  pl.BlockSpec((tk, tn), lambda i, j, k: (k, j)),
                  tile],
        out_specs=out_specs,
        out_shape=out_shape,
        compiler_params=_params("parallel", "arbitrary", "arbitrary"),
        name="mlp_down",
    )(a, w, res)


def _glu_kernel(a_ref, wv_ref, wg_ref, bv_ref, bg_ref, r_ref, o_ref, hb_ref, ssq_ref):
    @pl.when(pl.program_id(1) == 0)
    def _():
        ssq_ref[...] = jnp.zeros(ssq_ref.shape, F32)

    a = a_ref[...]
    val = jnp.dot(a, wv_ref[...], preferred_element_type=F32) + bv_ref[...]
    gate = jnp.dot(a, wg_ref[...], preferred_element_type=F32) + bg_ref[...]
    h = r_ref[...] + val * jax.nn.sigmoid(gate)
    o_ref[...] = h
    _emit_norm_inputs(h, slice(None), hb_ref, ssq_ref)


def _glu_residual(a, w, bias, res):
    t, k = a.shape
    n = w.shape[1] // 2
    tm = _row_tile(t)
    tn = _pick_tile(n, 256, LANES)
    nb = n // tn
    b2 = bias.reshape(1, 2 * n).astype(F32)
    return pl.pallas_call(
        _glu_kernel,
        grid=(t // tm, nb),
        in_specs=[pl.BlockSpec((tm, k), lambda i, j: (i, 0)),
                  pl.BlockSpec((k, tn), lambda i, j: (0, j)),
                  pl.BlockSpec((k, tn), lambda i, j: (0, nb + j)),
                  pl.BlockSpec((1, tn), lambda i, j: (0, j)),
                  pl.BlockSpec((1, tn), lambda i, j: (0, nb + j)),
                  pl.BlockSpec((tm, tn), lambda i, j: (i, j))],
        out_specs=(pl.BlockSpec((tm, tn), lambda i, j: (i, j)),
                   pl.BlockSpec((tm, tn), lambda i, j: (i, j)),
                   pl.BlockSpec((tm, 1), lambda i, j: (i, 0))),
        out_shape=_norm_out_shapes(t, n),
        compiler_params=_params("parallel", "arbitrary"),
        name="glu_residual",
    )(a, w, w, b2, b2, res)


def _s5_kernel(x_ref, ssq_ref, g_ref, w0_ref, cm_ref, pr_ref, pi_ref, prc_ref, pic_ref, d_ref, o_ref,
               m_scr, wst_scr, wout_scr, s_scr, xin_scr, *, n_steps, inv_d):
    q = SSM_CHUNK
    half = w0_ref.shape[2] // 2

    @pl.when(pl.program_id(1) == 0)
    def _build_weights():
        w0 = w0_ref[0]
        w0re, w0im = w0[:, :half], w0[:, half:]
        cm = cm_ref[0]
        cm_bf = cm.astype(BF16)
        row = lax.broadcasted_iota(jnp.int32, (LANES, LANES), 0)
        col = lax.broadcasted_iota(jnp.int32, (LANES, LANES), 1)
        skip = jnp.where(row == col, jnp.broadcast_to(d_ref[...], (LANES, LANES)), 0.0)
        zero_blk = jnp.zeros((LANES, LANES), BF16)
        for tau in range(q):
            ar = pr_ref[0, tau:tau + 1, :]
            ai = pi_ref[0, tau:tau + 1, :]
            w_tau = jnp.concatenate([w0re * ar - w0im * ai, w0im * ar + w0re * ai], axis=1).astype(BF16)
            t_st = q - 1 - tau
            wst_scr[t_st * LANES:(t_st + 1) * LANES, :] = w_tau
            k_tau = jnp.dot(w_tau, cm_bf, preferred_element_type=F32)
            if tau == 0:
                k_tau = k_tau + skip
            k_bf = k_tau.astype(BF16)
            for t in range(q - tau):
                m_scr[t * LANES:(t + 1) * LANES, (t + tau) * LANES:(t + tau + 1) * LANES] = k_bf
        for t in range(q):
            for t2 in range(t):
                m_scr[t * LANES:(t + 1) * LANES, t2 * LANES:(t2 + 1) * LANES] = zero_blk
        cre, cimn = cm[:half], cm[half:]
        for t in range(q):
            arc = prc_ref[0, :, t + 1:t + 2]
            aic = pic_ref[0, :, t + 1:t + 2]
            wout_scr[:half, t * LANES:(t + 1) * LANES] = (arc * cre + aic * cimn).astype(BF16)
            wout_scr[half:, t * LANES:(t + 1) * LANES] = (arc * cimn - aic * cre).astype(BF16)

    gain = g_ref[...]
    u = jnp.concatenate(
        [(x_ref[t].astype(F32) * lax.rsqrt(ssq_ref[t] * inv_d + EPS) * gain).astype(BF16) for t in range(q)],
        axis=1)
    s_scr[...] = jnp.dot(u, wst_scr[...], preferred_element_type=F32)

    shp = (SUBLANES, half)
    ar = jnp.broadcast_to(pr_ref[0, q:q + 1, :], shp)
    ai = jnp.broadcast_to(pi_ref[0, q:q + 1, :], shp)
    anr = jnp.broadcast_to(pr_ref[0, q + 1:q + 2, :], shp)
    ani = jnp.broadcast_to(pi_ref[0, q + 1:q + 2, :], shp)
    seg = lax.broadcasted_iota(jnp.int32, shp, 0)
    zeros = jnp.zeros(shp, F32)

    def shift_down(x):
        return jnp.where(seg == 0, 0.0, pltpu.roll(x, 1, 0))

    def advance(j, cr, ci):
        sj = s_scr[pl.ds(pl.multiple_of(j * SUBLANES, SUBLANES), SUBLANES), :]
        return ar * cr - ai * ci + sj[:, :half], ar * ci + ai * cr + sj[:, half:]

    er, ei = lax.fori_loop(0, n_steps, lambda j, c: advance(j, *c), (zeros, zeros))
    tr, ti = er, ei
    for _ in range(SSM_SEGMENTS - 1):
        sr, si = shift_down(tr), shift_down(ti)
        tr, ti = er + anr * sr - ani * si, ei + anr * si + ani * sr
    cr0, ci0 = shift_down(tr), shift_down(ti)

    def scan_store(j, c):
        cr, ci = c
        xin_scr[pl.ds(pl.multiple_of(j * SUBLANES, SUBLANES), SUBLANES), :] = jnp.concatenate([cr, ci], axis=1)
        return advance(j, cr, ci)

    lax.fori_loop(0, n_steps, scan_store, (cr0, ci0))

    y = (jnp.dot(u, m_scr[...], preferred_element_type=F32)
         + jnp.dot(xin_scr[...].astype(BF16), wout_scr[...], preferred_element_type=F32))
    z = jax.nn.gelu(y)
    for t in range(q):
        o_ref[t] = z[:, t * LANES:(t + 1) * LANES].astype(o_ref.dtype)


def _complex_pow(zr, zi, n):
    rr, ri = None, None
    br, bi = zr, zi
    while n:
        if n & 1:
            rr, ri = (br, bi) if rr is None else (rr * br - ri * bi, rr * bi + ri * br)
        n >>= 1
        if n:
            br, bi = br * br - bi * bi, 2.0 * br * bi
    return rr, ri


def _s5_tables(a_re, a_im, log_dt, b_re, b_im, c_re, c_im, n_steps):
    g, p = a_re.shape
    gpb = LANES // SSM_GROUP
    nblk = g // gpb
    dt = jnp.exp(log_dt.astype(F32))[:, None]
    ar = a_re.astype(F32)
    ai = a_im.astype(F32)
    mag = jnp.exp(dt * ar)
    ang = dt * ai
    abar_re = mag * jnp.cos(ang)
    abar_im = mag * jnp.sin(ang)
    nr = abar_re - 1.0
    ni = abar_im
    den = ar * ar + ai * ai
    f_re = (nr * ar + ni * ai) / den
    f_im = (ni * ar - nr * ai) / den
    br = b_re.astype(F32)
    bi = b_im.astype(F32)
    bb_re = f_re[..., None] * br - f_im[..., None] * bi
    bb_im = f_re[..., None] * bi + f_im[..., None] * br

    pw_r, pw_i = [jnp.ones_like(abar_re)], [jnp.zeros_like(abar_re)]
    for _ in range(SSM_CHUNK):
        pw_r.append(pw_r[-1] * abar_re - pw_i[-1] * abar_im)
        pw_i.append(pw_r[-2] * abar_im + pw_i[-1] * abar_re)
    seg_r, seg_i = _complex_pow(pw_r[-1], pw_i[-1], n_steps)
    pw_r.append(seg_r)
    pw_i.append(seg_i)
    n_rows = 24
    pr = jnp.stack(pw_r, 0).reshape(len(pw_r), nblk, gpb * p).transpose(1, 0, 2)
    pi = jnp.stack(pw_i, 0).reshape(len(pw_i), nblk, gpb * p).transpose(1, 0, 2)
    pad = ((0, 0), (0, n_rows - pr.shape[1]), (0, 0))
    pr = jnp.pad(pr, pad)
    pi = jnp.pad(pi, pad)
    prc = pr.transpose(0, 2, 1)
    pic = pi.transpose(0, 2, 1)

    eye = jnp.eye(gpb, dtype=F32)

    def in_to_state(bb):
        v = bb.reshape(nblk, gpb, p, SSM_GROUP).transpose(0, 1, 3, 2)
        e = v[:, :, :, None, :] * eye[None, :, None, :, None]
        return e.reshape(nblk, LANES, gpb * p)

    def state_to_out(c):
        v = c.astype(F32).reshape(nblk, gpb, SSM_GROUP, p).transpose(0, 1, 3, 2)
        e = v[:, :, :, None, :] * eye[None, :, None, :, None]
        return e.reshape(nblk, gpb * p, LANES)

    w0 = jnp.concatenate([in_to_state(bb_re), in_to_state(bb_im)], axis=2)
    cm = jnp.concatenate([state_to_out(c_re), -state_to_out(c_im)], axis=1)
    return w0, cm, pr, pi, prc, pic


def _s5_mixer(hb, ssq, gain, bsz, lp, a_re, a_im, log_dt, b_re, b_im, c_re, c_im, d_skip):
    t, d = hb.shape
    q, nseg = SSM_CHUNK, SSM_SEGMENTS
    assert lp % (q * nseg * 2) == 0
    n_steps = lp // (q * nseg)
    rows = n_steps * nseg

    def chunk_layout(v):
        w = v.shape[-1]
        return v.reshape(bsz, nseg, n_steps, q, w).transpose(3, 0, 2, 1, 4).reshape(q, bsz * rows, w)

    x = chunk_layout(hb)
    ssq_c = chunk_layout(ssq)

    w0, cm, pr, pi, prc, pic = _s5_tables(a_re, a_im, log_dt, b_re, b_im, c_re, c_im, n_steps)
    nblk = d // LANES
    n_state = w0.shape[2]
    kern = functools.partial(_s5_kernel, n_steps=n_steps, inv_d=1.0 / d)
    z = pl.pallas_call(
        kern,
        grid=(nblk, bsz),
        in_specs=[pl.BlockSpec((q, rows, LANES), lambda k, b: (0, b, k)),
                  pl.BlockSpec((q, rows, 1), lambda k, b: (0, b, 0)),
                  pl.BlockSpec((1, LANES), lambda k, b: (0, k)),
                  pl.BlockSpec((1, LANES, n_state), lambda k, b: (k, 0, 0)),
                  pl.BlockSpec((1, n_state, LANES), lambda k, b: (k, 0, 0)),
                  pl.BlockSpec((1,) + pr.shape[1:], lambda k, b: (k, 0, 0)),
                  pl.BlockSpec((1,) + pi.shape[1:], lambda k, b: (k, 0, 0)),
                  pl.BlockSpec((1,) + prc.shape[1:], lambda k, b: (k, 0, 0)),
                  pl.BlockSpec((1,) + pic.shape[1:], lambda k, b: (k, 0, 0)),
                  pl.BlockSpec((1, LANES), lambda k, b: (0, k))],
        out_specs=pl.BlockSpec((q, rows, LANES), lambda k, b: (0, b, k)),
        out_shape=jax.ShapeDtypeStruct((q, bsz * rows, d), BF16),
        scratch_shapes=[pltpu.VMEM((q * LANES, q * LANES), BF16),
                        pltpu.VMEM((q * LANES, n_state), BF16),
                        pltpu.VMEM((n_state, q * LANES), BF16),
                        pltpu.VMEM((rows, n_state), F32),
                        pltpu.VMEM((rows, n_state), F32)],
        compiler_params=_params("arbitrary", "arbitrary"),
        name="s5_mixer",
    )(x, ssq_c, gain.reshape(1, d).astype(F32), w0, cm, pr, pi, prc, pic, d_skip.reshape(1, d).astype(F32))
    return z.reshape(q, bsz, n_steps, nseg, d).transpose(1, 3, 2, 0, 4).reshape(t, d)


def _rope_tables(length):
    inv = ROPE_THETA ** (-jnp.arange(0, HEAD_DIM, 2, dtype=F32) / HEAD_DIM)
    ang = jnp.arange(length, dtype=F32)[:, None] * inv[None, :]
    cos, sin = jnp.cos(ang), jnp.sin(ang)
    return jnp.concatenate([cos, cos], axis=1), jnp.concatenate([-sin, sin], axis=1)


def _lambda_init(layer_idx):
    return 0.8 - 0.6 * math.exp(-0.3 * layer_idx)


def kernel(x, meta_tokens, norm_mix_g, norm_mlp_g, da_w_qkv, da_q_norm_g, da_k_norm_g, da_lambda, da_subln_g, da_w_o, ssm_a_re, ssm_a_im, ssm_log_dt, ssm_b_re, ssm_b_im, ssm_c_re, ssm_c_im, ssm_d, ssm_w_glu, ssm_b_glu, mlp_w_up, mlp_w_down):
    bsz, seq, d = x.shape
    depth = norm_mix_g.shape[0]
    n_mixers = 2
    length = N_META + seq
    lp = ((length + SEQ_ALIGN - 1) // SEQ_ALIGN) * SEQ_ALIGN
    meta = jnp.broadcast_to(meta_tokens.astype(x.dtype)[None], (bsz, N_META, d))
    pad = jnp.zeros((bsz, lp - length, d), x.dtype)
    h = jnp.concatenate([meta, x, pad], axis=1).reshape(bsz * lp, d)
    cos, sin = (jnp.tile(tab, (bsz, 1)) for tab in _rope_tables(lp))
    hb = ssq = None
    for i in range(depth):
        j = i // n_mixers
        if i % n_mixers == 0:
            hn = _rmsnorm(h, norm_mix_g[i])
            qkv = _qkv_proj(hn, da_w_qkv[j].astype(BF16), cos, sin, da_q_norm_g[j], da_k_norm_g[j])
            att = _diff_attention(qkv, da_lambda[j], da_subln_g[j], bsz, lp, d, _lambda_init(i))
            h, hb, ssq = _proj_residual(att, da_w_o[j].astype(BF16), h)
        else:
            z = _s5_mixer(hb, ssq, norm_mix_g[i], bsz, lp, ssm_a_re[j], ssm_a_im[j], ssm_log_dt[j],
                          ssm_b_re[j], ssm_b_im[j], ssm_c_re[j], ssm_c_im[j], ssm_d[j])
            h, hb, ssq = _glu_residual(z, ssm_w_glu[j].astype(BF16), ssm_b_glu[j], h)
        w_up = (mlp_w_up[i] * norm_mlp_g[i].astype(F32)[:, None]).astype(BF16)
        f = _mlp_up(hb, w_up, ssq)
        next_is_s5 = i + 1 < depth and (i + 1) % n_mixers == 1
        if next_is_s5:
            h, hb, ssq = _mlp_down(f, mlp_w_down[i].astype(BF16), h, True)
        else:
            h = _mlp_down(f, mlp_w_down[i].astype(BF16), h, False)
    return h.reshape(bsz, lp, d)[:, N_META:length]
```

```python
import functools
import math

import jax
import jax.numpy as jnp
from jax import lax
from jax.experimental import pallas as pl
from jax.experimental.pallas import tpu as pltpu

N_META = 16
SEQ_ALIGN = 256
HEAD_DIM = 128
LOG2E = 1.4426950408889634
ATTN_ROW_BLOCK = 64
ROPE_THETA = 10000.0
SSM_GROUP = 16
SSM_CHUNK = 16
SSM_SEGMENTS = 8
MXU_COLS = 256
LANES = 128
SUBLANES = 8
EPS = 1e-6
VMEM_LIMIT_BYTES = 56 * 1024 * 1024

F32 = jnp.float32
BF16 = jnp.bfloat16


def _pick_tile(n, target, mult):
    best = None
    for t in range(mult, min(n, target) + 1, mult):
        if n % t == 0:
            best = t
    assert best is not None, (n, target, mult)
    return best


def _row_tile(t):
    return _pick_tile(t, 1536, SEQ_ALIGN)


def _params(*sem):
    return pltpu.CompilerParams(dimension_semantics=sem, vmem_limit_bytes=VMEM_LIMIT_BYTES)


class _SideCasts:
    def __init__(self, jobs, n_steps, step_of):
        self.operands, self.in_specs, self.out_specs, self.out_shapes, self.has_gain = [], [], [], [], []
        for src, gain in jobs:
            r, c = src.shape
            cr = next(x for x in (16 << p for p in range(24)) if r % x == 0 and r // x <= n_steps)

            def idx(*g, last=r // cr - 1):
                return (jnp.minimum(step_of(*g), last), 0)

            self.operands.append(src)
            self.in_specs.append(pl.BlockSpec((cr, c), idx))
            if gain is not None:
                self.operands.append(gain.reshape(r, 1).astype(F32))
                self.in_specs.append(pl.BlockSpec((cr, 1), idx))
            self.out_specs.append(pl.BlockSpec((cr, c), idx))
            self.out_shapes.append(jax.ShapeDtypeStruct((r, c), BF16))
            self.has_gain.append(gain is not None)
        self.n_in = len(self.operands)
        self.n_out = len(self.out_shapes)

    def run(self, in_refs, out_refs):
        refs = iter(in_refs)
        for has_gain, o_ref in zip(self.has_gain, out_refs):
            v = next(refs)[...]
            if has_gain:
                v = v * next(refs)[...]
            o_ref[...] = v.astype(o_ref.dtype)


def _rmsnorm_kernel(x_ref, g_ref, o_ref):
    x = x_ref[...]
    ms = jnp.mean(x * x, axis=-1, keepdims=True)
    o_ref[...] = (x * lax.rsqrt(ms + EPS) * g_ref[...]).astype(o_ref.dtype)


def _rmsnorm(x, g):
    t, d = x.shape
    tr = _pick_tile(t, 256, 16)
    return pl.pallas_call(
        _rmsnorm_kernel,
        grid=(t // tr,),
        in_specs=[pl.BlockSpec((tr, d), lambda i: (i, 0)),
                  pl.BlockSpec((1, d), lambda i: (0, 0))],
        out_specs=pl.BlockSpec((tr, d), lambda i: (i, 0)),
        out_shape=jax.ShapeDtypeStruct((t, d), BF16),
        compiler_params=_params("parallel"),
        name="rmsnorm",
    )(x, g.reshape(1, d).astype(F32))


def _qkv_kernel(*refs, n_q_blocks, q_scale, side):
    a_ref, w_ref, cos_ref, sin_ref, qg_ref, kg_ref = refs[:6]
    side_in = refs[6:6 + side.n_in]
    o_ref = refs[6 + side.n_in]
    side_out = refs[7 + side.n_in:]
    acc = jnp.dot(a_ref[...], w_ref[...], preferred_element_type=F32)
    j = pl.program_id(1)
    tn = acc.shape[1]

    def norm_rope(g, scale):
        cos = cos_ref[...]
        sin = sin_ref[...]
        for u in range(tn // HEAD_DIM):
            x = acc[:, u * HEAD_DIM:(u + 1) * HEAD_DIM]
            ms = jnp.mean(x * x, axis=-1, keepdims=True)
            y = x * lax.rsqrt(ms + EPS) * g
            y = y * cos + pltpu.roll(y, HEAD_DIM // 2, 1) * sin
            if scale != 1.0:
                y = y * scale
            o_ref[:, u * HEAD_DIM:(u + 1) * HEAD_DIM] = y.astype(o_ref.dtype)

    @pl.when(j < n_q_blocks)
    def _():
        norm_rope(qg_ref[...], q_scale)
        side.run(side_in, side_out)

    @pl.when(jnp.logical_and(j >= n_q_blocks, j < 2 * n_q_blocks))
    def _():
        norm_rope(kg_ref[...], 1.0)
        side.run(side_in, side_out)

    @pl.when(j >= 2 * n_q_blocks)
    def _():
        o_ref[...] = acc.astype(o_ref.dtype)
        side.run(side_in, side_out)


def _qkv_proj(a, w, cos, sin, q_g, k_g, cast_jobs):
    t, d = a.shape
    n = w.shape[1]
    tm = _row_tile(t)
    tn = _pick_tile(d, 512, HEAD_DIM)
    nj = n // tn
    side = _SideCasts(cast_jobs, (t // tm) * nj, lambda i, j: i * nj + j)
    kern = functools.partial(_qkv_kernel, n_q_blocks=d // tn, q_scale=HEAD_DIM ** -0.5 * LOG2E, side=side)
    return pl.pallas_call(
        kern,
        grid=(t // tm, nj),
        in_specs=[pl.BlockSpec((tm, d), lambda i, j: (i, 0)),
                  pl.BlockSpec((d, tn), lambda i, j: (0, j)),
                  pl.BlockSpec((tm, HEAD_DIM), lambda i, j: (i, 0)),
                  pl.BlockSpec((tm, HEAD_DIM), lambda i, j: (i, 0)),
                  pl.BlockSpec((1, HEAD_DIM), lambda i, j: (0, 0)),
                  pl.BlockSpec((1, HEAD_DIM), lambda i, j: (0, 0))] + side.in_specs,
        out_specs=[pl.BlockSpec((tm, tn), lambda i, j: (i, j))] + side.out_specs,
        out_shape=[jax.ShapeDtypeStruct((t, n), BF16)] + side.out_shapes,
        compiler_params=_params("arbitrary", "arbitrary"),
        name="qkv_proj",
    )(a, w, cos, sin, q_g.reshape(1, HEAD_DIM).astype(F32), k_g.reshape(1, HEAD_DIM).astype(F32),
      *side.operands)


def _attn_kernel(q_ref, k_ref, v_ref, lam_ref, g_ref, o_ref,
                 m1_ref, l1_ref, a1_ref, m2_ref, l2_ref, a2_ref, s_even_ref, s_odd_ref, p_ref, alpha_ref,
                 *, blk, lam_init):
    s_bufs = (s_even_ref, s_odd_ref)
    nq = q_ref.shape[0] // blk
    stats = ((m1_ref, l1_ref, a1_ref), (m2_ref, l2_ref, a2_ref))
    lv = lam_ref[...]
    lam = (jnp.exp(jnp.sum(lv[0:1] * lv[1:2], axis=-1, keepdims=True))
           - jnp.exp(jnp.sum(lv[2:3] * lv[3:4], axis=-1, keepdims=True)) + lam_init)

    def rows(i):
        return pl.ds(pl.multiple_of(i * blk, blk), blk)

    def init_stats():
        for m_ref, l_ref, a_ref in stats:
            m_ref[...] = jnp.full(m_ref.shape, -jnp.inf, F32)
            l_ref[...] = jnp.zeros(l_ref.shape, F32)
            a_ref[...] = jnp.zeros(a_ref.shape, F32)

    def scores(qi, j, s_ref):
        q = q_ref[rows(qi), :]
        kk = k_ref[rows(j), :]
        for c in range(2):
            s_ref[c] = lax.dot_general(
                q[:, c * HEAD_DIM:(c + 1) * HEAD_DIM], kk[:, c * HEAD_DIM:(c + 1) * HEAD_DIM],
                (((1,), (1,)), ((), ())), preferred_element_type=F32)

    def softmax_pv(j, s_ref, masked):
        vv = v_ref[rows(j), :]
        for c, (m_ref, l_ref, a_ref) in enumerate(stats):
            for r in range(0, blk, ATTN_ROW_BLOCK):
                rb = slice(r, r + ATTN_ROW_BLOCK)
                s = s_ref[c, rb, :]
                if masked:
                    row = r + lax.broadcasted_iota(jnp.int32, s.shape, 0)
                    col = lax.broadcasted_iota(jnp.int32, s.shape, 1)
                    s = jnp.where(col <= row, s, -jnp.inf)
                m_old = m_ref[rb, :]
                m_new = jnp.maximum(m_old, jnp.max(s, axis=-1, keepdims=True))
                alpha = jnp.exp2(m_old - m_new)
                p = jnp.exp2(s - m_new)
                part = p[:, :LANES]
                for u in range(1, blk // LANES):
                    part = part + p[:, u * LANES:(u + 1) * LANES]
                l_ref[rb, :] = alpha * l_ref[rb, :] + part
                p_ref[c, rb, :] = p.astype(BF16)
                alpha_ref[c, rb, :] = alpha
                m_ref[rb, :] = m_new
            a_ref[...] = alpha_ref[c] * a_ref[...] + jnp.dot(p_ref[c], vv, preferred_element_type=F32)

    def finalize(qi):
        l1 = jnp.sum(l1_ref[...], axis=-1, keepdims=True)
        l2 = jnp.sum(l2_ref[...], axis=-1, keepdims=True)
        o = a1_ref[...] / l1 - lam * (a2_ref[...] / l2)
        ms = jnp.mean(o * o, axis=-1, keepdims=True)
        y = o * lax.rsqrt(ms + EPS) * g_ref[...] * (1.0 - lam_init)
        o_ref[rows(qi), :] = y.astype(o_ref.dtype)

    def item(n, next_qi, next_j, j, masked):
        for parity in range(2):
            @pl.when(n % 2 == parity)
            def _():
                scores(next_qi, next_j, s_bufs[1 - parity])
                softmax_pv(j, s_bufs[parity], masked)

    init_stats()
    scores(0, 0, s_bufs[0])

    def q_block(qi, carry):
        base = (qi * (qi + 1)) // 2

        def full_chunk(j, c):
            item(base + j, qi, j + 1, j, False)
            return c

        lax.fori_loop(0, qi, full_chunk, 0)
        item(base + qi, jnp.minimum(qi + 1, nq - 1), 0, qi, True)
        finalize(qi)
        init_stats()
        return carry

    lax.fori_loop(0, nq, q_block, 0)


def _diff_attention(qkv, lam_vecs, subln_g, bsz, lp, d, lam_init):
    t = qkv.shape[0]
    n_heads = d // (2 * HEAD_DIM)
    hw = 2 * HEAD_DIM
    blk = _pick_tile(lp, 768, SEQ_ALIGN)
    kern = functools.partial(_attn_kernel, blk=blk, lam_init=lam_init)
    return pl.pallas_call(
        kern,
        grid=(bsz, n_heads),
        in_specs=[pl.BlockSpec((lp, hw), lambda b, h: (b, h)),
                  pl.BlockSpec((lp, hw), lambda b, h: (b, n_heads + h)),
                  pl.BlockSpec((lp, hw), lambda b, h: (b, 2 * n_heads + h)),
                  pl.BlockSpec((4, HEAD_DIM), lambda b, h: (0, 0)),
                  pl.BlockSpec((1, hw), lambda b, h: (0, 0))],
        out_specs=pl.BlockSpec((lp, hw), lambda b, h: (b, h)),
        out_shape=jax.ShapeDtypeStruct((t, d), BF16),
        scratch_shapes=[pltpu.VMEM((blk, 1), F32), pltpu.VMEM((blk, LANES), F32), pltpu.VMEM((blk, hw), F32),
                        pltpu.VMEM((blk, 1), F32), pltpu.VMEM((blk, LANES), F32), pltpu.VMEM((blk, hw), F32),
                        pltpu.VMEM((2, blk, blk), F32), pltpu.VMEM((2, blk, blk), F32),
                        pltpu.VMEM((2, blk, blk), BF16), pltpu.VMEM((2, blk, 1), F32)],
        compiler_params=_params("parallel", "parallel"),
        name="diff_attention",
    )(qkv, qkv, qkv, lam_vecs.astype(F32), subln_g.reshape(1, hw).astype(F32))


def _col_panels(n):
    return [slice(u, u + MXU_COLS) for u in range(0, n, MXU_COLS)]


def _emit_norm_inputs(h, cols, hb_ref, ssq_ref):
    hb_ref[:, cols] = h.astype(hb_ref.dtype)
    ssq_ref[...] += jnp.sum(h * h, axis=-1, keepdims=True)


def _proj_residual_kernel(a_ref, w_ref, r_ref, o_ref, hb_ref, ssq_ref):
    @pl.when(pl.program_id(1) == 0)
    def _():
        ssq_ref[...] = jnp.zeros(ssq_ref.shape, F32)

    for cols in _col_panels(o_ref.shape[1]):
        h = r_ref[:, cols] + jnp.dot(a_ref[...], w_ref[:, cols], preferred_element_type=F32)
        o_ref[:, cols] = h
        _emit_norm_inputs(h, cols, hb_ref, ssq_ref)


def _norm_out_shapes(t, n):
    return (jax.ShapeDtypeStruct((t, n), F32), jax.ShapeDtypeStruct((t, n), BF16),
            jax.ShapeDtypeStruct((t, 1), F32))


def _proj_residual(a, w, res):
    t, k = a.shape
    n = w.shape[1]
    tm = _row_tile(t)
    tn = _pick_tile(n, 512, LANES)
    return pl.pallas_call(
        _proj_residual_kernel,
        grid=(t // tm, n // tn),
        in_specs=[pl.BlockSpec((tm, k), lambda i, j: (i, 0)),
                  pl.BlockSpec((k, tn), lambda i, j: (0, j)),
                  pl.BlockSpec((tm, tn), lambda i, j: (i, j))],
        out_specs=(pl.BlockSpec((tm, tn), lambda i, j: (i, j)),
                   pl.BlockSpec((tm, tn), lambda i, j: (i, j)),
                   pl.BlockSpec((tm, 1), lambda i, j: (i, 0))),
        out_shape=_norm_out_shapes(t, n),
        compiler_params=_params("parallel", "arbitrary"),
        name="proj_residual",
    )(a, w, res)


def _mlp_up_kernel(*refs, inv_d, side):
    a_ref, w_ref, ssq_ref = refs[:3]
    side_in = refs[3:3 + side.n_in]
    o_ref = refs[3 + side.n_in]
    side_out = refs[4 + side.n_in:]
    r2 = 1.0 / (ssq_ref[...] * inv_d + EPS)
    for cols in _col_panels(o_ref.shape[1]):
        u = jnp.dot(a_ref[...], w_ref[:, cols], preferred_element_type=F32)
        o_ref[:, cols] = (jnp.square(jnp.maximum(u, 0.0)) * r2).astype(o_ref.dtype)
    side.run(side_in, side_out)


def _mlp_up(hb, w, ssq, cast_jobs):
    t, k = hb.shape
    n = w.shape[1]
    tm = _row_tile(t)
    tn = _pick_tile(n, 512, LANES)
    nj = n // tn
    side = _SideCasts(cast_jobs, (t // tm) * nj, lambda i, j: i * nj + j)
    return pl.pallas_call(
        functools.partial(_mlp_up_kernel, inv_d=1.0 / k, side=side),
        grid=(t // tm, nj),
        in_specs=[pl.BlockSpec((tm, k), lambda i, j: (i, 0)),
                  pl.BlockSpec((k, tn), lambda i, j: (0, j)),
                  pl.BlockSpec((tm, 1), lambda i, j: (i, 0))] + side.in_specs,
        out_specs=[pl.BlockSpec((tm, tn), lambda i, j: (i, j))] + side.out_specs,
        out_shape=[jax.ShapeDtypeStruct((t, n), BF16)] + side.out_shapes,
        compiler_params=_params("arbitrary", "arbitrary"),
        name="mlp_up",
    )(hb, w, ssq, *side.operands)


def _mlp_down_kernel(a_ref, w_ref, r_ref, o_ref, *norm_refs, nk):
    j = pl.program_id(1)
    k = pl.program_id(2)

    def accumulate(base_ref, last):
        for cols in _col_panels(o_ref.shape[1]):
            h = base_ref[:, cols] + jnp.dot(a_ref[...], w_ref[:, cols], preferred_element_type=F32)
            o_ref[:, cols] = h
            if last and norm_refs:
                _emit_norm_inputs(h, cols, *norm_refs)

    if norm_refs:
        @pl.when(jnp.logical_and(j == 0, k == 0))
        def _():
            norm_refs[1][...] = jnp.zeros(norm_refs[1].shape, F32)

    if nk == 1:
        accumulate(r_ref, True)
        return

    @pl.when(k == 0)
    def _():
        accumulate(r_ref, False)

    @pl.when(jnp.logical_and(k > 0, k < nk - 1))
    def _():
        accumulate(o_ref, False)

    @pl.when(k == nk - 1)
    def _():
        accumulate(o_ref, True)


def _mlp_down(a, w, res, emit_norm):
    t, kdim = a.shape
    n = w.shape[1]
    tm = _row_tile(t)
    tn = _pick_tile(n, 1024, LANES)
    tk = _pick_tile(kdim, 2048, LANES)
    nk = kdim // tk
    tile = pl.BlockSpec((tm, tn), lambda i, j, k: (i, j))
    if emit_norm:
        out_specs = (tile, tile, pl.BlockSpec((tm, 1), lambda i, j, k: (i, 0)))
        out_shape = _norm_out_shapes(t, n)
    else:
        out_specs = tile
        out_shape = jax.ShapeDtypeStruct((t, n), F32)
    return pl.pallas_call(
        functools.partial(_mlp_down_kernel, nk=nk),
        grid=(t // tm, n // tn, nk),
        in_specs=[pl.BlockSpec((tm, tk), lambda i, j, k: (i, k)),
                  pl.BlockSpec((tk, tn), lambda i, j, k: (k, j)),
                  tile],
        out_specs=out_specs,
        out_shape=out_shape,
        compiler_params=_params("parallel", "arbitrary", "arbitrary"),
        name="mlp_down",
    )(a, w, res)


def _glu_kernel(a_ref, wv_ref, wg_ref, bv_ref, bg_ref, r_ref, o_ref, hb_ref, ssq_ref):
    @pl.when(pl.program_id(1) == 0)
    def _():
        ssq_ref[...] = jnp.zeros(ssq_ref.shape, F32)

    a = a_ref[...]
    val = jnp.dot(a, wv_ref[...], preferred_element_type=F32) + bv_ref[...]
    gate = jnp.dot(a, wg_ref[...], preferred_element_type=F32) + bg_ref[...]
    h = r_ref[...] + val * jax.nn.sigmoid(gate)
    o_ref[...] = h
    _emit_norm_inputs(h, slice(None), hb_ref, ssq_ref)


def _glu_residual(a, w, bias, res):
    t, k = a.shape
    n = w.shape[1] // 2
    tm = _row_tile(t)
    tn = _pick_tile(n, 256, LANES)
    nb = n // tn
    b2 = bias.reshape(1, 2 * n).astype(F32)
    return pl.pallas_call(
        _glu_kernel,
        grid=(t // tm, nb),
        in_specs=[pl.BlockSpec((tm, k), lambda i, j: (i, 0)),
                  pl.BlockSpec((k, tn), lambda i, j: (0, j)),
                  pl.BlockSpec((k, tn), lambda i, j: (0, nb + j)),
                  pl.BlockSpec((1, tn), lambda i, j: (0, j)),
                  pl.BlockSpec((1, tn), lambda i, j: (0, nb + j)),
                  pl.BlockSpec((tm, tn), lambda i, j: (i, j))],
        out_specs=(pl.BlockSpec((tm, tn), lambda i, j: (i, j)),
                   pl.BlockSpec((tm, tn), lambda i, j: (i, j)),
                   pl.BlockSpec((tm, 1), lambda i, j: (i, 0))),
        out_shape=_norm_out_shapes(t, n),
        compiler_params=_params("parallel", "arbitrary"),
        name="glu_residual",
    )(a, w, w, b2, b2, res)


def _s5_kernel(x_ref, ssq_ref, g_ref, w0_ref, cm_ref, pr_ref, pi_ref, prc_ref, pic_ref, d_ref, o_ref,
               m_scr, wst_scr, wout_scr, s_scr, xin_scr, *, n_steps, inv_d):
    q = SSM_CHUNK
    half = w0_ref.shape[2] // 2

    @pl.when(pl.program_id(1) == 0)
    def _build_weights():
        w0 = w0_ref[0]
        w0re, w0im = w0[:, :half], w0[:, half:]
        cm = cm_ref[0]
        cm_bf = cm.astype(BF16)
        row = lax.broadcasted_iota(jnp.int32, (LANES, LANES), 0)
        col = lax.broadcasted_iota(jnp.int32, (LANES, LANES), 1)
        skip = jnp.where(row == col, jnp.broadcast_to(d_ref[...], (LANES, LANES)), 0.0)
        zero_blk = jnp.zeros((LANES, LANES), BF16)
        for tau in range(q):
            ar = pr_ref[0, tau:tau + 1, :]
            ai = pi_ref[0, tau:tau + 1, :]
            w_tau = jnp.concatenate([w0re * ar - w0im * ai, w0im * ar + w0re * ai], axis=1).astype(BF16)
            t_st = q - 1 - tau
            wst_scr[t_st * LANES:(t_st + 1) * LANES, :] = w_tau
            k_tau = jnp.dot(w_tau, cm_bf, preferred_element_type=F32)
            if tau == 0:
                k_tau = k_tau + skip
            k_bf = k_tau.astype(BF16)
            for t in range(q - tau):
                m_scr[t * LANES:(t + 1) * LANES, (t + tau) * LANES:(t + tau + 1) * LANES] = k_bf
        for t in range(q):
            for t2 in range(t):
                m_scr[t * LANES:(t + 1) * LANES, t2 * LANES:(t2 + 1) * LANES] = zero_blk
        cre, cimn = cm[:half], cm[half:]
        for t in range(q):
            arc = prc_ref[0, :, t + 1:t + 2]
            aic = pic_ref[0, :, t + 1:t + 2]
            wout_scr[:half, t * LANES:(t + 1) * LANES] = (arc * cre + aic * cimn).astype(BF16)
            wout_scr[half:, t * LANES:(t + 1) * LANES] = (arc * cimn - aic * cre).astype(BF16)

    gain = g_ref[...]
    u = jnp.concatenate(
        [(x_ref[t].astype(F32) * lax.rsqrt(ssq_ref[t] * inv_d + EPS) * gain).astype(BF16) for t in range(q)],
        axis=1)
    s_scr[...] = jnp.dot(u, wst_scr[...], preferred_element_type=F32)

    shp = (SUBLANES, half)
    ar = jnp.broadcast_to(pr_ref[0, q:q + 1, :], shp)
    ai = jnp.broadcast_to(pi_ref[0, q:q + 1, :], shp)
    anr = jnp.broadcast_to(pr_ref[0, q + 1:q + 2, :], shp)
    ani = jnp.broadcast_to(pi_ref[0, q + 1:q + 2, :], shp)
    seg = lax.broadcasted_iota(jnp.int32, shp, 0)
    zeros = jnp.zeros(shp, F32)

    def shift_down(x):
        return jnp.where(seg == 0, 0.0, pltpu.roll(x, 1, 0))

    def advance(j, cr, ci):
        sj = s_scr[pl.ds(pl.multiple_of(j * SUBLANES, SUBLANES), SUBLANES), :]
        return ar * cr - ai * ci + sj[:, :half], ar * ci + ai * cr + sj[:, half:]

    er, ei = lax.fori_loop(0, n_steps, lambda j, c: advance(j, *c), (zeros, zeros))
    tr, ti = er, ei
    for _ in range(SSM_SEGMENTS - 1):
        sr, si = shift_down(tr), shift_down(ti)
        tr, ti = er + anr * sr - ani * si, ei + anr * si + ani * sr
    cr0, ci0 = shift_down(tr), shift_down(ti)

    def scan_store(j, c):
        cr, ci = c
        xin_scr[pl.ds(pl.multiple_of(j * SUBLANES, SUBLANES), SUBLANES), :] = jnp.concatenate([cr, ci], axis=1)
        return advance(j, cr, ci)

    lax.fori_loop(0, n_steps, scan_store, (cr0, ci0))

    y = (jnp.dot(u, m_scr[...], preferred_element_type=F32)
         + jnp.dot(xin_scr[...].astype(BF16), wout_scr[...], preferred_element_type=F32))
    z = jax.nn.gelu(y)
    for t in range(q):
        o_ref[t] = z[:, t * LANES:(t + 1) * LANES].astype(o_ref.dtype)


def _complex_pow(zr, zi, n):
    rr, ri = None, None
    br, bi = zr, zi
    while n:
        if n & 1:
            rr, ri = (br, bi) if rr is None else (rr * br - ri * bi, rr * bi + ri * br)
        n >>= 1
        if n:
            br, bi = br * br - bi * bi, 2.0 * br * bi
    return rr, ri


def _s5_tables(a_re, a_im, log_dt, b_re, b_im, c_re, c_im, n_steps):
    g, p = a_re.shape
    gpb = LANES // SSM_GROUP
    nblk = g // gpb
    dt = jnp.exp(log_dt.astype(F32))[:, None]
    ar = a_re.astype(F32)
    ai = a_im.astype(F32)
    mag = jnp.exp(dt * ar)
    ang = dt * ai
    abar_re = mag * jnp.cos(ang)
    abar_im = mag * jnp.sin(ang)
    nr = abar_re - 1.0
    ni = abar_im
    den = ar * ar + ai * ai
    f_re = (nr * ar + ni * ai) / den
    f_im = (ni * ar - nr * ai) / den
    br = b_re.astype(F32)
    bi = b_im.astype(F32)
    bb_re = f_re[..., None] * br - f_im[..., None] * bi
    bb_im = f_re[..., None] * bi + f_im[..., None] * br

    pw_r, pw_i = [jnp.ones_like(abar_re)], [jnp.zeros_like(abar_re)]
    for _ in range(SSM_CHUNK):
        pw_r.append(pw_r[-1] * abar_re - pw_i[-1] * abar_im)
        pw_i.append(pw_r[-2] * abar_im + pw_i[-1] * abar_re)
    seg_r, seg_i = _complex_pow(pw_r[-1], pw_i[-1], n_steps)
    pw_r.append(seg_r)
    pw_i.append(seg_i)
    n_rows = 24
    pr = jnp.stack(pw_r, 0).reshape(len(pw_r), nblk, gpb * p).transpose(1, 0, 2)
    pi = jnp.stack(pw_i, 0).reshape(len(pw_i), nblk, gpb * p).transpose(1, 0, 2)
    pad = ((0, 0), (0, n_rows - pr.shape[1]), (0, 0))
    pr = jnp.pad(pr, pad)
    pi = jnp.pad(pi, pad)
    prc = pr.transpose(0, 2, 1)
    pic = pi.transpose(0, 2, 1)

    eye = jnp.eye(gpb, dtype=F32)

    def in_to_state(bb):
        v = bb.reshape(nblk, gpb, p, SSM_GROUP).transpose(0, 1, 3, 2)
        e = v[:, :, :, None, :] * eye[None, :, None, :, None]
        return e.reshape(nblk, LANES, gpb * p)

    def state_to_out(c):
        v = c.astype(F32).reshape(nblk, gpb, SSM_GROUP, p).transpose(0, 1, 3, 2)
        e = v[:, :, :, None, :] * eye[None, :, None, :, None]
        return e.reshape(nblk, gpb * p, LANES)

    w0 = jnp.concatenate([in_to_state(bb_re), in_to_state(bb_im)], axis=2)
    cm = jnp.concatenate([state_to_out(c_re), -state_to_out(c_im)], axis=1)
    return w0, cm, pr, pi, prc, pic


def _s5_mixer(hb, ssq, gain, bsz, lp, a_re, a_im, log_dt, b_re, b_im, c_re, c_im, d_skip):
    t, d = hb.shape
    q, nseg = SSM_CHUNK, SSM_SEGMENTS
    assert lp % (q * nseg * 2) == 0
    n_steps = lp // (q * nseg)
    rows = n_steps * nseg

    def chunk_layout(v):
        w = v.shape[-1]
        return v.reshape(bsz, nseg, n_steps, q, w).transpose(3, 0, 2, 1, 4).reshape(q, bsz * rows, w)

    x = chunk_layout(hb)
    ssq_c = chunk_layout(ssq)

    w0, cm, pr, pi, prc, pic = _s5_tables(a_re, a_im, log_dt, b_re, b_im, c_re, c_im, n_steps)
    nblk = d // LANES
    n_state = w0.shape[2]
    kern = functools.partial(_s5_kernel, n_steps=n_steps, inv_d=1.0 / d)
    z = pl.pallas_call(
        kern,
        grid=(nblk, bsz),
        in_specs=[pl.BlockSpec((q, rows, LANES), lambda k, b: (0, b, k)),
                  pl.BlockSpec((q, rows, 1), lambda k, b: (0, b, 0)),
                  pl.BlockSpec((1, LANES), lambda k, b: (0, k)),
                  pl.BlockSpec((1, LANES, n_state), lambda k, b: (k, 0, 0)),
                  pl.BlockSpec((1, n_state, LANES), lambda k, b: (k, 0, 0)),
                  pl.BlockSpec((1,) + pr.shape[1:], lambda k, b: (k, 0, 0)),
                  pl.BlockSpec((1,) + pi.shape[1:], lambda k, b: (k, 0, 0)),
                  pl.BlockSpec((1,) + prc.shape[1:], lambda k, b: (k, 0, 0)),
                  pl.BlockSpec((1,) + pic.shape[1:], lambda k, b: (k, 0, 0)),
                  pl.BlockSpec((1, LANES), lambda k, b: (0, k))],
        out_specs=pl.BlockSpec((q, rows, LANES), lambda k, b: (0, b, k)),
        out_shape=jax.ShapeDtypeStruct((q, bsz * rows, d), BF16),
        scratch_shapes=[pltpu.VMEM((q * LANES, q * LANES), BF16),
                        pltpu.VMEM((q * LANES, n_state), BF16),
                        pltpu.VMEM((n_state, q * LANES), BF16),
                        pltpu.VMEM((rows, n_state), F32),
                        pltpu.VMEM((rows, n_state), F32)],
        compiler_params=_params("arbitrary", "arbitrary"),
        name="s5_mixer",
    )(x, ssq_c, gain.reshape(1, d).astype(F32), w0, cm, pr, pi, prc, pic, d_skip.reshape(1, d).astype(F32))
    return z.reshape(q, bsz, n_steps, nseg, d).transpose(1, 3, 2, 0, 4).reshape(t, d)


def _rope_tables(length):
    inv = ROPE_THETA ** (-jnp.arange(0, HEAD_DIM, 2, dtype=F32) / HEAD_DIM)
    ang = jnp.arange(length, dtype=F32)[:, None] * inv[None, :]
    cos, sin = jnp.cos(ang), jnp.sin(ang)
    return jnp.concatenate([cos, cos], axis=1), jnp.concatenate([-sin, sin], axis=1)


def _lambda_init(layer_idx):
    return 0.8 - 0.6 * math.exp(-0.3 * layer_idx)


def kernel(x, meta_tokens, norm_mix_g, norm_mlp_g, da_w_qkv, da_q_norm_g, da_k_norm_g, da_lambda, da_subln_g, da_w_o, ssm_a_re, ssm_a_im, ssm_log_dt, ssm_b_re, ssm_b_im, ssm_c_re, ssm_c_im, ssm_d, ssm_w_glu, ssm_b_glu, mlp_w_up, mlp_w_down):
    bsz, seq, d = x.shape
    depth = norm_mix_g.shape[0]
    n_mixers = 2
    length = N_META + seq
    lp = ((length + SEQ_ALIGN - 1) // SEQ_ALIGN) * SEQ_ALIGN
    meta = jnp.broadcast_to(meta_tokens.astype(x.dtype)[None], (bsz, N_META, d))
    pad = jnp.zeros((bsz, lp - length, d), x.dtype)
    h = jnp.concatenate([meta, x, pad], axis=1).reshape(bsz * lp, d)
    cos, sin = (jnp.tile(tab, (bsz, 1)) for tab in _rope_tables(lp))
    def up_job(i):
        return mlp_w_up[i], norm_mlp_g[i]

    def bf16_weight(name, src, gain=None):
        if name not in wb:
            wb[name] = (src if gain is None else src * gain.astype(F32)[:, None]).astype(BF16)
        return wb[name]

    wb = {}
    hb = ssq = None
    for i in range(depth):
        j = i // n_mixers
        nxt_s5 = i + 1 < depth and (i + 1) % n_mixers == 1
        if i % n_mixers == 0:
            hn = _rmsnorm(h, norm_mix_g[i])
            qkv, wb["o", j], wb["up", i] = _qkv_proj(
                hn, bf16_weight(("qkv", j), da_w_qkv[j]), cos, sin, da_q_norm_g[j], da_k_norm_g[j],
                [(da_w_o[j], None), up_job(i)])
            att = _diff_attention(qkv, da_lambda[j], da_subln_g[j], bsz, lp, d, _lambda_init(i))
            h, hb, ssq = _proj_residual(att, wb["o", j], h)
        else:
            z = _s5_mixer(hb, ssq, norm_mix_g[i], bsz, lp, ssm_a_re[j], ssm_a_im[j], ssm_log_dt[j],
                          ssm_b_re[j], ssm_b_im[j], ssm_c_re[j], ssm_c_im[j], ssm_d[j])
            h, hb, ssq = _glu_residual(z, bf16_weight(("glu", j), ssm_w_glu[j]), ssm_b_glu[j], h)
        jobs = [(("down", i), mlp_w_down[i], None)]
        if nxt_s5:
            jobs += [(("glu", (i + 1) // n_mixers), ssm_w_glu[(i + 1) // n_mixers], None),
                     (("up", i + 1),) + up_job(i + 1)]
        f, *casts = _mlp_up(hb, bf16_weight(("up", i), *up_job(i)), ssq, [job[1:] for job in jobs])
        wb.update({job[0]: c for job, c in zip(jobs, casts)})
        if nxt_s5:
            h, hb, ssq = _mlp_down(f, wb["down", i], h, True)
        else:
            h = _mlp_down(f, wb["down", i], h, False)
    return h.reshape(bsz, lp, d)[:, N_META:length]
```

```python
import functools
import math

import jax
import jax.numpy as jnp
from jax import lax
from jax.experimental import pallas as pl
from jax.experimental.pallas import tpu as pltpu

N_META = 16
SEQ_ALIGN = 256
HEAD_DIM = 128
LOG2E = 1.4426950408889634
ATTN_ROW_BLOCK = 64
ROPE_THETA = 10000.0
SSM_GROUP = 16
SSM_CHUNK = 16
SSM_SEGMENTS = 8
MXU_COLS = 256
LANES = 128
SUBLANES = 8
EPS = 1e-6
VMEM_LIMIT_BYTES = 56 * 1024 * 1024

F32 = jnp.float32
BF16 = jnp.bfloat16


def _pick_tile(n, target, mult):
    best = None
    for t in range(mult, min(n, target) + 1, mult):
        if n % t == 0:
            best = t
    assert best is not None, (n, target, mult)
    return best


def _row_tile(t):
    return _pick_tile(t, 1536, SEQ_ALIGN)


def _params(*sem):
    return pltpu.CompilerParams(dimension_semantics=sem, vmem_limit_bytes=VMEM_LIMIT_BYTES)


class _SideCasts:
    def __init__(self, jobs, n_steps, step_of):
        self.operands, self.in_specs, self.out_specs, self.out_shapes, self.has_gain = [], [], [], [], []
        for src, layer, gain in jobs:
            _, r, c = src.shape
            cr = next(x for x in (16 << p for p in range(24)) if r % x == 0 and r // x <= n_steps)

            def idx(*g, last=r // cr - 1):
                return (jnp.minimum(step_of(*g), last), 0)

            def src_idx(*g, layer=layer, last=r // cr - 1):
                return (layer, jnp.minimum(step_of(*g), last), 0)

            self.operands.append(src)
            self.in_specs.append(pl.BlockSpec((None, cr, c), src_idx))
            if gain is not None:
                self.operands.append(gain.reshape(r, 1).astype(F32))
                self.in_specs.append(pl.BlockSpec((cr, 1), idx))
            self.out_specs.append(pl.BlockSpec((cr, c), idx))
            self.out_shapes.append(jax.ShapeDtypeStruct((r, c), BF16))
            self.has_gain.append(gain is not None)
        self.n_in = len(self.operands)
        self.n_out = len(self.out_shapes)

    def run(self, in_refs, out_refs):
        refs = iter(in_refs)
        for has_gain, o_ref in zip(self.has_gain, out_refs):
            v = next(refs)[...]
            if has_gain:
                v = v * next(refs)[...]
            o_ref[...] = v.astype(o_ref.dtype)


def _rmsnorm_kernel(x_ref, g_ref, o_ref):
    x = x_ref[...]
    ms = jnp.mean(x * x, axis=-1, keepdims=True)
    o_ref[...] = (x * lax.rsqrt(ms + EPS) * g_ref[...]).astype(o_ref.dtype)


def _rmsnorm(x, g):
    t, d = x.shape
    tr = _pick_tile(t, 256, 16)
    return pl.pallas_call(
        _rmsnorm_kernel,
        grid=(t // tr,),
        in_specs=[pl.BlockSpec((tr, d), lambda i: (i, 0)),
                  pl.BlockSpec((1, d), lambda i: (0, 0))],
        out_specs=pl.BlockSpec((tr, d), lambda i: (i, 0)),
        out_shape=jax.ShapeDtypeStruct((t, d), BF16),
        compiler_params=_params("parallel"),
        name="rmsnorm",
    )(x, g.reshape(1, d).astype(F32))


def _qkv_kernel(*refs, n_q_blocks, q_scale, side):
    a_ref, w_ref, cos_ref, sin_ref, qg_ref, kg_ref = refs[:6]
    side_in = refs[6:6 + side.n_in]
    o_ref = refs[6 + side.n_in]
    side_out = refs[7 + side.n_in:]
    acc = jnp.dot(a_ref[...], w_ref[...], preferred_element_type=F32)
    j = pl.program_id(1)
    tn = acc.shape[1]

    def norm_rope(g, scale):
        cos = cos_ref[...]
        sin = sin_ref[...]
        for u in range(tn // HEAD_DIM):
            x = acc[:, u * HEAD_DIM:(u + 1) * HEAD_DIM]
            ms = jnp.mean(x * x, axis=-1, keepdims=True)
            y = x * lax.rsqrt(ms + EPS) * g
            y = y * cos + pltpu.roll(y, HEAD_DIM // 2, 1) * sin
            if scale != 1.0:
                y = y * scale
            o_ref[:, u * HEAD_DIM:(u + 1) * HEAD_DIM] = y.astype(o_ref.dtype)

    @pl.when(j < n_q_blocks)
    def _():
        norm_rope(qg_ref[...], q_scale)
        side.run(side_in, side_out)

    @pl.when(jnp.logical_and(j >= n_q_blocks, j < 2 * n_q_blocks))
    def _():
        norm_rope(kg_ref[...], 1.0)
        side.run(side_in, side_out)

    @pl.when(j >= 2 * n_q_blocks)
    def _():
        o_ref[...] = acc.astype(o_ref.dtype)
        side.run(side_in, side_out)


def _qkv_proj(a, w, cos, sin, q_g, k_g, cast_jobs):
    t, d = a.shape
    n = w.shape[1]
    tm = _row_tile(t)
    tn = _pick_tile(d, 512, HEAD_DIM)
    nj = n // tn
    side = _SideCasts(cast_jobs, (t // tm) * nj, lambda i, j: i * nj + j)
    kern = functools.partial(_qkv_kernel, n_q_blocks=d // tn, q_scale=HEAD_DIM ** -0.5 * LOG2E, side=side)
    return pl.pallas_call(
        kern,
        grid=(t // tm, nj),
        in_specs=[pl.BlockSpec((tm, d), lambda i, j: (i, 0)),
                  pl.BlockSpec((d, tn), lambda i, j: (0, j)),
                  pl.BlockSpec((tm, HEAD_DIM), lambda i, j: (i, 0)),
                  pl.BlockSpec((tm, HEAD_DIM), lambda i, j: (i, 0)),
                  pl.BlockSpec((1, HEAD_DIM), lambda i, j: (0, 0)),
                  pl.BlockSpec((1, HEAD_DIM), lambda i, j: (0, 0))] + side.in_specs,
        out_specs=[pl.BlockSpec((tm, tn), lambda i, j: (i, j))] + side.out_specs,
        out_shape=[jax.ShapeDtypeStruct((t, n), BF16)] + side.out_shapes,
        compiler_params=_params("arbitrary", "arbitrary"),
        name="qkv_proj",
    )(a, w, cos, sin, q_g.reshape(1, HEAD_DIM).astype(F32), k_g.reshape(1, HEAD_DIM).astype(F32),
      *side.operands)


def _attn_kernel(q_ref, k_ref, v_ref, lam_ref, g_ref, o_ref,
                 m1_ref, l1_ref, a1_ref, m2_ref, l2_ref, a2_ref, s_even_ref, s_odd_ref, p_ref, alpha_ref,
                 *, blk, lam_init):
    s_bufs = (s_even_ref, s_odd_ref)
    nq = q_ref.shape[0] // blk
    stats = ((m1_ref, l1_ref, a1_ref), (m2_ref, l2_ref, a2_ref))
    lv = lam_ref[...]
    lam = (jnp.exp(jnp.sum(lv[0:1] * lv[1:2], axis=-1, keepdims=True))
           - jnp.exp(jnp.sum(lv[2:3] * lv[3:4], axis=-1, keepdims=True)) + lam_init)

    def rows(i):
        return pl.ds(pl.multiple_of(i * blk, blk), blk)

    def init_stats():
        for m_ref, l_ref, a_ref in stats:
            m_ref[...] = jnp.full(m_ref.shape, -jnp.inf, F32)
            l_ref[...] = jnp.zeros(l_ref.shape, F32)
            a_ref[...] = jnp.zeros(a_ref.shape, F32)

    def scores(qi, j, s_ref):
        q = q_ref[rows(qi), :]
        kk = k_ref[rows(j), :]
        for c in range(2):
            s_ref[c] = lax.dot_general(
                q[:, c * HEAD_DIM:(c + 1) * HEAD_DIM], kk[:, c * HEAD_DIM:(c + 1) * HEAD_DIM],
                (((1,), (1,)), ((), ())), preferred_element_type=F32)

    def softmax_pv(j, s_ref, masked):
        vv = v_ref[rows(j), :]
        for c, (m_ref, l_ref, a_ref) in enumerate(stats):
            for r in range(0, blk, ATTN_ROW_BLOCK):
                rb = slice(r, r + ATTN_ROW_BLOCK)
                s = s_ref[c, rb, :]
                if masked:
                    row = r + lax.broadcasted_iota(jnp.int32, s.shape, 0)
                    col = lax.broadcasted_iota(jnp.int32, s.shape, 1)
                    s = jnp.where(col <= row, s, -jnp.inf)
                m_old = m_ref[rb, :]
                m_new = jnp.maximum(m_old, jnp.max(s, axis=-1, keepdims=True))
                alpha = jnp.exp2(m_old - m_new)
                p = jnp.exp2(s - m_new)
                part = p[:, :LANES]
                for u in range(1, blk // LANES):
                    part = part + p[:, u * LANES:(u + 1) * LANES]
                l_ref[rb, :] = alpha * l_ref[rb, :] + part
                p_ref[c, rb, :] = p.astype(BF16)
                alpha_ref[c, rb, :] = alpha
                m_ref[rb, :] = m_new
            a_ref[...] = alpha_ref[c] * a_ref[...] + jnp.dot(p_ref[c], vv, preferred_element_type=F32)

    def finalize(qi):
        l1 = jnp.sum(l1_ref[...], axis=-1, keepdims=True)
        l2 = jnp.sum(l2_ref[...], axis=-1, keepdims=True)
        o = a1_ref[...] / l1 - lam * (a2_ref[...] / l2)
        ms = jnp.mean(o * o, axis=-1, keepdims=True)
        y = o * lax.rsqrt(ms + EPS) * g_ref[...] * (1.0 - lam_init)
        o_ref[rows(qi), :] = y.astype(o_ref.dtype)

    def item(n, next_qi, next_j, j, masked):
        for parity in range(2):
            @pl.when(n % 2 == parity)
            def _():
                scores(next_qi, next_j, s_bufs[1 - parity])
                softmax_pv(j, s_bufs[parity], masked)

    init_stats()
    scores(0, 0, s_bufs[0])

    def q_block(qi, carry):
        base = (qi * (qi + 1)) // 2

        def full_chunk(j, c):
            item(base + j, qi, j + 1, j, False)
            return c

        lax.fori_loop(0, qi, full_chunk, 0)
        item(base + qi, jnp.minimum(qi + 1, nq - 1), 0, qi, True)
        finalize(qi)
        init_stats()
        return carry

    lax.fori_loop(0, nq, q_block, 0)


def _diff_attention(qkv, lam_vecs, subln_g, bsz, lp, d, lam_init):
    t = qkv.shape[0]
    n_heads = d // (2 * HEAD_DIM)
    hw = 2 * HEAD_DIM
    blk = _pick_tile(lp, 768, SEQ_ALIGN)
    kern = functools.partial(_attn_kernel, blk=blk, lam_init=lam_init)
    return pl.pallas_call(
        kern,
        grid=(bsz, n_heads),
        in_specs=[pl.BlockSpec((lp, hw), lambda b, h: (b, h)),
                  pl.BlockSpec((lp, hw), lambda b, h: (b, n_heads + h)),
                  pl.BlockSpec((lp, hw), lambda b, h: (b, 2 * n_heads + h)),
                  pl.BlockSpec((4, HEAD_DIM), lambda b, h: (0, 0)),
                  pl.BlockSpec((1, hw), lambda b, h: (0, 0))],
        out_specs=pl.BlockSpec((lp, hw), lambda b, h: (b, h)),
        out_shape=jax.ShapeDtypeStruct((t, d), BF16),
        scratch_shapes=[pltpu.VMEM((blk, 1), F32), pltpu.VMEM((blk, LANES), F32), pltpu.VMEM((blk, hw), F32),
                        pltpu.VMEM((blk, 1), F32), pltpu.VMEM((blk, LANES), F32), pltpu.VMEM((blk, hw), F32),
                        pltpu.VMEM((2, blk, blk), F32), pltpu.VMEM((2, blk, blk), F32),
                        pltpu.VMEM((2, blk, blk), BF16), pltpu.VMEM((2, blk, 1), F32)],
        compiler_params=_params("parallel", "parallel"),
        name="diff_attention",
    )(qkv, qkv, qkv, lam_vecs.astype(F32), subln_g.reshape(1, hw).astype(F32))


def _col_panels(n):
    return [slice(u, u + MXU_COLS) for u in range(0, n, MXU_COLS)]


def _emit_norm_inputs(h, cols, hb_ref, ssq_ref):
    hb_ref[:, cols] = h.astype(hb_ref.dtype)
    ssq_ref[...] += jnp.sum(h * h, axis=-1, keepdims=True)


def _proj_residual_kernel(a_ref, w_ref, r_ref, o_ref, hb_ref, ssq_ref):
    @pl.when(pl.program_id(1) == 0)
    def _():
        ssq_ref[...] = jnp.zeros(ssq_ref.shape, F32)

    for cols in _col_panels(o_ref.shape[1]):
        h = r_ref[:, cols] + jnp.dot(a_ref[...], w_ref[:, cols], preferred_element_type=F32)
        o_ref[:, cols] = h
        _emit_norm_inputs(h, cols, hb_ref, ssq_ref)


def _norm_out_shapes(t, n):
    return (jax.ShapeDtypeStruct((t, n), F32), jax.ShapeDtypeStruct((t, n), BF16),
            jax.ShapeDtypeStruct((t, 1), F32))


def _proj_residual(a, w, res):
    t, k = a.shape
    n = w.shape[1]
    tm = _row_tile(t)
    tn = _pick_tile(n, 512, LANES)
    return pl.pallas_call(
        _proj_residual_kernel,
        grid=(t // tm, n // tn),
        in_specs=[pl.BlockSpec((tm, k), lambda i, j: (i, 0)),
                  pl.BlockSpec((k, tn), lambda i, j: (0, j)),
                  pl.BlockSpec((tm, tn), lambda i, j: (i, j))],
        out_specs=(pl.BlockSpec((tm, tn), lambda i, j: (i, j)),
                   pl.BlockSpec((tm, tn), lambda i, j: (i, j)),
                   pl.BlockSpec((tm, 1), lambda i, j: (i, 0))),
        out_shape=_norm_out_shapes(t, n),
        compiler_params=_params("parallel", "arbitrary"),
        name="proj_residual",
    )(a, w, res)


def _mlp_up_kernel(*refs, inv_d, side):
    a_ref, w_ref, ssq_ref = refs[:3]
    side_in = refs[3:3 + side.n_in]
    o_ref = refs[3 + side.n_in]
    side_out = refs[4 + side.n_in:]
    r2 = 1.0 / (ssq_ref[...] * inv_d + EPS)
    for cols in _col_panels(o_ref.shape[1]):
        u = jnp.dot(a_ref[...], w_ref[:, cols], preferred_element_type=F32)
        o_ref[:, cols] = (jnp.square(jnp.maximum(u, 0.0)) * r2).astype(o_ref.dtype)
    side.run(side_in, side_out)


def _mlp_up(hb, w, ssq, cast_jobs):
    t, k = hb.shape
    n = w.shape[1]
    tm = _row_tile(t)
    tn = _pick_tile(n, 512, LANES)
    nj = n // tn
    side = _SideCasts(cast_jobs, (t // tm) * nj, lambda i, j: i * nj + j)
    return pl.pallas_call(
        functools.partial(_mlp_up_kernel, inv_d=1.0 / k, side=side),
        grid=(t // tm, nj),
        in_specs=[pl.BlockSpec((tm, k), lambda i, j: (i, 0)),
                  pl.BlockSpec((k, tn), lambda i, j: (0, j)),
                  pl.BlockSpec((tm, 1), lambda i, j: (i, 0))] + side.in_specs,
        out_specs=[pl.BlockSpec((tm, tn), lambda i, j: (i, j))] + side.out_specs,
        out_shape=[jax.ShapeDtypeStruct((t, n), BF16)] + side.out_shapes,
        compiler_params=_params("arbitrary", "arbitrary"),
        name="mlp_up",
    )(hb, w, ssq, *side.operands)


def _mlp_down_kernel(a_ref, w_ref, r_ref, o_ref, *norm_refs, nk):
    j = pl.program_id(1)
    k = pl.program_id(2)

    def accumulate(base_ref, last):
        for cols in _col_panels(o_ref.shape[1]):
            h = base_ref[:, cols] + jnp.dot(a_ref[...], w_ref[:, cols], preferred_element_type=F32)
            o_ref[:, cols] = h
            if last and norm_refs:
                _emit_norm_inputs(h, cols, *norm_refs)

    if norm_refs:
        @pl.when(jnp.logical_and(j == 0, k == 0))
        def _():
            norm_refs[1][...] = jnp.zeros(norm_refs[1].shape, F32)

    if nk == 1:
        accumulate(r_ref, True)
        return

    @pl.when(k == 0)
    def _():
        accumulate(r_ref, False)

    @pl.when(jnp.logical_and(k > 0, k < nk - 1))
    def _():
        accumulate(o_ref, False)

    @pl.when(k == nk - 1)
    def _():
        accumulate(o_ref, True)


def _mlp_down(a, w, res, emit_norm):
    t, kdim = a.shape
    n = w.shape[1]
    tm = _row_tile(t)
    tn = _pick_tile(n, 1024, LANES)
    tk = _pick_tile(kdim, 2048, LANES)
    nk = kdim // tk
    tile = pl.BlockSpec((tm, tn), lambda i, j, k: (i, j))
    if emit_norm:
        out_specs = (tile, tile, pl.BlockSpec((tm, 1), lambda i, j, k: (i, 0)))
        out_shape = _norm_out_shapes(t, n)
    else:
        out_specs = tile
        out_shape = jax.ShapeDtypeStruct((t, n), F32)
    return pl.pallas_call(
        functools.partial(_mlp_down_kernel, nk=nk),
        grid=(t // tm, n // tn, nk),
        in_specs=[pl.BlockSpec((tm, tk), lambda i, j, k: (i, k)),
                  pl.BlockSpec((tk, tn), lambda i, j, k: (k, j)),
                  tile],
        out_specs=out_specs,
        out_shape=out_shape,
        compiler_params=_params("parallel", "arbitrary", "arbitrary"),
        name="mlp_down",
    )(a, w, res)


def _glu_kernel(a_ref, wv_ref, wg_ref, bv_ref, bg_ref, r_ref, o_ref, hb_ref, ssq_ref):
    @pl.when(pl.program_id(1) == 0)
    def _():
        ssq_ref[...] = jnp.zeros(ssq_ref.shape, F32)

    a = a_ref[...]
    val = jnp.dot(a, wv_ref[...], preferred_element_type=F32) + bv_ref[...]
    gate = jnp.dot(a, wg_ref[...], preferred_element_type=F32) + bg_ref[...]
    h = r_ref[...] + val * jax.nn.sigmoid(gate)
    o_ref[...] = h
    _emit_norm_inputs(h, slice(None), hb_ref, ssq_ref)


def _glu_residual(a, w, bias, res):
    t, k = a.shape
    n = w.shape[1] // 2
    tm = _row_tile(t)
    tn = _pick_tile(n, 256, LANES)
    nb = n // tn
    b2 = bias.reshape(1, 2 * n).astype(F32)
    return pl.pallas_call(
        _glu_kernel,
        grid=(t // tm, nb),
        in_specs=[pl.BlockSpec((tm, k), lambda i, j: (i, 0)),
                  pl.BlockSpec((k, tn), lambda i, j: (0, j)),
                  pl.BlockSpec((k, tn), lambda i, j: (0, nb + j)),
                  pl.BlockSpec((1, tn), lambda i, j: (0, j)),
                  pl.BlockSpec((1, tn), lambda i, j: (0, nb + j)),
                  pl.BlockSpec((tm, tn), lambda i, j: (i, j))],
        out_specs=(pl.BlockSpec((tm, tn), lambda i, j: (i, j)),
                   pl.BlockSpec((tm, tn), lambda i, j: (i, j)),
                   pl.BlockSpec((tm, 1), lambda i, j: (i, 0))),
        out_shape=_norm_out_shapes(t, n),
        compiler_params=_params("parallel", "arbitrary"),
        name="glu_residual",
    )(a, w, w, b2, b2, res)


def _s5_kernel(x_ref, ssq_ref, g_ref, w0_ref, cm_ref, pr_ref, pi_ref, prc_ref, pic_ref, d_ref, o_ref,
               m_scr, wst_scr, wout_scr, s_scr, xin_scr, *, n_steps, inv_d):
    q = SSM_CHUNK
    half = w0_ref.shape[2] // 2

    @pl.when(pl.program_id(1) == 0)
    def _build_weights():
        w0 = w0_ref[0]
        w0re, w0im = w0[:, :half], w0[:, half:]
        cm = cm_ref[0]
        cm_bf = cm.astype(BF16)
        row = lax.broadcasted_iota(jnp.int32, (LANES, LANES), 0)
        col = lax.broadcasted_iota(jnp.int32, (LANES, LANES), 1)
        skip = jnp.where(row == col, jnp.broadcast_to(d_ref[...], (LANES, LANES)), 0.0)
        zero_blk = jnp.zeros((LANES, LANES), BF16)
        for tau in range(q):
            ar = pr_ref[0, tau:tau + 1, :]
            ai = pi_ref[0, tau:tau + 1, :]
            w_tau = jnp.concatenate([w0re * ar - w0im * ai, w0im * ar + w0re * ai], axis=1).astype(BF16)
            t_st = q - 1 - tau
            wst_scr[t_st * LANES:(t_st + 1) * LANES, :] = w_tau
            k_tau = jnp.dot(w_tau, cm_bf, preferred_element_type=F32)
            if tau == 0:
                k_tau = k_tau + skip
            k_bf = k_tau.astype(BF16)
            for t in range(q - tau):
                m_scr[t * LANES:(t + 1) * LANES, (t + tau) * LANES:(t + tau + 1) * LANES] = k_bf
        for t in range(q):
            for t2 in range(t):
                m_scr[t * LANES:(t + 1) * LANES, t2 * LANES:(t2 + 1) * LANES] = zero_blk
        cre, cimn = cm[:half], cm[half:]
        for t in range(q):
            arc = prc_ref[0, :, t + 1:t + 2]
            aic = pic_ref[0, :, t + 1:t + 2]
            wout_scr[:half, t * LANES:(t + 1) * LANES] = (arc * cre + aic * cimn).astype(BF16)
            wout_scr[half:, t * LANES:(t + 1) * LANES] = (arc * cimn - aic * cre).astype(BF16)

    gain = g_ref[...]
    u = jnp.concatenate(
        [(x_ref[t].astype(F32) * lax.rsqrt(ssq_ref[t] * inv_d + EPS) * gain).astype(BF16) for t in range(q)],
        axis=1)
    s_scr[...] = jnp.dot(u, wst_scr[...], preferred_element_type=F32)

    shp = (SUBLANES, half)
    ar = jnp.broadcast_to(pr_ref[0, q:q + 1, :], shp)
    ai = jnp.broadcast_to(pi_ref[0, q:q + 1, :], shp)
    anr = jnp.broadcast_to(pr_ref[0, q + 1:q + 2, :], shp)
    ani = jnp.broadcast_to(pi_ref[0, q + 1:q + 2, :], shp)
    seg = lax.broadcasted_iota(jnp.int32, shp, 0)
    zeros = jnp.zeros(shp, F32)

    def shift_down(x):
        return jnp.where(seg == 0, 0.0, pltpu.roll(x, 1, 0))

    def advance(j, cr, ci):
        sj = s_scr[pl.ds(pl.multiple_of(j * SUBLANES, SUBLANES), SUBLANES), :]
        return ar * cr - ai * ci + sj[:, :half], ar * ci + ai * cr + sj[:, half:]

    er, ei = lax.fori_loop(0, n_steps, lambda j, c: advance(j, *c), (zeros, zeros))
    tr, ti = er, ei
    for _ in range(SSM_SEGMENTS - 1):
        sr, si = shift_down(tr), shift_down(ti)
        tr, ti = er + anr * sr - ani * si, ei + anr * si + ani * sr
    cr0, ci0 = shift_down(tr), shift_down(ti)

    def scan_store(j, c):
        cr, ci = c
        xin_scr[pl.ds(pl.multiple_of(j * SUBLANES, SUBLANES), SUBLANES), :] = jnp.concatenate([cr, ci], axis=1)
        return advance(j, cr, ci)

    lax.fori_loop(0, n_steps, scan_store, (cr0, ci0))

    xin = xin_scr[...].astype(BF16)
    steps_per_panel = 4
    for t0 in range(0, q, steps_per_panel):
        k_hi = (t0 + steps_per_panel) * LANES
        cols = slice(t0 * LANES, k_hi)
        y = (jnp.dot(u[:, :k_hi], m_scr[:k_hi, cols], preferred_element_type=F32)
             + jnp.dot(xin, wout_scr[:, cols], preferred_element_type=F32))
        z = jax.nn.gelu(y)
        for t in range(steps_per_panel):
            o_ref[t0 + t] = z[:, t * LANES:(t + 1) * LANES].astype(o_ref.dtype)


def _complex_pow(zr, zi, n):
    rr, ri = None, None
    br, bi = zr, zi
    while n:
        if n & 1:
            rr, ri = (br, bi) if rr is None else (rr * br - ri * bi, rr * bi + ri * br)
        n >>= 1
        if n:
            br, bi = br * br - bi * bi, 2.0 * br * bi
    return rr, ri


def _s5_tables(a_re, a_im, log_dt, b_re, b_im, c_re, c_im, n_steps):
    g, p = a_re.shape
    gpb = LANES // SSM_GROUP
    nblk = g // gpb
    dt = jnp.exp(log_dt.astype(F32))[:, None]
    ar = a_re.astype(F32)
    ai = a_im.astype(F32)
    mag = jnp.exp(dt * ar)
    ang = dt * ai
    abar_re = mag * jnp.cos(ang)
    abar_im = mag * jnp.sin(ang)
    nr = abar_re - 1.0
    ni = abar_im
    den = ar * ar + ai * ai
    f_re = (nr * ar + ni * ai) / den
    f_im = (ni * ar - nr * ai) / den
    br = b_re.astype(F32)
    bi = b_im.astype(F32)
    bb_re = f_re[..., None] * br - f_im[..., None] * bi
    bb_im = f_re[..., None] * bi + f_im[..., None] * br

    pw_r, pw_i = [jnp.ones_like(abar_re)], [jnp.zeros_like(abar_re)]
    for _ in range(SSM_CHUNK):
        pw_r.append(pw_r[-1] * abar_re - pw_i[-1] * abar_im)
        pw_i.append(pw_r[-2] * abar_im + pw_i[-1] * abar_re)
    seg_r, seg_i = _complex_pow(pw_r[-1], pw_i[-1], n_steps)
    pw_r.append(seg_r)
    pw_i.append(seg_i)
    n_rows = 24
    pr = jnp.stack(pw_r, 0).reshape(len(pw_r), nblk, gpb * p).transpose(1, 0, 2)
    pi = jnp.stack(pw_i, 0).reshape(len(pw_i), nblk, gpb * p).transpose(1, 0, 2)
    pad = ((0, 0), (0, n_rows - pr.shape[1]), (0, 0))
    pr = jnp.pad(pr, pad)
    pi = jnp.pad(pi, pad)
    prc = pr.transpose(0, 2, 1)
    pic = pi.transpose(0, 2, 1)

    eye = jnp.eye(gpb, dtype=F32)

    def in_to_state(bb):
        v = bb.reshape(nblk, gpb, p, SSM_GROUP).transpose(0, 1, 3, 2)
        e = v[:, :, :, None, :] * eye[None, :, None, :, None]
        return e.reshape(nblk, LANES, gpb * p)

    def state_to_out(c):
        v = c.astype(F32).reshape(nblk, gpb, SSM_GROUP, p).transpose(0, 1, 3, 2)
        e = v[:, :, :, None, :] * eye[None, :, None, :, None]
        return e.reshape(nblk, gpb * p, LANES)

    w0 = jnp.concatenate([in_to_state(bb_re), in_to_state(bb_im)], axis=2)
    cm = jnp.concatenate([state_to_out(c_re), -state_to_out(c_im)], axis=1)
    return w0, cm, pr, pi, prc, pic


def _s5_mixer(hb, ssq, gain, bsz, lp, a_re, a_im, log_dt, b_re, b_im, c_re, c_im, d_skip):
    t, d = hb.shape
    q, nseg = SSM_CHUNK, SSM_SEGMENTS
    assert lp % (q * nseg * 2) == 0
    n_steps = lp // (q * nseg)
    rows = n_steps * nseg

    def chunk_layout(v):
        w = v.shape[-1]
        return v.reshape(bsz, nseg, n_steps, q, w).transpose(3, 0, 2, 1, 4).reshape(q, bsz * rows, w)

    x = chunk_layout(hb)
    ssq_c = chunk_layout(ssq)

    w0, cm, pr, pi, prc, pic = _s5_tables(a_re, a_im, log_dt, b_re, b_im, c_re, c_im, n_steps)
    nblk = d // LANES
    n_state = w0.shape[2]
    kern = functools.partial(_s5_kernel, n_steps=n_steps, inv_d=1.0 / d)
    z = pl.pallas_call(
        kern,
        grid=(nblk, bsz),
        in_specs=[pl.BlockSpec((q, rows, LANES), lambda k, b: (0, b, k)),
                  pl.BlockSpec((q, rows, 1), lambda k, b: (0, b, 0)),
                  pl.BlockSpec((1, LANES), lambda k, b: (0, k)),
                  pl.BlockSpec((1, LANES, n_state), lambda k, b: (k, 0, 0)),
                  pl.BlockSpec((1, n_state, LANES), lambda k, b: (k, 0, 0)),
                  pl.BlockSpec((1,) + pr.shape[1:], lambda k, b: (k, 0, 0)),
                  pl.BlockSpec((1,) + pi.shape[1:], lambda k, b: (k, 0, 0)),
                  pl.BlockSpec((1,) + prc.shape[1:], lambda k, b: (k, 0, 0)),
                  pl.BlockSpec((1,) + pic.shape[1:], lambda k, b: (k, 0, 0)),
                  pl.BlockSpec((1, LANES), lambda k, b: (0, k))],
        out_specs=pl.BlockSpec((q, rows, LANES), lambda k, b: (0, b, k)),
        out_shape=jax.ShapeDtypeStruct((q, bsz * rows, d), BF16),
        scratch_shapes=[pltpu.VMEM((q * LANES, q * LANES), BF16),
                        pltpu.VMEM((q * LANES, n_state), BF16),
                        pltpu.VMEM((n_state, q * LANES), BF16),
                        pltpu.VMEM((rows, n_state), F32),
                        pltpu.VMEM((rows, n_state), F32)],
        compiler_params=_params("arbitrary", "arbitrary"),
        name="s5_mixer",
    )(x, ssq_c, gain.reshape(1, d).astype(F32), w0, cm, pr, pi, prc, pic, d_skip.reshape(1, d).astype(F32))
    return z.reshape(q, bsz, n_steps, nseg, d).transpose(1, 3, 2, 0, 4).reshape(t, d)


def _rope_tables(length):
    inv = ROPE_THETA ** (-jnp.arange(0, HEAD_DIM, 2, dtype=F32) / HEAD_DIM)
    ang = jnp.arange(length, dtype=F32)[:, None] * inv[None, :]
    cos, sin = jnp.cos(ang), jnp.sin(ang)
    return jnp.concatenate([cos, cos], axis=1), jnp.concatenate([-sin, sin], axis=1)


def _lambda_init(layer_idx):
    return 0.8 - 0.6 * math.exp(-0.3 * layer_idx)


def kernel(x, meta_tokens, norm_mix_g, norm_mlp_g, da_w_qkv, da_q_norm_g, da_k_norm_g, da_lambda, da_subln_g, da_w_o, ssm_a_re, ssm_a_im, ssm_log_dt, ssm_b_re, ssm_b_im, ssm_c_re, ssm_c_im, ssm_d, ssm_w_glu, ssm_b_glu, mlp_w_up, mlp_w_down):
    bsz, seq, d = x.shape
    depth = norm_mix_g.shape[0]
    n_mixers = 2
    length = N_META + seq
    lp = ((length + SEQ_ALIGN - 1) // SEQ_ALIGN) * SEQ_ALIGN
    meta = jnp.broadcast_to(meta_tokens.astype(x.dtype)[None], (bsz, N_META, d))
    pad = jnp.zeros((bsz, lp - length, d), x.dtype)
    h = jnp.concatenate([meta, x, pad], axis=1).reshape(bsz * lp, d)
    cos, sin = (jnp.tile(tab, (bsz, 1)) for tab in _rope_tables(lp))
    def up_job(i):
        return mlp_w_up, i, norm_mlp_g[i]

    def bf16_weight(name, stack, layer, gain=None):
        if name not in wb:
            src = stack[layer]
            wb[name] = (src if gain is None else src * gain.astype(F32)[:, None]).astype(BF16)
        return wb[name]

    wb = {}
    hb = ssq = None
    for i in range(depth):
        j = i // n_mixers
        nxt_s5 = i + 1 < depth and (i + 1) % n_mixers == 1
        if i % n_mixers == 0:
            hn = _rmsnorm(h, norm_mix_g[i])
            qkv, wb["o", j], wb["up", i] = _qkv_proj(
                hn, bf16_weight(("qkv", j), da_w_qkv, j), cos, sin, da_q_norm_g[j], da_k_norm_g[j],
                [(da_w_o, j, None), up_job(i)])
            att = _diff_attention(qkv, da_lambda[j], da_subln_g[j], bsz, lp, d, _lambda_init(i))
            h, hb, ssq = _proj_residual(att, wb["o", j], h)
        else:
            z = _s5_mixer(hb, ssq, norm_mix_g[i], bsz, lp, ssm_a_re[j], ssm_a_im[j], ssm_log_dt[j],
                          ssm_b_re[j], ssm_b_im[j], ssm_c_re[j], ssm_c_im[j], ssm_d[j])
            h, hb, ssq = _glu_residual(z, bf16_weight(("glu", j), ssm_w_glu, j), ssm_b_glu[j], h)
        jobs = [(("down", i), mlp_w_down, i, None)]
        if nxt_s5:
            jobs += [(("glu", (i + 1) // n_mixers), ssm_w_glu, (i + 1) // n_mixers, None),
                     (("up", i + 1),) + up_job(i + 1)]
        f, *casts = _mlp_up(hb, bf16_weight(("up", i), *up_job(i)), ssq, [job[1:] for job in jobs])
        wb.update({job[0]: c for job, c in zip(jobs, casts)})
        if nxt_s5:
            h, hb, ssq = _mlp_down(f, wb["down", i], h, True)
        else:
            h = _mlp_down(f, wb["down", i], h, False)
    return h.reshape(bsz, lp, d)[:, N_META:length]
```

```python
import functools
import math

import jax
import jax.numpy as jnp
from jax import lax
from jax.experimental import pallas as pl
from jax.experimental.pallas import tpu as pltpu

N_META = 16
SEQ_ALIGN = 256
HEAD_DIM = 128
LOG2E = 1.4426950408889634
ATTN_ROW_BLOCK = 64
ROPE_THETA = 10000.0
SSM_GROUP = 16
SSM_CHUNK = 16
SSM_SEGMENTS = 8
MXU_COLS = 256
LANES = 128
SUBLANES = 8
EPS = 1e-6
VMEM_LIMIT_BYTES = 56 * 1024 * 1024

F32 = jnp.float32
BF16 = jnp.bfloat16


def _pick_tile(n, target, mult):
    best = None
    for t in range(mult, min(n, target) + 1, mult):
        if n % t == 0:
            best = t
    assert best is not None, (n, target, mult)
    return best


def _row_tile(t):
    return _pick_tile(t, 1536, SEQ_ALIGN)


def _params(*sem):
    return pltpu.CompilerParams(dimension_semantics=sem, vmem_limit_bytes=VMEM_LIMIT_BYTES)


class _SideCasts:
    def __init__(self, jobs, n_steps, step_of):
        self.operands, self.in_specs, self.out_specs, self.out_shapes, self.has_gain = [], [], [], [], []
        for src, layer, gain in jobs:
            _, r, c = src.shape
            cr = next(x for x in (16 << p for p in range(24)) if r % x == 0 and r // x <= n_steps)

            def idx(*g, last=r // cr - 1):
                return (jnp.minimum(step_of(*g), last), 0)

            def src_idx(*g, layer=layer, last=r // cr - 1):
                return (layer, jnp.minimum(step_of(*g), last), 0)

            self.operands.append(src)
            self.in_specs.append(pl.BlockSpec((None, cr, c), src_idx))
            if gain is not None:
                self.operands.append(gain.reshape(r, 1).astype(F32))
                self.in_specs.append(pl.BlockSpec((cr, 1), idx))
            self.out_specs.append(pl.BlockSpec((cr, c), idx))
            self.out_shapes.append(jax.ShapeDtypeStruct((r, c), BF16))
            self.has_gain.append(gain is not None)
        self.n_in = len(self.operands)
        self.n_out = len(self.out_shapes)

    def run(self, in_refs, out_refs):
        refs = iter(in_refs)
        for has_gain, o_ref in zip(self.has_gain, out_refs):
            v = next(refs)[...]
            if has_gain:
                v = v * next(refs)[...]
            o_ref[...] = v.astype(o_ref.dtype)


def _rmsnorm_kernel(x_ref, g_ref, o_ref):
    x = x_ref[...]
    ms = jnp.mean(x * x, axis=-1, keepdims=True)
    o_ref[...] = (x * lax.rsqrt(ms + EPS) * g_ref[...]).astype(o_ref.dtype)


def _rmsnorm(x, g):
    t, d = x.shape
    tr = _pick_tile(t, 256, 16)
    return pl.pallas_call(
        _rmsnorm_kernel,
        grid=(t // tr,),
        in_specs=[pl.BlockSpec((tr, d), lambda i: (i, 0)),
                  pl.BlockSpec((1, d), lambda i: (0, 0))],
        out_specs=pl.BlockSpec((tr, d), lambda i: (i, 0)),
        out_shape=jax.ShapeDtypeStruct((t, d), BF16),
        compiler_params=_params("parallel"),
        name="rmsnorm",
    )(x, g.reshape(1, d).astype(F32))


def _embed_norm_kernel(x_ref, meta_ref, g_ref, h_ref, hn_ref, *, n_meta, seq):
    tr = h_ref.shape[0]
    i = pl.program_id(1)

    @pl.when(i == 0)
    def _():
        h_ref[:n_meta, :] = meta_ref[...]
        h_ref[n_meta:, :] = x_ref[:tr - n_meta, :]

    @pl.when(i > 0)
    def _():
        row = i * tr + lax.broadcasted_iota(jnp.int32, h_ref.shape, 0)
        h_ref[...] = jnp.where(row < n_meta + seq, x_ref[...], 0.0)

    h = h_ref[...]
    ms = jnp.mean(h * h, axis=-1, keepdims=True)
    hn_ref[...] = (h * lax.rsqrt(ms + EPS) * g_ref[...]).astype(hn_ref.dtype)


def _embed_norm(x, meta_tokens, g, lp):
    bsz, seq, d = x.shape
    n_meta = meta_tokens.shape[0]
    tr = _pick_tile(lp, 256, 16)
    assert n_meta % 16 == 0 and n_meta < tr and tr < seq
    nt = lp // tr
    tile = pl.BlockSpec((tr, d), lambda b, i: (b * nt + i, 0))

    def x_window(b, i):
        return (b, pl.multiple_of(jnp.maximum(i * tr - n_meta, 0), 16), 0)

    return pl.pallas_call(
        functools.partial(_embed_norm_kernel, n_meta=n_meta, seq=seq),
        grid=(bsz, nt),
        in_specs=[pl.BlockSpec((pl.Squeezed(), pl.Element(tr, (0, lp - n_meta - seq)), pl.Element(d)), x_window),
                  pl.BlockSpec((n_meta, d), lambda b, i: (0, 0)),
                  pl.BlockSpec((1, d), lambda b, i: (0, 0))],
        out_specs=(tile, tile),
        out_shape=(jax.ShapeDtypeStruct((bsz * lp, d), F32), jax.ShapeDtypeStruct((bsz * lp, d), BF16)),
        compiler_params=_params("parallel", "arbitrary"),
        name="embed_norm",
    )(x, meta_tokens.astype(F32), g.reshape(1, d).astype(F32))


def _qkv_kernel(*refs, n_q_blocks, q_scale, side):
    a_ref, w_ref, cos_ref, sin_ref, qg_ref, kg_ref = refs[:6]
    side_in = refs[6:6 + side.n_in]
    o_ref = refs[6 + side.n_in]
    side_out = refs[7 + side.n_in:]
    acc = jnp.dot(a_ref[...], w_ref[...], preferred_element_type=F32)
    j = pl.program_id(1)
    tn = acc.shape[1]

    def norm_rope(g, scale):
        cos = cos_ref[...]
        sin = sin_ref[...]
        for u in range(tn // HEAD_DIM):
            x = acc[:, u * HEAD_DIM:(u + 1) * HEAD_DIM]
            ms = jnp.mean(x * x, axis=-1, keepdims=True)
            y = x * lax.rsqrt(ms + EPS) * g
            y = y * cos + pltpu.roll(y, HEAD_DIM // 2, 1) * sin
            if scale != 1.0:
                y = y * scale
            o_ref[:, u * HEAD_DIM:(u + 1) * HEAD_DIM] = y.astype(o_ref.dtype)

    @pl.when(j < n_q_blocks)
    def _():
        norm_rope(qg_ref[...], q_scale)
        side.run(side_in, side_out)

    @pl.when(jnp.logical_and(j >= n_q_blocks, j < 2 * n_q_blocks))
    def _():
        norm_rope(kg_ref[...], 1.0)
        side.run(side_in, side_out)

    @pl.when(j >= 2 * n_q_blocks)
    def _():
        o_ref[...] = acc.astype(o_ref.dtype)
        side.run(side_in, side_out)


def _qkv_proj(a, w, cos, sin, q_g, k_g, cast_jobs):
    t, d = a.shape
    n = w.shape[1]
    tm = _row_tile(t)
    tn = _pick_tile(d, 512, HEAD_DIM)
    nj = n // tn
    side = _SideCasts(cast_jobs, (t // tm) * nj, lambda i, j: i * nj + j)
    kern = functools.partial(_qkv_kernel, n_q_blocks=d // tn, q_scale=HEAD_DIM ** -0.5 * LOG2E, side=side)
    return pl.pallas_call(
        kern,
        grid=(t // tm, nj),
        in_specs=[pl.BlockSpec((tm, d), lambda i, j: (i, 0)),
                  pl.BlockSpec((d, tn), lambda i, j: (0, j)),
                  pl.BlockSpec((tm, HEAD_DIM), lambda i, j: (i, 0)),
                  pl.BlockSpec((tm, HEAD_DIM), lambda i, j: (i, 0)),
                  pl.BlockSpec((1, HEAD_DIM), lambda i, j: (0, 0)),
                  pl.BlockSpec((1, HEAD_DIM), lambda i, j: (0, 0))] + side.in_specs,
        out_specs=[pl.BlockSpec((tm, tn), lambda i, j: (i, j))] + side.out_specs,
        out_shape=[jax.ShapeDtypeStruct((t, n), BF16)] + side.out_shapes,
        compiler_params=_params("arbitrary", "arbitrary"),
        name="qkv_proj",
    )(a, w, cos, sin, q_g.reshape(1, HEAD_DIM).astype(F32), k_g.reshape(1, HEAD_DIM).astype(F32),
      *side.operands)


def _attn_kernel(q_ref, k_ref, v_ref, lam_ref, g_ref, o_ref,
                 m1_ref, l1_ref, a1_ref, m2_ref, l2_ref, a2_ref, s_even_ref, s_odd_ref, p_ref, alpha_ref,
                 *, blk, lam_init):
    s_bufs = (s_even_ref, s_odd_ref)
    nq = q_ref.shape[0] // blk
    stats = ((m1_ref, l1_ref, a1_ref), (m2_ref, l2_ref, a2_ref))
    lv = lam_ref[...]
    lam = (jnp.exp(jnp.sum(lv[0:1] * lv[1:2], axis=-1, keepdims=True))
           - jnp.exp(jnp.sum(lv[2:3] * lv[3:4], axis=-1, keepdims=True)) + lam_init)

    def rows(i):
        return pl.ds(pl.multiple_of(i * blk, blk), blk)

    def init_stats():
        for m_ref, l_ref, a_ref in stats:
            m_ref[...] = jnp.full(m_ref.shape, -jnp.inf, F32)
            l_ref[...] = jnp.zeros(l_ref.shape, F32)
            a_ref[...] = jnp.zeros(a_ref.shape, F32)

    def scores(qi, j, s_ref):
        q = q_ref[rows(qi), :]
        kk = k_ref[rows(j), :]
        for c in range(2):
            s_ref[c] = lax.dot_general(
                q[:, c * HEAD_DIM:(c + 1) * HEAD_DIM], kk[:, c * HEAD_DIM:(c + 1) * HEAD_DIM],
                (((1,), (1,)), ((), ())), preferred_element_type=F32)

    def softmax_pv(j, s_ref, masked):
        vv = v_ref[rows(j), :]
        for c, (m_ref, l_ref, a_ref) in enumerate(stats):
            for r in range(0, blk, ATTN_ROW_BLOCK):
                rb = slice(r, r + ATTN_ROW_BLOCK)
                s = s_ref[c, rb, :]
                if masked:
                    row = r + lax.broadcasted_iota(jnp.int32, s.shape, 0)
                    col = lax.broadcasted_iota(jnp.int32, s.shape, 1)
                    s = jnp.where(col <= row, s, -jnp.inf)
                m_old = m_ref[rb, :]
                m_new = jnp.maximum(m_old, jnp.max(s, axis=-1, keepdims=True))
                alpha = jnp.exp2(m_old - m_new)
                p = jnp.exp2(s - m_new)
                part = p[:, :LANES]
                for u in range(1, blk // LANES):
                    part = part + p[:, u * LANES:(u + 1) * LANES]
                l_ref[rb, :] = alpha * l_ref[rb, :] + part
                p_ref[c, rb, :] = p.astype(BF16)
                alpha_ref[c, rb, :] = alpha
                m_ref[rb, :] = m_new
            a_ref[...] = alpha_ref[c] * a_ref[...] + jnp.dot(p_ref[c], vv, preferred_element_type=F32)

    def finalize(qi):
        l1 = jnp.sum(l1_ref[...], axis=-1, keepdims=True)
        l2 = jnp.sum(l2_ref[...], axis=-1, keepdims=True)
        o = a1_ref[...] / l1 - lam * (a2_ref[...] / l2)
        ms = jnp.mean(o * o, axis=-1, keepdims=True)
        y = o * lax.rsqrt(ms + EPS) * g_ref[...] * (1.0 - lam_init)
        o_ref[rows(qi), :] = y.astype(o_ref.dtype)

    def items(n, *work):
        for parity in range(2):
            @pl.when(n % 2 == parity)
            def _():
                for step, (next_qi, next_j, j, masked) in enumerate(work):
                    mine = (parity + step) % 2
                    scores(next_qi, next_j, s_bufs[1 - mine])
                    softmax_pv(j, s_bufs[mine], masked)

    init_stats()
    scores(0, 0, s_bufs[0])

    def q_block(qi, carry):
        base = (qi * (qi + 1)) // 2
        diagonal = (jnp.minimum(qi + 1, nq - 1), 0, qi, True)

        def full_chunk(j, c):
            items(base + j, (qi, j + 1, j, False))
            return c

        lax.fori_loop(0, qi, full_chunk, 0)
        items(base + qi, diagonal)
        finalize(qi)
        init_stats()
        return carry

    lax.fori_loop(0, nq, q_block, 0)


def _diff_attention(qkv, lam_vecs, subln_g, bsz, lp, d, lam_init):
    t = qkv.shape[0]
    n_heads = d // (2 * HEAD_DIM)
    hw = 2 * HEAD_DIM
    blk = _pick_tile(lp, 768, SEQ_ALIGN)
    kern = functools.partial(_attn_kernel, blk=blk, lam_init=lam_init)
    return pl.pallas_call(
        kern,
        grid=(bsz, n_heads),
        in_specs=[pl.BlockSpec((lp, hw), lambda b, h: (b, h)),
                  pl.BlockSpec((lp, hw), lambda b, h: (b, n_heads + h)),
                  pl.BlockSpec((lp, hw), lambda b, h: (b, 2 * n_heads + h)),
                  pl.BlockSpec((4, HEAD_DIM), lambda b, h: (0, 0)),
                  pl.BlockSpec((1, hw), lambda b, h: (0, 0))],
        out_specs=pl.BlockSpec((lp, hw), lambda b, h: (b, h)),
        out_shape=jax.ShapeDtypeStruct((t, d), BF16),
        scratch_shapes=[pltpu.VMEM((blk, 1), F32), pltpu.VMEM((blk, LANES), F32), pltpu.VMEM((blk, hw), F32),
                        pltpu.VMEM((blk, 1), F32), pltpu.VMEM((blk, LANES), F32), pltpu.VMEM((blk, hw), F32),
                        pltpu.VMEM((2, blk, blk), F32), pltpu.VMEM((2, blk, blk), F32),
                        pltpu.VMEM((2, blk, blk), BF16), pltpu.VMEM((2, blk, 1), F32)],
        compiler_params=_params("parallel", "parallel"),
        name="diff_attention",
    )(qkv, qkv, qkv, lam_vecs.astype(F32), subln_g.reshape(1, hw).astype(F32))


def _col_panels(n):
    return [slice(u, u + MXU_COLS) for u in range(0, n, MXU_COLS)]


def _emit_norm_inputs(h, cols, hb_ref, ssq_ref):
    hb_ref[:, cols] = h.astype(hb_ref.dtype)
    ssq_ref[...] += jnp.sum(h * h, axis=-1, keepdims=True)


def _proj_residual_kernel(a_ref, w_ref, r_ref, o_ref, hb_ref, ssq_ref):
    @pl.when(pl.program_id(1) == 0)
    def _():
        ssq_ref[...] = jnp.zeros(ssq_ref.shape, F32)

    for cols in _col_panels(o_ref.shape[1]):
        h = r_ref[:, cols] + jnp.dot(a_ref[...], w_ref[:, cols], preferred_element_type=F32)
        o_ref[:, cols] = h
        _emit_norm_inputs(h, cols, hb_ref, ssq_ref)


def _norm_out_shapes(t, n):
    return (jax.ShapeDtypeStruct((t, n), F32), jax.ShapeDtypeStruct((t, n), BF16),
            jax.ShapeDtypeStruct((t, 1), F32))


def _proj_residual(a, w, res):
    t, k = a.shape
    n = w.shape[1]
    tm = _row_tile(t)
    tn = _pick_tile(n, 512, LANES)
    return pl.pallas_call(
        _proj_residual_kernel,
        grid=(t // tm, n // tn),
        in_specs=[pl.BlockSpec((tm, k), lambda i, j: (i, 0)),
                  pl.BlockSpec((k, tn), lambda i, j: (0, j)),
                  pl.BlockSpec((tm, tn), lambda i, j: (i, j))],
        out_specs=(pl.BlockSpec((tm, tn), lambda i, j: (i, j)),
                   pl.BlockSpec((tm, tn), lambda i, j: (i, j)),
                   pl.BlockSpec((tm, 1), lambda i, j: (i, 0))),
        out_shape=_norm_out_shapes(t, n),
        compiler_params=_params("parallel", "arbitrary"),
        name="proj_residual",
    )(a, w, res)


def _mlp_up_kernel(*refs, inv_d, side):
    a_ref, w_ref, ssq_ref = refs[:3]
    side_in = refs[3:3 + side.n_in]
    o_ref = refs[3 + side.n_in]
    side_out = refs[4 + side.n_in:]
    r2 = 1.0 / (ssq_ref[...] * inv_d + EPS)
    side.run(side_in, side_out)
    for cols in _col_panels(o_ref.shape[1]):
        u = jnp.dot(a_ref[...], w_ref[:, cols], preferred_element_type=F32)
        o_ref[:, cols] = (jnp.square(jnp.maximum(u, 0.0)) * r2).astype(o_ref.dtype)


def _mlp_up(hb, w, ssq, cast_jobs):
    t, k = hb.shape
    n = w.shape[1]
    tm = _row_tile(t)
    tn = _pick_tile(n, 1024, LANES)
    nj = n // tn
    side = _SideCasts(cast_jobs, (t // tm) * nj, lambda i, j: i * nj + j)
    return pl.pallas_call(
        functools.partial(_mlp_up_kernel, inv_d=1.0 / k, side=side),
        grid=(t // tm, nj),
        in_specs=[pl.BlockSpec((tm, k), lambda i, j: (i, 0), pipeline_mode=pl.Buffered(1)),
                  pl.BlockSpec((k, tn), lambda i, j: (0, j)),
                  pl.BlockSpec((tm, 1), lambda i, j: (i, 0))] + side.in_specs,
        out_specs=[pl.BlockSpec((tm, tn), lambda i, j: (i, j))] + side.out_specs,
        out_shape=[jax.ShapeDtypeStruct((t, n), BF16)] + side.out_shapes,
        compiler_params=_params("arbitrary", "arbitrary"),
        name="mlp_up",
    )(hb, w, ssq, *side.operands)


def _mlp_down_kernel(a_ref, w_ref, r_ref, o_ref, *norm_refs, nk, col_axis=1):
    j = pl.program_id(col_axis)
    k = pl.program_id(col_axis + 1)

    def accumulate(base_ref, last):
        for cols in _col_panels(o_ref.shape[1]):
            h = base_ref[:, cols] + jnp.dot(a_ref[...], w_ref[:, cols], preferred_element_type=F32)
            o_ref[:, cols] = h
            if last and norm_refs:
                _emit_norm_inputs(h, cols, *norm_refs)

    if norm_refs:
        @pl.when(jnp.logical_and(j == 0, k == 0))
        def _():
            norm_refs[1][...] = jnp.zeros(norm_refs[1].shape, F32)

    if nk == 1:
        accumulate(r_ref, True)
        return

    @pl.when(k == 0)
    def _():
        accumulate(r_ref, False)

    @pl.when(jnp.logical_and(k > 0, k < nk - 1))
    def _():
        accumulate(o_ref, False)

    @pl.when(k == nk - 1)
    def _():
        accumulate(o_ref, True)


def _mlp_down(a, w, res, emit_norm):
    t, kdim = a.shape
    n = w.shape[1]
    tm = _row_tile(t)
    tn = _pick_tile(n, 1024, LANES)
    tk = _pick_tile(kdim, 2048, LANES)
    nk = kdim // tk
    tile = pl.BlockSpec((tm, tn), lambda i, j, k: (i, j))
    if emit_norm:
        out_specs = (tile, tile, pl.BlockSpec((tm, 1), lambda i, j, k: (i, 0)))
        out_shape = _norm_out_shapes(t, n)
    else:
        out_specs = tile
        out_shape = jax.ShapeDtypeStruct((t, n), F32)
    return pl.pallas_call(
        functools.partial(_mlp_down_kernel, nk=nk),
        grid=(t // tm, n // tn, nk),
        in_specs=[pl.BlockSpec((tm, tk), lambda i, j, k: (i, k)),
                  pl.BlockSpec((tk, tn), lambda i, j, k: (k, j)),
                  tile],
        out_specs=out_specs,
        out_shape=out_shape,
        compiler_params=_params("parallel", "arbitrary", "arbitrary"),
        name="mlp_down",
    )(a, w, res)


def _mlp_down_final(a, w, res, bsz, lp, row0, seq):
    kdim = a.shape[1]
    n = w.shape[1]
    tm = _pick_tile(seq, 1024, 16)
    nt = seq // tm
    tn = _pick_tile(n, 1024, LANES)
    tk = _pick_tile(kdim, 4096, LANES)
    nk = kdim // tk

    assert lp % 16 == 0 and row0 % 16 == 0 and tm % 16 == 0

    def row_start(b, m):
        return pl.multiple_of(b * lp + row0 + m * tm, 16)

    return pl.pallas_call(
        functools.partial(_mlp_down_kernel, nk=nk, col_axis=2),
        grid=(bsz, nt, n // tn, nk),
        in_specs=[pl.BlockSpec((pl.Element(tm), pl.Element(tk)), lambda b, m, j, k: (row_start(b, m), k * tk)),
                  pl.BlockSpec((tk, tn), lambda b, m, j, k: (k, j)),
                  pl.BlockSpec((pl.Element(tm), pl.Element(tn)), lambda b, m, j, k: (row_start(b, m), j * tn))],
        out_specs=pl.BlockSpec((tm, tn), lambda b, m, j, k: (b * nt + m, j)),
        out_shape=jax.ShapeDtypeStruct((bsz * seq, n), F32),
        compiler_params=_params("parallel", "parallel", "arbitrary", "arbitrary"),
        name="mlp_down_final",
    )(a, w, res)


def _glu_kernel(a_ref, wv_ref, wg_ref, bv_ref, bg_ref, r_ref, o_ref, hb_ref, ssq_ref):
    @pl.when(pl.program_id(1) == 0)
    def _():
        ssq_ref[...] = jnp.zeros(ssq_ref.shape, F32)

    for cols in _col_panels(o_ref.shape[1]):
        val = jnp.dot(a_ref[...], wv_ref[:, cols], preferred_element_type=F32) + bv_ref[:, cols]
        gate = jnp.dot(a_ref[...], wg_ref[:, cols], preferred_element_type=F32) + bg_ref[:, cols]
        h = r_ref[:, cols] + val * jax.nn.sigmoid(gate)
        o_ref[:, cols] = h
        _emit_norm_inputs(h, cols, hb_ref, ssq_ref)


def _glu_residual(a, w, bias, res):
    t, k = a.shape
    n = w.shape[1] // 2
    tm = _row_tile(t)
    tn = _pick_tile(n, 512, LANES)
    nb = n // tn
    b2 = bias.reshape(1, 2 * n).astype(F32)
    return pl.pallas_call(
        _glu_kernel,
        grid=(t // tm, nb),
        in_specs=[pl.BlockSpec((tm, k), lambda i, j: (i, 0), pipeline_mode=pl.Buffered(1)),
                  pl.BlockSpec((k, tn), lambda i, j: (0, j)),
                  pl.BlockSpec((k, tn), lambda i, j: (0, nb + j)),
                  pl.BlockSpec((1, tn), lambda i, j: (0, j)),
                  pl.BlockSpec((1, tn), lambda i, j: (0, nb + j)),
                  pl.BlockSpec((tm, tn), lambda i, j: (i, j))],
        out_specs=(pl.BlockSpec((tm, tn), lambda i, j: (i, j)),
                   pl.BlockSpec((tm, tn), lambda i, j: (i, j)),
                   pl.BlockSpec((tm, 1), lambda i, j: (i, 0))),
        out_shape=_norm_out_shapes(t, n),
        compiler_params=_params("parallel", "arbitrary"),
        name="glu_residual",
    )(a, w, w, b2, b2, res)


def _s5_kernel(x_ref, ssq_ref, g_ref, w0_ref, cm_ref, pr_ref, pi_ref, prc_ref, pic_ref, d_ref, o_ref,
               m_scr, wst_scr, wout_scr, s_scr, xin_scr, *, n_steps, inv_d):
    q = SSM_CHUNK
    half = w0_ref.shape[2] // 2

    @pl.when(pl.program_id(1) == 0)
    def _build_weights():
        w0 = w0_ref[0]
        w0re, w0im = w0[:, :half], w0[:, half:]
        cm = cm_ref[0]
        cm_bf = cm.astype(BF16)
        row = lax.broadcasted_iota(jnp.int32, (LANES, LANES), 0)
        col = lax.broadcasted_iota(jnp.int32, (LANES, LANES), 1)
        skip = jnp.where(row == col, jnp.broadcast_to(d_ref[...], (LANES, LANES)), 0.0)
        zero_blk = jnp.zeros((LANES, LANES), BF16)
        for tau in range(q):
            ar = pr_ref[0, tau:tau + 1, :]
            ai = pi_ref[0, tau:tau + 1, :]
            w_tau = jnp.concatenate([w0re * ar - w0im * ai, w0im * ar + w0re * ai], axis=1).astype(BF16)
            t_st = q - 1 - tau
            wst_scr[t_st * LANES:(t_st + 1) * LANES, :] = w_tau
            k_tau = jnp.dot(w_tau, cm_bf, preferred_element_type=F32)
            if tau == 0:
                k_tau = k_tau + skip
            k_bf = k_tau.astype(BF16)
            for t in range(q - tau):
                m_scr[t * LANES:(t + 1) * LANES, (t + tau) * LANES:(t + tau + 1) * LANES] = k_bf
        for t in range(q):
            for t2 in range(t):
                m_scr[t * LANES:(t + 1) * LANES, t2 * LANES:(t2 + 1) * LANES] = zero_blk
        cre, cimn = cm[:half], cm[half:]
        for t in range(q):
            arc = prc_ref[0, :, t + 1:t + 2]
            aic = pic_ref[0, :, t + 1:t + 2]
            wout_scr[:half, t * LANES:(t + 1) * LANES] = (arc * cre + aic * cimn).astype(BF16)
            wout_scr[half:, t * LANES:(t + 1) * LANES] = (arc * cimn - aic * cre).astype(BF16)

    gain = g_ref[...]
    u = jnp.concatenate(
        [(x_ref[t].astype(F32) * lax.rsqrt(ssq_ref[t] * inv_d + EPS) * gain).astype(BF16) for t in range(q)],
        axis=1)
    s_scr[...] = jnp.dot(u, wst_scr[...], preferred_element_type=F32)

    shp = (SUBLANES, half)
    ar = jnp.broadcast_to(pr_ref[0, q:q + 1, :], shp)
    ai = jnp.broadcast_to(pi_ref[0, q:q + 1, :], shp)
    anr = jnp.broadcast_to(pr_ref[0, q + 1:q + 2, :], shp)
    ani = jnp.broadcast_to(pi_ref[0, q + 1:q + 2, :], shp)
    seg = lax.broadcasted_iota(jnp.int32, shp, 0)
    zeros = jnp.zeros(shp, F32)

    def shift_down(x):
        return jnp.where(seg == 0, 0.0, pltpu.roll(x, 1, 0))

    def advance(j, cr, ci):
        sj = s_scr[pl.ds(pl.multiple_of(j * SUBLANES, SUBLANES), SUBLANES), :]
        return ar * cr - ai * ci + sj[:, :half], ar * ci + ai * cr + sj[:, half:]

    er, ei = lax.fori_loop(0, n_steps, lambda j, c: advance(j, *c), (zeros, zeros))
    tr, ti = er, ei
    for _ in range(SSM_SEGMENTS - 1):
        sr, si = shift_down(tr), shift_down(ti)
        tr, ti = er + anr * sr - ani * si, ei + anr * si + ani * sr
    cr0, ci0 = shift_down(tr), shift_down(ti)

    def scan_store(j, c):
        cr, ci = c
        xin_scr[pl.ds(pl.multiple_of(j * SUBLANES, SUBLANES), SUBLANES), :] = jnp.concatenate([cr, ci], axis=1)
        return advance(j, cr, ci)

    lax.fori_loop(0, n_steps, scan_store, (cr0, ci0))

    xin = xin_scr[...].astype(BF16)
    steps_per_panel = 4
    for t0 in range(0, q, steps_per_panel):
        k_hi = (t0 + steps_per_panel) * LANES
        cols = slice(t0 * LANES, k_hi)
        y = (jnp.dot(u[:, :k_hi], m_scr[:k_hi, cols], preferred_element_type=F32)
             + jnp.dot(xin, wout_scr[:, cols], preferred_element_type=F32))
        z = jax.nn.gelu(y)
        for t in range(steps_per_panel):
            o_ref[t0 + t] = z[:, t * LANES:(t + 1) * LANES].astype(o_ref.dtype)


def _complex_pow(zr, zi, n):
    rr, ri = None, None
    br, bi = zr, zi
    while n:
        if n & 1:
            rr, ri = (br, bi) if rr is None else (rr * br - ri * bi, rr * bi + ri * br)
        n >>= 1
        if n:
            br, bi = br * br - bi * bi, 2.0 * br * bi
    return rr, ri


def _s5_tables(a_re, a_im, log_dt, b_re, b_im, c_re, c_im, n_steps):
    g, p = a_re.shape
    gpb = LANES // SSM_GROUP
    nblk = g // gpb
    dt = jnp.exp(log_dt.astype(F32))[:, None]
    ar = a_re.astype(F32)
    ai = a_im.astype(F32)
    mag = jnp.exp(dt * ar)
    ang = dt * ai
    abar_re = mag * jnp.cos(ang)
    abar_im = mag * jnp.sin(ang)
    nr = abar_re - 1.0
    ni = abar_im
    den = ar * ar + ai * ai
    f_re = (nr * ar + ni * ai) / den
    f_im = (ni * ar - nr * ai) / den
    br = b_re.astype(F32)
    bi = b_im.astype(F32)
    bb_re = f_re[..., None] * br - f_im[..., None] * bi
    bb_im = f_re[..., None] * bi + f_im[..., None] * br

    pw_r, pw_i = [jnp.ones_like(abar_re)], [jnp.zeros_like(abar_re)]
    for _ in range(SSM_CHUNK):
        pw_r.append(pw_r[-1] * abar_re - pw_i[-1] * abar_im)
        pw_i.append(pw_r[-2] * abar_im + pw_i[-1] * abar_re)
    seg_r, seg_i = _complex_pow(pw_r[-1], pw_i[-1], n_steps)
    pw_r.append(seg_r)
    pw_i.append(seg_i)
    n_rows = 24
    pr = jnp.stack(pw_r, 0).reshape(len(pw_r), nblk, gpb * p).transpose(1, 0, 2)
    pi = jnp.stack(pw_i, 0).reshape(len(pw_i), nblk, gpb * p).transpose(1, 0, 2)
    pad = ((0, 0), (0, n_rows - pr.shape[1]), (0, 0))
    pr = jnp.pad(pr, pad)
    pi = jnp.pad(pi, pad)
    prc = pr.transpose(0, 2, 1)
    pic = pi.transpose(0, 2, 1)

    eye = jnp.eye(gpb, dtype=F32)

    def in_to_state(bb):
        v = bb.reshape(nblk, gpb, p, SSM_GROUP).transpose(0, 1, 3, 2)
        e = v[:, :, :, None, :] * eye[None, :, None, :, None]
        return e.reshape(nblk, LANES, gpb * p)

    def state_to_out(c):
        v = c.astype(F32).reshape(nblk, gpb, SSM_GROUP, p).transpose(0, 1, 3, 2)
        e = v[:, :, :, None, :] * eye[None, :, None, :, None]
        return e.reshape(nblk, gpb * p, LANES)

    w0 = jnp.concatenate([in_to_state(bb_re), in_to_state(bb_im)], axis=2)
    cm = jnp.concatenate([state_to_out(c_re), -state_to_out(c_im)], axis=1)
    return w0, cm, pr, pi, prc, pic


def _s5_mixer(hb, ssq, gain, bsz, lp, a_re, a_im, log_dt, b_re, b_im, c_re, c_im, d_skip):
    t, d = hb.shape
    q, nseg = SSM_CHUNK, SSM_SEGMENTS
    assert lp % (q * nseg * 2) == 0
    n_steps = lp // (q * nseg)
    rows = n_steps * nseg

    def chunk_layout(v):
        w = v.shape[-1]
        return v.reshape(bsz, nseg, n_steps, q, w).transpose(3, 0, 2, 1, 4).reshape(q, bsz * rows, w)

    x = chunk_layout(hb)
    ssq_c = chunk_layout(ssq)

    w0, cm, pr, pi, prc, pic = _s5_tables(a_re, a_im, log_dt, b_re, b_im, c_re, c_im, n_steps)
    nblk = d // LANES
    n_state = w0.shape[2]
    kern = functools.partial(_s5_kernel, n_steps=n_steps, inv_d=1.0 / d)
    z = pl.pallas_call(
        kern,
        grid=(nblk, bsz),
        in_specs=[pl.BlockSpec((q, rows, LANES), lambda k, b: (0, b, k)),
                  pl.BlockSpec((q, rows, 1), lambda k, b: (0, b, 0)),
                  pl.BlockSpec((1, LANES), lambda k, b: (0, k)),
                  pl.BlockSpec((1, LANES, n_state), lambda k, b: (k, 0, 0)),
                  pl.BlockSpec((1, n_state, LANES), lambda k, b: (k, 0, 0)),
                  pl.BlockSpec((1,) + pr.shape[1:], lambda k, b: (k, 0, 0)),
                  pl.BlockSpec((1,) + pi.shape[1:], lambda k, b: (k, 0, 0)),
                  pl.BlockSpec((1,) + prc.shape[1:], lambda k, b: (k, 0, 0)),
                  pl.BlockSpec((1,) + pic.shape[1:], lambda k, b: (k, 0, 0)),
                  pl.BlockSpec((1, LANES), lambda k, b: (0, k))],
        out_specs=pl.BlockSpec((q, rows, LANES), lambda k, b: (0, b, k)),
        out_shape=jax.ShapeDtypeStruct((q, bsz * rows, d), BF16),
        scratch_shapes=[pltpu.VMEM((q * LANES, q * LANES), BF16),
                        pltpu.VMEM((q * LANES, n_state), BF16),
                        pltpu.VMEM((n_state, q * LANES), BF16),
                        pltpu.VMEM((rows, n_state), F32),
                        pltpu.VMEM((rows, n_state), F32)],
        compiler_params=_params("arbitrary", "arbitrary"),
        name="s5_mixer",
    )(x, ssq_c, gain.reshape(1, d).astype(F32), w0, cm, pr, pi, prc, pic, d_skip.reshape(1, d).astype(F32))
    return z.reshape(q, bsz, n_steps, nseg, d).transpose(1, 3, 2, 0, 4).reshape(t, d)


def _rope_tables(length):
    inv = ROPE_THETA ** (-jnp.arange(0, HEAD_DIM, 2, dtype=F32) / HEAD_DIM)
    ang = jnp.arange(length, dtype=F32)[:, None] * inv[None, :]
    cos, sin = jnp.cos(ang), jnp.sin(ang)
    return jnp.concatenate([cos, cos], axis=1), jnp.concatenate([-sin, sin], axis=1)


def _lambda_init(layer_idx):
    return 0.8 - 0.6 * math.exp(-0.3 * layer_idx)


def kernel(x, meta_tokens, norm_mix_g, norm_mlp_g, da_w_qkv, da_q_norm_g, da_k_norm_g, da_lambda, da_subln_g, da_w_o, ssm_a_re, ssm_a_im, ssm_log_dt, ssm_b_re, ssm_b_im, ssm_c_re, ssm_c_im, ssm_d, ssm_w_glu, ssm_b_glu, mlp_w_up, mlp_w_down):
    bsz, seq, d = x.shape
    depth = norm_mix_g.shape[0]
    n_mixers = 2
    length = N_META + seq
    lp = ((length + SEQ_ALIGN - 1) // SEQ_ALIGN) * SEQ_ALIGN
    assert meta_tokens.shape[0] == N_META and depth >= 1
    h, hn = _embed_norm(x, meta_tokens, norm_mix_g[0], lp)
    cos, sin = (jnp.tile(tab, (bsz, 1)) for tab in _rope_tables(lp))
    def up_job(i):
        return mlp_w_up, i, norm_mlp_g[i]

    def bf16_weight(name, stack, layer, gain=None):
        if name not in wb:
            src = stack[layer]
            wb[name] = (src if gain is None else src * gain.astype(F32)[:, None]).astype(BF16)
        return wb[name]

    wb = {}
    hb = ssq = None
    for i in range(depth):
        j = i // n_mixers
        nxt_s5 = i + 1 < depth and (i + 1) % n_mixers == 1
        if i % n_mixers == 0:
            if i > 0:
                hn = _rmsnorm(h, norm_mix_g[i])
            qkv, wb["o", j], wb["up", i] = _qkv_proj(
                hn, bf16_weight(("qkv", j), da_w_qkv, j), cos, sin, da_q_norm_g[j], da_k_norm_g[j],
                [(da_w_o, j, None), up_job(i)])
            att = _diff_attention(qkv, da_lambda[j], da_subln_g[j], bsz, lp, d, _lambda_init(i))
            h, hb, ssq = _proj_residual(att, wb["o", j], h)
        else:
            z = _s5_mixer(hb, ssq, norm_mix_g[i], bsz, lp, ssm_a_re[j], ssm_a_im[j], ssm_log_dt[j],
                          ssm_b_re[j], ssm_b_im[j], ssm_c_re[j], ssm_c_im[j], ssm_d[j])
            h, hb, ssq = _glu_residual(z, bf16_weight(("glu", j), ssm_w_glu, j), ssm_b_glu[j], h)
        jobs = [(("down", i), mlp_w_down, i, None)]
        if nxt_s5:
            jobs += [(("glu", (i + 1) // n_mixers), ssm_w_glu, (i + 1) // n_mixers, None),
                     (("up", i + 1),) + up_job(i + 1)]
        f, *casts = _mlp_up(hb, bf16_weight(("up", i), *up_job(i)), ssq, [job[1:] for job in jobs])
        wb.update({job[0]: c for job, c in zip(jobs, casts)})
        if i == depth - 1:
            return _mlp_down_final(f, wb["down", i], h, bsz, lp, N_META, seq).reshape(bsz, seq, d)
        if nxt_s5:
            h, hb, ssq = _mlp_down(f, wb["down", i], h, True)
        else:
            h = _mlp_down(f, wb["down", i], h, False)
```

```python
import functools
import math

import jax
import jax.numpy as jnp
from jax import lax
from jax.experimental import pallas as pl
from jax.experimental.pallas import tpu as pltpu

N_META = 16
SEQ_ALIGN = 256
HEAD_DIM = 128
LOG2E = 1.4426950408889634
ATTN_ROW_BLOCK = 64
ROPE_THETA = 10000.0
SSM_GROUP = 16
SSM_CHUNK = 16
SSM_SEGMENTS = 8
MXU_COLS = 256
LANES = 128
SUBLANES = 8
EPS = 1e-6
VMEM_LIMIT_BYTES = 56 * 1024 * 1024

F32 = jnp.float32
BF16 = jnp.bfloat16


def _pick_tile(n, target, mult):
    best = None
    for t in range(mult, min(n, target) + 1, mult):
        if n % t == 0:
            best = t
    assert best is not None, (n, target, mult)
    return best


def _row_tile(t):
    return _pick_tile(t, 1536, SEQ_ALIGN)


def _params(*sem):
    return pltpu.CompilerParams(dimension_semantics=sem, vmem_limit_bytes=VMEM_LIMIT_BYTES)


class _SideCasts:
    def __init__(self, jobs, n_steps, step_of):
        self.operands, self.in_specs, self.out_specs, self.out_shapes, self.has_gain = [], [], [], [], []
        for src, layer, gain in jobs:
            _, r, c = src.shape
            cr = next(x for x in (16 << p for p in range(24)) if r % x == 0 and r // x <= n_steps)

            def idx(*g, last=r // cr - 1):
                return (jnp.minimum(step_of(*g), last), 0)

            def src_idx(*g, layer=layer, last=r // cr - 1):
                return (layer, jnp.minimum(step_of(*g), last), 0)

            self.operands.append(src)
            self.in_specs.append(pl.BlockSpec((None, cr, c), src_idx))
            if gain is not None:
                self.operands.append(gain.reshape(r, 1).astype(F32))
                self.in_specs.append(pl.BlockSpec((cr, 1), idx))
            self.out_specs.append(pl.BlockSpec((cr, c), idx))
            self.out_shapes.append(jax.ShapeDtypeStruct((r, c), BF16))
            self.has_gain.append(gain is not None)
        self.n_in = len(self.operands)
        self.n_out = len(self.out_shapes)

    def run(self, in_refs, out_refs):
        refs = iter(in_refs)
        for has_gain, o_ref in zip(self.has_gain, out_refs):
            v = next(refs)[...]
            if has_gain:
                v = v * next(refs)[...]
            o_ref[...] = v.astype(o_ref.dtype)


def _rmsnorm_kernel(x_ref, g_ref, o_ref):
    x = x_ref[...]
    ms = jnp.mean(x * x, axis=-1, keepdims=True)
    o_ref[...] = (x * lax.rsqrt(ms + EPS) * g_ref[...]).astype(o_ref.dtype)


def _rmsnorm(x, g):
    t, d = x.shape
    tr = _pick_tile(t, 256, 16)
    return pl.pallas_call(
        _rmsnorm_kernel,
        grid=(t // tr,),
        in_specs=[pl.BlockSpec((tr, d), lambda i: (i, 0)),
                  pl.BlockSpec((1, d), lambda i: (0, 0))],
        out_specs=pl.BlockSpec((tr, d), lambda i: (i, 0)),
        out_shape=jax.ShapeDtypeStruct((t, d), BF16),
        compiler_params=_params("parallel"),
        name="rmsnorm",
    )(x, g.reshape(1, d).astype(F32))


def _embed_norm_kernel(x_ref, meta_ref, g_ref, h_ref, hn_ref, *, n_meta, seq):
    tr = h_ref.shape[0]
    i = pl.program_id(1)

    @pl.when(i == 0)
    def _():
        h_ref[:n_meta, :] = meta_ref[...]
        h_ref[n_meta:, :] = x_ref[:tr - n_meta, :]

    @pl.when(i > 0)
    def _():
        row = i * tr + lax.broadcasted_iota(jnp.int32, h_ref.shape, 0)
        h_ref[...] = jnp.where(row < n_meta + seq, x_ref[...], 0.0)

    h = h_ref[...]
    ms = jnp.mean(h * h, axis=-1, keepdims=True)
    hn_ref[...] = (h * lax.rsqrt(ms + EPS) * g_ref[...]).astype(hn_ref.dtype)


def _embed_norm(x, meta_tokens, g, lp):
    bsz, seq, d = x.shape
    n_meta = meta_tokens.shape[0]
    tr = _pick_tile(lp, 256, 16)
    assert n_meta % 16 == 0 and n_meta < tr and tr < seq
    nt = lp // tr
    tile = pl.BlockSpec((tr, d), lambda b, i: (b * nt + i, 0))

    def x_window(b, i):
        return (b, pl.multiple_of(jnp.maximum(i * tr - n_meta, 0), 16), 0)

    return pl.pallas_call(
        functools.partial(_embed_norm_kernel, n_meta=n_meta, seq=seq),
        grid=(bsz, nt),
        in_specs=[pl.BlockSpec((pl.Squeezed(), pl.Element(tr, (0, lp - n_meta - seq)), pl.Element(d)), x_window),
                  pl.BlockSpec((n_meta, d), lambda b, i: (0, 0)),
                  pl.BlockSpec((1, d), lambda b, i: (0, 0))],
        out_specs=(tile, tile),
        out_shape=(jax.ShapeDtypeStruct((bsz * lp, d), F32), jax.ShapeDtypeStruct((bsz * lp, d), BF16)),
        compiler_params=_params("parallel", "arbitrary"),
        name="embed_norm",
    )(x, meta_tokens.astype(F32), g.reshape(1, d).astype(F32))


def _qkv_kernel(*refs, n_q_blocks, q_scale, side):
    a_ref, w_ref, cos_ref, sin_ref, qg_ref, kg_ref = refs[:6]
    side_in = refs[6:6 + side.n_in]
    o_ref = refs[6 + side.n_in]
    side_out = refs[7 + side.n_in:]
    acc = jnp.dot(a_ref[...], w_ref[...], preferred_element_type=F32)
    j = pl.program_id(1)
    tn = acc.shape[1]

    def norm_rope(g, scale):
        cos = cos_ref[...]
        sin = sin_ref[...]
        for u in range(tn // HEAD_DIM):
            x = acc[:, u * HEAD_DIM:(u + 1) * HEAD_DIM]
            ms = jnp.mean(x * x, axis=-1, keepdims=True)
            y = x * lax.rsqrt(ms + EPS) * g
            y = y * cos + pltpu.roll(y, HEAD_DIM // 2, 1) * sin
            if scale != 1.0:
                y = y * scale
            o_ref[:, u * HEAD_DIM:(u + 1) * HEAD_DIM] = y.astype(o_ref.dtype)

    @pl.when(j < n_q_blocks)
    def _():
        norm_rope(qg_ref[...], q_scale)
        side.run(side_in, side_out)

    @pl.when(jnp.logical_and(j >= n_q_blocks, j < 2 * n_q_blocks))
    def _():
        norm_rope(kg_ref[...], 1.0)
        side.run(side_in, side_out)

    @pl.when(j >= 2 * n_q_blocks)
    def _():
        o_ref[...] = acc.astype(o_ref.dtype)
        side.run(side_in, side_out)


def _qkv_proj(a, w, cos, sin, q_g, k_g, cast_jobs):
    t, d = a.shape
    n = w.shape[1]
    tm = _row_tile(t)
    tn = _pick_tile(d, 512, HEAD_DIM)
    nj = n // tn
    side = _SideCasts(cast_jobs, (t // tm) * nj, lambda i, j: i * nj + j)
    kern = functools.partial(_qkv_kernel, n_q_blocks=d // tn, q_scale=HEAD_DIM ** -0.5 * LOG2E, side=side)
    return pl.pallas_call(
        kern,
        grid=(t // tm, nj),
        in_specs=[pl.BlockSpec((tm, d), lambda i, j: (i, 0)),
                  pl.BlockSpec((d, tn), lambda i, j: (0, j)),
                  pl.BlockSpec((tm, HEAD_DIM), lambda i, j: (i, 0)),
                  pl.BlockSpec((tm, HEAD_DIM), lambda i, j: (i, 0)),
                  pl.BlockSpec((1, HEAD_DIM), lambda i, j: (0, 0)),
                  pl.BlockSpec((1, HEAD_DIM), lambda i, j: (0, 0))] + side.in_specs,
        out_specs=[pl.BlockSpec((tm, tn), lambda i, j: (i, j))] + side.out_specs,
        out_shape=[jax.ShapeDtypeStruct((t, n), BF16)] + side.out_shapes,
        compiler_params=_params("arbitrary", "arbitrary"),
        name="qkv_proj",
    )(a, w, cos, sin, q_g.reshape(1, HEAD_DIM).astype(F32), k_g.reshape(1, HEAD_DIM).astype(F32),
      *side.operands)


def _attn_kernel(q_ref, k_ref, v_ref, lam_ref, g_ref, o_ref,
                 m1_ref, l1_ref, a1_ref, m2_ref, l2_ref, a2_ref, s_even_ref, s_odd_ref, p_ref, alpha_ref,
                 *, blk, lam_init):
    s_bufs = (s_even_ref, s_odd_ref)
    nq = q_ref.shape[0] // blk
    stats = ((m1_ref, l1_ref, a1_ref), (m2_ref, l2_ref, a2_ref))
    lv = lam_ref[...]
    lam = (jnp.exp(jnp.sum(lv[0:1] * lv[1:2], axis=-1, keepdims=True))
           - jnp.exp(jnp.sum(lv[2:3] * lv[3:4], axis=-1, keepdims=True)) + lam_init)

    def rows(i):
        return pl.ds(pl.multiple_of(i * blk, blk), blk)

    def init_stats():
        for m_ref, l_ref, a_ref in stats:
            m_ref[...] = jnp.full(m_ref.shape, -jnp.inf, F32)
            l_ref[...] = jnp.zeros(l_ref.shape, F32)
            a_ref[...] = jnp.zeros(a_ref.shape, F32)

    def scores(qi, j, s_ref):
        q = q_ref[rows(qi), :]
        kk = k_ref[rows(j), :]
        for c in range(2):
            s_ref[c, :, :blk] = lax.dot_general(
                q[:, c * HEAD_DIM:(c + 1) * HEAD_DIM], kk[:, c * HEAD_DIM:(c + 1) * HEAD_DIM],
                (((1,), (1,)), ((), ())), preferred_element_type=F32)

    def softmax_pv(j, s_ref, masked):
        vv = v_ref[rows(j), :]
        for c, (m_ref, l_ref, a_ref) in enumerate(stats):
            for r in range(0, blk, ATTN_ROW_BLOCK):
                rb = slice(r, r + ATTN_ROW_BLOCK)
                s = s_ref[c, rb, :blk]
                if masked:
                    row = r + lax.broadcasted_iota(jnp.int32, s.shape, 0)
                    col = lax.broadcasted_iota(jnp.int32, s.shape, 1)
                    s = jnp.where(col <= row, s, -jnp.inf)
                m_old = m_ref[rb, :]
                m_new = jnp.maximum(m_old, jnp.max(s, axis=-1, keepdims=True))
                alpha = jnp.exp2(m_old - m_new)
                p = jnp.exp2(s - m_new)
                part = p[:, :LANES]
                for u in range(1, blk // LANES):
                    part = part + p[:, u * LANES:(u + 1) * LANES]
                l_ref[rb, :] = alpha * l_ref[rb, :] + part
                p_ref[c, rb, :] = p.astype(BF16)
                alpha_ref[c, rb, :] = alpha
                m_ref[rb, :] = m_new
            a_ref[...] = alpha_ref[c] * a_ref[...] + jnp.dot(p_ref[c], vv, preferred_element_type=F32)

    def finalize(qi):
        l1 = jnp.sum(l1_ref[...], axis=-1, keepdims=True)
        l2 = jnp.sum(l2_ref[...], axis=-1, keepdims=True)
        o = a1_ref[...] / l1 - lam * (a2_ref[...] / l2)
        ms = jnp.mean(o * o, axis=-1, keepdims=True)
        y = o * lax.rsqrt(ms + EPS) * g_ref[...] * (1.0 - lam_init)
        o_ref[rows(qi), :] = y.astype(o_ref.dtype)

    def items(n, *work):
        for parity in range(2):
            @pl.when(n % 2 == parity)
            def _():
                for step, (next_qi, next_j, j, masked) in enumerate(work):
                    mine = (parity + step) % 2
                    scores(next_qi, next_j, s_bufs[1 - mine])
                    softmax_pv(j, s_bufs[mine], masked)

    init_stats()
    scores(0, 0, s_bufs[0])

    def q_block(qi, carry):
        base = (qi * (qi + 1)) // 2
        diagonal = (jnp.minimum(qi + 1, nq - 1), 0, qi, True)

        def full_chunk(j, c):
            items(base + j, (qi, j + 1, j, False))
            return c

        lax.fori_loop(0, qi, full_chunk, 0)
        items(base + qi, diagonal)
        finalize(qi)
        init_stats()
        return carry

    lax.fori_loop(0, nq, q_block, 0)


def _diff_attention(qkv, lam_vecs, subln_g, bsz, lp, d, lam_init):
    t = qkv.shape[0]
    n_heads = d // (2 * HEAD_DIM)
    hw = 2 * HEAD_DIM
    blk = _pick_tile(lp, 768, SEQ_ALIGN)
    kern = functools.partial(_attn_kernel, blk=blk, lam_init=lam_init)
    return pl.pallas_call(
        kern,
        grid=(bsz, n_heads),
        in_specs=[pl.BlockSpec((lp, hw), lambda b, h: (b, h)),
                  pl.BlockSpec((lp, hw), lambda b, h: (b, n_heads + h)),
                  pl.BlockSpec((lp, hw), lambda b, h: (b, 2 * n_heads + h)),
                  pl.BlockSpec((4, HEAD_DIM), lambda b, h: (0, 0)),
                  pl.BlockSpec((1, hw), lambda b, h: (0, 0))],
        out_specs=pl.BlockSpec((lp, hw), lambda b, h: (b, h)),
        out_shape=jax.ShapeDtypeStruct((t, d), BF16),
        scratch_shapes=[pltpu.VMEM((blk, 1), F32), pltpu.VMEM((blk, LANES), F32), pltpu.VMEM((blk, hw), F32),
                        pltpu.VMEM((blk, 1), F32), pltpu.VMEM((blk, LANES), F32), pltpu.VMEM((blk, hw), F32),
                        pltpu.VMEM((2, blk, blk + LANES), F32), pltpu.VMEM((2, blk, blk + LANES), F32),
                        pltpu.VMEM((2, blk, blk), BF16), pltpu.VMEM((2, blk, 1), F32)],
        compiler_params=_params("parallel", "parallel"),
        name="diff_attention",
    )(qkv, qkv, qkv, lam_vecs.astype(F32), subln_g.reshape(1, hw).astype(F32))


def _col_panels(n):
    return [slice(u, u + MXU_COLS) for u in range(0, n, MXU_COLS)]


def _emit_norm_inputs(h, cols, hb_ref, ssq_ref):
    hb_ref[:, cols] = h.astype(hb_ref.dtype)
    ssq_ref[...] += jnp.sum(h * h, axis=-1, keepdims=True)


def _proj_residual_kernel(a_ref, w_ref, r_ref, o_ref, hb_ref, ssq_ref):
    @pl.when(pl.program_id(1) == 0)
    def _():
        ssq_ref[...] = jnp.zeros(ssq_ref.shape, F32)

    for cols in _col_panels(o_ref.shape[1]):
        h = r_ref[:, cols] + jnp.dot(a_ref[...], w_ref[:, cols], preferred_element_type=F32)
        o_ref[:, cols] = h
        _emit_norm_inputs(h, cols, hb_ref, ssq_ref)


def _norm_out_shapes(t, n):
    return (jax.ShapeDtypeStruct((t, n), F32), jax.ShapeDtypeStruct((t, n), BF16),
            jax.ShapeDtypeStruct((t, 1), F32))


def _proj_residual(a, w, res):
    t, k = a.shape
    n = w.shape[1]
    tm = _row_tile(t)
    tn = _pick_tile(n, 512, LANES)
    return pl.pallas_call(
        _proj_residual_kernel,
        grid=(t // tm, n // tn),
        in_specs=[pl.BlockSpec((tm, k), lambda i, j: (i, 0)),
                  pl.BlockSpec((k, tn), lambda i, j: (0, j)),
                  pl.BlockSpec((tm, tn), lambda i, j: (i, j))],
        out_specs=(pl.BlockSpec((tm, tn), lambda i, j: (i, j)),
                   pl.BlockSpec((tm, tn), lambda i, j: (i, j)),
                   pl.BlockSpec((tm, 1), lambda i, j: (i, 0))),
        out_shape=_norm_out_shapes(t, n),
        compiler_params=_params("parallel", "arbitrary"),
        name="proj_residual",
    )(a, w, res)


def _mlp_up_kernel(*refs, inv_d, side):
    a_ref, w_ref, ssq_ref = refs[:3]
    side_in = refs[3:3 + side.n_in]
    o_ref = refs[3 + side.n_in]
    side_out = refs[4 + side.n_in:]
    r2 = 1.0 / (ssq_ref[...] * inv_d + EPS)
    side.run(side_in, side_out)
    for cols in _col_panels(o_ref.shape[1]):
        u = jnp.dot(a_ref[...], w_ref[:, cols], preferred_element_type=F32)
        o_ref[:, cols] = (jnp.square(jnp.maximum(u, 0.0)) * r2).astype(o_ref.dtype)


def _mlp_up(hb, w, ssq, cast_jobs):
    t, k = hb.shape
    n = w.shape[1]
    tm = _row_tile(t)
    tn = _pick_tile(n, 512, LANES)
    nj = n // tn
    side = _SideCasts(cast_jobs, (t // tm) * nj, lambda i, j: i * nj + j)
    return pl.pallas_call(
        functools.partial(_mlp_up_kernel, inv_d=1.0 / k, side=side),
        grid=(t // tm, nj),
        in_specs=[pl.BlockSpec((tm, k), lambda i, j: (i, 0)),
                  pl.BlockSpec((k, tn), lambda i, j: (0, j)),
                  pl.BlockSpec((tm, 1), lambda i, j: (i, 0))] + side.in_specs,
        out_specs=[pl.BlockSpec((tm, tn), lambda i, j: (i, j))] + side.out_specs,
        out_shape=[jax.ShapeDtypeStruct((t, n), BF16)] + side.out_shapes,
        compiler_params=_params("arbitrary", "arbitrary"),
        name="mlp_up",
    )(hb, w, ssq, *side.operands)


def _mlp_down_kernel(a_ref, w_ref, r_ref, o_ref, *norm_refs, nk, col_axis=1):
    j = pl.program_id(col_axis)
    k = pl.program_id(col_axis + 1)

    def accumulate(base_ref, last):
        for cols in _col_panels(o_ref.shape[1]):
            h = base_ref[:, cols] + jnp.dot(a_ref[...], w_ref[:, cols], preferred_element_type=F32)
            o_ref[:, cols] = h
            if last and norm_refs:
                _emit_norm_inputs(h, cols, *norm_refs)

    if norm_refs:
        @pl.when(jnp.logical_and(j == 0, k == 0))
        def _():
            norm_refs[1][...] = jnp.zeros(norm_refs[1].shape, F32)

    if nk == 1:
        accumulate(r_ref, True)
        return

    @pl.when(k == 0)
    def _():
        accumulate(r_ref, False)

    @pl.when(jnp.logical_and(k > 0, k < nk - 1))
    def _():
        accumulate(o_ref, False)

    @pl.when(k == nk - 1)
    def _():
        accumulate(o_ref, True)


def _mlp_down(a, w, res, emit_norm):
    t, kdim = a.shape
    n = w.shape[1]
    tm = _row_tile(t)
    tn = _pick_tile(n, 1024, LANES)
    tk = _pick_tile(kdim, 2048, LANES)
    nk = kdim // tk
    tile = pl.BlockSpec((tm, tn), lambda i, j, k: (i, j))
    if emit_norm:
        out_specs = (tile, tile, pl.BlockSpec((tm, 1), lambda i, j, k: (i, 0)))
        out_shape = _norm_out_shapes(t, n)
    else:
        out_specs = tile
        out_shape = jax.ShapeDtypeStruct((t, n), F32)
    return pl.pallas_call(
        functools.partial(_mlp_down_kernel, nk=nk),
        grid=(t // tm, n // tn, nk),
        in_specs=[pl.BlockSpec((tm, tk), lambda i, j, k: (i, k)),
                  pl.BlockSpec((tk, tn), lambda i, j, k: (k, j)),
                  tile],
        out_specs=out_specs,
        out_shape=out_shape,
        compiler_params=_params("parallel", "arbitrary", "arbitrary"),
        name="mlp_down",
    )(a, w, res)


def _mlp_down_final(a, w, res, bsz, lp, row0, seq):
    kdim = a.shape[1]
    n = w.shape[1]
    tm = _pick_tile(seq, 1024, 16)
    nt = seq // tm
    tn = _pick_tile(n, 1024, LANES)
    tk = _pick_tile(kdim, 4096, LANES)
    nk = kdim // tk

    assert lp % 16 == 0 and row0 % 16 == 0 and tm % 16 == 0

    def row_start(b, m):
        return pl.multiple_of(b * lp + row0 + m * tm, 16)

    return pl.pallas_call(
        functools.partial(_mlp_down_kernel, nk=nk, col_axis=2),
        grid=(bsz, nt, n // tn, nk),
        in_specs=[pl.BlockSpec((pl.Element(tm), pl.Element(tk)), lambda b, m, j, k: (row_start(b, m), k * tk)),
                  pl.BlockSpec((tk, tn), lambda b, m, j, k: (k, j)),
                  pl.BlockSpec((pl.Element(tm), pl.Element(tn)), lambda b, m, j, k: (row_start(b, m), j * tn))],
        out_specs=pl.BlockSpec((tm, tn), lambda b, m, j, k: (b * nt + m, j)),
        out_shape=jax.ShapeDtypeStruct((bsz * seq, n), F32),
        compiler_params=_params("parallel", "parallel", "arbitrary", "arbitrary"),
        name="mlp_down_final",
    )(a, w, res)


def _glu_kernel(a_ref, wv_ref, wg_ref, bv_ref, bg_ref, r_ref, o_ref, hb_ref, ssq_ref):
    @pl.when(pl.program_id(1) == 0)
    def _():
        ssq_ref[...] = jnp.zeros(ssq_ref.shape, F32)

    for cols in _col_panels(o_ref.shape[1]):
        val = jnp.dot(a_ref[...], wv_ref[:, cols], preferred_element_type=F32) + bv_ref[:, cols]
        gate = jnp.dot(a_ref[...], wg_ref[:, cols], preferred_element_type=F32) + bg_ref[:, cols]
        h = r_ref[:, cols] + val * jax.nn.sigmoid(gate)
        o_ref[:, cols] = h
        _emit_norm_inputs(h, cols, hb_ref, ssq_ref)


def _glu_residual(a, w, bias, res):
    t, k = a.shape
    n = w.shape[1] // 2
    tm = _row_tile(t)
    tn = _pick_tile(n, 256, LANES)
    nb = n // tn
    b2 = bias.reshape(1, 2 * n).astype(F32)
    return pl.pallas_call(
        _glu_kernel,
        grid=(t // tm, nb),
        in_specs=[pl.BlockSpec((tm, k), lambda i, j: (i, 0)),
                  pl.BlockSpec((k, tn), lambda i, j: (0, j)),
                  pl.BlockSpec((k, tn), lambda i, j: (0, nb + j)),
                  pl.BlockSpec((1, tn), lambda i, j: (0, j)),
                  pl.BlockSpec((1, tn), lambda i, j: (0, nb + j)),
                  pl.BlockSpec((tm, tn), lambda i, j: (i, j))],
        out_specs=(pl.BlockSpec((tm, tn), lambda i, j: (i, j)),
                   pl.BlockSpec((tm, tn), lambda i, j: (i, j)),
                   pl.BlockSpec((tm, 1), lambda i, j: (i, 0))),
        out_shape=_norm_out_shapes(t, n),
        compiler_params=_params("parallel", "arbitrary"),
        name="glu_residual",
    )(a, w, w, b2, b2, res)


def _s5_kernel(x_ref, ssq_ref, g_ref, w0_ref, cm_ref, pr_ref, pi_ref, prc_ref, pic_ref, d_ref, o_ref,
               m_scr, wst_scr, wout_scr, s_scr, xin_scr, *, n_steps, inv_d):
    q = SSM_CHUNK
    half = w0_ref.shape[2] // 2

    @pl.when(pl.program_id(1) == 0)
    def _build_weights():
        w0 = w0_ref[0]
        w0re, w0im = w0[:, :half], w0[:, half:]
        cm = cm_ref[0]
        cm_bf = cm.astype(BF16)
        row = lax.broadcasted_iota(jnp.int32, (LANES, LANES), 0)
        col = lax.broadcasted_iota(jnp.int32, (LANES, LANES), 1)
        skip = jnp.where(row == col, jnp.broadcast_to(d_ref[...], (LANES, LANES)), 0.0)
        zero_blk = jnp.zeros((LANES, LANES), BF16)
        for tau in range(q):
            ar = pr_ref[0, tau:tau + 1, :]
            ai = pi_ref[0, tau:tau + 1, :]
            w_tau = jnp.concatenate([w0re * ar - w0im * ai, w0im * ar + w0re * ai], axis=1).astype(BF16)
            t_st = q - 1 - tau
            wst_scr[t_st * LANES:(t_st + 1) * LANES, :] = w_tau
            k_tau = jnp.dot(w_tau, cm_bf, preferred_element_type=F32)
            if tau == 0:
                k_tau = k_tau + skip
            k_bf = k_tau.astype(BF16)
            for t in range(q - tau):
                m_scr[t * LANES:(t + 1) * LANES, (t + tau) * LANES:(t + tau + 1) * LANES] = k_bf
        for t in range(q):
            for t2 in range(t):
                m_scr[t * LANES:(t + 1) * LANES, t2 * LANES:(t2 + 1) * LANES] = zero_blk
        cre, cimn = cm[:half], cm[half:]
        for t in range(q):
            arc = prc_ref[0, :, t + 1:t + 2]
            aic = pic_ref[0, :, t + 1:t + 2]
            wout_scr[:half, t * LANES:(t + 1) * LANES] = (arc * cre + aic * cimn).astype(BF16)
            wout_scr[half:, t * LANES:(t + 1) * LANES] = (arc * cimn - aic * cre).astype(BF16)

    gain = g_ref[...]
    u = jnp.concatenate(
        [(x_ref[t].astype(F32) * lax.rsqrt(ssq_ref[t] * inv_d + EPS) * gain).astype(BF16) for t in range(q)],
        axis=1)
    s_scr[...] = jnp.dot(u, wst_scr[...], preferred_element_type=F32)

    shp = (SUBLANES, half)
    ar = jnp.broadcast_to(pr_ref[0, q:q + 1, :], shp)
    ai = jnp.broadcast_to(pi_ref[0, q:q + 1, :], shp)
    anr = jnp.broadcast_to(pr_ref[0, q + 1:q + 2, :], shp)
    ani = jnp.broadcast_to(pi_ref[0, q + 1:q + 2, :], shp)
    seg = lax.broadcasted_iota(jnp.int32, shp, 0)
    zeros = jnp.zeros(shp, F32)

    def shift_down(x):
        return jnp.where(seg == 0, 0.0, pltpu.roll(x, 1, 0))

    def advance(j, cr, ci):
        sj = s_scr[pl.ds(pl.multiple_of(j * SUBLANES, SUBLANES), SUBLANES), :]
        return ar * cr - ai * ci + sj[:, :half], ar * ci + ai * cr + sj[:, half:]

    er, ei = lax.fori_loop(0, n_steps, lambda j, c: advance(j, *c), (zeros, zeros))
    tr, ti = er, ei
    for _ in range(SSM_SEGMENTS - 1):
        sr, si = shift_down(tr), shift_down(ti)
        tr, ti = er + anr * sr - ani * si, ei + anr * si + ani * sr
    cr0, ci0 = shift_down(tr), shift_down(ti)

    def scan_store(j, c):
        cr, ci = c
        xin_scr[pl.ds(pl.multiple_of(j * SUBLANES, SUBLANES), SUBLANES), :] = jnp.concatenate([cr, ci], axis=1)
        return advance(j, cr, ci)

    lax.fori_loop(0, n_steps, scan_store, (cr0, ci0))

    xin = xin_scr[...].astype(BF16)
    steps_per_panel = 4
    for t0 in range(0, q, steps_per_panel):
        k_hi = (t0 + steps_per_panel) * LANES
        cols = slice(t0 * LANES, k_hi)
        y = (jnp.dot(u[:, :k_hi], m_scr[:k_hi, cols], preferred_element_type=F32)
             + jnp.dot(xin, wout_scr[:, cols], preferred_element_type=F32))
        z = jax.nn.gelu(y)
        for t in range(steps_per_panel):
            o_ref[t0 + t] = z[:, t * LANES:(t + 1) * LANES].astype(o_ref.dtype)


def _complex_pow(zr, zi, n):
    rr, ri = None, None
    br, bi = zr, zi
    while n:
        if n & 1:
            rr, ri = (br, bi) if rr is None else (rr * br - ri * bi, rr * bi + ri * br)
        n >>= 1
        if n:
            br, bi = br * br - bi * bi, 2.0 * br * bi
    return rr, ri


def _s5_tables(a_re, a_im, log_dt, b_re, b_im, c_re, c_im, n_steps):
    g, p = a_re.shape
    gpb = LANES // SSM_GROUP
    nblk = g // gpb
    dt = jnp.exp(log_dt.astype(F32))[:, None]
    ar = a_re.astype(F32)
    ai = a_im.astype(F32)
    mag = jnp.exp(dt * ar)
    ang = dt * ai
    abar_re = mag * jnp.cos(ang)
    abar_im = mag * jnp.sin(ang)
    nr = abar_re - 1.0
    ni = abar_im
    den = ar * ar + ai * ai
    f_re = (nr * ar + ni * ai) / den
    f_im = (ni * ar - nr * ai) / den
    br = b_re.astype(F32)
    bi = b_im.astype(F32)
    bb_re = f_re[..., None] * br - f_im[..., None] * bi
    bb_im = f_re[..., None] * bi + f_im[..., None] * br

    pw_r, pw_i = [jnp.ones_like(abar_re)], [jnp.zeros_like(abar_re)]
    for _ in range(SSM_CHUNK):
        pw_r.append(pw_r[-1] * abar_re - pw_i[-1] * abar_im)
        pw_i.append(pw_r[-2] * abar_im + pw_i[-1] * abar_re)
    seg_r, seg_i = _complex_pow(pw_r[-1], pw_i[-1], n_steps)
    pw_r.append(seg_r)
    pw_i.append(seg_i)
    n_rows = 24
    pr = jnp.stack(pw_r, 0).reshape(len(pw_r), nblk, gpb * p).transpose(1, 0, 2)
    pi = jnp.stack(pw_i, 0).reshape(len(pw_i), nblk, gpb * p).transpose(1, 0, 2)
    pad = ((0, 0), (0, n_rows - pr.shape[1]), (0, 0))
    pr = jnp.pad(pr, pad)
    pi = jnp.pad(pi, pad)
    prc = pr.transpose(0, 2, 1)
    pic = pi.transpose(0, 2, 1)

    eye = jnp.eye(gpb, dtype=F32)

    def in_to_state(bb):
        v = bb.reshape(nblk, gpb, p, SSM_GROUP).transpose(0, 1, 3, 2)
        e = v[:, :, :, None, :] * eye[None, :, None, :, None]
        return e.reshape(nblk, LANES, gpb * p)

    def state_to_out(c):
        v = c.astype(F32).reshape(nblk, gpb, SSM_GROUP, p).transpose(0, 1, 3, 2)
        e = v[:, :, :, None, :] * eye[None, :, None, :, None]
        return e.reshape(nblk, gpb * p, LANES)

    w0 = jnp.concatenate([in_to_state(bb_re), in_to_state(bb_im)], axis=2)
    cm = jnp.concatenate([state_to_out(c_re), -state_to_out(c_im)], axis=1)
    return w0, cm, pr, pi, prc, pic


def _s5_mixer(hb, ssq, gain, bsz, lp, a_re, a_im, log_dt, b_re, b_im, c_re, c_im, d_skip):
    t, d = hb.shape
    q, nseg = SSM_CHUNK, SSM_SEGMENTS
    assert lp % (q * nseg * 2) == 0
    n_steps = lp // (q * nseg)
    rows = n_steps * nseg

    def chunk_layout(v):
        w = v.shape[-1]
        return v.reshape(bsz, nseg, n_steps, q, w).transpose(3, 0, 2, 1, 4).reshape(q, bsz * rows, w)

    x = chunk_layout(hb)
    ssq_c = chunk_layout(ssq)

    w0, cm, pr, pi, prc, pic = _s5_tables(a_re, a_im, log_dt, b_re, b_im, c_re, c_im, n_steps)
    nblk = d // LANES
    n_state = w0.shape[2]
    kern = functools.partial(_s5_kernel, n_steps=n_steps, inv_d=1.0 / d)
    z = pl.pallas_call(
        kern,
        grid=(nblk, bsz),
        in_specs=[pl.BlockSpec((q, rows, LANES), lambda k, b: (0, b, k)),
                  pl.BlockSpec((q, rows, 1), lambda k, b: (0, b, 0)),
                  pl.BlockSpec((1, LANES), lambda k, b: (0, k)),
                  pl.BlockSpec((1, LANES, n_state), lambda k, b: (k, 0, 0)),
                  pl.BlockSpec((1, n_state, LANES), lambda k, b: (k, 0, 0)),
                  pl.BlockSpec((1,) + pr.shape[1:], lambda k, b: (k, 0, 0)),
                  pl.BlockSpec((1,) + pi.shape[1:], lambda k, b: (k, 0, 0)),
                  pl.BlockSpec((1,) + prc.shape[1:], lambda k, b: (k, 0, 0)),
                  pl.BlockSpec((1,) + pic.shape[1:], lambda k, b: (k, 0, 0)),
                  pl.BlockSpec((1, LANES), lambda k, b: (0, k))],
        out_specs=pl.BlockSpec((q, rows, LANES), lambda k, b: (0, b, k)),
        out_shape=jax.ShapeDtypeStruct((q, bsz * rows, d), BF16),
        scratch_shapes=[pltpu.VMEM((q * LANES, q * LANES), BF16),
                        pltpu.VMEM((q * LANES, n_state), BF16),
                        pltpu.VMEM((n_state, q * LANES), BF16),
                        pltpu.VMEM((rows, n_state), F32),
                        pltpu.VMEM((rows, n_state), F32)],
        compiler_params=_params("arbitrary", "arbitrary"),
        name="s5_mixer",
    )(x, ssq_c, gain.reshape(1, d).astype(F32), w0, cm, pr, pi, prc, pic, d_skip.reshape(1, d).astype(F32))
    return z.reshape(q, bsz, n_steps, nseg, d).transpose(1, 3, 2, 0, 4).reshape(t, d)


def _rope_tables(length):
    inv = ROPE_THETA ** (-jnp.arange(0, HEAD_DIM, 2, dtype=F32) / HEAD_DIM)
    ang = jnp.arange(length, dtype=F32)[:, None] * inv[None, :]
    cos, sin = jnp.cos(ang), jnp.sin(ang)
    return jnp.concatenate([cos, cos], axis=1), jnp.concatenate([-sin, sin], axis=1)


def _lambda_init(layer_idx):
    return 0.8 - 0.6 * math.exp(-0.3 * layer_idx)


def kernel(x, meta_tokens, norm_mix_g, norm_mlp_g, da_w_qkv, da_q_norm_g, da_k_norm_g, da_lambda, da_subln_g, da_w_o, ssm_a_re, ssm_a_im, ssm_log_dt, ssm_b_re, ssm_b_im, ssm_c_re, ssm_c_im, ssm_d, ssm_w_glu, ssm_b_glu, mlp_w_up, mlp_w_down):
    bsz, seq, d = x.shape
    depth = norm_mix_g.shape[0]
    n_mixers = 2
    length = N_META + seq
    lp = ((length + SEQ_ALIGN - 1) // SEQ_ALIGN) * SEQ_ALIGN
    assert meta_tokens.shape[0] == N_META and depth >= 1
    h, hn = _embed_norm(x, meta_tokens, norm_mix_g[0], lp)
    cos, sin = (jnp.tile(tab, (bsz, 1)) for tab in _rope_tables(lp))
    def up_job(i):
        return mlp_w_up, i, norm_mlp_g[i]

    def bf16_weight(name, stack, layer, gain=None):
        if name not in wb:
            src = stack[layer]
            wb[name] = (src if gain is None else src * gain.astype(F32)[:, None]).astype(BF16)
        return wb[name]

    wb = {}
    hb = ssq = None
    for i in range(depth):
        j = i // n_mixers
        nxt_s5 = i + 1 < depth and (i + 1) % n_mixers == 1
        if i % n_mixers == 0:
            if i > 0:
                hn = _rmsnorm(h, norm_mix_g[i])
            qkv, wb["o", j], wb["up", i] = _qkv_proj(
                hn, bf16_weight(("qkv", j), da_w_qkv, j), cos, sin, da_q_norm_g[j], da_k_norm_g[j],
                [(da_w_o, j, None), up_job(i)])
            att = _diff_attention(qkv, da_lambda[j], da_subln_g[j], bsz, lp, d, _lambda_init(i))
            h, hb, ssq = _proj_residual(att, wb["o", j], h)
        else:
            z = _s5_mixer(hb, ssq, norm_mix_g[i], bsz, lp, ssm_a_re[j], ssm_a_im[j], ssm_log_dt[j],
                          ssm_b_re[j], ssm_b_im[j], ssm_c_re[j], ssm_c_im[j], ssm_d[j])
            h, hb, ssq = _glu_residual(z, bf16_weight(("glu", j), ssm_w_glu, j), ssm_b_glu[j], h)
        jobs = [(("down", i), mlp_w_down, i, None)]
        if nxt_s5:
            jobs += [(("glu", (i + 1) // n_mixers), ssm_w_glu, (i + 1) // n_mixers, None),
                     (("up", i + 1),) + up_job(i + 1)]
        f, *casts = _mlp_up(hb, bf16_weight(("up", i), *up_job(i)), ssq, [job[1:] for job in jobs])
        wb.update({job[0]: c for job, c in zip(jobs, casts)})
        if i == depth - 1:
            return _mlp_down_final(f, wb["down", i], h, bsz, lp, N_META, seq).reshape(bsz, seq, d)
        if nxt_s5:
            h, hb, ssq = _mlp_down(f, wb["down", i], h, True)
        else:
            h = _mlp_down(f, wb["down", i], h, False)
```

```python
import functools
import math

import jax
import jax.numpy as jnp
from jax import lax
from jax.experimental import pallas as pl
from jax.experimental.pallas import tpu as pltpu

N_META = 16
SEQ_ALIGN = 256
HEAD_DIM = 128
LOG2E = 1.4426950408889634
ATTN_ROW_BLOCK = 64
ROPE_THETA = 10000.0
SSM_GROUP = 16
SSM_CHUNK = 16
SSM_SEGMENTS = 8
MXU_COLS = 256
LANES = 128
SUBLANES = 8
EPS = 1e-6
VMEM_LIMIT_BYTES = 56 * 1024 * 1024

F32 = jnp.float32
BF16 = jnp.bfloat16


def _pick_tile(n, target, mult):
    best = None
    for t in range(mult, min(n, target) + 1, mult):
        if n % t == 0:
            best = t
    assert best is not None, (n, target, mult)
    return best


def _row_tile(t):
    return _pick_tile(t, 1536, SEQ_ALIGN)


def _params(*sem):
    return pltpu.CompilerParams(dimension_semantics=sem, vmem_limit_bytes=VMEM_LIMIT_BYTES)


class _SideCasts:
    def __init__(self, jobs, n_steps, step_of):
        self.operands, self.in_specs, self.out_specs, self.out_shapes, self.has_gain = [], [], [], [], []
        for src, layer, gain in jobs:
            _, r, c = src.shape
            cr = next(x for x in (16 << p for p in range(24)) if r % x == 0 and r // x <= n_steps)

            def idx(*g, last=r // cr - 1):
                return (jnp.minimum(step_of(*g), last), 0)

            def src_idx(*g, layer=layer, last=r // cr - 1):
                return (layer, jnp.minimum(step_of(*g), last), 0)

            self.operands.append(src)
            self.in_specs.append(pl.BlockSpec((None, cr, c), src_idx))
            if gain is not None:
                self.operands.append(gain.reshape(r, 1).astype(F32))
                self.in_specs.append(pl.BlockSpec((cr, 1), idx))
            self.out_specs.append(pl.BlockSpec((cr, c), idx))
            self.out_shapes.append(jax.ShapeDtypeStruct((r, c), BF16))
            self.has_gain.append(gain is not None)
        self.n_in = len(self.operands)
        self.n_out = len(self.out_shapes)

    def run(self, in_refs, out_refs):
        refs = iter(in_refs)
        for has_gain, o_ref in zip(self.has_gain, out_refs):
            v = next(refs)[...]
            if has_gain:
                v = v * next(refs)[...]
            o_ref[...] = v.astype(o_ref.dtype)


def _rmsnorm_kernel(x_ref, g_ref, o_ref):
    x = x_ref[...]
    ms = jnp.mean(x * x, axis=-1, keepdims=True)
    o_ref[...] = (x * lax.rsqrt(ms + EPS) * g_ref[...]).astype(o_ref.dtype)


def _rmsnorm(x, g):
    t, d = x.shape
    tr = _pick_tile(t, 256, 16)
    return pl.pallas_call(
        _rmsnorm_kernel,
        grid=(t // tr,),
        in_specs=[pl.BlockSpec((tr, d), lambda i: (i, 0)),
                  pl.BlockSpec((1, d), lambda i: (0, 0))],
        out_specs=pl.BlockSpec((tr, d), lambda i: (i, 0)),
        out_shape=jax.ShapeDtypeStruct((t, d), BF16),
        compiler_params=_params("parallel"),
        name="rmsnorm",
    )(x, g.reshape(1, d).astype(F32))


def _embed_norm_kernel(x_ref, meta_ref, g_ref, h_ref, hn_ref, *, n_meta, seq):
    tr = h_ref.shape[0]
    i = pl.program_id(1)

    @pl.when(i == 0)
    def _():
        h_ref[:n_meta, :] = meta_ref[...]
        h_ref[n_meta:, :] = x_ref[:tr - n_meta, :]

    @pl.when(i > 0)
    def _():
        row = i * tr + lax.broadcasted_iota(jnp.int32, h_ref.shape, 0)
        h_ref[...] = jnp.where(row < n_meta + seq, x_ref[...], 0.0)

    h = h_ref[...]
    ms = jnp.mean(h * h, axis=-1, keepdims=True)
    hn_ref[...] = (h * lax.rsqrt(ms + EPS) * g_ref[...]).astype(hn_ref.dtype)


def _embed_norm(x, meta_tokens, g, lp):
    bsz, seq, d = x.shape
    n_meta = meta_tokens.shape[0]
    tr = _pick_tile(lp, 256, 16)
    assert n_meta % 16 == 0 and n_meta < tr and tr < seq
    nt = lp // tr
    tile = pl.BlockSpec((tr, d), lambda b, i: (b * nt + i, 0))

    def x_window(b, i):
        return (b, pl.multiple_of(jnp.maximum(i * tr - n_meta, 0), 16), 0)

    return pl.pallas_call(
        functools.partial(_embed_norm_kernel, n_meta=n_meta, seq=seq),
        grid=(bsz, nt),
        in_specs=[pl.BlockSpec((pl.Squeezed(), pl.Element(tr, (0, lp - n_meta - seq)), pl.Element(d)), x_window),
                  pl.BlockSpec((n_meta, d), lambda b, i: (0, 0)),
                  pl.BlockSpec((1, d), lambda b, i: (0, 0))],
        out_specs=(tile, tile),
        out_shape=(jax.ShapeDtypeStruct((bsz * lp, d), F32), jax.ShapeDtypeStruct((bsz * lp, d), BF16)),
        compiler_params=_params("parallel", "arbitrary"),
        name="embed_norm",
    )(x, meta_tokens.astype(F32), g.reshape(1, d).astype(F32))


def _qkv_kernel(*refs, n_q_blocks, n_col_blocks, q_scale, side):
    a_ref, w_ref, cos_ref, sin_ref, qg_ref, kg_ref = refs[:6]
    side_in = refs[6:6 + side.n_in]
    o_ref = refs[6 + side.n_in]
    side_out = refs[7 + side.n_in:7 + side.n_in + side.n_out]
    accs = refs[7 + side.n_in + side.n_out:]
    j = pl.program_id(1)
    tn = o_ref.shape[1]

    def matmul(acc_ref):
        acc_ref[...] = jnp.dot(a_ref[...], w_ref[...], preferred_element_type=F32)

    def norm_rope(acc_ref, g, scale):
        cos = cos_ref[...]
        sin = sin_ref[...]
        for u in range(tn // HEAD_DIM):
            x = acc_ref[:, u * HEAD_DIM:(u + 1) * HEAD_DIM]
            ms = jnp.mean(x * x, axis=-1, keepdims=True)
            y = x * lax.rsqrt(ms + EPS) * g
            y = y * cos + pltpu.roll(y, HEAD_DIM // 2, 1) * sin
            if scale != 1.0:
                y = y * scale
            o_ref[:, u * HEAD_DIM:(u + 1) * HEAD_DIM] = y.astype(o_ref.dtype)

    def finish(kind, acc_ref):
        if kind == "q":
            norm_rope(acc_ref, qg_ref[...], q_scale)
        elif kind == "k":
            norm_rope(acc_ref, kg_ref[...], 1.0)
        else:
            o_ref[...] = acc_ref[...].astype(o_ref.dtype)

    prev = j - 1
    kinds = (("q", prev < n_q_blocks),
             ("k", jnp.logical_and(prev >= n_q_blocks, prev < 2 * n_q_blocks)),
             ("v", prev >= 2 * n_q_blocks))

    @pl.when(j == 0)
    def _():
        matmul(accs[0])
        side.run(side_in, side_out)

    for parity in range(2):
        for kind, is_kind in kinds:
            @pl.when(jnp.logical_and(jnp.logical_and(j >= 1, j < n_col_blocks),
                                     jnp.logical_and(j % 2 == parity, is_kind)))
            def _():
                matmul(accs[parity])
                finish(kind, accs[1 - parity])
                side.run(side_in, side_out)

    @pl.when(j == n_col_blocks)
    def _():
        finish("v", accs[(n_col_blocks - 1) % 2])
        side.run(side_in, side_out)


def _qkv_proj(a, w, cos, sin, q_g, k_g, cast_jobs):
    t, d = a.shape
    n = w.shape[1]
    tm = _row_tile(t)
    tn = _pick_tile(d, 512, HEAD_DIM)
    nj = n // tn
    assert n == 3 * d
    n_steps = nj + 1
    side = _SideCasts(cast_jobs, (t // tm) * n_steps, lambda i, j: i * n_steps + j)
    kern = functools.partial(_qkv_kernel, n_q_blocks=d // tn, n_col_blocks=nj,
                             q_scale=HEAD_DIM ** -0.5 * LOG2E, side=side)
    return pl.pallas_call(
        kern,
        grid=(t // tm, n_steps),
        in_specs=[pl.BlockSpec((tm, d), lambda i, j: (i, 0)),
                  pl.BlockSpec((d, tn), lambda i, j: (0, jnp.minimum(j, nj - 1))),
                  pl.BlockSpec((tm, HEAD_DIM), lambda i, j: (i, 0)),
                  pl.BlockSpec((tm, HEAD_DIM), lambda i, j: (i, 0)),
                  pl.BlockSpec((1, HEAD_DIM), lambda i, j: (0, 0)),
                  pl.BlockSpec((1, HEAD_DIM), lambda i, j: (0, 0))] + side.in_specs,
        out_specs=[pl.BlockSpec((tm, tn), lambda i, j: (i, jnp.maximum(j - 1, 0)))] + side.out_specs,
        out_shape=[jax.ShapeDtypeStruct((t, n), BF16)] + side.out_shapes,
        scratch_shapes=[pltpu.VMEM((tm, tn), F32), pltpu.VMEM((tm, tn), F32)],
        compiler_params=_params("arbitrary", "arbitrary"),
        name="qkv_proj",
    )(a, w, cos, sin, q_g.reshape(1, HEAD_DIM).astype(F32), k_g.reshape(1, HEAD_DIM).astype(F32),
      *side.operands)


def _attn_kernel(q_ref, k_ref, v_ref, lam_ref, g_ref, o_ref,
                 m1_ref, l1_ref, a1_ref, m2_ref, l2_ref, a2_ref, s_even_ref, s_odd_ref, p_ref, alpha_ref,
                 *, blk, lam_init):
    s_bufs = (s_even_ref, s_odd_ref)
    nq = q_ref.shape[0] // blk
    stats = ((m1_ref, l1_ref, a1_ref), (m2_ref, l2_ref, a2_ref))
    lv = lam_ref[...]
    lam = (jnp.exp(jnp.sum(lv[0:1] * lv[1:2], axis=-1, keepdims=True))
           - jnp.exp(jnp.sum(lv[2:3] * lv[3:4], axis=-1, keepdims=True)) + lam_init)

    def rows(i):
        return pl.ds(pl.multiple_of(i * blk, blk), blk)

    def init_stats():
        for m_ref, l_ref, a_ref in stats:
            m_ref[...] = jnp.full(m_ref.shape, -jnp.inf, F32)
            l_ref[...] = jnp.zeros(l_ref.shape, F32)
            a_ref[...] = jnp.zeros(a_ref.shape, F32)

    def scores(qi, j, s_ref):
        q = q_ref[rows(qi), :]
        kk = k_ref[rows(j), :]
        for c in range(2):
            s_ref[c, :, :blk] = lax.dot_general(
                q[:, c * HEAD_DIM:(c + 1) * HEAD_DIM], kk[:, c * HEAD_DIM:(c + 1) * HEAD_DIM],
                (((1,), (1,)), ((), ())), preferred_element_type=F32)

    def softmax_pv(j, s_ref, masked):
        vv = v_ref[rows(j), :]
        for c, (m_ref, l_ref, a_ref) in enumerate(stats):
            for r in range(0, blk, ATTN_ROW_BLOCK):
                rb = slice(r, r + ATTN_ROW_BLOCK)
                s = s_ref[c, rb, :blk]
                if masked:
                    row = r + lax.broadcasted_iota(jnp.int32, s.shape, 0)
                    col = lax.broadcasted_iota(jnp.int32, s.shape, 1)
                    s = jnp.where(col <= row, s, -jnp.inf)
                m_old = m_ref[rb, :]
                m_new = jnp.maximum(m_old, jnp.max(s, axis=-1, keepdims=True))
                alpha = jnp.exp2(m_old - m_new)
                p = jnp.exp2(s - m_new)
                part = p[:, :LANES]
                for u in range(1, blk // LANES):
                    part = part + p[:, u * LANES:(u + 1) * LANES]
                l_ref[rb, :] = alpha * l_ref[rb, :] + part
                p_ref[c, rb, :] = p.astype(BF16)
                alpha_ref[c, rb, :] = alpha
                m_ref[rb, :] = m_new
            a_ref[...] = alpha_ref[c] * a_ref[...] + jnp.dot(p_ref[c], vv, preferred_element_type=F32)

    def finalize(qi):
        l1 = jnp.sum(l1_ref[...], axis=-1, keepdims=True)
        l2 = jnp.sum(l2_ref[...], axis=-1, keepdims=True)
        o = a1_ref[...] / l1 - lam * (a2_ref[...] / l2)
        ms = jnp.mean(o * o, axis=-1, keepdims=True)
        y = o * lax.rsqrt(ms + EPS) * g_ref[...] * (1.0 - lam_init)
        o_ref[rows(qi), :] = y.astype(o_ref.dtype)

    def items(n, *work):
        for parity in range(2):
            @pl.when(n % 2 == parity)
            def _():
                for step, (next_qi, next_j, j, masked) in enumerate(work):
                    mine = (parity + step) % 2
                    scores(next_qi, next_j, s_bufs[1 - mine])
                    softmax_pv(j, s_bufs[mine], masked)

    init_stats()
    scores(0, 0, s_bufs[0])

    def q_block(qi, carry):
        base = (qi * (qi + 1)) // 2
        diagonal = (jnp.minimum(qi + 1, nq - 1), 0, qi, True)

        def full_chunk(j, c):
            items(base + j, (qi, j + 1, j, False))
            return c

        lax.fori_loop(0, qi, full_chunk, 0)
        items(base + qi, diagonal)
        finalize(qi)
        init_stats()
        return carry

    lax.fori_loop(0, nq, q_block, 0)


def _diff_attention(qkv, lam_vecs, subln_g, bsz, lp, d, lam_init):
    t = qkv.shape[0]
    n_heads = d // (2 * HEAD_DIM)
    hw = 2 * HEAD_DIM
    blk = _pick_tile(lp, 768, SEQ_ALIGN)
    kern = functools.partial(_attn_kernel, blk=blk, lam_init=lam_init)
    return pl.pallas_call(
        kern,
        grid=(bsz, n_heads),
        in_specs=[pl.BlockSpec((lp, hw), lambda b, h: (b, h)),
                  pl.BlockSpec((lp, hw), lambda b, h: (b, n_heads + h)),
                  pl.BlockSpec((lp, hw), lambda b, h: (b, 2 * n_heads + h)),
                  pl.BlockSpec((4, HEAD_DIM), lambda b, h: (0, 0)),
                  pl.BlockSpec((1, hw), lambda b, h: (0, 0))],
        out_specs=pl.BlockSpec((lp, hw), lambda b, h: (b, h)),
        out_shape=jax.ShapeDtypeStruct((t, d), BF16),
        scratch_shapes=[pltpu.VMEM((blk, 1), F32), pltpu.VMEM((blk, LANES), F32), pltpu.VMEM((blk, hw), F32),
                        pltpu.VMEM((blk, 1), F32), pltpu.VMEM((blk, LANES), F32), pltpu.VMEM((blk, hw), F32),
                        pltpu.VMEM((2, blk, blk + LANES), F32), pltpu.VMEM((2, blk, blk + LANES), F32),
                        pltpu.VMEM((2, blk, blk), BF16), pltpu.VMEM((2, blk, 1), F32)],
        compiler_params=_params("parallel", "parallel"),
        name="diff_attention",
    )(qkv, qkv, qkv, lam_vecs.astype(F32), subln_g.reshape(1, hw).astype(F32))


def _col_panels(n):
    return [slice(u, u + MXU_COLS) for u in range(0, n, MXU_COLS)]


def _emit_norm_inputs(h, cols, hb_ref, ssq_ref):
    hb_ref[:, cols] = h.astype(hb_ref.dtype)
    ssq_ref[...] += jnp.sum(h * h, axis=-1, keepdims=True)


def _proj_residual_kernel(a_ref, w_ref, r_ref, o_ref, hb_ref, ssq_ref):
    @pl.when(pl.program_id(1) == 0)
    def _():
        ssq_ref[...] = jnp.zeros(ssq_ref.shape, F32)

    for cols in _col_panels(o_ref.shape[1]):
        h = r_ref[:, cols] + jnp.dot(a_ref[...], w_ref[:, cols], preferred_element_type=F32)
        o_ref[:, cols] = h
        _emit_norm_inputs(h, cols, hb_ref, ssq_ref)


def _norm_out_shapes(t, n):
    return (jax.ShapeDtypeStruct((t, n), F32), jax.ShapeDtypeStruct((t, n), BF16),
            jax.ShapeDtypeStruct((t, 1), F32))


def _proj_residual(a, w, res):
    t, k = a.shape
    n = w.shape[1]
    tm = _row_tile(t)
    tn = _pick_tile(n, 512, LANES)
    return pl.pallas_call(
        _proj_residual_kernel,
        grid=(t // tm, n // tn),
        in_specs=[pl.BlockSpec((tm, k), lambda i, j: (i, 0)),
                  pl.BlockSpec((k, tn), lambda i, j: (0, j)),
                  pl.BlockSpec((tm, tn), lambda i, j: (i, j))],
        out_specs=(pl.BlockSpec((tm, tn), lambda i, j: (i, j)),
                   pl.BlockSpec((tm, tn), lambda i, j: (i, j)),
                   pl.BlockSpec((tm, 1), lambda i, j: (i, 0))),
        out_shape=_norm_out_shapes(t, n),
        compiler_params=_params("parallel", "arbitrary"),
        name="proj_residual",
    )(a, w, res)


def _mlp_up_kernel(*refs, inv_d, side):
    a_ref, w_ref, ssq_ref = refs[:3]
    side_in = refs[3:3 + side.n_in]
    o_ref = refs[3 + side.n_in]
    side_out = refs[4 + side.n_in:]
    r2 = 1.0 / (ssq_ref[...] * inv_d + EPS)
    side.run(side_in, side_out)
    for cols in _col_panels(o_ref.shape[1]):
        u = jnp.dot(a_ref[...], w_ref[:, cols], preferred_element_type=F32)
        o_ref[:, cols] = (jnp.square(jnp.maximum(u, 0.0)) * r2).astype(o_ref.dtype)


def _mlp_up(hb, w, ssq, cast_jobs):
    t, k = hb.shape
    n = w.shape[1]
    tm = _row_tile(t)
    tn = _pick_tile(n, 512, LANES)
    nj = n // tn
    side = _SideCasts(cast_jobs, (t // tm) * nj, lambda i, j: i * nj + j)
    return pl.pallas_call(
        functools.partial(_mlp_up_kernel, inv_d=1.0 / k, side=side),
        grid=(t // tm, nj),
        in_specs=[pl.BlockSpec((tm, k), lambda i, j: (i, 0)),
                  pl.BlockSpec((k, tn), lambda i, j: (0, j)),
                  pl.BlockSpec((tm, 1), lambda i, j: (i, 0))] + side.in_specs,
        out_specs=[pl.BlockSpec((tm, tn), lambda i, j: (i, j))] + side.out_specs,
        out_shape=[jax.ShapeDtypeStruct((t, n), BF16)] + side.out_shapes,
        compiler_params=_params("arbitrary", "arbitrary"),
        name="mlp_up",
    )(hb, w, ssq, *side.operands)


def _mlp_down_kernel(a_ref, w_ref, r_ref, o_ref, *norm_refs, nk, col_axis=1):
    j = pl.program_id(col_axis)
    k = pl.program_id(col_axis + 1)

    def accumulate(base_ref, last):
        for cols in _col_panels(o_ref.shape[1]):
            h = base_ref[:, cols] + jnp.dot(a_ref[...], w_ref[:, cols], preferred_element_type=F32)
            o_ref[:, cols] = h
            if last and norm_refs:
                _emit_norm_inputs(h, cols, *norm_refs)

    if norm_refs:
        @pl.when(jnp.logical_and(j == 0, k == 0))
        def _():
            norm_refs[1][...] = jnp.zeros(norm_refs[1].shape, F32)

    if nk == 1:
        accumulate(r_ref, True)
        return

    @pl.when(k == 0)
    def _():
        accumulate(r_ref, False)

    @pl.when(jnp.logical_and(k > 0, k < nk - 1))
    def _():
        accumulate(o_ref, False)

    @pl.when(k == nk - 1)
    def _():
        accumulate(o_ref, True)


def _mlp_down(a, w, res, emit_norm):
    t, kdim = a.shape
    n = w.shape[1]
    tm = _row_tile(t)
    tn = _pick_tile(n, 1024, LANES)
    tk = _pick_tile(kdim, 2048, LANES)
    nk = kdim // tk
    tile = pl.BlockSpec((tm, tn), lambda i, j, k: (i, j))
    if emit_norm:
        out_specs = (tile, tile, pl.BlockSpec((tm, 1), lambda i, j, k: (i, 0)))
        out_shape = _norm_out_shapes(t, n)
    else:
        out_specs = tile
        out_shape = jax.ShapeDtypeStruct((t, n), F32)
    return pl.pallas_call(
        functools.partial(_mlp_down_kernel, nk=nk),
        grid=(t // tm, n // tn, nk),
        in_specs=[pl.BlockSpec((tm, tk), lambda i, j, k: (i, k)),
                  pl.BlockSpec((tk, tn), lambda i, j, k: (k, j)),
                  tile],
        out_specs=out_specs,
        out_shape=out_shape,
        compiler_params=_params("parallel", "arbitrary", "arbitrary"),
        name="mlp_down",
    )(a, w, res)


def _mlp_down_final(a, w, res, bsz, lp, row0, seq):
    kdim = a.shape[1]
    n = w.shape[1]
    tm = _pick_tile(seq, 1024, 16)
    nt = seq // tm
    tn = _pick_tile(n, 1024, LANES)
    tk = _pick_tile(kdim, 4096, LANES)
    nk = kdim // tk

    assert lp % 16 == 0 and row0 % 16 == 0 and tm % 16 == 0

    def row_start(b, m):
        return pl.multiple_of(b * lp + row0 + m * tm, 16)

    return pl.pallas_call(
        functools.partial(_mlp_down_kernel, nk=nk, col_axis=2),
        grid=(bsz, nt, n // tn, nk),
        in_specs=[pl.BlockSpec((pl.Element(tm), pl.Element(tk)), lambda b, m, j, k: (row_start(b, m), k * tk)),
                  pl.BlockSpec((tk, tn), lambda b, m, j, k: (k, j)),
                  pl.BlockSpec((pl.Element(tm), pl.Element(tn)), lambda b, m, j, k: (row_start(b, m), j * tn))],
        out_specs=pl.BlockSpec((tm, tn), lambda b, m, j, k: (b * nt + m, j)),
        out_shape=jax.ShapeDtypeStruct((bsz * seq, n), F32),
        compiler_params=_params("parallel", "parallel", "arbitrary", "arbitrary"),
        name="mlp_down_final",
    )(a, w, res)


def _glu_kernel(a_ref, wv_ref, wg_ref, bv_ref, bg_ref, r_ref, o_ref, hb_ref, ssq_ref):
    @pl.when(pl.program_id(1) == 0)
    def _():
        ssq_ref[...] = jnp.zeros(ssq_ref.shape, F32)

    for cols in _col_panels(o_ref.shape[1]):
        val = jnp.dot(a_ref[...], wv_ref[:, cols], preferred_element_type=F32) + bv_ref[:, cols]
        gate = jnp.dot(a_ref[...], wg_ref[:, cols], preferred_element_type=F32) + bg_ref[:, cols]
        h = r_ref[:, cols] + val * jax.nn.sigmoid(gate)
        o_ref[:, cols] = h
        _emit_norm_inputs(h, cols, hb_ref, ssq_ref)


def _glu_residual(a, w, bias, res):
    t, k = a.shape
    n = w.shape[1] // 2
    tm = _row_tile(t)
    tn = _pick_tile(n, 256, LANES)
    nb = n // tn
    b2 = bias.reshape(1, 2 * n).astype(F32)
    return pl.pallas_call(
        _glu_kernel,
        grid=(t // tm, nb),
        in_specs=[pl.BlockSpec((tm, k), lambda i, j: (i, 0)),
                  pl.BlockSpec((k, tn), lambda i, j: (0, j)),
                  pl.BlockSpec((k, tn), lambda i, j: (0, nb + j)),
                  pl.BlockSpec((1, tn), lambda i, j: (0, j)),
                  pl.BlockSpec((1, tn), lambda i, j: (0, nb + j)),
                  pl.BlockSpec((tm, tn), lambda i, j: (i, j))],
        out_specs=(pl.BlockSpec((tm, tn), lambda i, j: (i, j)),
                   pl.BlockSpec((tm, tn), lambda i, j: (i, j)),
                   pl.BlockSpec((tm, 1), lambda i, j: (i, 0))),
        out_shape=_norm_out_shapes(t, n),
        compiler_params=_params("parallel", "arbitrary"),
        name="glu_residual",
    )(a, w, w, b2, b2, res)


def _s5_kernel(x_ref, ssq_ref, g_ref, w0_ref, cm_ref, pr_ref, pi_ref, prc_ref, pic_ref, d_ref, o_ref,
               m_scr, wst_scr, wout_scr, s_scr, xin_scr, *, n_steps, inv_d):
    q = SSM_CHUNK
    half = w0_ref.shape[2] // 2

    @pl.when(pl.program_id(1) == 0)
    def _build_weights():
        w0 = w0_ref[0]
        w0re, w0im = w0[:, :half], w0[:, half:]
        cm = cm_ref[0]
        cm_bf = cm.astype(BF16)
        row = lax.broadcasted_iota(jnp.int32, (LANES, LANES), 0)
        col = lax.broadcasted_iota(jnp.int32, (LANES, LANES), 1)
        skip = jnp.where(row == col, jnp.broadcast_to(d_ref[...], (LANES, LANES)), 0.0)
        zero_blk = jnp.zeros((LANES, LANES), BF16)
        for tau in range(q):
            ar = pr_ref[0, tau:tau + 1, :]
            ai = pi_ref[0, tau:tau + 1, :]
            w_tau = jnp.concatenate([w0re * ar - w0im * ai, w0im * ar + w0re * ai], axis=1).astype(BF16)
            t_st = q - 1 - tau
            wst_scr[t_st * LANES:(t_st + 1) * LANES, :] = w_tau
            k_tau = jnp.dot(w_tau, cm_bf, preferred_element_type=F32)
            if tau == 0:
                k_tau = k_tau + skip
            k_bf = k_tau.astype(BF16)
            for t in range(q - tau):
                m_scr[t * LANES:(t + 1) * LANES, (t + tau) * LANES:(t + tau + 1) * LANES] = k_bf
        for t in range(q):
            for t2 in range(t):
                m_scr[t * LANES:(t + 1) * LANES, t2 * LANES:(t2 + 1) * LANES] = zero_blk
        cre, cimn = cm[:half], cm[half:]
        for t in range(q):
            arc = prc_ref[0, :, t + 1:t + 2]
            aic = pic_ref[0, :, t + 1:t + 2]
            wout_scr[:half, t * LANES:(t + 1) * LANES] = (arc * cre + aic * cimn).astype(BF16)
            wout_scr[half:, t * LANES:(t + 1) * LANES] = (arc * cimn - aic * cre).astype(BF16)

    gain = g_ref[...]
    u = jnp.concatenate(
        [(x_ref[t].astype(F32) * lax.rsqrt(ssq_ref[t] * inv_d + EPS) * gain).astype(BF16) for t in range(q)],
        axis=1)
    s_scr[...] = jnp.dot(u, wst_scr[...], preferred_element_type=F32)

    shp = (SUBLANES, half)
    ar = jnp.broadcast_to(pr_ref[0, q:q + 1, :], shp)
    ai = jnp.broadcast_to(pi_ref[0, q:q + 1, :], shp)
    anr = jnp.broadcast_to(pr_ref[0, q + 1:q + 2, :], shp)
    ani = jnp.broadcast_to(pi_ref[0, q + 1:q + 2, :], shp)
    seg = lax.broadcasted_iota(jnp.int32, shp, 0)
    zeros = jnp.zeros(shp, F32)

    def shift_down(x):
        return jnp.where(seg == 0, 0.0, pltpu.roll(x, 1, 0))

    def advance(j, cr, ci):
        sj = s_scr[pl.ds(pl.multiple_of(j * SUBLANES, SUBLANES), SUBLANES), :]
        return ar * cr - ai * ci + sj[:, :half], ar * ci + ai * cr + sj[:, half:]

    er, ei = lax.fori_loop(0, n_steps, lambda j, c: advance(j, *c), (zeros, zeros))
    tr, ti = er, ei
    for _ in range(SSM_SEGMENTS - 1):
        sr, si = shift_down(tr), shift_down(ti)
        tr, ti = er + anr * sr - ani * si, ei + anr * si + ani * sr
    cr0, ci0 = shift_down(tr), shift_down(ti)

    def scan_store(j, c):
        cr, ci = c
        xin_scr[pl.ds(pl.multiple_of(j * SUBLANES, SUBLANES), SUBLANES), :] = jnp.concatenate([cr, ci], axis=1)
        return advance(j, cr, ci)

    lax.fori_loop(0, n_steps, scan_store, (cr0, ci0))

    xin = xin_scr[...].astype(BF16)
    steps_per_panel = 4
    for t0 in range(0, q, steps_per_panel):
        k_hi = (t0 + steps_per_panel) * LANES
        cols = slice(t0 * LANES, k_hi)
        y = (jnp.dot(u[:, :k_hi], m_scr[:k_hi, cols], preferred_element_type=F32)
             + jnp.dot(xin, wout_scr[:, cols], preferred_element_type=F32))
        z = jax.nn.gelu(y)
        for t in range(steps_per_panel):
            o_ref[t0 + t] = z[:, t * LANES:(t + 1) * LANES].astype(o_ref.dtype)


def _complex_pow(zr, zi, n):
    rr, ri = None, None
    br, bi = zr, zi
    while n:
        if n & 1:
            rr, ri = (br, bi) if rr is None else (rr * br - ri * bi, rr * bi + ri * br)
        n >>= 1
        if n:
            br, bi = br * br - bi * bi, 2.0 * br * bi
    return rr, ri


def _s5_tables(a_re, a_im, log_dt, b_re, b_im, c_re, c_im, n_steps):
    g, p = a_re.shape
    gpb = LANES // SSM_GROUP
    nblk = g // gpb
    dt = jnp.exp(log_dt.astype(F32))[:, None]
    ar = a_re.astype(F32)
    ai = a_im.astype(F32)
    mag = jnp.exp(dt * ar)
    ang = dt * ai
    abar_re = mag * jnp.cos(ang)
    abar_im = mag * jnp.sin(ang)
    nr = abar_re - 1.0
    ni = abar_im
    den = ar * ar + ai * ai
    f_re = (nr * ar + ni * ai) / den
    f_im = (ni * ar - nr * ai) / den
    br = b_re.astype(F32)
    bi = b_im.astype(F32)
    bb_re = f_re[..., None] * br - f_im[..., None] * bi
    bb_im = f_re[..., None] * bi + f_im[..., None] * br

    pw_r, pw_i = [jnp.ones_like(abar_re)], [jnp.zeros_like(abar_re)]
    for _ in range(SSM_CHUNK):
        pw_r.append(pw_r[-1] * abar_re - pw_i[-1] * abar_im)
        pw_i.append(pw_r[-2] * abar_im + pw_i[-1] * abar_re)
    seg_r, seg_i = _complex_pow(pw_r[-1], pw_i[-1], n_steps)
    pw_r.append(seg_r)
    pw_i.append(seg_i)
    n_rows = 24
    pr = jnp.stack(pw_r, 0).reshape(len(pw_r), nblk, gpb * p).transpose(1, 0, 2)
    pi = jnp.stack(pw_i, 0).reshape(len(pw_i), nblk, gpb * p).transpose(1, 0, 2)
    pad = ((0, 0), (0, n_rows - pr.shape[1]), (0, 0))
    pr = jnp.pad(pr, pad)
    pi = jnp.pad(pi, pad)
    prc = pr.transpose(0, 2, 1)
    pic = pi.transpose(0, 2, 1)

    eye = jnp.eye(gpb, dtype=F32)

    def in_to_state(bb):
        v = bb.reshape(nblk, gpb, p, SSM_GROUP).transpose(0, 1, 3, 2)
        e = v[:, :, :, None, :] * eye[None, :, None, :, None]
        return e.reshape(nblk, LANES, gpb * p)

    def state_to_out(c):
        v = c.astype(F32).reshape(nblk, gpb, SSM_GROUP, p).transpose(0, 1, 3, 2)
        e = v[:, :, :, None, :] * eye[None, :, None, :, None]
        return e.reshape(nblk, gpb * p, LANES)

    w0 = jnp.concatenate([in_to_state(bb_re), in_to_state(bb_im)], axis=2)
    cm = jnp.concatenate([state_to_out(c_re), -state_to_out(c_im)], axis=1)
    return w0, cm, pr, pi, prc, pic


def _s5_mixer(hb, ssq, gain, bsz, lp, a_re, a_im, log_dt, b_re, b_im, c_re, c_im, d_skip):
    t, d = hb.shape
    q, nseg = SSM_CHUNK, SSM_SEGMENTS
    assert lp % (q * nseg * 2) == 0
    n_steps = lp // (q * nseg)
    rows = n_steps * nseg

    def chunk_layout(v):
        w = v.shape[-1]
        return v.reshape(bsz, nseg, n_steps, q, w).transpose(3, 0, 2, 1, 4).reshape(q, bsz * rows, w)

    x = chunk_layout(hb)
    ssq_c = chunk_layout(ssq)

    w0, cm, pr, pi, prc, pic = _s5_tables(a_re, a_im, log_dt, b_re, b_im, c_re, c_im, n_steps)
    nblk = d // LANES
    n_state = w0.shape[2]
    kern = functools.partial(_s5_kernel, n_steps=n_steps, inv_d=1.0 / d)
    z = pl.pallas_call(
        kern,
        grid=(nblk, bsz),
        in_specs=[pl.BlockSpec((q, rows, LANES), lambda k, b: (0, b, k)),
                  pl.BlockSpec((q, rows, 1), lambda k, b: (0, b, 0)),
                  pl.BlockSpec((1, LANES), lambda k, b: (0, k)),
                  pl.BlockSpec((1, LANES, n_state), lambda k, b: (k, 0, 0)),
                  pl.BlockSpec((1, n_state, LANES), lambda k, b: (k, 0, 0)),
                  pl.BlockSpec((1,) + pr.shape[1:], lambda k, b: (k, 0, 0)),
                  pl.BlockSpec((1,) + pi.shape[1:], lambda k, b: (k, 0, 0)),
                  pl.BlockSpec((1,) + prc.shape[1:], lambda k, b: (k, 0, 0)),
                  pl.BlockSpec((1,) + pic.shape[1:], lambda k, b: (k, 0, 0)),
                  pl.BlockSpec((1, LANES), lambda k, b: (0, k))],
        out_specs=pl.BlockSpec((q, rows, LANES), lambda k, b: (0, b, k)),
        out_shape=jax.ShapeDtypeStruct((q, bsz * rows, d), BF16),
        scratch_shapes=[pltpu.VMEM((q * LANES, q * LANES), BF16),
                        pltpu.VMEM((q * LANES, n_state), BF16),
                        pltpu.VMEM((n_state, q * LANES), BF16),
                        pltpu.VMEM((rows, n_state), F32),
                        pltpu.VMEM((rows, n_state), F32)],
        compiler_params=_params("arbitrary", "arbitrary"),
        name="s5_mixer",
    )(x, ssq_c, gain.reshape(1, d).astype(F32), w0, cm, pr, pi, prc, pic, d_skip.reshape(1, d).astype(F32))
    return z.reshape(q, bsz, n_steps, nseg, d).transpose(1, 3, 2, 0, 4).reshape(t, d)


def _rope_tables(length):
    inv = ROPE_THETA ** (-jnp.arange(0, HEAD_DIM, 2, dtype=F32) / HEAD_DIM)
    ang = jnp.arange(length, dtype=F32)[:, None] * inv[None, :]
    cos, sin = jnp.cos(ang), jnp.sin(ang)
    return jnp.concatenate([cos, cos], axis=1), jnp.concatenate([-sin, sin], axis=1)


def _lambda_init(layer_idx):
    return 0.8 - 0.6 * math.exp(-0.3 * layer_idx)


def kernel(x, meta_tokens, norm_mix_g, norm_mlp_g, da_w_qkv, da_q_norm_g, da_k_norm_g, da_lambda, da_subln_g, da_w_o, ssm_a_re, ssm_a_im, ssm_log_dt, ssm_b_re, ssm_b_im, ssm_c_re, ssm_c_im, ssm_d, ssm_w_glu, ssm_b_glu, mlp_w_up, mlp_w_down):
    bsz, seq, d = x.shape
    depth = norm_mix_g.shape[0]
    n_mixers = 2
    length = N_META + seq
    lp = ((length + SEQ_ALIGN - 1) // SEQ_ALIGN) * SEQ_ALIGN
    assert meta_tokens.shape[0] == N_META and depth >= 1
    h, hn = _embed_norm(x, meta_tokens, norm_mix_g[0], lp)
    cos, sin = (jnp.tile(tab, (bsz, 1)) for tab in _rope_tables(lp))
    def up_job(i):
        return mlp_w_up, i, norm_mlp_g[i]

    def bf16_weight(name, stack, layer, gain=None):
        if name not in wb:
            src = stack[layer]
            wb[name] = (src if gain is None else src * gain.astype(F32)[:, None]).astype(BF16)
        return wb[name]

    wb = {}
    hb = ssq = None
    for i in range(depth):
        j = i // n_mixers
        nxt_s5 = i + 1 < depth and (i + 1) % n_mixers == 1
        if i % n_mixers == 0:
            if i > 0:
                hn = _rmsnorm(h, norm_mix_g[i])
            qkv, wb["o", j], wb["up", i] = _qkv_proj(
                hn, bf16_weight(("qkv", j), da_w_qkv, j), cos, sin, da_q_norm_g[j], da_k_norm_g[j],
                [(da_w_o, j, None), up_job(i)])
            att = _diff_attention(qkv, da_lambda[j], da_subln_g[j], bsz, lp, d, _lambda_init(i))
            h, hb, ssq = _proj_residual(att, wb["o", j], h)
        else:
            z = _s5_mixer(hb, ssq, norm_mix_g[i], bsz, lp, ssm_a_re[j], ssm_a_im[j], ssm_log_dt[j],
                          ssm_b_re[j], ssm_b_im[j], ssm_c_re[j], ssm_c_im[j], ssm_d[j])
            h, hb, ssq = _glu_residual(z, bf16_weight(("glu", j), ssm_w_glu, j), ssm_b_glu[j], h)
        jobs = [(("down", i), mlp_w_down, i, None)]
        if nxt_s5:
            jobs += [(("glu", (i + 1) // n_mixers), ssm_w_glu, (i + 1) // n_mixers, None),
                     (("up", i + 1),) + up_job(i + 1)]
        f, *casts = _mlp_up(hb, bf16_weight(("up", i), *up_job(i)), ssq, [job[1:] for job in jobs])
        wb.update({job[0]: c for job, c in zip(jobs, casts)})
        if i == depth - 1:
            return _mlp_down_final(f, wb["down", i], h, bsz, lp, N_META, seq).reshape(bsz, seq, d)
        if nxt_s5:
            h, hb, ssq = _mlp_down(f, wb["down", i], h, True)
        else:
            h = _mlp_down(f, wb["down", i], h, False)
```

```python
import functools
import math

import jax
import jax.numpy as jnp
from jax import lax
from jax.experimental import pallas as pl
from jax.experimental.pallas import tpu as pltpu

N_META = 16
SEQ_ALIGN = 256
HEAD_DIM = 128
LOG2E = 1.4426950408889634
ATTN_ROW_BLOCK = 64
ROPE_THETA = 10000.0
SSM_GROUP = 16
SSM_CHUNK = 16
SSM_SEGMENTS = 8
MXU_COLS = 256
LANES = 128
SUBLANES = 8
EPS = 1e-6
VMEM_LIMIT_BYTES = 56 * 1024 * 1024

F32 = jnp.float32
BF16 = jnp.bfloat16


def _pick_tile(n, target, mult):
    best = None
    for t in range(mult, min(n, target) + 1, mult):
        if n % t == 0:
            best = t
    assert best is not None, (n, target, mult)
    return best


def _row_tile(t):
    return _pick_tile(t, 1536, SEQ_ALIGN)


def _params(*sem):
    return pltpu.CompilerParams(dimension_semantics=sem, vmem_limit_bytes=VMEM_LIMIT_BYTES)


class _SideCasts:
    def __init__(self, jobs, n_steps, step_of):
        self.operands, self.in_specs, self.out_specs, self.out_shapes, self.has_gain = [], [], [], [], []
        for src, layer, gain in jobs:
            _, r, c = src.shape
            cr = next(x for x in (16 << p for p in range(24)) if r % x == 0 and r // x <= n_steps)

            def idx(*g, last=r // cr - 1):
                return (jnp.minimum(step_of(*g), last), 0)

            def src_idx(*g, layer=layer, last=r // cr - 1):
                return (layer, jnp.minimum(step_of(*g), last), 0)

            self.operands.append(src)
            self.in_specs.append(pl.BlockSpec((None, cr, c), src_idx))
            if gain is not None:
                self.operands.append(gain.reshape(r, 1).astype(F32))
                self.in_specs.append(pl.BlockSpec((cr, 1), idx))
            self.out_specs.append(pl.BlockSpec((cr, c), idx))
            self.out_shapes.append(jax.ShapeDtypeStruct((r, c), BF16))
            self.has_gain.append(gain is not None)
        self.n_in = len(self.operands)
        self.n_out = len(self.out_shapes)

    def run(self, in_refs, out_refs):
        refs = iter(in_refs)
        for has_gain, o_ref in zip(self.has_gain, out_refs):
            v = next(refs)[...]
            if has_gain:
                v = v * next(refs)[...]
            o_ref[...] = v.astype(o_ref.dtype)


def _rmsnorm_kernel(x_ref, g_ref, o_ref):
    x = x_ref[...]
    ms = jnp.mean(x * x, axis=-1, keepdims=True)
    o_ref[...] = (x * lax.rsqrt(ms + EPS) * g_ref[...]).astype(o_ref.dtype)


def _rmsnorm(x, g):
    t, d = x.shape
    tr = _pick_tile(t, 256, 16)
    return pl.pallas_call(
        _rmsnorm_kernel,
        grid=(t // tr,),
        in_specs=[pl.BlockSpec((tr, d), lambda i: (i, 0)),
                  pl.BlockSpec((1, d), lambda i: (0, 0))],
        out_specs=pl.BlockSpec((tr, d), lambda i: (i, 0)),
        out_shape=jax.ShapeDtypeStruct((t, d), BF16),
        compiler_params=_params("parallel"),
        name="rmsnorm",
    )(x, g.reshape(1, d).astype(F32))


def _embed_norm_kernel(x_ref, meta_ref, g_ref, h_ref, hn_ref, *, n_meta, seq):
    tr = h_ref.shape[0]
    i = pl.program_id(1)

    @pl.when(i == 0)
    def _():
        h_ref[:n_meta, :] = meta_ref[...]
        h_ref[n_meta:, :] = x_ref[:tr - n_meta, :]

    @pl.when(i > 0)
    def _():
        row = i * tr + lax.broadcasted_iota(jnp.int32, h_ref.shape, 0)
        h_ref[...] = jnp.where(row < n_meta + seq, x_ref[...], 0.0)

    h = h_ref[...]
    ms = jnp.mean(h * h, axis=-1, keepdims=True)
    hn_ref[...] = (h * lax.rsqrt(ms + EPS) * g_ref[...]).astype(hn_ref.dtype)


def _embed_norm(x, meta_tokens, g, lp):
    bsz, seq, d = x.shape
    n_meta = meta_tokens.shape[0]
    tr = _pick_tile(lp, 256, 16)
    assert n_meta % 16 == 0 and n_meta < tr and tr < seq
    nt = lp // tr
    tile = pl.BlockSpec((tr, d), lambda b, i: (b * nt + i, 0))

    def x_window(b, i):
        return (b, pl.multiple_of(jnp.maximum(i * tr - n_meta, 0), 16), 0)

    return pl.pallas_call(
        functools.partial(_embed_norm_kernel, n_meta=n_meta, seq=seq),
        grid=(bsz, nt),
        in_specs=[pl.BlockSpec((pl.Squeezed(), pl.Element(tr, (0, lp - n_meta - seq)), pl.Element(d)), x_window),
                  pl.BlockSpec((n_meta, d), lambda b, i: (0, 0)),
                  pl.BlockSpec((1, d), lambda b, i: (0, 0))],
        out_specs=(tile, tile),
        out_shape=(jax.ShapeDtypeStruct((bsz * lp, d), F32), jax.ShapeDtypeStruct((bsz * lp, d), BF16)),
        compiler_params=_params("parallel", "arbitrary"),
        name="embed_norm",
    )(x, meta_tokens.astype(F32), g.reshape(1, d).astype(F32))


def _qkv_kernel(*refs, n_q_blocks, q_scale, side):
    a_ref, w_ref, cos_ref, sin_ref, qg_ref, kg_ref = refs[:6]
    side_in = refs[6:6 + side.n_in]
    o_ref = refs[6 + side.n_in]
    side_out = refs[7 + side.n_in:]
    acc = jnp.dot(a_ref[...], w_ref[...], preferred_element_type=F32)
    j = pl.program_id(1)
    tn = acc.shape[1]

    def norm_rope(g, scale):
        cos = cos_ref[...]
        sin = sin_ref[...]
        for u in range(tn // HEAD_DIM):
            x = acc[:, u * HEAD_DIM:(u + 1) * HEAD_DIM]
            ms = jnp.mean(x * x, axis=-1, keepdims=True)
            y = x * lax.rsqrt(ms + EPS) * g
            y = y * cos + pltpu.roll(y, HEAD_DIM // 2, 1) * sin
            if scale != 1.0:
                y = y * scale
            o_ref[:, u * HEAD_DIM:(u + 1) * HEAD_DIM] = y.astype(o_ref.dtype)

    @pl.when(j < n_q_blocks)
    def _():
        norm_rope(qg_ref[...], q_scale)
        side.run(side_in, side_out)

    @pl.when(jnp.logical_and(j >= n_q_blocks, j < 2 * n_q_blocks))
    def _():
        norm_rope(kg_ref[...], 1.0)
        side.run(side_in, side_out)

    @pl.when(j >= 2 * n_q_blocks)
    def _():
        o_ref[...] = acc.astype(o_ref.dtype)
        side.run(side_in, side_out)


def _qkv_proj(a, w, cos, sin, q_g, k_g, cast_jobs):
    t, d = a.shape
    n = w.shape[1]
    tm = _row_tile(t)
    tn = _pick_tile(d, 512, HEAD_DIM)
    nj = n // tn
    side = _SideCasts(cast_jobs, (t // tm) * nj, lambda i, j: i * nj + j)
    kern = functools.partial(_qkv_kernel, n_q_blocks=d // tn, q_scale=HEAD_DIM ** -0.5 * LOG2E, side=side)
    return pl.pallas_call(
        kern,
        grid=(t // tm, nj),
        in_specs=[pl.BlockSpec((tm, d), lambda i, j: (i, 0)),
                  pl.BlockSpec((d, tn), lambda i, j: (0, j)),
                  pl.BlockSpec((tm, HEAD_DIM), lambda i, j: (i, 0)),
                  pl.BlockSpec((tm, HEAD_DIM), lambda i, j: (i, 0)),
                  pl.BlockSpec((1, HEAD_DIM), lambda i, j: (0, 0)),
                  pl.BlockSpec((1, HEAD_DIM), lambda i, j: (0, 0))] + side.in_specs,
        out_specs=[pl.BlockSpec((tm, tn), lambda i, j: (i, j))] + side.out_specs,
        out_shape=[jax.ShapeDtypeStruct((t, n), BF16)] + side.out_shapes,
        compiler_params=_params("arbitrary", "arbitrary"),
        name="qkv_proj",
    )(a, w, cos, sin, q_g.reshape(1, HEAD_DIM).astype(F32), k_g.reshape(1, HEAD_DIM).astype(F32),
      *side.operands)


def _attn_kernel(q_ref, k_ref, v_ref, lam_ref, g_ref, o_ref,
                 m1_ref, l1_ref, a1_ref, m2_ref, l2_ref, a2_ref, s_even_ref, s_odd_ref, p_ref, alpha_ref,
                 *, blk, lam_init):
    s_bufs = (s_even_ref, s_odd_ref)
    nq = q_ref.shape[0] // blk
    stats = ((m1_ref, l1_ref, a1_ref), (m2_ref, l2_ref, a2_ref))
    lv = lam_ref[...]
    lam = (jnp.exp(jnp.sum(lv[0:1] * lv[1:2], axis=-1, keepdims=True))
           - jnp.exp(jnp.sum(lv[2:3] * lv[3:4], axis=-1, keepdims=True)) + lam_init)

    def rows(i):
        return pl.ds(pl.multiple_of(i * blk, blk), blk)

    def init_stats():
        for m_ref, l_ref, a_ref in stats:
            m_ref[...] = jnp.full(m_ref.shape, -jnp.inf, F32)
            l_ref[...] = jnp.zeros(l_ref.shape, F32)
            a_ref[...] = jnp.zeros(a_ref.shape, F32)

    def scores(qi, j, s_ref):
        q = q_ref[rows(qi), :]
        kk = k_ref[rows(j), :]
        for c in range(2):
            s_ref[c, :, :blk] = lax.dot_general(
                q[:, c * HEAD_DIM:(c + 1) * HEAD_DIM], kk[:, c * HEAD_DIM:(c + 1) * HEAD_DIM],
                (((1,), (1,)), ((), ())), preferred_element_type=F32)

    def softmax_pv(j, s_ref, masked):
        vv = v_ref[rows(j), :]
        for c, (m_ref, l_ref, a_ref) in enumerate(stats):
            for r in range(0, blk, ATTN_ROW_BLOCK):
                rb = slice(r, r + ATTN_ROW_BLOCK)
                s = s_ref[c, rb, :blk]
                if masked:
                    row = r + lax.broadcasted_iota(jnp.int32, s.shape, 0)
                    col = lax.broadcasted_iota(jnp.int32, s.shape, 1)
                    s = jnp.where(col <= row, s, -jnp.inf)
                m_old = m_ref[rb, :]
                row_max = jnp.broadcast_to(jnp.max(s, axis=-1, keepdims=True), m_old.shape)
                m_new = jnp.maximum(m_old, row_max)
                alpha = jnp.exp2(m_old - m_new)
                part = None
                for u in range(blk // LANES):
                    lanes = slice(u * LANES, (u + 1) * LANES)
                    tile = jnp.exp2(s[:, lanes] - m_new)
                    p_ref[c, rb, lanes] = tile.astype(BF16)
                    part = tile if part is None else part + tile
                l_ref[rb, :] = alpha * l_ref[rb, :] + part
                alpha_ref[c, rb, :] = alpha
                m_ref[rb, :] = m_new
            alpha_all = alpha_ref[c]
            a_ref[...] = (jnp.concatenate([alpha_all] * (a_ref.shape[1] // LANES), axis=1) * a_ref[...]
                          + jnp.dot(p_ref[c], vv, preferred_element_type=F32))

    def finalize(qi):
        l1 = jnp.sum(l1_ref[...], axis=-1, keepdims=True)
        l2 = jnp.sum(l2_ref[...], axis=-1, keepdims=True)
        o = a1_ref[...] / l1 - lam * (a2_ref[...] / l2)
        ms = jnp.mean(o * o, axis=-1, keepdims=True)
        y = o * lax.rsqrt(ms + EPS) * g_ref[...] * (1.0 - lam_init)
        o_ref[rows(qi), :] = y.astype(o_ref.dtype)

    def items(n, *work):
        for parity in range(2):
            @pl.when(n % 2 == parity)
            def _():
                for step, (next_qi, next_j, j, masked) in enumerate(work):
                    mine = (parity + step) % 2
                    scores(next_qi, next_j, s_bufs[1 - mine])
                    softmax_pv(j, s_bufs[mine], masked)

    init_stats()
    scores(0, 0, s_bufs[0])

    def q_block(qi, carry):
        base = (qi * (qi + 1)) // 2
        diagonal = (jnp.minimum(qi + 1, nq - 1), 0, qi, True)

        def full_chunk(j, c):
            items(base + j, (qi, j + 1, j, False))
            return c

        lax.fori_loop(0, qi, full_chunk, 0)
        items(base + qi, diagonal)
        finalize(qi)
        init_stats()
        return carry

    lax.fori_loop(0, nq, q_block, 0)


def _diff_attention(qkv, lam_vecs, subln_g, bsz, lp, d, lam_init):
    t = qkv.shape[0]
    n_heads = d // (2 * HEAD_DIM)
    hw = 2 * HEAD_DIM
    blk = _pick_tile(lp, 768, SEQ_ALIGN)
    kern = functools.partial(_attn_kernel, blk=blk, lam_init=lam_init)
    return pl.pallas_call(
        kern,
        grid=(bsz, n_heads),
        in_specs=[pl.BlockSpec((lp, hw), lambda b, h: (b, h)),
                  pl.BlockSpec((lp, hw), lambda b, h: (b, n_heads + h)),
                  pl.BlockSpec((lp, hw), lambda b, h: (b, 2 * n_heads + h)),
                  pl.BlockSpec((4, HEAD_DIM), lambda b, h: (0, 0)),
                  pl.BlockSpec((1, hw), lambda b, h: (0, 0))],
        out_specs=pl.BlockSpec((lp, hw), lambda b, h: (b, h)),
        out_shape=jax.ShapeDtypeStruct((t, d), BF16),
        scratch_shapes=[pltpu.VMEM((blk, LANES), F32), pltpu.VMEM((blk, LANES), F32), pltpu.VMEM((blk, hw), F32),
                        pltpu.VMEM((blk, LANES), F32), pltpu.VMEM((blk, LANES), F32), pltpu.VMEM((blk, hw), F32),
                        pltpu.VMEM((2, blk, blk + LANES), F32), pltpu.VMEM((2, blk, blk + LANES), F32),
                        pltpu.VMEM((2, blk, blk), BF16), pltpu.VMEM((2, blk, LANES), F32)],
        compiler_params=_params("parallel", "parallel"),
        name="diff_attention",
    )(qkv, qkv, qkv, lam_vecs.astype(F32), subln_g.reshape(1, hw).astype(F32))


def _col_panels(n):
    return [slice(u, u + MXU_COLS) for u in range(0, n, MXU_COLS)]


def _emit_norm_inputs(h, cols, hb_ref, ssq_ref):
    hb_ref[:, cols] = h.astype(hb_ref.dtype)
    ssq_ref[...] += jnp.sum(h * h, axis=-1, keepdims=True)


def _proj_residual_kernel(a_ref, w_ref, r_ref, o_ref, hb_ref, ssq_ref):
    @pl.when(pl.program_id(1) == 0)
    def _():
        ssq_ref[...] = jnp.zeros(ssq_ref.shape, F32)

    for cols in _col_panels(o_ref.shape[1]):
        h = r_ref[:, cols] + jnp.dot(a_ref[...], w_ref[:, cols], preferred_element_type=F32)
        o_ref[:, cols] = h
        _emit_norm_inputs(h, cols, hb_ref, ssq_ref)


def _norm_out_shapes(t, n):
    return (jax.ShapeDtypeStruct((t, n), F32), jax.ShapeDtypeStruct((t, n), BF16),
            jax.ShapeDtypeStruct((t, 1), F32))


def _proj_residual(a, w, res):
    t, k = a.shape
    n = w.shape[1]
    tm = _row_tile(t)
    tn = _pick_tile(n, 512, LANES)
    return pl.pallas_call(
        _proj_residual_kernel,
        grid=(t // tm, n // tn),
        in_specs=[pl.BlockSpec((tm, k), lambda i, j: (i, 0)),
                  pl.BlockSpec((k, tn), lambda i, j: (0, j)),
                  pl.BlockSpec((tm, tn), lambda i, j: (i, j))],
        out_specs=(pl.BlockSpec((tm, tn), lambda i, j: (i, j)),
                   pl.BlockSpec((tm, tn), lambda i, j: (i, j)),
                   pl.BlockSpec((tm, 1), lambda i, j: (i, 0))),
        out_shape=_norm_out_shapes(t, n),
        compiler_params=_params("parallel", "arbitrary"),
        name="proj_residual",
    )(a, w, res)


def _mlp_up_kernel(*refs, inv_d, side):
    a_ref, w_ref, ssq_ref = refs[:3]
    side_in = refs[3:3 + side.n_in]
    o_ref = refs[3 + side.n_in]
    side_out = refs[4 + side.n_in:]
    r2 = 1.0 / (ssq_ref[...] * inv_d + EPS)
    side.run(side_in, side_out)
    for cols in _col_panels(o_ref.shape[1]):
        u = jnp.dot(a_ref[...], w_ref[:, cols], preferred_element_type=F32)
        o_ref[:, cols] = (jnp.square(jnp.maximum(u, 0.0)) * r2).astype(o_ref.dtype)


def _mlp_up(hb, w, ssq, cast_jobs):
    t, k = hb.shape
    n = w.shape[1]
    tm = _row_tile(t)
    tn = _pick_tile(n, 512, LANES)
    nj = n // tn
    side = _SideCasts(cast_jobs, (t // tm) * nj, lambda i, j: i * nj + j)
    return pl.pallas_call(
        functools.partial(_mlp_up_kernel, inv_d=1.0 / k, side=side),
        grid=(t // tm, nj),
        in_specs=[pl.BlockSpec((tm, k), lambda i, j: (i, 0)),
                  pl.BlockSpec((k, tn), lambda i, j: (0, j)),
                  pl.BlockSpec((tm, 1), lambda i, j: (i, 0))] + side.in_specs,
        out_specs=[pl.BlockSpec((tm, tn), lambda i, j: (i, j))] + side.out_specs,
        out_shape=[jax.ShapeDtypeStruct((t, n), BF16)] + side.out_shapes,
        compiler_params=_params("arbitrary", "arbitrary"),
        name="mlp_up",
    )(hb, w, ssq, *side.operands)


def _mlp_down_kernel(a_ref, w_ref, r_ref, o_ref, *norm_refs, nk, col_axis=1):
    j = pl.program_id(col_axis)
    k = pl.program_id(col_axis + 1)

    def accumulate(base_ref, last):
        for cols in _col_panels(o_ref.shape[1]):
            h = base_ref[:, cols] + jnp.dot(a_ref[...], w_ref[:, cols], preferred_element_type=F32)
            o_ref[:, cols] = h
            if last and norm_refs:
                _emit_norm_inputs(h, cols, *norm_refs)

    if norm_refs:
        @pl.when(jnp.logical_and(j == 0, k == 0))
        def _():
            norm_refs[1][...] = jnp.zeros(norm_refs[1].shape, F32)

    if nk == 1:
        accumulate(r_ref, True)
        return

    @pl.when(k == 0)
    def _():
        accumulate(r_ref, False)

    @pl.when(jnp.logical_and(k > 0, k < nk - 1))
    def _():
        accumulate(o_ref, False)

    @pl.when(k == nk - 1)
    def _():
        accumulate(o_ref, True)


def _mlp_down(a, w, res, emit_norm):
    t, kdim = a.shape
    n = w.shape[1]
    tm = _row_tile(t)
    tn = _pick_tile(n, 1024, LANES)
    tk = _pick_tile(kdim, 2048, LANES)
    nk = kdim // tk
    tile = pl.BlockSpec((tm, tn), lambda i, j, k: (i, j))
    if emit_norm:
        out_specs = (tile, tile, pl.BlockSpec((tm, 1), lambda i, j, k: (i, 0)))
        out_shape = _norm_out_shapes(t, n)
    else:
        out_specs = tile
        out_shape = jax.ShapeDtypeStruct((t, n), F32)
    return pl.pallas_call(
        functools.partial(_mlp_down_kernel, nk=nk),
        grid=(t // tm, n // tn, nk),
        in_specs=[pl.BlockSpec((tm, tk), lambda i, j, k: (i, k)),
                  pl.BlockSpec((tk, tn), lambda i, j, k: (k, j)),
                  tile],
        out_specs=out_specs,
        out_shape=out_shape,
        compiler_params=_params("parallel", "arbitrary", "arbitrary"),
        name="mlp_down",
    )(a, w, res)


def _mlp_down_final(a, w, res, bsz, lp, row0, seq):
    kdim = a.shape[1]
    n = w.shape[1]
    tm = _pick_tile(seq, 1024, 16)
    nt = seq // tm
    tn = _pick_tile(n, 1024, LANES)
    tk = _pick_tile(kdim, 4096, LANES)
    nk = kdim // tk

    assert lp % 16 == 0 and row0 % 16 == 0 and tm % 16 == 0

    def row_start(b, m):
        return pl.multiple_of(b * lp + row0 + m * tm, 16)

    return pl.pallas_call(
        functools.partial(_mlp_down_kernel, nk=nk, col_axis=2),
        grid=(bsz, nt, n // tn, nk),
        in_specs=[pl.BlockSpec((pl.Element(tm), pl.Element(tk)), lambda b, m, j, k: (row_start(b, m), k * tk)),
                  pl.BlockSpec((tk, tn), lambda b, m, j, k: (k, j)),
                  pl.BlockSpec((pl.Element(tm), pl.Element(tn)), lambda b, m, j, k: (row_start(b, m), j * tn))],
        out_specs=pl.BlockSpec((tm, tn), lambda b, m, j, k: (b * nt + m, j)),
        out_shape=jax.ShapeDtypeStruct((bsz * seq, n), F32),
        compiler_params=_params("parallel", "parallel", "arbitrary", "arbitrary"),
        name="mlp_down_final",
    )(a, w, res)


def _glu_kernel(a_ref, wv_ref, wg_ref, bv_ref, bg_ref, r_ref, o_ref, hb_ref, ssq_ref):
    @pl.when(pl.program_id(1) == 0)
    def _():
        ssq_ref[...] = jnp.zeros(ssq_ref.shape, F32)

    for cols in _col_panels(o_ref.shape[1]):
        val = jnp.dot(a_ref[...], wv_ref[:, cols], preferred_element_type=F32) + bv_ref[:, cols]
        gate = jnp.dot(a_ref[...], wg_ref[:, cols], preferred_element_type=F32) + bg_ref[:, cols]
        h = r_ref[:, cols] + val * jax.nn.sigmoid(gate)
        o_ref[:, cols] = h
        _emit_norm_inputs(h, cols, hb_ref, ssq_ref)


def _glu_residual(a, w, bias, res):
    t, k = a.shape
    n = w.shape[1] // 2
    tm = _row_tile(t)
    tn = _pick_tile(n, 256, LANES)
    nb = n // tn
    b2 = bias.reshape(1, 2 * n).astype(F32)
    return pl.pallas_call(
        _glu_kernel,
        grid=(t // tm, nb),
        in_specs=[pl.BlockSpec((tm, k), lambda i, j: (i, 0)),
                  pl.BlockSpec((k, tn), lambda i, j: (0, j)),
                  pl.BlockSpec((k, tn), lambda i, j: (0, nb + j)),
                  pl.BlockSpec((1, tn), lambda i, j: (0, j)),
                  pl.BlockSpec((1, tn), lambda i, j: (0, nb + j)),
                  pl.BlockSpec((tm, tn), lambda i, j: (i, j))],
        out_specs=(pl.BlockSpec((tm, tn), lambda i, j: (i, j)),
                   pl.BlockSpec((tm, tn), lambda i, j: (i, j)),
                   pl.BlockSpec((tm, 1), lambda i, j: (i, 0))),
        out_shape=_norm_out_shapes(t, n),
        compiler_params=_params("parallel", "arbitrary"),
        name="glu_residual",
    )(a, w, w, b2, b2, res)


def _s5_kernel(x_ref, ssq_ref, g_ref, w0_ref, cm_ref, pr_ref, pi_ref, prc_ref, pic_ref, d_ref, o_ref,
               m_scr, wst_scr, wout_scr, s_scr, xin_scr, *, n_steps, inv_d):
    q = SSM_CHUNK
    half = w0_ref.shape[2] // 2

    @pl.when(pl.program_id(1) == 0)
    def _build_weights():
        w0 = w0_ref[0]
        w0re, w0im = w0[:, :half], w0[:, half:]
        cm = cm_ref[0]
        cm_bf = cm.astype(BF16)
        row = lax.broadcasted_iota(jnp.int32, (LANES, LANES), 0)
        col = lax.broadcasted_iota(jnp.int32, (LANES, LANES), 1)
        skip = jnp.where(row == col, jnp.broadcast_to(d_ref[...], (LANES, LANES)), 0.0)
        zero_blk = jnp.zeros((LANES, LANES), BF16)
        for tau in range(q):
            ar = pr_ref[0, tau:tau + 1, :]
            ai = pi_ref[0, tau:tau + 1, :]
            w_tau = jnp.concatenate([w0re * ar - w0im * ai, w0im * ar + w0re * ai], axis=1).astype(BF16)
            t_st = q - 1 - tau
            wst_scr[t_st * LANES:(t_st + 1) * LANES, :] = w_tau
            k_tau = jnp.dot(w_tau, cm_bf, preferred_element_type=F32)
            if tau == 0:
                k_tau = k_tau + skip
            k_bf = k_tau.astype(BF16)
            for t in range(q - tau):
                m_scr[t * LANES:(t + 1) * LANES, (t + tau) * LANES:(t + tau + 1) * LANES] = k_bf
        for t in range(q):
            for t2 in range(t):
                m_scr[t * LANES:(t + 1) * LANES, t2 * LANES:(t2 + 1) * LANES] = zero_blk
        cre, cimn = cm[:half], cm[half:]
        for t in range(q):
            arc = prc_ref[0, :, t + 1:t + 2]
            aic = pic_ref[0, :, t + 1:t + 2]
            wout_scr[:half, t * LANES:(t + 1) * LANES] = (arc * cre + aic * cimn).astype(BF16)
            wout_scr[half:, t * LANES:(t + 1) * LANES] = (arc * cimn - aic * cre).astype(BF16)

    gain = g_ref[...]
    u = jnp.concatenate(
        [(x_ref[t].astype(F32) * lax.rsqrt(ssq_ref[t] * inv_d + EPS) * gain).astype(BF16) for t in range(q)],
        axis=1)
    s_scr[...] = jnp.dot(u, wst_scr[...], preferred_element_type=F32)

    shp = (SUBLANES, half)
    ar = jnp.broadcast_to(pr_ref[0, q:q + 1, :], shp)
    ai = jnp.broadcast_to(pi_ref[0, q:q + 1, :], shp)
    anr = jnp.broadcast_to(pr_ref[0, q + 1:q + 2, :], shp)
    ani = jnp.broadcast_to(pi_ref[0, q + 1:q + 2, :], shp)
    seg = lax.broadcasted_iota(jnp.int32, shp, 0)
    zeros = jnp.zeros(shp, F32)

    def shift_down(x):
        return jnp.where(seg == 0, 0.0, pltpu.roll(x, 1, 0))

    def advance(j, cr, ci):
        sj = s_scr[pl.ds(pl.multiple_of(j * SUBLANES, SUBLANES), SUBLANES), :]
        return ar * cr - ai * ci + sj[:, :half], ar * ci + ai * cr + sj[:, half:]

    er, ei = lax.fori_loop(0, n_steps, lambda j, c: advance(j, *c), (zeros, zeros))
    tr, ti = er, ei
    for _ in range(SSM_SEGMENTS - 1):
        sr, si = shift_down(tr), shift_down(ti)
        tr, ti = er + anr * sr - ani * si, ei + anr * si + ani * sr
    cr0, ci0 = shift_down(tr), shift_down(ti)

    def scan_store(j, c):
        cr, ci = c
        xin_scr[pl.ds(pl.multiple_of(j * SUBLANES, SUBLANES), SUBLANES), :] = jnp.concatenate([cr, ci], axis=1)
        return advance(j, cr, ci)

    lax.fori_loop(0, n_steps, scan_store, (cr0, ci0))

    xin = xin_scr[...].astype(BF16)
    steps_per_panel = 4
    for t0 in range(0, q, steps_per_panel):
        k_hi = (t0 + steps_per_panel) * LANES
        cols = slice(t0 * LANES, k_hi)
        y = (jnp.dot(u[:, :k_hi], m_scr[:k_hi, cols], preferred_element_type=F32)
             + jnp.dot(xin, wout_scr[:, cols], preferred_element_type=F32))
        z = jax.nn.gelu(y)
        for t in range(steps_per_panel):
            o_ref[t0 + t] = z[:, t * LANES:(t + 1) * LANES].astype(o_ref.dtype)


def _complex_pow(zr, zi, n):
    rr, ri = None, None
    br, bi = zr, zi
    while n:
        if n & 1:
            rr, ri = (br, bi) if rr is None else (rr * br - ri * bi, rr * bi + ri * br)
        n >>= 1
        if n:
            br, bi = br * br - bi * bi, 2.0 * br * bi
    return rr, ri


def _s5_tables(a_re, a_im, log_dt, b_re, b_im, c_re, c_im, n_steps):
    g, p = a_re.shape
    gpb = LANES // SSM_GROUP
    nblk = g // gpb
    dt = jnp.exp(log_dt.astype(F32))[:, None]
    ar = a_re.astype(F32)
    ai = a_im.astype(F32)
    mag = jnp.exp(dt * ar)
    ang = dt * ai
    abar_re = mag * jnp.cos(ang)
    abar_im = mag * jnp.sin(ang)
    nr = abar_re - 1.0
    ni = abar_im
    den = ar * ar + ai * ai
    f_re = (nr * ar + ni * ai) / den
    f_im = (ni * ar - nr * ai) / den
    br = b_re.astype(F32)
    bi = b_im.astype(F32)
    bb_re = f_re[..., None] * br - f_im[..., None] * bi
    bb_im = f_re[..., None] * bi + f_im[..., None] * br

    pw_r, pw_i = [jnp.ones_like(abar_re)], [jnp.zeros_like(abar_re)]
    for _ in range(SSM_CHUNK):
        pw_r.append(pw_r[-1] * abar_re - pw_i[-1] * abar_im)
        pw_i.append(pw_r[-2] * abar_im + pw_i[-1] * abar_re)
    seg_r, seg_i = _complex_pow(pw_r[-1], pw_i[-1], n_steps)
    pw_r.append(seg_r)
    pw_i.append(seg_i)
    n_rows = 24
    pr = jnp.stack(pw_r, 0).reshape(len(pw_r), nblk, gpb * p).transpose(1, 0, 2)
    pi = jnp.stack(pw_i, 0).reshape(len(pw_i), nblk, gpb * p).transpose(1, 0, 2)
    pad = ((0, 0), (0, n_rows - pr.shape[1]), (0, 0))
    pr = jnp.pad(pr, pad)
    pi = jnp.pad(pi, pad)
    prc = pr.transpose(0, 2, 1)
    pic = pi.transpose(0, 2, 1)

    eye = jnp.eye(gpb, dtype=F32)

    def in_to_state(bb):
        v = bb.reshape(nblk, gpb, p, SSM_GROUP).transpose(0, 1, 3, 2)
        e = v[:, :, :, None, :] * eye[None, :, None, :, None]
        return e.reshape(nblk, LANES, gpb * p)

    def state_to_out(c):
        v = c.astype(F32).reshape(nblk, gpb, SSM_GROUP, p).transpose(0, 1, 3, 2)
        e = v[:, :, :, None, :] * eye[None, :, None, :, None]
        return e.reshape(nblk, gpb * p, LANES)

    w0 = jnp.concatenate([in_to_state(bb_re), in_to_state(bb_im)], axis=2)
    cm = jnp.concatenate([state_to_out(c_re), -state_to_out(c_im)], axis=1)
    return w0, cm, pr, pi, prc, pic


def _s5_mixer(hb, ssq, gain, bsz, lp, a_re, a_im, log_dt, b_re, b_im, c_re, c_im, d_skip):
    t, d = hb.shape
    q, nseg = SSM_CHUNK, SSM_SEGMENTS
    assert lp % (q * nseg * 2) == 0
    n_steps = lp // (q * nseg)
    rows = n_steps * nseg

    def chunk_layout(v):
        w = v.shape[-1]
        return v.reshape(bsz, nseg, n_steps, q, w).transpose(3, 0, 2, 1, 4).reshape(q, bsz * rows, w)

    x = chunk_layout(hb)
    ssq_c = chunk_layout(ssq)

    w0, cm, pr, pi, prc, pic = _s5_tables(a_re, a_im, log_dt, b_re, b_im, c_re, c_im, n_steps)
    nblk = d // LANES
    n_state = w0.shape[2]
    kern = functools.partial(_s5_kernel, n_steps=n_steps, inv_d=1.0 / d)
    z = pl.pallas_call(
        kern,
        grid=(nblk, bsz),
        in_specs=[pl.BlockSpec((q, rows, LANES), lambda k, b: (0, b, k)),
                  pl.BlockSpec((q, rows, 1), lambda k, b: (0, b, 0)),
                  pl.BlockSpec((1, LANES), lambda k, b: (0, k)),
                  pl.BlockSpec((1, LANES, n_state), lambda k, b: (k, 0, 0)),
                  pl.BlockSpec((1, n_state, LANES), lambda k, b: (k, 0, 0)),
                  pl.BlockSpec((1,) + pr.shape[1:], lambda k, b: (k, 0, 0)),
                  pl.BlockSpec((1,) + pi.shape[1:], lambda k, b: (k, 0, 0)),
                  pl.BlockSpec((1,) + prc.shape[1:], lambda k, b: (k, 0, 0)),
                  pl.BlockSpec((1,) + pic.shape[1:], lambda k, b: (k, 0, 0)),
                  pl.BlockSpec((1, LANES), lambda k, b: (0, k))],
        out_specs=pl.BlockSpec((q, rows, LANES), lambda k, b: (0, b, k)),
        out_shape=jax.ShapeDtypeStruct((q, bsz * rows, d), BF16),
        scratch_shapes=[pltpu.VMEM((q * LANES, q * LANES), BF16),
                        pltpu.VMEM((q * LANES, n_state), BF16),
                        pltpu.VMEM((n_state, q * LANES), BF16),
                        pltpu.VMEM((rows, n_state), F32),
                        pltpu.VMEM((rows, n_state), F32)],
        compiler_params=_params("arbitrary", "arbitrary"),
        name="s5_mixer",
    )(x, ssq_c, gain.reshape(1, d).astype(F32), w0, cm, pr, pi, prc, pic, d_skip.reshape(1, d).astype(F32))
    return z.reshape(q, bsz, n_steps, nseg, d).transpose(1, 3, 2, 0, 4).reshape(t, d)


def _rope_tables(length):
    inv = ROPE_THETA ** (-jnp.arange(0, HEAD_DIM, 2, dtype=F32) / HEAD_DIM)
    ang = jnp.arange(length, dtype=F32)[:, None] * inv[None, :]
    cos, sin = jnp.cos(ang), jnp.sin(ang)
    return jnp.concatenate([cos, cos], axis=1), jnp.concatenate([-sin, sin], axis=1)


def _lambda_init(layer_idx):
    return 0.8 - 0.6 * math.exp(-0.3 * layer_idx)


def kernel(x, meta_tokens, norm_mix_g, norm_mlp_g, da_w_qkv, da_q_norm_g, da_k_norm_g, da_lambda, da_subln_g, da_w_o, ssm_a_re, ssm_a_im, ssm_log_dt, ssm_b_re, ssm_b_im, ssm_c_re, ssm_c_im, ssm_d, ssm_w_glu, ssm_b_glu, mlp_w_up, mlp_w_down):
    bsz, seq, d = x.shape
    depth = norm_mix_g.shape[0]
    n_mixers = 2
    length = N_META + seq
    lp = ((length + SEQ_ALIGN - 1) // SEQ_ALIGN) * SEQ_ALIGN
    assert meta_tokens.shape[0] == N_META and depth >= 1
    h, hn = _embed_norm(x, meta_tokens, norm_mix_g[0], lp)
    cos, sin = (jnp.tile(tab, (bsz, 1)) for tab in _rope_tables(lp))
    def up_job(i):
        return mlp_w_up, i, norm_mlp_g[i]

    def bf16_weight(name, stack, layer, gain=None):
        if name not in wb:
            src = stack[layer]
            wb[name] = (src if gain is None else src * gain.astype(F32)[:, None]).astype(BF16)
        return wb[name]

    wb = {}
    hb = ssq = None
    for i in range(depth):
        j = i // n_mixers
        nxt_s5 = i + 1 < depth and (i + 1) % n_mixers == 1
        if i % n_mixers == 0:
            if i > 0:
                hn = _rmsnorm(h, norm_mix_g[i])
            qkv, wb["o", j], wb["up", i] = _qkv_proj(
                hn, bf16_weight(("qkv", j), da_w_qkv, j), cos, sin, da_q_norm_g[j], da_k_norm_g[j],
                [(da_w_o, j, None), up_job(i)])
            att = _diff_attention(qkv, da_lambda[j], da_subln_g[j], bsz, lp, d, _lambda_init(i))
            h, hb, ssq = _proj_residual(att, wb["o", j], h)
        else:
            z = _s5_mixer(hb, ssq, norm_mix_g[i], bsz, lp, ssm_a_re[j], ssm_a_im[j], ssm_log_dt[j],
                          ssm_b_re[j], ssm_b_im[j], ssm_c_re[j], ssm_c_im[j], ssm_d[j])
            h, hb, ssq = _glu_residual(z, bf16_weight(("glu", j), ssm_w_glu, j), ssm_b_glu[j], h)
        jobs = [(("down", i), mlp_w_down, i, None)]
        if nxt_s5:
            jobs += [(("glu", (i + 1) // n_mixers), ssm_w_glu, (i + 1) // n_mixers, None),
                     (("up", i + 1),) + up_job(i + 1)]
        f, *casts = _mlp_up(hb, bf16_weight(("up", i), *up_job(i)), ssq, [job[1:] for job in jobs])
        wb.update({job[0]: c for job, c in zip(jobs, casts)})
        if i == depth - 1:
            return _mlp_down_final(f, wb["down", i], h, bsz, lp, N_META, seq).reshape(bsz, seq, d)
        if nxt_s5:
            h, hb, ssq = _mlp_down(f, wb["down", i], h, True)
        else:
            h = _mlp_down(f, wb["down", i], h, False)
```

```python
import functools
import math

import jax
import jax.numpy as jnp
from jax import lax
from jax.experimental import pallas as pl
from jax.experimental.pallas import tpu as pltpu

N_META = 16
SEQ_ALIGN = 256
HEAD_DIM = 128
LOG2E = 1.4426950408889634
ATTN_ROW_BLOCK = 64
ROPE_THETA = 10000.0
SSM_GROUP = 16
SSM_CHUNK = 16
SSM_SEGMENTS = 8
MXU_COLS = 256
LANES = 128
SUBLANES = 8
EPS = 1e-6
VMEM_LIMIT_BYTES = 56 * 1024 * 1024

F32 = jnp.float32
BF16 = jnp.bfloat16


def _pick_tile(n, target, mult):
    best = None
    for t in range(mult, min(n, target) + 1, mult):
        if n % t == 0:
            best = t
    assert best is not None, (n, target, mult)
    return best


def _row_tile(t):
    return _pick_tile(t, 1536, SEQ_ALIGN if t % SEQ_ALIGN == 0 else 16)


def _params(*sem):
    return pltpu.CompilerParams(dimension_semantics=sem, vmem_limit_bytes=VMEM_LIMIT_BYTES)


class _SideCasts:
    def __init__(self, jobs, n_steps, step_of):
        self.operands, self.in_specs, self.out_specs, self.out_shapes, self.has_gain = [], [], [], [], []
        for src, layer, gain in jobs:
            _, r, c = src.shape
            cr = next(x for x in (16 << p for p in range(24)) if r % x == 0 and r // x <= n_steps)

            def idx(*g, last=r // cr - 1):
                return (jnp.minimum(step_of(*g), last), 0)

            def src_idx(*g, layer=layer, last=r // cr - 1):
                return (layer, jnp.minimum(step_of(*g), last), 0)

            self.operands.append(src)
            self.in_specs.append(pl.BlockSpec((None, cr, c), src_idx))
            if gain is not None:
                self.operands.append(gain.reshape(r, 1).astype(F32))
                self.in_specs.append(pl.BlockSpec((cr, 1), idx))
            self.out_specs.append(pl.BlockSpec((cr, c), idx))
            self.out_shapes.append(jax.ShapeDtypeStruct((r, c), BF16))
            self.has_gain.append(gain is not None)
        self.n_in = len(self.operands)
        self.n_out = len(self.out_shapes)

    def run(self, in_refs, out_refs):
        refs = iter(in_refs)
        for has_gain, o_ref in zip(self.has_gain, out_refs):
            v = next(refs)[...]
            if has_gain:
                v = v * next(refs)[...]
            o_ref[...] = v.astype(o_ref.dtype)


def _rmsnorm_kernel(x_ref, g_ref, o_ref):
    x = x_ref[...]
    ms = jnp.mean(x * x, axis=-1, keepdims=True)
    o_ref[...] = (x * lax.rsqrt(ms + EPS) * g_ref[...]).astype(o_ref.dtype)


def _rmsnorm(x, g):
    t, d = x.shape
    tr = _pick_tile(t, 256, 16)
    return pl.pallas_call(
        _rmsnorm_kernel,
        grid=(t // tr,),
        in_specs=[pl.BlockSpec((tr, d), lambda i: (i, 0)),
                  pl.BlockSpec((1, d), lambda i: (0, 0))],
        out_specs=pl.BlockSpec((tr, d), lambda i: (i, 0)),
        out_shape=jax.ShapeDtypeStruct((t, d), BF16),
        compiler_params=_params("parallel"),
        name="rmsnorm",
    )(x, g.reshape(1, d).astype(F32))


def _embed_norm_kernel(x_ref, meta_ref, g_ref, h_ref, hn_ref, *, n_meta, seq):
    tr = h_ref.shape[0]
    i = pl.program_id(1)

    @pl.when(i == 0)
    def _():
        h_ref[:n_meta, :] = meta_ref[...]
        h_ref[n_meta:, :] = x_ref[:tr - n_meta, :]

    @pl.when(i > 0)
    def _():
        row = i * tr + lax.broadcasted_iota(jnp.int32, h_ref.shape, 0)
        h_ref[...] = jnp.where(row < n_meta + seq, x_ref[...], 0.0)

    h = h_ref[...]
    ms = jnp.mean(h * h, axis=-1, keepdims=True)
    hn_ref[...] = (h * lax.rsqrt(ms + EPS) * g_ref[...]).astype(hn_ref.dtype)


def _embed_norm(x, meta_tokens, g, lp):
    bsz, seq, d = x.shape
    n_meta = meta_tokens.shape[0]
    tr = _pick_tile(lp, 256, 16)
    assert n_meta % 16 == 0 and n_meta < tr and tr < seq
    nt = lp // tr
    tile = pl.BlockSpec((tr, d), lambda b, i: (b * nt + i, 0))

    def x_window(b, i):
        return (b, pl.multiple_of(jnp.maximum(i * tr - n_meta, 0), 16), 0)

    return pl.pallas_call(
        functools.partial(_embed_norm_kernel, n_meta=n_meta, seq=seq),
        grid=(bsz, nt),
        in_specs=[pl.BlockSpec((pl.Squeezed(), pl.Element(tr, (0, lp - n_meta - seq)), pl.Element(d)), x_window),
                  pl.BlockSpec((n_meta, d), lambda b, i: (0, 0)),
                  pl.BlockSpec((1, d), lambda b, i: (0, 0))],
        out_specs=(tile, tile),
        out_shape=(jax.ShapeDtypeStruct((bsz * lp, d), F32), jax.ShapeDtypeStruct((bsz * lp, d), BF16)),
        compiler_params=_params("parallel", "arbitrary"),
        name="embed_norm",
    )(x, meta_tokens.astype(F32), g.reshape(1, d).astype(F32))


def _qkv_kernel(*refs, n_q_blocks, q_scale, side):
    a_ref, w_ref, cos_ref, sin_ref, qg_ref, kg_ref = refs[:6]
    side_in = refs[6:6 + side.n_in]
    o_ref = refs[6 + side.n_in]
    side_out = refs[7 + side.n_in:]
    acc = jnp.dot(a_ref[...], w_ref[...], preferred_element_type=F32)
    j = pl.program_id(1)
    tn = acc.shape[1]

    def norm_rope(g, scale):
        cos = cos_ref[...]
        sin = sin_ref[...]
        for u in range(tn // HEAD_DIM):
            x = acc[:, u * HEAD_DIM:(u + 1) * HEAD_DIM]
            ms = jnp.mean(x * x, axis=-1, keepdims=True)
            y = x * lax.rsqrt(ms + EPS) * g
            y = y * cos + pltpu.roll(y, HEAD_DIM // 2, 1) * sin
            if scale != 1.0:
                y = y * scale
            o_ref[:, u * HEAD_DIM:(u + 1) * HEAD_DIM] = y.astype(o_ref.dtype)

    @pl.when(j < n_q_blocks)
    def _():
        norm_rope(qg_ref[...], q_scale)
        side.run(side_in, side_out)

    @pl.when(jnp.logical_and(j >= n_q_blocks, j < 2 * n_q_blocks))
    def _():
        norm_rope(kg_ref[...], 1.0)
        side.run(side_in, side_out)

    @pl.when(j >= 2 * n_q_blocks)
    def _():
        o_ref[...] = acc.astype(o_ref.dtype)
        side.run(side_in, side_out)


def _qkv_proj(a, w, cos, sin, q_g, k_g, cast_jobs):
    t, d = a.shape
    n = w.shape[1]
    tm = _row_tile(t)
    tn = _pick_tile(d, 512, HEAD_DIM)
    nj = n // tn
    side = _SideCasts(cast_jobs, (t // tm) * nj, lambda i, j: i * nj + j)
    kern = functools.partial(_qkv_kernel, n_q_blocks=d // tn, q_scale=HEAD_DIM ** -0.5 * LOG2E, side=side)
    return pl.pallas_call(
        kern,
        grid=(t // tm, nj),
        in_specs=[pl.BlockSpec((tm, d), lambda i, j: (i, 0)),
                  pl.BlockSpec((d, tn), lambda i, j: (0, j)),
                  pl.BlockSpec((tm, HEAD_DIM), lambda i, j: (i, 0)),
                  pl.BlockSpec((tm, HEAD_DIM), lambda i, j: (i, 0)),
                  pl.BlockSpec((1, HEAD_DIM), lambda i, j: (0, 0)),
                  pl.BlockSpec((1, HEAD_DIM), lambda i, j: (0, 0))] + side.in_specs,
        out_specs=[pl.BlockSpec((tm, tn), lambda i, j: (i, j))] + side.out_specs,
        out_shape=[jax.ShapeDtypeStruct((t, n), BF16)] + side.out_shapes,
        compiler_params=_params("arbitrary", "arbitrary"),
        name="qkv_proj",
    )(a, w, cos, sin, q_g.reshape(1, HEAD_DIM).astype(F32), k_g.reshape(1, HEAD_DIM).astype(F32),
      *side.operands)


def _attn_kernel(q_ref, k_ref, v_ref, lam_ref, g_ref, o_ref,
                 m1_ref, l1_ref, a1_ref, m2_ref, l2_ref, a2_ref, s_even_ref, s_odd_ref, p_ref, alpha_ref,
                 *, blk, lam_init):
    s_bufs = (s_even_ref, s_odd_ref)
    nq = q_ref.shape[0] // blk
    stats = ((m1_ref, l1_ref, a1_ref), (m2_ref, l2_ref, a2_ref))
    lv = lam_ref[...]
    lam = (jnp.exp(jnp.sum(lv[0:1] * lv[1:2], axis=-1, keepdims=True))
           - jnp.exp(jnp.sum(lv[2:3] * lv[3:4], axis=-1, keepdims=True)) + lam_init)

    def rows(i):
        return pl.ds(pl.multiple_of(i * blk, blk), blk)

    def init_stats():
        for m_ref, l_ref, a_ref in stats:
            m_ref[...] = jnp.full(m_ref.shape, -jnp.inf, F32)
            l_ref[...] = jnp.zeros(l_ref.shape, F32)
            a_ref[...] = jnp.zeros(a_ref.shape, F32)

    def scores(qi, j, s_ref):
        q = q_ref[rows(qi), :]
        kk = k_ref[rows(j), :]
        for c in range(2):
            s_ref[c, :, :blk] = lax.dot_general(
                q[:, c * HEAD_DIM:(c + 1) * HEAD_DIM], kk[:, c * HEAD_DIM:(c + 1) * HEAD_DIM],
                (((1,), (1,)), ((), ())), preferred_element_type=F32)

    def softmax_pv(j, s_ref, masked):
        vv = v_ref[rows(j), :]
        for c, (m_ref, l_ref, a_ref) in enumerate(stats):
            for r in range(0, blk, ATTN_ROW_BLOCK):
                rb = slice(r, r + ATTN_ROW_BLOCK)
                s = s_ref[c, rb, :blk]
                if masked:
                    row = r + lax.broadcasted_iota(jnp.int32, s.shape, 0)
                    col = lax.broadcasted_iota(jnp.int32, s.shape, 1)
                    s = jnp.where(col <= row, s, -jnp.inf)
                m_old = m_ref[rb, :]
                m_new = jnp.maximum(m_old, jnp.max(s, axis=-1, keepdims=True))
                alpha = jnp.exp2(m_old - m_new)
                p = jnp.exp2(s - m_new)
                part = p[:, :LANES]
                for u in range(1, blk // LANES):
                    part = part + p[:, u * LANES:(u + 1) * LANES]
                l_ref[rb, :] = alpha * l_ref[rb, :] + part
                p_ref[c, rb, :] = p.astype(BF16)
                alpha_ref[c, rb, :] = alpha
                m_ref[rb, :] = m_new
            a_ref[...] = alpha_ref[c] * a_ref[...] + jnp.dot(p_ref[c], vv, preferred_element_type=F32)

    def finalize(qi):
        l1 = jnp.sum(l1_ref[...], axis=-1, keepdims=True)
        l2 = jnp.sum(l2_ref[...], axis=-1, keepdims=True)
        o = a1_ref[...] / l1 - lam * (a2_ref[...] / l2)
        ms = jnp.mean(o * o, axis=-1, keepdims=True)
        y = o * lax.rsqrt(ms + EPS) * g_ref[...] * (1.0 - lam_init)
        o_ref[rows(qi), :] = y.astype(o_ref.dtype)

    def items(n, *work):
        for parity in range(2):
            @pl.when(n % 2 == parity)
            def _():
                for step, (next_qi, next_j, j, masked) in enumerate(work):
                    mine = (parity + step) % 2
                    scores(next_qi, next_j, s_bufs[1 - mine])
                    softmax_pv(j, s_bufs[mine], masked)

    init_stats()
    scores(0, 0, s_bufs[0])

    def q_block(qi, carry):
        base = (qi * (qi + 1)) // 2
        diagonal = (jnp.minimum(qi + 1, nq - 1), 0, qi, True)

        def full_chunk(j, c):
            items(base + j, (qi, j + 1, j, False))
            return c

        lax.fori_loop(0, qi, full_chunk, 0)
        items(base + qi, diagonal)
        finalize(qi)
        init_stats()
        return carry

    lax.fori_loop(0, nq, q_block, 0)


def _diff_attention(qkv, lam_vecs, subln_g, bsz, lp, d, lam_init):
    t = qkv.shape[0]
    n_heads = d // (2 * HEAD_DIM)
    hw = 2 * HEAD_DIM
    blk = _pick_tile(lp, 768, SEQ_ALIGN)
    kern = functools.partial(_attn_kernel, blk=blk, lam_init=lam_init)
    return pl.pallas_call(
        kern,
        grid=(bsz, n_heads),
        in_specs=[pl.BlockSpec((lp, hw), lambda b, h: (b, h)),
                  pl.BlockSpec((lp, hw), lambda b, h: (b, n_heads + h)),
                  pl.BlockSpec((lp, hw), lambda b, h: (b, 2 * n_heads + h)),
                  pl.BlockSpec((4, HEAD_DIM), lambda b, h: (0, 0)),
                  pl.BlockSpec((1, hw), lambda b, h: (0, 0))],
        out_specs=pl.BlockSpec((lp, hw), lambda b, h: (b, h)),
        out_shape=jax.ShapeDtypeStruct((t, d), BF16),
        scratch_shapes=[pltpu.VMEM((blk, 1), F32), pltpu.VMEM((blk, LANES), F32), pltpu.VMEM((blk, hw), F32),
                        pltpu.VMEM((blk, 1), F32), pltpu.VMEM((blk, LANES), F32), pltpu.VMEM((blk, hw), F32),
                        pltpu.VMEM((2, blk, blk + LANES), F32), pltpu.VMEM((2, blk, blk + LANES), F32),
                        pltpu.VMEM((2, blk, blk), BF16), pltpu.VMEM((2, blk, 1), F32)],
        compiler_params=_params("parallel", "parallel"),
        name="diff_attention",
    )(qkv, qkv, qkv, lam_vecs.astype(F32), subln_g.reshape(1, hw).astype(F32))


def _col_panels(n):
    return [slice(u, u + MXU_COLS) for u in range(0, n, MXU_COLS)]


def _emit_norm_inputs(h, cols, hb_ref, ssq_ref):
    hb_ref[:, cols] = h.astype(hb_ref.dtype)
    ssq_ref[...] += jnp.sum(h * h, axis=-1, keepdims=True)


def _proj_residual_kernel(a_ref, w_ref, r_ref, o_ref, hb_ref, ssq_ref):
    @pl.when(pl.program_id(1) == 0)
    def _():
        ssq_ref[...] = jnp.zeros(ssq_ref.shape, F32)

    for cols in _col_panels(o_ref.shape[1]):
        h = r_ref[:, cols] + jnp.dot(a_ref[...], w_ref[:, cols], preferred_element_type=F32)
        o_ref[:, cols] = h
        _emit_norm_inputs(h, cols, hb_ref, ssq_ref)


def _norm_out_shapes(t, n):
    return (jax.ShapeDtypeStruct((t, n), F32), jax.ShapeDtypeStruct((t, n), BF16),
            jax.ShapeDtypeStruct((t, 1), F32))


def _proj_residual(a, w, res):
    t, k = a.shape
    n = w.shape[1]
    tm = _row_tile(t)
    tn = _pick_tile(n, 512, LANES)
    return pl.pallas_call(
        _proj_residual_kernel,
        grid=(t // tm, n // tn),
        in_specs=[pl.BlockSpec((tm, k), lambda i, j: (i, 0)),
                  pl.BlockSpec((k, tn), lambda i, j: (0, j)),
                  pl.BlockSpec((tm, tn), lambda i, j: (i, j))],
        out_specs=(pl.BlockSpec((tm, tn), lambda i, j: (i, j)),
                   pl.BlockSpec((tm, tn), lambda i, j: (i, j)),
                   pl.BlockSpec((tm, 1), lambda i, j: (i, 0))),
        out_shape=_norm_out_shapes(t, n),
        compiler_params=_params("parallel", "arbitrary"),
        name="proj_residual",
    )(a, w, res)


def _mlp_up_kernel(*refs, inv_d, side):
    a_ref, w_ref, ssq_ref = refs[:3]
    side_in = refs[3:3 + side.n_in]
    o_ref = refs[3 + side.n_in]
    side_out = refs[4 + side.n_in:]
    r2 = 1.0 / (ssq_ref[...] * inv_d + EPS)
    side.run(side_in, side_out)
    for cols in _col_panels(o_ref.shape[1]):
        u = jnp.dot(a_ref[...], w_ref[:, cols], preferred_element_type=F32)
        o_ref[:, cols] = (jnp.square(jnp.maximum(u, 0.0)) * r2).astype(o_ref.dtype)


def _mlp_up(hb, w, ssq, cast_jobs):
    t, k = hb.shape
    n = w.shape[1]
    tm = _row_tile(t)
    tn = _pick_tile(n, 512, LANES)
    nj = n // tn
    side = _SideCasts(cast_jobs, (t // tm) * nj, lambda i, j: i * nj + j)
    return pl.pallas_call(
        functools.partial(_mlp_up_kernel, inv_d=1.0 / k, side=side),
        grid=(t // tm, nj),
        in_specs=[pl.BlockSpec((tm, k), lambda i, j: (i, 0)),
                  pl.BlockSpec((k, tn), lambda i, j: (0, j)),
                  pl.BlockSpec((tm, 1), lambda i, j: (i, 0))] + side.in_specs,
        out_specs=[pl.BlockSpec((tm, tn), lambda i, j: (i, j))] + side.out_specs,
        out_shape=[jax.ShapeDtypeStruct((t, n), BF16)] + side.out_shapes,
        compiler_params=_params("arbitrary", "arbitrary"),
        name="mlp_up",
    )(hb, w, ssq, *side.operands)


def _mlp_down_kernel(a_ref, w_ref, r_ref, o_ref, *norm_refs, nk, col_axis=1):
    j = pl.program_id(col_axis)
    k = pl.program_id(col_axis + 1)

    def accumulate(base_ref, last):
        for cols in _col_panels(o_ref.shape[1]):
            h = base_ref[:, cols] + jnp.dot(a_ref[...], w_ref[:, cols], preferred_element_type=F32)
            o_ref[:, cols] = h
            if last and norm_refs:
                _emit_norm_inputs(h, cols, *norm_refs)

    if norm_refs:
        @pl.when(jnp.logical_and(j == 0, k == 0))
        def _():
            norm_refs[1][...] = jnp.zeros(norm_refs[1].shape, F32)

    if nk == 1:
        accumulate(r_ref, True)
        return

    @pl.when(k == 0)
    def _():
        accumulate(r_ref, False)

    @pl.when(jnp.logical_and(k > 0, k < nk - 1))
    def _():
        accumulate(o_ref, False)

    @pl.when(k == nk - 1)
    def _():
        accumulate(o_ref, True)


def _mlp_down(a, w, res, emit_norm, tk_target=2048):
    t, kdim = a.shape
    n = w.shape[1]
    tm = _row_tile(t)
    tn = _pick_tile(n, 1024, LANES)
    tk = _pick_tile(kdim, tk_target, LANES)
    nk = kdim // tk
    tile = pl.BlockSpec((tm, tn), lambda i, j, k: (i, j))
    if emit_norm:
        out_specs = (tile, tile, pl.BlockSpec((tm, 1), lambda i, j, k: (i, 0)))
        out_shape = _norm_out_shapes(t, n)
    else:
        out_specs = tile
        out_shape = jax.ShapeDtypeStruct((t, n), F32)
    return pl.pallas_call(
        functools.partial(_mlp_down_kernel, nk=nk),
        grid=(t // tm, n // tn, nk),
        in_specs=[pl.BlockSpec((tm, tk), lambda i, j, k: (i, k)),
                  pl.BlockSpec((tk, tn), lambda i, j, k: (k, j)),
                  tile],
        out_specs=out_specs,
        out_shape=out_shape,
        compiler_params=_params("parallel", "arbitrary", "arbitrary"),
        name="mlp_down",
    )(a, w, res)


def _mlp_down_final(a, w, res, bsz, lp, row0, seq):
    kdim = a.shape[1]
    n = w.shape[1]
    tm = _pick_tile(seq, 1024, 16)
    nt = seq // tm
    tn = _pick_tile(n, 1024, LANES)
    tk = _pick_tile(kdim, 4096, LANES)
    nk = kdim // tk

    assert lp % 16 == 0 and row0 % 16 == 0 and tm % 16 == 0

    def row_start(b, m):
        return pl.multiple_of(b * lp + row0 + m * tm, 16)

    return pl.pallas_call(
        functools.partial(_mlp_down_kernel, nk=nk, col_axis=2),
        grid=(bsz, nt, n // tn, nk),
        in_specs=[pl.BlockSpec((pl.Element(tm), pl.Element(tk)), lambda b, m, j, k: (row_start(b, m), k * tk)),
                  pl.BlockSpec((tk, tn), lambda b, m, j, k: (k, j)),
                  pl.BlockSpec((pl.Element(tm), pl.Element(tn)), lambda b, m, j, k: (row_start(b, m), j * tn))],
        out_specs=pl.BlockSpec((tm, tn), lambda b, m, j, k: (b * nt + m, j)),
        out_shape=jax.ShapeDtypeStruct((bsz * seq, n), F32),
        compiler_params=_params("parallel", "parallel", "arbitrary", "arbitrary"),
        name="mlp_down_final",
    )(a, w, res)


def _glu_kernel(a_ref, wv_ref, wg_ref, bv_ref, bg_ref, r_ref, o_ref, hb_ref, ssq_ref, *, col_axis=1):
    @pl.when(pl.program_id(col_axis) == 0)
    def _():
        ssq_ref[...] = jnp.zeros(ssq_ref.shape, F32)

    for cols in _col_panels(o_ref.shape[1]):
        val = jnp.dot(a_ref[...], wv_ref[:, cols], preferred_element_type=F32) + bv_ref[:, cols]
        gate = jnp.dot(a_ref[...], wg_ref[:, cols], preferred_element_type=F32) + bg_ref[:, cols]
        h = r_ref[:, cols] + val * jax.nn.sigmoid(gate)
        o_ref[:, cols] = h
        _emit_norm_inputs(h, cols, hb_ref, ssq_ref)


def _glu_residual(a, w, bias, res):
    t, k = a.shape
    n = w.shape[1] // 2
    tm = _row_tile(t)
    tn = _pick_tile(n, 256, LANES)
    nb = n // tn
    b2 = bias.reshape(1, 2 * n).astype(F32)
    return pl.pallas_call(
        _glu_kernel,
        grid=(t // tm, nb),
        in_specs=[pl.BlockSpec((tm, k), lambda i, j: (i, 0)),
                  pl.BlockSpec((k, tn), lambda i, j: (0, j)),
                  pl.BlockSpec((k, tn), lambda i, j: (0, nb + j)),
                  pl.BlockSpec((1, tn), lambda i, j: (0, j)),
                  pl.BlockSpec((1, tn), lambda i, j: (0, nb + j)),
                  pl.BlockSpec((tm, tn), lambda i, j: (i, j))],
        out_specs=(pl.BlockSpec((tm, tn), lambda i, j: (i, j)),
                   pl.BlockSpec((tm, tn), lambda i, j: (i, j)),
                   pl.BlockSpec((tm, 1), lambda i, j: (i, 0))),
        out_shape=_norm_out_shapes(t, n),
        compiler_params=_params("parallel", "arbitrary"),
        name="glu_residual",
    )(a, w, w, b2, b2, res)


def _glu_residual_rows(a, w, bias, res, bsz, lp, row0, seq):
    k = a.shape[1]
    n = w.shape[1] // 2
    tm = _pick_tile(seq, 1024, 16)
    nt = seq // tm
    tn = _pick_tile(n, 256, LANES)
    nb = n // tn
    b2 = bias.reshape(1, 2 * n).astype(F32)
    assert lp % 16 == 0 and row0 % 16 == 0 and tm % 16 == 0

    def row_start(b, m):
        return pl.multiple_of(b * lp + row0 + m * tm, 16)

    tile = pl.BlockSpec((tm, tn), lambda b, m, j: (b * nt + m, j))
    return pl.pallas_call(
        functools.partial(_glu_kernel, col_axis=2),
        grid=(bsz, nt, nb),
        in_specs=[pl.BlockSpec((pl.Element(tm), pl.Element(k)), lambda b, m, j: (row_start(b, m), 0)),
                  pl.BlockSpec((k, tn), lambda b, m, j: (0, j)),
                  pl.BlockSpec((k, tn), lambda b, m, j: (0, nb + j)),
                  pl.BlockSpec((1, tn), lambda b, m, j: (0, j)),
                  pl.BlockSpec((1, tn), lambda b, m, j: (0, nb + j)),
                  pl.BlockSpec((pl.Element(tm), pl.Element(tn)), lambda b, m, j: (row_start(b, m), j * tn))],
        out_specs=(tile, tile, pl.BlockSpec((tm, 1), lambda b, m, j: (b * nt + m, 0))),
        out_shape=_norm_out_shapes(bsz * seq, n),
        compiler_params=_params("parallel", "parallel", "arbitrary"),
        name="glu_residual_rows",
    )(a, w, w, b2, b2, res)


def _s5_kernel(x_ref, ssq_ref, g_ref, w0_ref, cm_ref, pr_ref, pi_ref, prc_ref, pic_ref, d_ref, o_ref,
               m_scr, wst_scr, wout_scr, s_scr, xin_scr, *, n_steps, inv_d):
    q = SSM_CHUNK
    half = w0_ref.shape[2] // 2

    @pl.when(pl.program_id(1) == 0)
    def _build_weights():
        w0 = w0_ref[0]
        w0re, w0im = w0[:, :half], w0[:, half:]
        cm = cm_ref[0]
        cm_bf = cm.astype(BF16)
        row = lax.broadcasted_iota(jnp.int32, (LANES, LANES), 0)
        col = lax.broadcasted_iota(jnp.int32, (LANES, LANES), 1)
        skip = jnp.where(row == col, jnp.broadcast_to(d_ref[...], (LANES, LANES)), 0.0)
        zero_blk = jnp.zeros((LANES, LANES), BF16)
        for tau in range(q):
            ar = pr_ref[0, tau:tau + 1, :]
            ai = pi_ref[0, tau:tau + 1, :]
            w_tau = jnp.concatenate([w0re * ar - w0im * ai, w0im * ar + w0re * ai], axis=1).astype(BF16)
            t_st = q - 1 - tau
            wst_scr[t_st * LANES:(t_st + 1) * LANES, :] = w_tau
            k_tau = jnp.dot(w_tau, cm_bf, preferred_element_type=F32)
            if tau == 0:
                k_tau = k_tau + skip
            k_bf = k_tau.astype(BF16)
            for t in range(q - tau):
                m_scr[t * LANES:(t + 1) * LANES, (t + tau) * LANES:(t + tau + 1) * LANES] = k_bf
        for t in range(q):
            for t2 in range(t):
                m_scr[t * LANES:(t + 1) * LANES, t2 * LANES:(t2 + 1) * LANES] = zero_blk
        cre, cimn = cm[:half], cm[half:]
        for t in range(q):
            arc = prc_ref[0, :, t + 1:t + 2]
            aic = pic_ref[0, :, t + 1:t + 2]
            wout_scr[:half, t * LANES:(t + 1) * LANES] = (arc * cre + aic * cimn).astype(BF16)
            wout_scr[half:, t * LANES:(t + 1) * LANES] = (arc * cimn - aic * cre).astype(BF16)

    gain = g_ref[...]
    u = jnp.concatenate(
        [(x_ref[t].astype(F32) * lax.rsqrt(ssq_ref[t] * inv_d + EPS) * gain).astype(BF16) for t in range(q)],
        axis=1)
    s_scr[...] = jnp.dot(u, wst_scr[...], preferred_element_type=F32)

    shp = (SUBLANES, half)
    ar = jnp.broadcast_to(pr_ref[0, q:q + 1, :], shp)
    ai = jnp.broadcast_to(pi_ref[0, q:q + 1, :], shp)
    anr = jnp.broadcast_to(pr_ref[0, q + 1:q + 2, :], shp)
    ani = jnp.broadcast_to(pi_ref[0, q + 1:q + 2, :], shp)
    seg = lax.broadcasted_iota(jnp.int32, shp, 0)
    zeros = jnp.zeros(shp, F32)

    def shift_down(x):
        return jnp.where(seg == 0, 0.0, pltpu.roll(x, 1, 0))

    def advance(j, cr, ci):
        sj = s_scr[pl.ds(pl.multiple_of(j * SUBLANES, SUBLANES), SUBLANES), :]
        return ar * cr - ai * ci + sj[:, :half], ar * ci + ai * cr + sj[:, half:]

    er, ei = lax.fori_loop(0, n_steps, lambda j, c: advance(j, *c), (zeros, zeros))
    tr, ti = er, ei
    for _ in range(SSM_SEGMENTS - 1):
        sr, si = shift_down(tr), shift_down(ti)
        tr, ti = er + anr * sr - ani * si, ei + anr * si + ani * sr
    cr0, ci0 = shift_down(tr), shift_down(ti)

    def scan_store(j, c):
        cr, ci = c
        xin_scr[pl.ds(pl.multiple_of(j * SUBLANES, SUBLANES), SUBLANES), :] = jnp.concatenate([cr, ci], axis=1)
        return advance(j, cr, ci)

    lax.fori_loop(0, n_steps, scan_store, (cr0, ci0))

    xin = xin_scr[...].astype(BF16)
    steps_per_panel = 4
    for t0 in range(0, q, steps_per_panel):
        k_hi = (t0 + steps_per_panel) * LANES
        cols = slice(t0 * LANES, k_hi)
        y = (jnp.dot(u[:, :k_hi], m_scr[:k_hi, cols], preferred_element_type=F32)
             + jnp.dot(xin, wout_scr[:, cols], preferred_element_type=F32))
        z = jax.nn.gelu(y)
        for t in range(steps_per_panel):
            o_ref[t0 + t] = z[:, t * LANES:(t + 1) * LANES].astype(o_ref.dtype)


def _complex_pow(zr, zi, n):
    rr, ri = None, None
    br, bi = zr, zi
    while n:
        if n & 1:
            rr, ri = (br, bi) if rr is None else (rr * br - ri * bi, rr * bi + ri * br)
        n >>= 1
        if n:
            br, bi = br * br - bi * bi, 2.0 * br * bi
    return rr, ri


def _s5_tables(a_re, a_im, log_dt, b_re, b_im, c_re, c_im, n_steps):
    g, p = a_re.shape
    gpb = LANES // SSM_GROUP
    nblk = g // gpb
    dt = jnp.exp(log_dt.astype(F32))[:, None]
    ar = a_re.astype(F32)
    ai = a_im.astype(F32)
    mag = jnp.exp(dt * ar)
    ang = dt * ai
    abar_re = mag * jnp.cos(ang)
    abar_im = mag * jnp.sin(ang)
    nr = abar_re - 1.0
    ni = abar_im
    den = ar * ar + ai * ai
    f_re = (nr * ar + ni * ai) / den
    f_im = (ni * ar - nr * ai) / den
    br = b_re.astype(F32)
    bi = b_im.astype(F32)
    bb_re = f_re[..., None] * br - f_im[..., None] * bi
    bb_im = f_re[..., None] * bi + f_im[..., None] * br

    pw_r, pw_i = [jnp.ones_like(abar_re)], [jnp.zeros_like(abar_re)]
    for _ in range(SSM_CHUNK):
        pw_r.append(pw_r[-1] * abar_re - pw_i[-1] * abar_im)
        pw_i.append(pw_r[-2] * abar_im + pw_i[-1] * abar_re)
    seg_r, seg_i = _complex_pow(pw_r[-1], pw_i[-1], n_steps)
    pw_r.append(seg_r)
    pw_i.append(seg_i)
    n_rows = 24
    pr = jnp.stack(pw_r, 0).reshape(len(pw_r), nblk, gpb * p).transpose(1, 0, 2)
    pi = jnp.stack(pw_i, 0).reshape(len(pw_i), nblk, gpb * p).transpose(1, 0, 2)
    pad = ((0, 0), (0, n_rows - pr.shape[1]), (0, 0))
    pr = jnp.pad(pr, pad)
    pi = jnp.pad(pi, pad)
    prc = pr.transpose(0, 2, 1)
    pic = pi.transpose(0, 2, 1)

    eye = jnp.eye(gpb, dtype=F32)

    def in_to_state(bb):
        v = bb.reshape(nblk, gpb, p, SSM_GROUP).transpose(0, 1, 3, 2)
        e = v[:, :, :, None, :] * eye[None, :, None, :, None]
        return e.reshape(nblk, LANES, gpb * p)

    def state_to_out(c):
        v = c.astype(F32).reshape(nblk, gpb, SSM_GROUP, p).transpose(0, 1, 3, 2)
        e = v[:, :, :, None, :] * eye[None, :, None, :, None]
        return e.reshape(nblk, gpb * p, LANES)

    w0 = jnp.concatenate([in_to_state(bb_re), in_to_state(bb_im)], axis=2)
    cm = jnp.concatenate([state_to_out(c_re), -state_to_out(c_im)], axis=1)
    return w0, cm, pr, pi, prc, pic


def _s5_mixer(hb, ssq, gain, bsz, lp, a_re, a_im, log_dt, b_re, b_im, c_re, c_im, d_skip):
    t, d = hb.shape
    q, nseg = SSM_CHUNK, SSM_SEGMENTS
    assert lp % (q * nseg * 2) == 0
    n_steps = lp // (q * nseg)
    rows = n_steps * nseg

    def chunk_layout(v):
        w = v.shape[-1]
        return v.reshape(bsz, nseg, n_steps, q, w).transpose(3, 0, 2, 1, 4).reshape(q, bsz * rows, w)

    x = chunk_layout(hb)
    ssq_c = chunk_layout(ssq)

    w0, cm, pr, pi, prc, pic = _s5_tables(a_re, a_im, log_dt, b_re, b_im, c_re, c_im, n_steps)
    nblk = d // LANES
    n_state = w0.shape[2]
    kern = functools.partial(_s5_kernel, n_steps=n_steps, inv_d=1.0 / d)
    z = pl.pallas_call(
        kern,
        grid=(nblk, bsz),
        in_specs=[pl.BlockSpec((q, rows, LANES), lambda k, b: (0, b, k)),
                  pl.BlockSpec((q, rows, 1), lambda k, b: (0, b, 0)),
                  pl.BlockSpec((1, LANES), lambda k, b: (0, k)),
                  pl.BlockSpec((1, LANES, n_state), lambda k, b: (k, 0, 0)),
                  pl.BlockSpec((1, n_state, LANES), lambda k, b: (k, 0, 0)),
                  pl.BlockSpec((1,) + pr.shape[1:], lambda k, b: (k, 0, 0)),
                  pl.BlockSpec((1,) + pi.shape[1:], lambda k, b: (k, 0, 0)),
                  pl.BlockSpec((1,) + prc.shape[1:], lambda k, b: (k, 0, 0)),
                  pl.BlockSpec((1,) + pic.shape[1:], lambda k, b: (k, 0, 0)),
                  pl.BlockSpec((1, LANES), lambda k, b: (0, k))],
        out_specs=pl.BlockSpec((q, rows, LANES), lambda k, b: (0, b, k)),
        out_shape=jax.ShapeDtypeStruct((q, bsz * rows, d), BF16),
        scratch_shapes=[pltpu.VMEM((q * LANES, q * LANES), BF16),
                        pltpu.VMEM((q * LANES, n_state), BF16),
                        pltpu.VMEM((n_state, q * LANES), BF16),
                        pltpu.VMEM((rows, n_state), F32),
                        pltpu.VMEM((rows, n_state), F32)],
        compiler_params=_params("arbitrary", "arbitrary"),
        name="s5_mixer",
    )(x, ssq_c, gain.reshape(1, d).astype(F32), w0, cm, pr, pi, prc, pic, d_skip.reshape(1, d).astype(F32))
    return z.reshape(q, bsz, n_steps, nseg, d).transpose(1, 3, 2, 0, 4).reshape(t, d)


def _rope_tables(length):
    inv = ROPE_THETA ** (-jnp.arange(0, HEAD_DIM, 2, dtype=F32) / HEAD_DIM)
    ang = jnp.arange(length, dtype=F32)[:, None] * inv[None, :]
    cos, sin = jnp.cos(ang), jnp.sin(ang)
    return jnp.concatenate([cos, cos], axis=1), jnp.concatenate([-sin, sin], axis=1)


def _lambda_init(layer_idx):
    return 0.8 - 0.6 * math.exp(-0.3 * layer_idx)


def kernel(x, meta_tokens, norm_mix_g, norm_mlp_g, da_w_qkv, da_q_norm_g, da_k_norm_g, da_lambda, da_subln_g, da_w_o, ssm_a_re, ssm_a_im, ssm_log_dt, ssm_b_re, ssm_b_im, ssm_c_re, ssm_c_im, ssm_d, ssm_w_glu, ssm_b_glu, mlp_w_up, mlp_w_down):
    bsz, seq, d = x.shape
    depth = norm_mix_g.shape[0]
    n_mixers = 2
    length = N_META + seq
    lp = ((length + SEQ_ALIGN - 1) // SEQ_ALIGN) * SEQ_ALIGN
    assert meta_tokens.shape[0] == N_META and depth >= 1
    h, hn = _embed_norm(x, meta_tokens, norm_mix_g[0], lp)
    cos, sin = (jnp.tile(tab, (bsz, 1)) for tab in _rope_tables(lp))
    def up_job(i):
        return mlp_w_up, i, norm_mlp_g[i]

    def bf16_weight(name, stack, layer, gain=None):
        if name not in wb:
            src = stack[layer]
            wb[name] = (src if gain is None else src * gain.astype(F32)[:, None]).astype(BF16)
        return wb[name]

    wb = {}
    hb = ssq = None
    only_output_rows = False
    for i in range(depth):
        j = i // n_mixers
        nxt_s5 = i + 1 < depth and (i + 1) % n_mixers == 1
        if i % n_mixers == 0:
            if i > 0:
                hn = _rmsnorm(h, norm_mix_g[i])
            qkv, wb["o", j], wb["up", i] = _qkv_proj(
                hn, bf16_weight(("qkv", j), da_w_qkv, j), cos, sin, da_q_norm_g[j], da_k_norm_g[j],
                [(da_w_o, j, None), up_job(i)])
            att = _diff_attention(qkv, da_lambda[j], da_subln_g[j], bsz, lp, d, _lambda_init(i))
            h, hb, ssq = _proj_residual(att, wb["o", j], h)
        else:
            z = _s5_mixer(hb, ssq, norm_mix_g[i], bsz, lp, ssm_a_re[j], ssm_a_im[j], ssm_log_dt[j],
                          ssm_b_re[j], ssm_b_im[j], ssm_c_re[j], ssm_c_im[j], ssm_d[j])
            w_glu = bf16_weight(("glu", j), ssm_w_glu, j)
            if i == depth - 1:
                h, hb, ssq = _glu_residual_rows(z, w_glu, ssm_b_glu[j], h, bsz, lp, N_META, seq)
                only_output_rows = True
            else:
                h, hb, ssq = _glu_residual(z, w_glu, ssm_b_glu[j], h)
        jobs =[(("down", i), mlp_w_down, i, None)]
        if nxt_s5:
            jobs += [(("glu", (i + 1) // n_mixers), ssm_w_glu, (i + 1) // n_mixers, None),
                     (("up", i + 1),) + up_job(i + 1)]
        f, *casts = _mlp_up(hb, bf16_weight(("up", i), *up_job(i)), ssq, [job[1:] for job in jobs])
        wb.update({job[0]: c for job, c in zip(jobs, casts)})
        if i == depth - 1:
            if only_output_rows:
                return _mlp_down(f, wb["down", i], h, False, tk_target=4096).reshape(bsz, seq, d)
            return _mlp_down_final(f, wb["down", i], h, bsz, lp, N_META, seq).reshape(bsz, seq, d)
        if nxt_s5:
            h, hb, ssq = _mlp_down(f, wb["down", i], h, True)
        else:
            h = _mlp_down(f, wb["down", i], h, False)
```

```python
import functools
import math

import jax
import jax.numpy as jnp
from jax import lax
from jax.experimental import pallas as pl
from jax.experimental.pallas import tpu as pltpu

N_META = 16
SEQ_ALIGN = 256
HEAD_DIM = 128
LOG2E = 1.4426950408889634
ATTN_ROW_BLOCK = 64
ROPE_THETA = 10000.0
SSM_GROUP = 16
SSM_CHUNK = 16
SSM_SEGMENTS = 8
MXU_COLS = 256
LANES = 128
SUBLANES = 8
EPS = 1e-6
VMEM_LIMIT_BYTES = 56 * 1024 * 1024

F32 = jnp.float32
BF16 = jnp.bfloat16


def _pick_tile(n, target, mult):
    best = None
    for t in range(mult, min(n, target) + 1, mult):
        if n % t == 0:
            best = t
    assert best is not None, (n, target, mult)
    return best


def _row_tile(t):
    return _pick_tile(t, 1536, SEQ_ALIGN if t % SEQ_ALIGN == 0 else 16)


def _params(*sem):
    return pltpu.CompilerParams(dimension_semantics=sem, vmem_limit_bytes=VMEM_LIMIT_BYTES)


class _SideCasts:
    def __init__(self, jobs, n_steps, step_of):
        self.operands, self.in_specs, self.out_specs, self.out_shapes, self.has_gain = [], [], [], [], []
        for src, layer, gain in jobs:
            _, r, c = src.shape
            cr = next(x for x in (16 << p for p in range(24)) if r % x == 0 and r // x <= n_steps)

            def idx(*g, last=r // cr - 1):
                return (jnp.minimum(step_of(*g), last), 0)

            def src_idx(*g, layer=layer, last=r // cr - 1):
                return (layer, jnp.minimum(step_of(*g), last), 0)

            self.operands.append(src)
            self.in_specs.append(pl.BlockSpec((None, cr, c), src_idx))
            if gain is not None:
                self.operands.append(gain.reshape(r, 1).astype(F32))
                self.in_specs.append(pl.BlockSpec((cr, 1), idx))
            self.out_specs.append(pl.BlockSpec((cr, c), idx))
            self.out_shapes.append(jax.ShapeDtypeStruct((r, c), BF16))
            self.has_gain.append(gain is not None)
        self.n_in = len(self.operands)
        self.n_out = len(self.out_shapes)

    def run(self, in_refs, out_refs):
        refs = iter(in_refs)
        for has_gain, o_ref in zip(self.has_gain, out_refs):
            v = next(refs)[...]
            if has_gain:
                v = v * next(refs)[...]
            o_ref[...] = v.astype(o_ref.dtype)


def _rmsnorm_kernel(x_ref, g_ref, o_ref):
    x = x_ref[...]
    ms = jnp.mean(x * x, axis=-1, keepdims=True)
    o_ref[...] = (x * lax.rsqrt(ms + EPS) * g_ref[...]).astype(o_ref.dtype)


def _rmsnorm(x, g):
    t, d = x.shape
    tr = _pick_tile(t, 256, 16)
    return pl.pallas_call(
        _rmsnorm_kernel,
        grid=(t // tr,),
        in_specs=[pl.BlockSpec((tr, d), lambda i: (i, 0)),
                  pl.BlockSpec((1, d), lambda i: (0, 0))],
        out_specs=pl.BlockSpec((tr, d), lambda i: (i, 0)),
        out_shape=jax.ShapeDtypeStruct((t, d), BF16),
        compiler_params=_params("parallel"),
        name="rmsnorm",
    )(x, g.reshape(1, d).astype(F32))


def _embed_norm_kernel(x_ref, meta_ref, g_ref, h_ref, hn_ref, *, n_meta, seq):
    tr = h_ref.shape[0]
    i = pl.program_id(1)

    @pl.when(i == 0)
    def _():
        h_ref[:n_meta, :] = meta_ref[...]
        h_ref[n_meta:, :] = x_ref[:tr - n_meta, :]

    @pl.when(i > 0)
    def _():
        row = i * tr + lax.broadcasted_iota(jnp.int32, h_ref.shape, 0)
        h_ref[...] = jnp.where(row < n_meta + seq, x_ref[...], 0.0)

    h = h_ref[...]
    ms = jnp.mean(h * h, axis=-1, keepdims=True)
    hn_ref[...] = (h * lax.rsqrt(ms + EPS) * g_ref[...]).astype(hn_ref.dtype)


def _embed_norm(x, meta_tokens, g, lp):
    bsz, seq, d = x.shape
    n_meta = meta_tokens.shape[0]
    tr = _pick_tile(lp, 256, 16)
    assert n_meta % 16 == 0 and n_meta < tr and tr < seq
    nt = lp // tr
    tile = pl.BlockSpec((tr, d), lambda b, i: (b * nt + i, 0))

    def x_window(b, i):
        return (b, pl.multiple_of(jnp.maximum(i * tr - n_meta, 0), 16), 0)

    return pl.pallas_call(
        functools.partial(_embed_norm_kernel, n_meta=n_meta, seq=seq),
        grid=(bsz, nt),
        in_specs=[pl.BlockSpec((pl.Squeezed(), pl.Element(tr, (0, lp - n_meta - seq)), pl.Element(d)), x_window),
                  pl.BlockSpec((n_meta, d), lambda b, i: (0, 0)),
                  pl.BlockSpec((1, d), lambda b, i: (0, 0))],
        out_specs=(tile, tile),
        out_shape=(jax.ShapeDtypeStruct((bsz * lp, d), F32), jax.ShapeDtypeStruct((bsz * lp, d), BF16)),
        compiler_params=_params("parallel", "arbitrary"),
        name="embed_norm",
    )(x, meta_tokens.astype(F32), g.reshape(1, d).astype(F32))


def _qkv_kernel(*refs, n_q_blocks, q_scale, side):
    a_ref, w_ref, cos_ref, sin_ref, qg_ref, kg_ref = refs[:6]
    side_in = refs[6:6 + side.n_in]
    o_ref = refs[6 + side.n_in]
    side_out = refs[7 + side.n_in:]
    acc = jnp.dot(a_ref[...], w_ref[...], preferred_element_type=F32)
    j = pl.program_id(1)
    tn = acc.shape[1]

    def norm_rope(g, scale):
        cos = cos_ref[...]
        sin = sin_ref[...]
        for u in range(tn // HEAD_DIM):
            x = acc[:, u * HEAD_DIM:(u + 1) * HEAD_DIM]
            ms = jnp.mean(x * x, axis=-1, keepdims=True)
            y = x * lax.rsqrt(ms + EPS) * g
            y = y * cos + pltpu.roll(y, HEAD_DIM // 2, 1) * sin
            if scale != 1.0:
                y = y * scale
            o_ref[:, u * HEAD_DIM:(u + 1) * HEAD_DIM] = y.astype(o_ref.dtype)

    @pl.when(j < n_q_blocks)
    def _():
        norm_rope(qg_ref[...], q_scale)
        side.run(side_in, side_out)

    @pl.when(jnp.logical_and(j >= n_q_blocks, j < 2 * n_q_blocks))
    def _():
        norm_rope(kg_ref[...], 1.0)
        side.run(side_in, side_out)

    @pl.when(j >= 2 * n_q_blocks)
    def _():
        o_ref[...] = acc.astype(o_ref.dtype)
        side.run(side_in, side_out)


def _qkv_proj(a, w, cos, sin, q_g, k_g, cast_jobs):
    t, d = a.shape
    n = w.shape[1]
    tm = _row_tile(t)
    tn = _pick_tile(d, 512, HEAD_DIM)
    nj = n // tn
    side = _SideCasts(cast_jobs, (t // tm) * nj, lambda i, j: i * nj + j)
    kern = functools.partial(_qkv_kernel, n_q_blocks=d // tn, q_scale=HEAD_DIM ** -0.5 * LOG2E, side=side)
    return pl.pallas_call(
        kern,
        grid=(t // tm, nj),
        in_specs=[pl.BlockSpec((tm, d), lambda i, j: (i, 0)),
                  pl.BlockSpec((d, tn), lambda i, j: (0, j)),
                  pl.BlockSpec((tm, HEAD_DIM), lambda i, j: (i, 0)),
                  pl.BlockSpec((tm, HEAD_DIM), lambda i, j: (i, 0)),
                  pl.BlockSpec((1, HEAD_DIM), lambda i, j: (0, 0)),
                  pl.BlockSpec((1, HEAD_DIM), lambda i, j: (0, 0))] + side.in_specs,
        out_specs=[pl.BlockSpec((tm, tn), lambda i, j: (i, j))] + side.out_specs,
        out_shape=[jax.ShapeDtypeStruct((t, n), BF16)] + side.out_shapes,
        compiler_params=_params("arbitrary", "arbitrary"),
        name="qkv_proj",
    )(a, w, cos, sin, q_g.reshape(1, HEAD_DIM).astype(F32), k_g.reshape(1, HEAD_DIM).astype(F32),
      *side.operands)


def _attn_kernel(q_ref, k_ref, v_ref, lam_ref, g_ref, o_ref,
                 m1_ref, l1_ref, a1_ref, m2_ref, l2_ref, a2_ref, s_even_ref, s_odd_ref, p_ref, alpha_ref,
                 *, blk, lam_init):
    s_bufs = (s_even_ref, s_odd_ref)
    nq = q_ref.shape[0] // blk
    stats = ((m1_ref, l1_ref, a1_ref), (m2_ref, l2_ref, a2_ref))
    lv = lam_ref[...]
    lam = (jnp.exp(jnp.sum(lv[0:1] * lv[1:2], axis=-1, keepdims=True))
           - jnp.exp(jnp.sum(lv[2:3] * lv[3:4], axis=-1, keepdims=True)) + lam_init)

    def rows(i):
        return pl.ds(pl.multiple_of(i * blk, blk), blk)

    def init_stats():
        for m_ref, l_ref, a_ref in stats:
            m_ref[...] = jnp.full(m_ref.shape, -jnp.inf, F32)
            l_ref[...] = jnp.zeros(l_ref.shape, F32)
            a_ref[...] = jnp.zeros(a_ref.shape, F32)

    def scores(qi, j, s_ref):
        q = q_ref[rows(qi), :]
        kk = k_ref[rows(j), :]
        for c in range(2):
            s_ref[c, :, :blk] = lax.dot_general(
                q[:, c * HEAD_DIM:(c + 1) * HEAD_DIM], kk[:, c * HEAD_DIM:(c + 1) * HEAD_DIM],
                (((1,), (1,)), ((), ())), preferred_element_type=F32)

    def softmax_pv(j, s_ref, masked):
        vv = v_ref[rows(j), :]
        for c, (m_ref, l_ref, a_ref) in enumerate(stats):
            for r in range(0, blk, ATTN_ROW_BLOCK):
                rb = slice(r, r + ATTN_ROW_BLOCK)
                s = s_ref[c, rb, :blk]
                if masked:
                    row = r + lax.broadcasted_iota(jnp.int32, s.shape, 0)
                    col = lax.broadcasted_iota(jnp.int32, s.shape, 1)
                    s = jnp.where(col <= row, s, -jnp.inf)
                m_old = m_ref[rb, :]
                m_new = jnp.maximum(m_old, jnp.max(s, axis=-1, keepdims=True))
                alpha = jnp.exp2(m_old - m_new)
                p = jnp.exp2(s - m_new)
                part = p[:, :LANES]
                for u in range(1, blk // LANES):
                    part = part + p[:, u * LANES:(u + 1) * LANES]
                l_ref[rb, :] = alpha * l_ref[rb, :] + part
                p_ref[c, rb, :] = p.astype(BF16)
                alpha_ref[c, rb, :] = alpha
                m_ref[rb, :] = m_new
            a_ref[...] = alpha_ref[c] * a_ref[...] + jnp.dot(p_ref[c], vv, preferred_element_type=F32)

    def finalize(qi):
        l1 = jnp.sum(l1_ref[...], axis=-1, keepdims=True)
        l2 = jnp.sum(l2_ref[...], axis=-1, keepdims=True)
        o = a1_ref[...] / l1 - lam * (a2_ref[...] / l2)
        ms = jnp.mean(o * o, axis=-1, keepdims=True)
        y = o * lax.rsqrt(ms + EPS) * g_ref[...] * (1.0 - lam_init)
        o_ref[rows(qi), :] = y.astype(o_ref.dtype)

    def items(n, *work):
        for parity in range(2):
            @pl.when(n % 2 == parity)
            def _():
                for step, (next_qi, next_j, j, masked) in enumerate(work):
                    mine = (parity + step) % 2
                    scores(next_qi, next_j, s_bufs[1 - mine])
                    softmax_pv(j, s_bufs[mine], masked)

    init_stats()
    scores(0, 0, s_bufs[0])

    def q_block(qi, carry):
        base = (qi * (qi + 1)) // 2
        diagonal = (jnp.minimum(qi + 1, nq - 1), 0, qi, True)

        def full_chunk(j, c):
            items(base + j, (qi, j + 1, j, False))
            return c

        lax.fori_loop(0, qi, full_chunk, 0)
        items(base + qi, diagonal)
        finalize(qi)
        init_stats()
        return carry

    lax.fori_loop(0, nq, q_block, 0)


def _diff_attention(qkv, lam_vecs, subln_g, bsz, lp, d, lam_init):
    t = qkv.shape[0]
    n_heads = d // (2 * HEAD_DIM)
    hw = 2 * HEAD_DIM
    blk = _pick_tile(lp, 768, SEQ_ALIGN)
    kern = functools.partial(_attn_kernel, blk=blk, lam_init=lam_init)
    return pl.pallas_call(
        kern,
        grid=(bsz, n_heads),
        in_specs=[pl.BlockSpec((lp, hw), lambda b, h: (b, h)),
                  pl.BlockSpec((lp, hw), lambda b, h: (b, n_heads + h)),
                  pl.BlockSpec((lp, hw), lambda b, h: (b, 2 * n_heads + h)),
                  pl.BlockSpec((4, HEAD_DIM), lambda b, h: (0, 0)),
                  pl.BlockSpec((1, hw), lambda b, h: (0, 0))],
        out_specs=pl.BlockSpec((lp, hw), lambda b, h: (b, h)),
        out_shape=jax.ShapeDtypeStruct((t, d), BF16),
        scratch_shapes=[pltpu.VMEM((blk, 1), F32), pltpu.VMEM((blk, LANES), F32), pltpu.VMEM((blk, hw), F32),
                        pltpu.VMEM((blk, 1), F32), pltpu.VMEM((blk, LANES), F32), pltpu.VMEM((blk, hw), F32),
                        pltpu.VMEM((2, blk, blk + LANES), F32), pltpu.VMEM((2, blk, blk + LANES), F32),
                        pltpu.VMEM((2, blk, blk), BF16), pltpu.VMEM((2, blk, 1), F32)],
        compiler_params=_params("parallel", "parallel"),
        name="diff_attention",
    )(qkv, qkv, qkv, lam_vecs.astype(F32), subln_g.reshape(1, hw).astype(F32))


def _col_panels(n):
    return [slice(u, u + MXU_COLS) for u in range(0, n, MXU_COLS)]


def _emit_norm_inputs(h, cols, hb_ref, ssq_ref):
    hb_ref[:, cols] = h.astype(hb_ref.dtype)
    ssq_ref[...] += jnp.sum(h * h, axis=-1, keepdims=True)


def _proj_residual_kernel(a_ref, w_ref, r_ref, o_ref, hb_ref, ssq_ref):
    @pl.when(pl.program_id(1) == 0)
    def _():
        ssq_ref[...] = jnp.zeros(ssq_ref.shape, F32)

    for cols in _col_panels(o_ref.shape[1]):
        h = r_ref[:, cols] + jnp.dot(a_ref[...], w_ref[:, cols], preferred_element_type=F32)
        o_ref[:, cols] = h
        _emit_norm_inputs(h, cols, hb_ref, ssq_ref)


def _norm_out_shapes(t, n):
    return (jax.ShapeDtypeStruct((t, n), F32), jax.ShapeDtypeStruct((t, n), BF16),
            jax.ShapeDtypeStruct((t, 1), F32))


def _proj_residual(a, w, res):
    t, k = a.shape
    n = w.shape[1]
    tm = _row_tile(t)
    tn = _pick_tile(n, 512, LANES)
    return pl.pallas_call(
        _proj_residual_kernel,
        grid=(t // tm, n // tn),
        in_specs=[pl.BlockSpec((tm, k), lambda i, j: (i, 0)),
                  pl.BlockSpec((k, tn), lambda i, j: (0, j)),
                  pl.BlockSpec((tm, tn), lambda i, j: (i, j))],
        out_specs=(pl.BlockSpec((tm, tn), lambda i, j: (i, j)),
                   pl.BlockSpec((tm, tn), lambda i, j: (i, j)),
                   pl.BlockSpec((tm, 1), lambda i, j: (i, 0))),
        out_shape=_norm_out_shapes(t, n),
        compiler_params=_params("parallel", "arbitrary"),
        name="proj_residual",
    )(a, w, res)


def _mlp_up_kernel(*refs, inv_d, side):
    a_ref, w_ref, ssq_ref = refs[:3]
    side_in = refs[3:3 + side.n_in]
    o_ref = refs[3 + side.n_in]
    side_out = refs[4 + side.n_in:]
    r2 = 1.0 / (ssq_ref[...] * inv_d + EPS)
    side.run(side_in, side_out)
    for cols in _col_panels(o_ref.shape[1]):
        u = jnp.dot(a_ref[...], w_ref[:, cols], preferred_element_type=F32)
        o_ref[:, cols] = (jnp.square(jnp.maximum(u, 0.0)) * r2).astype(o_ref.dtype)


def _mlp_up(hb, w, ssq, cast_jobs):
    t, k = hb.shape
    n = w.shape[1]
    tm = _row_tile(t)
    tn = _pick_tile(n, 512, LANES)
    nj = n // tn
    side = _SideCasts(cast_jobs, (t // tm) * nj, lambda i, j: i * nj + j)
    return pl.pallas_call(
        functools.partial(_mlp_up_kernel, inv_d=1.0 / k, side=side),
        grid=(t // tm, nj),
        in_specs=[pl.BlockSpec((tm, k), lambda i, j: (i, 0)),
                  pl.BlockSpec((k, tn), lambda i, j: (0, j)),
                  pl.BlockSpec((tm, 1), lambda i, j: (i, 0))] + side.in_specs,
        out_specs=[pl.BlockSpec((tm, tn), lambda i, j: (i, j))] + side.out_specs,
        out_shape=[jax.ShapeDtypeStruct((t, n), BF16)] + side.out_shapes,
        compiler_params=_params("arbitrary", "arbitrary"),
        name="mlp_up",
    )(hb, w, ssq, *side.operands)


def _mlp_down_kernel(a_ref, w_ref, r_ref, o_ref, *norm_refs, nk, col_axis=1):
    j = pl.program_id(col_axis)
    k = pl.program_id(col_axis + 1)

    def accumulate(base_ref, last):
        for cols in _col_panels(o_ref.shape[1]):
            h = base_ref[:, cols] + jnp.dot(a_ref[...], w_ref[:, cols], preferred_element_type=F32)
            o_ref[:, cols] = h
            if last and norm_refs:
                _emit_norm_inputs(h, cols, *norm_refs)

    if norm_refs:
        @pl.when(jnp.logical_and(j == 0, k == 0))
        def _():
            norm_refs[1][...] = jnp.zeros(norm_refs[1].shape, F32)

    if nk == 1:
        accumulate(r_ref, True)
        return

    @pl.when(k == 0)
    def _():
        accumulate(r_ref, False)

    @pl.when(jnp.logical_and(k > 0, k < nk - 1))
    def _():
        accumulate(o_ref, False)

    @pl.when(k == nk - 1)
    def _():
        accumulate(o_ref, True)


def _mlp_down(a, w, res, emit_norm):
    t, kdim = a.shape
    n = w.shape[1]
    tm = _pick_tile(t, 1024, SEQ_ALIGN if t % SEQ_ALIGN == 0 else 16)
    tn = _pick_tile(n, 1024, LANES)
    tk = _pick_tile(kdim, 4096, LANES)
    nk = kdim // tk
    tile = pl.BlockSpec((tm, tn), lambda i, j, k: (i, j))
    if emit_norm:
        out_specs = (tile, tile, pl.BlockSpec((tm, 1), lambda i, j, k: (i, 0)))
        out_shape = _norm_out_shapes(t, n)
    else:
        out_specs = tile
        out_shape = jax.ShapeDtypeStruct((t, n), F32)
    return pl.pallas_call(
        functools.partial(_mlp_down_kernel, nk=nk),
        grid=(t // tm, n // tn, nk),
        in_specs=[pl.BlockSpec((tm, tk), lambda i, j, k: (i, k)),
                  pl.BlockSpec((tk, tn), lambda i, j, k: (k, j)),
                  tile],
        out_specs=out_specs,
        out_shape=out_shape,
        compiler_params=_params("parallel", "arbitrary", "arbitrary"),
        name="mlp_down",
    )(a, w, res)


def _mlp_down_final(a, w, res, bsz, lp, row0, seq):
    kdim = a.shape[1]
    n = w.shape[1]
    tm = _pick_tile(seq, 1024, 16)
    nt = seq // tm
    tn = _pick_tile(n, 1024, LANES)
    tk = _pick_tile(kdim, 4096, LANES)
    nk = kdim // tk

    assert lp % 16 == 0 and row0 % 16 == 0 and tm % 16 == 0

    def row_start(b, m):
        return pl.multiple_of(b * lp + row0 + m * tm, 16)

    return pl.pallas_call(
        functools.partial(_mlp_down_kernel, nk=nk, col_axis=2),
        grid=(bsz, nt, n // tn, nk),
        in_specs=[pl.BlockSpec((pl.Element(tm), pl.Element(tk)), lambda b, m, j, k: (row_start(b, m), k * tk)),
                  pl.BlockSpec((tk, tn), lambda b, m, j, k: (k, j)),
                  pl.BlockSpec((pl.Element(tm), pl.Element(tn)), lambda b, m, j, k: (row_start(b, m), j * tn))],
        out_specs=pl.BlockSpec((tm, tn), lambda b, m, j, k: (b * nt + m, j)),
        out_shape=jax.ShapeDtypeStruct((bsz * seq, n), F32),
        compiler_params=_params("parallel", "parallel", "arbitrary", "arbitrary"),
        name="mlp_down_final",
    )(a, w, res)


def _glu_kernel(a_ref, wv_ref, wg_ref, bv_ref, bg_ref, r_ref, o_ref, hb_ref, ssq_ref, *, col_axis=1):
    @pl.when(pl.program_id(col_axis) == 0)
    def _():
        ssq_ref[...] = jnp.zeros(ssq_ref.shape, F32)

    for cols in _col_panels(o_ref.shape[1]):
        val = jnp.dot(a_ref[...], wv_ref[:, cols], preferred_element_type=F32) + bv_ref[:, cols]
        gate = jnp.dot(a_ref[...], wg_ref[:, cols], preferred_element_type=F32) + bg_ref[:, cols]
        h = r_ref[:, cols] + val * jax.nn.sigmoid(gate)
        o_ref[:, cols] = h
        _emit_norm_inputs(h, cols, hb_ref, ssq_ref)


def _glu_residual(a, w, bias, res):
    t, k = a.shape
    n = w.shape[1] // 2
    tm = _row_tile(t)
    tn = _pick_tile(n, 256, LANES)
    nb = n // tn
    b2 = bias.reshape(1, 2 * n).astype(F32)
    return pl.pallas_call(
        _glu_kernel,
        grid=(t // tm, nb),
        in_specs=[pl.BlockSpec((tm, k), lambda i, j: (i, 0)),
                  pl.BlockSpec((k, tn), lambda i, j: (0, j)),
                  pl.BlockSpec((k, tn), lambda i, j: (0, nb + j)),
                  pl.BlockSpec((1, tn), lambda i, j: (0, j)),
                  pl.BlockSpec((1, tn), lambda i, j: (0, nb + j)),
                  pl.BlockSpec((tm, tn), lambda i, j: (i, j))],
        out_specs=(pl.BlockSpec((tm, tn), lambda i, j: (i, j)),
                   pl.BlockSpec((tm, tn), lambda i, j: (i, j)),
                   pl.BlockSpec((tm, 1), lambda i, j: (i, 0))),
        out_shape=_norm_out_shapes(t, n),
        compiler_params=_params("parallel", "arbitrary"),
        name="glu_residual",
    )(a, w, w, b2, b2, res)


def _glu_residual_rows(a, w, bias, res, bsz, lp, row0, seq):
    k = a.shape[1]
    n = w.shape[1] // 2
    tm = _pick_tile(seq, 1024, 16)
    nt = seq // tm
    tn = _pick_tile(n, 256, LANES)
    nb = n // tn
    b2 = bias.reshape(1, 2 * n).astype(F32)
    assert lp % 16 == 0 and row0 % 16 == 0 and tm % 16 == 0

    def row_start(b, m):
        return pl.multiple_of(b * lp + row0 + m * tm, 16)

    tile = pl.BlockSpec((tm, tn), lambda b, m, j: (b * nt + m, j))
    return pl.pallas_call(
        functools.partial(_glu_kernel, col_axis=2),
        grid=(bsz, nt, nb),
        in_specs=[pl.BlockSpec((pl.Element(tm), pl.Element(k)), lambda b, m, j: (row_start(b, m), 0)),
                  pl.BlockSpec((k, tn), lambda b, m, j: (0, j)),
                  pl.BlockSpec((k, tn), lambda b, m, j: (0, nb + j)),
                  pl.BlockSpec((1, tn), lambda b, m, j: (0, j)),
                  pl.BlockSpec((1, tn), lambda b, m, j: (0, nb + j)),
                  pl.BlockSpec((pl.Element(tm), pl.Element(tn)), lambda b, m, j: (row_start(b, m), j * tn))],
        out_specs=(tile, tile, pl.BlockSpec((tm, 1), lambda b, m, j: (b * nt + m, 0))),
        out_shape=_norm_out_shapes(bsz * seq, n),
        compiler_params=_params("parallel", "parallel", "arbitrary"),
        name="glu_residual_rows",
    )(a, w, w, b2, b2, res)


def _s5_kernel(x_ref, ssq_ref, g_ref, w0_ref, cm_ref, pr_ref, pi_ref, prc_ref, pic_ref, d_ref, o_ref,
               m_scr, wst_scr, wout_scr, s_scr, xin_scr, *, n_steps, inv_d):
    q = SSM_CHUNK
    half = w0_ref.shape[2] // 2

    @pl.when(pl.program_id(1) == 0)
    def _build_weights():
        w0 = w0_ref[0]
        w0re, w0im = w0[:, :half], w0[:, half:]
        cm = cm_ref[0]
        cm_bf = cm.astype(BF16)
        row = lax.broadcasted_iota(jnp.int32, (LANES, LANES), 0)
        col = lax.broadcasted_iota(jnp.int32, (LANES, LANES), 1)
        skip = jnp.where(row == col, jnp.broadcast_to(d_ref[...], (LANES, LANES)), 0.0)
        zero_blk = jnp.zeros((LANES, LANES), BF16)
        for tau in range(q):
            ar = pr_ref[0, tau:tau + 1, :]
            ai = pi_ref[0, tau:tau + 1, :]
            w_tau = jnp.concatenate([w0re * ar - w0im * ai, w0im * ar + w0re * ai], axis=1).astype(BF16)
            t_st = q - 1 - tau
            wst_scr[t_st * LANES:(t_st + 1) * LANES, :] = w_tau
            k_tau = jnp.dot(w_tau, cm_bf, preferred_element_type=F32)
            if tau == 0:
                k_tau = k_tau + skip
            k_bf = k_tau.astype(BF16)
            for t in range(q - tau):
                m_scr[t * LANES:(t + 1) * LANES, (t + tau) * LANES:(t + tau + 1) * LANES] = k_bf
        for t in range(q):
            for t2 in range(t):
                m_scr[t * LANES:(t + 1) * LANES, t2 * LANES:(t2 + 1) * LANES] = zero_blk
        cre, cimn = cm[:half], cm[half:]
        for t in range(q):
            arc = prc_ref[0, :, t + 1:t + 2]
            aic = pic_ref[0, :, t + 1:t + 2]
            wout_scr[:half, t * LANES:(t + 1) * LANES] = (arc * cre + aic * cimn).astype(BF16)
            wout_scr[half:, t * LANES:(t + 1) * LANES] = (arc * cimn - aic * cre).astype(BF16)

    gain = g_ref[...]
    u = jnp.concatenate(
        [(x_ref[t].astype(F32) * lax.rsqrt(ssq_ref[t] * inv_d + EPS) * gain).astype(BF16) for t in range(q)],
        axis=1)
    s_scr[...] = jnp.dot(u, wst_scr[...], preferred_element_type=F32)

    shp = (SUBLANES, half)
    ar = jnp.broadcast_to(pr_ref[0, q:q + 1, :], shp)
    ai = jnp.broadcast_to(pi_ref[0, q:q + 1, :], shp)
    anr = jnp.broadcast_to(pr_ref[0, q + 1:q + 2, :], shp)
    ani = jnp.broadcast_to(pi_ref[0, q + 1:q + 2, :], shp)
    seg = lax.broadcasted_iota(jnp.int32, shp, 0)
    zeros = jnp.zeros(shp, F32)

    def shift_down(x):
        return jnp.where(seg == 0, 0.0, pltpu.roll(x, 1, 0))

    def advance(j, cr, ci):
        sj = s_scr[pl.ds(pl.multiple_of(j * SUBLANES, SUBLANES), SUBLANES), :]
        return ar * cr - ai * ci + sj[:, :half], ar * ci + ai * cr + sj[:, half:]

    er, ei = lax.fori_loop(0, n_steps, lambda j, c: advance(j, *c), (zeros, zeros))
    tr, ti = er, ei
    for _ in range(SSM_SEGMENTS - 1):
        sr, si = shift_down(tr), shift_down(ti)
        tr, ti = er + anr * sr - ani * si, ei + anr * si + ani * sr
    cr0, ci0 = shift_down(tr), shift_down(ti)

    def scan_store(j, c):
        cr, ci = c
        xin_scr[pl.ds(pl.multiple_of(j * SUBLANES, SUBLANES), SUBLANES), :] = jnp.concatenate([cr, ci], axis=1)
        return advance(j, cr, ci)

    lax.fori_loop(0, n_steps, scan_store, (cr0, ci0))

    xin = xin_scr[...].astype(BF16)
    steps_per_panel = 4
    for t0 in range(0, q, steps_per_panel):
        k_hi = (t0 + steps_per_panel) * LANES
        cols = slice(t0 * LANES, k_hi)
        y = (jnp.dot(u[:, :k_hi], m_scr[:k_hi, cols], preferred_element_type=F32)
             + jnp.dot(xin, wout_scr[:, cols], preferred_element_type=F32))
        z = jax.nn.gelu(y)
        for t in range(steps_per_panel):
            o_ref[t0 + t] = z[:, t * LANES:(t + 1) * LANES].astype(o_ref.dtype)


def _complex_pow(zr, zi, n):
    rr, ri = None, None
    br, bi = zr, zi
    while n:
        if n & 1:
            rr, ri = (br, bi) if rr is None else (rr * br - ri * bi, rr * bi + ri * br)
        n >>= 1
        if n:
            br, bi = br * br - bi * bi, 2.0 * br * bi
    return rr, ri


def _s5_tables(a_re, a_im, log_dt, b_re, b_im, c_re, c_im, n_steps):
    g, p = a_re.shape
    gpb = LANES // SSM_GROUP
    nblk = g // gpb
    dt = jnp.exp(log_dt.astype(F32))[:, None]
    ar = a_re.astype(F32)
    ai = a_im.astype(F32)
    mag = jnp.exp(dt * ar)
    ang = dt * ai
    abar_re = mag * jnp.cos(ang)
    abar_im = mag * jnp.sin(ang)
    nr = abar_re - 1.0
    ni = abar_im
    den = ar * ar + ai * ai
    f_re = (nr * ar + ni * ai) / den
    f_im = (ni * ar - nr * ai) / den
    br = b_re.astype(F32)
    bi = b_im.astype(F32)
    bb_re = f_re[..., None] * br - f_im[..., None] * bi
    bb_im = f_re[..., None] * bi + f_im[..., None] * br

    pw_r, pw_i = [jnp.ones_like(abar_re)], [jnp.zeros_like(abar_re)]
    for _ in range(SSM_CHUNK):
        pw_r.append(pw_r[-1] * abar_re - pw_i[-1] * abar_im)
        pw_i.append(pw_r[-2] * abar_im + pw_i[-1] * abar_re)
    seg_r, seg_i = _complex_pow(pw_r[-1], pw_i[-1], n_steps)
    pw_r.append(seg_r)
    pw_i.append(seg_i)
    n_rows = 24
    pr = jnp.stack(pw_r, 0).reshape(len(pw_r), nblk, gpb * p).transpose(1, 0, 2)
    pi = jnp.stack(pw_i, 0).reshape(len(pw_i), nblk, gpb * p).transpose(1, 0, 2)
    pad = ((0, 0), (0, n_rows - pr.shape[1]), (0, 0))
    pr = jnp.pad(pr, pad)
    pi = jnp.pad(pi, pad)
    prc = pr.transpose(0, 2, 1)
    pic = pi.transpose(0, 2, 1)

    eye = jnp.eye(gpb, dtype=F32)

    def in_to_state(bb):
        v = bb.reshape(nblk, gpb, p, SSM_GROUP).transpose(0, 1, 3, 2)
        e = v[:, :, :, None, :] * eye[None, :, None, :, None]
        return e.reshape(nblk, LANES, gpb * p)

    def state_to_out(c):
        v = c.astype(F32).reshape(nblk, gpb, SSM_GROUP, p).transpose(0, 1, 3, 2)
        e = v[:, :, :, None, :] * eye[None, :, None, :, None]
        return e.reshape(nblk, gpb * p, LANES)

    w0 = jnp.concatenate([in_to_state(bb_re), in_to_state(bb_im)], axis=2)
    cm = jnp.concatenate([state_to_out(c_re), -state_to_out(c_im)], axis=1)
    return w0, cm, pr, pi, prc, pic


def _s5_mixer(hb, ssq, gain, bsz, lp, a_re, a_im, log_dt, b_re, b_im, c_re, c_im, d_skip):
    t, d = hb.shape
    q, nseg = SSM_CHUNK, SSM_SEGMENTS
    assert lp % (q * nseg * 2) == 0
    n_steps = lp // (q * nseg)
    rows = n_steps * nseg

    def chunk_layout(v):
        w = v.shape[-1]
        return v.reshape(bsz, nseg, n_steps, q, w).transpose(3, 0, 2, 1, 4).reshape(q, bsz * rows, w)

    x = chunk_layout(hb)
    ssq_c = chunk_layout(ssq)

    w0, cm, pr, pi, prc, pic = _s5_tables(a_re, a_im, log_dt, b_re, b_im, c_re, c_im, n_steps)
    nblk = d // LANES
    n_state = w0.shape[2]
    kern = functools.partial(_s5_kernel, n_steps=n_steps, inv_d=1.0 / d)
    z = pl.pallas_call(
        kern,
        grid=(nblk, bsz),
        in_specs=[pl.BlockSpec((q, rows, LANES), lambda k, b: (0, b, k)),
                  pl.BlockSpec((q, rows, 1), lambda k, b: (0, b, 0)),
                  pl.BlockSpec((1, LANES), lambda k, b: (0, k)),
                  pl.BlockSpec((1, LANES, n_state), lambda k, b: (k, 0, 0)),
                  pl.BlockSpec((1, n_state, LANES), lambda k, b: (k, 0, 0)),
                  pl.BlockSpec((1,) + pr.shape[1:], lambda k, b: (k, 0, 0)),
                  pl.BlockSpec((1,) + pi.shape[1:], lambda k, b: (k, 0, 0)),
                  pl.BlockSpec((1,) + prc.shape[1:], lambda k, b: (k, 0, 0)),
                  pl.BlockSpec((1,) + pic.shape[1:], lambda k, b: (k, 0, 0)),
                  pl.BlockSpec((1, LANES), lambda k, b: (0, k))],
        out_specs=pl.BlockSpec((q, rows, LANES), lambda k, b: (0, b, k)),
        out_shape=jax.ShapeDtypeStruct((q, bsz * rows, d), BF16),
        scratch_shapes=[pltpu.VMEM((q * LANES, q * LANES), BF16),
                        pltpu.VMEM((q * LANES, n_state), BF16),
                        pltpu.VMEM((n_state, q * LANES), BF16),
                        pltpu.VMEM((rows, n_state), F32),
                        pltpu.VMEM((rows, n_state), F32)],
        compiler_params=_params("arbitrary", "arbitrary"),
        name="s5_mixer",
    )(x, ssq_c, gain.reshape(1, d).astype(F32), w0, cm, pr, pi, prc, pic, d_skip.reshape(1, d).astype(F32))
    return z.reshape(q, bsz, n_steps, nseg, d).transpose(1, 3, 2, 0, 4).reshape(t, d)


def _rope_tables(length):
    inv = ROPE_THETA ** (-jnp.arange(0, HEAD_DIM, 2, dtype=F32) / HEAD_DIM)
    ang = jnp.arange(length, dtype=F32)[:, None] * inv[None, :]
    cos, sin = jnp.cos(ang), jnp.sin(ang)
    return jnp.concatenate([cos, cos], axis=1), jnp.concatenate([-sin, sin], axis=1)


def _lambda_init(layer_idx):
    return 0.8 - 0.6 * math.exp(-0.3 * layer_idx)


def kernel(x, meta_tokens, norm_mix_g, norm_mlp_g, da_w_qkv, da_q_norm_g, da_k_norm_g, da_lambda, da_subln_g, da_w_o, ssm_a_re, ssm_a_im, ssm_log_dt, ssm_b_re, ssm_b_im, ssm_c_re, ssm_c_im, ssm_d, ssm_w_glu, ssm_b_glu, mlp_w_up, mlp_w_down):
    bsz, seq, d = x.shape
    depth = norm_mix_g.shape[0]
    n_mixers = 2
    length = N_META + seq
    lp = ((length + SEQ_ALIGN - 1) // SEQ_ALIGN) * SEQ_ALIGN
    assert meta_tokens.shape[0] == N_META and depth >= 1
    h, hn = _embed_norm(x, meta_tokens, norm_mix_g[0], lp)
    cos, sin = (jnp.tile(tab, (bsz, 1)) for tab in _rope_tables(lp))
    def up_job(i):
        return mlp_w_up, i, norm_mlp_g[i]

    def bf16_weight(name, stack, layer, gain=None):
        if name not in wb:
            src = stack[layer]
            wb[name] = (src if gain is None else src * gain.astype(F32)[:, None]).astype(BF16)
        return wb[name]

    wb = {}
    hb = ssq = None
    only_output_rows = False
    for i in range(depth):
        j = i // n_mixers
        nxt_s5 = i + 1 < depth and (i + 1) % n_mixers == 1
        if i % n_mixers == 0:
            if i > 0:
                hn = _rmsnorm(h, norm_mix_g[i])
            qkv, wb["o", j], wb["up", i] = _qkv_proj(
                hn, bf16_weight(("qkv", j), da_w_qkv, j), cos, sin, da_q_norm_g[j], da_k_norm_g[j],
                [(da_w_o, j, None), up_job(i)])
            att = _diff_attention(qkv, da_lambda[j], da_subln_g[j], bsz, lp, d, _lambda_init(i))
            h, hb, ssq = _proj_residual(att, wb["o", j], h)
        else:
            z = _s5_mixer(hb, ssq, norm_mix_g[i], bsz, lp, ssm_a_re[j], ssm_a_im[j], ssm_log_dt[j],
                          ssm_b_re[j], ssm_b_im[j], ssm_c_re[j], ssm_c_im[j], ssm_d[j])
            w_glu = bf16_weight(("glu", j), ssm_w_glu, j)
            if i == depth - 1:
                h, hb, ssq = _glu_residual_rows(z, w_glu, ssm_b_glu[j], h, bsz, lp, N_META, seq)
                only_output_rows = True
            else:
                h, hb, ssq = _glu_residual(z, w_glu, ssm_b_glu[j], h)
        jobs =[(("down", i), mlp_w_down, i, None)]
        if nxt_s5:
            jobs += [(("glu", (i + 1) // n_mixers), ssm_w_glu, (i + 1) // n_mixers, None),
                     (("up", i + 1),) + up_job(i + 1)]
        f, *casts = _mlp_up(hb, bf16_weight(("up", i), *up_job(i)), ssq, [job[1:] for job in jobs])
        wb.update({job[0]: c for job, c in zip(jobs, casts)})
        if i == depth - 1:
            if only_output_rows:
                return _mlp_down(f, wb["down", i], h, False).reshape(bsz, seq, d)
            return _mlp_down_final(f, wb["down", i], h, bsz, lp, N_META, seq).reshape(bsz, seq, d)
        if nxt_s5:
            h, hb, ssq = _mlp_down(f, wb["down", i], h, True)
        else:
            h = _mlp_down(f, wb["down", i], h, False)
```

```python
import functools
import math

import jax
import jax.numpy as jnp
from jax import lax
from jax.experimental import pallas as pl
from jax.experimental.pallas import tpu as pltpu

N_META = 16
SEQ_ALIGN = 256
HEAD_DIM = 128
LOG2E = 1.4426950408889634
ATTN_ROW_BLOCK = 64
ROPE_THETA = 10000.0
SSM_GROUP = 16
SSM_CHUNK = 16
SSM_SEGMENTS = 8
MXU_COLS = 256
LANES = 128
SUBLANES = 8
EPS = 1e-6
VMEM_LIMIT_BYTES = 56 * 1024 * 1024

F32 = jnp.float32
BF16 = jnp.bfloat16


def _pick_tile(n, target, mult):
    best = None
    for t in range(mult, min(n, target) + 1, mult):
        if n % t == 0:
            best = t
    assert best is not None, (n, target, mult)
    return best


def _row_tile(t):
    return _pick_tile(t, 1536, SEQ_ALIGN if t % SEQ_ALIGN == 0 else 16)


def _params(*sem):
    return pltpu.CompilerParams(dimension_semantics=sem, vmem_limit_bytes=VMEM_LIMIT_BYTES)


class _SideCasts:
    def __init__(self, jobs, n_steps, step_of):
        self.operands, self.in_specs, self.out_specs, self.out_shapes, self.has_gain = [], [], [], [], []
        for src, layer, gain in jobs:
            _, r, c = src.shape
            cr = next(x for x in (16 << p for p in range(24)) if r % x == 0 and r // x <= n_steps)

            def idx(*g, last=r // cr - 1):
                return (jnp.minimum(step_of(*g), last), 0)

            def src_idx(*g, layer=layer, last=r // cr - 1):
                return (layer, jnp.minimum(step_of(*g), last), 0)

            self.operands.append(src)
            self.in_specs.append(pl.BlockSpec((None, cr, c), src_idx))
            if gain is not None:
                self.operands.append(gain.reshape(r, 1).astype(F32))
                self.in_specs.append(pl.BlockSpec((cr, 1), idx))
            self.out_specs.append(pl.BlockSpec((cr, c), idx))
            self.out_shapes.append(jax.ShapeDtypeStruct((r, c), BF16))
            self.has_gain.append(gain is not None)
        self.n_in = len(self.operands)
        self.n_out = len(self.out_shapes)

    def run(self, in_refs, out_refs):
        refs = iter(in_refs)
        for has_gain, o_ref in zip(self.has_gain, out_refs):
            v = next(refs)[...]
            if has_gain:
                v = v * next(refs)[...]
            o_ref[...] = v.astype(o_ref.dtype)


def _rmsnorm_kernel(x_ref, g_ref, o_ref):
    x = x_ref[...]
    ms = jnp.mean(x * x, axis=-1, keepdims=True)
    o_ref[...] = (x * lax.rsqrt(ms + EPS) * g_ref[...]).astype(o_ref.dtype)


def _rmsnorm(x, g):
    t, d = x.shape
    tr = _pick_tile(t, 256, 16)
    return pl.pallas_call(
        _rmsnorm_kernel,
        grid=(t // tr,),
        in_specs=[pl.BlockSpec((tr, d), lambda i: (i, 0)),
                  pl.BlockSpec((1, d), lambda i: (0, 0))],
        out_specs=pl.BlockSpec((tr, d), lambda i: (i, 0)),
        out_shape=jax.ShapeDtypeStruct((t, d), BF16),
        compiler_params=_params("parallel"),
        name="rmsnorm",
    )(x, g.reshape(1, d).astype(F32))


def _embed_norm_kernel(x_ref, meta_ref, g_ref, h_ref, hn_ref, *, n_meta, seq):
    tr = h_ref.shape[0]
    i = pl.program_id(1)

    @pl.when(i == 0)
    def _():
        h_ref[:n_meta, :] = meta_ref[...]
        h_ref[n_meta:, :] = x_ref[:tr - n_meta, :]

    @pl.when(i > 0)
    def _():
        row = i * tr + lax.broadcasted_iota(jnp.int32, h_ref.shape, 0)
        h_ref[...] = jnp.where(row < n_meta + seq, x_ref[...], 0.0)

    h = h_ref[...]
    ms = jnp.mean(h * h, axis=-1, keepdims=True)
    hn_ref[...] = (h * lax.rsqrt(ms + EPS) * g_ref[...]).astype(hn_ref.dtype)


def _embed_norm(x, meta_tokens, g, lp):
    bsz, seq, d = x.shape
    n_meta = meta_tokens.shape[0]
    tr = _pick_tile(lp, 256, 16)
    assert n_meta % 16 == 0 and n_meta < tr and tr < seq
    nt = lp // tr
    tile = pl.BlockSpec((tr, d), lambda b, i: (b * nt + i, 0))

    def x_window(b, i):
        return (b, pl.multiple_of(jnp.maximum(i * tr - n_meta, 0), 16), 0)

    return pl.pallas_call(
        functools.partial(_embed_norm_kernel, n_meta=n_meta, seq=seq),
        grid=(bsz, nt),
        in_specs=[pl.BlockSpec((pl.Squeezed(), pl.Element(tr, (0, lp - n_meta - seq)), pl.Element(d)), x_window),
                  pl.BlockSpec((n_meta, d), lambda b, i: (0, 0)),
                  pl.BlockSpec((1, d), lambda b, i: (0, 0))],
        out_specs=(tile, tile),
        out_shape=(jax.ShapeDtypeStruct((bsz * lp, d), F32), jax.ShapeDtypeStruct((bsz * lp, d), BF16)),
        compiler_params=_params("parallel", "arbitrary"),
        name="embed_norm",
    )(x, meta_tokens.astype(F32), g.reshape(1, d).astype(F32))


def _qkv_kernel(*refs, n_q_blocks, q_scale, side):
    a_ref, w_ref, cos_ref, sin_ref, qg_ref, kg_ref = refs[:6]
    side_in = refs[6:6 + side.n_in]
    o_ref = refs[6 + side.n_in]
    side_out = refs[7 + side.n_in:]
    acc = jnp.dot(a_ref[...], w_ref[...], preferred_element_type=F32)
    j = pl.program_id(1)
    tn = acc.shape[1]

    def norm_rope(g, scale):
        cos = cos_ref[...]
        sin = sin_ref[...]
        for u in range(tn // HEAD_DIM):
            x = acc[:, u * HEAD_DIM:(u + 1) * HEAD_DIM]
            ms = jnp.mean(x * x, axis=-1, keepdims=True)
            y = x * lax.rsqrt(ms + EPS) * g
            y = y * cos + pltpu.roll(y, HEAD_DIM // 2, 1) * sin
            if scale != 1.0:
                y = y * scale
            o_ref[:, u * HEAD_DIM:(u + 1) * HEAD_DIM] = y.astype(o_ref.dtype)

    @pl.when(j < n_q_blocks)
    def _():
        norm_rope(qg_ref[...], q_scale)
        side.run(side_in, side_out)

    @pl.when(jnp.logical_and(j >= n_q_blocks, j < 2 * n_q_blocks))
    def _():
        norm_rope(kg_ref[...], 1.0)
        side.run(side_in, side_out)

    @pl.when(j >= 2 * n_q_blocks)
    def _():
        o_ref[...] = acc.astype(o_ref.dtype)
        side.run(side_in, side_out)


def _qkv_proj(a, w, cos, sin, q_g, k_g, cast_jobs):
    t, d = a.shape
    n = w.shape[1]
    tm = _row_tile(t)
    tn = _pick_tile(d, 512, HEAD_DIM)
    nj = n // tn
    side = _SideCasts(cast_jobs, (t // tm) * nj, lambda i, j: i * nj + j)
    kern = functools.partial(_qkv_kernel, n_q_blocks=d // tn, q_scale=HEAD_DIM ** -0.5 * LOG2E, side=side)
    return pl.pallas_call(
        kern,
        grid=(t // tm, nj),
        in_specs=[pl.BlockSpec((tm, d), lambda i, j: (i, 0)),
                  pl.BlockSpec((d, tn), lambda i, j: (0, j)),
                  pl.BlockSpec((tm, HEAD_DIM), lambda i, j: (i, 0)),
                  pl.BlockSpec((tm, HEAD_DIM), lambda i, j: (i, 0)),
                  pl.BlockSpec((1, HEAD_DIM), lambda i, j: (0, 0)),
                  pl.BlockSpec((1, HEAD_DIM), lambda i, j: (0, 0))] + side.in_specs,
        out_specs=[pl.BlockSpec((tm, tn), lambda i, j: (i, j))] + side.out_specs,
        out_shape=[jax.ShapeDtypeStruct((t, n), BF16)] + side.out_shapes,
        compiler_params=_params("arbitrary", "arbitrary"),
        name="qkv_proj",
    )(a, w, cos, sin, q_g.reshape(1, HEAD_DIM).astype(F32), k_g.reshape(1, HEAD_DIM).astype(F32),
      *side.operands)


def _attn_kernel(q_ref, k_ref, v_ref, lam_ref, g_ref, o_ref,
                 m1_ref, l1_ref, a1_ref, m2_ref, l2_ref, a2_ref, s_even_ref, s_odd_ref, p_ref, alpha_ref,
                 *, blk, lam_init):
    s_bufs = (s_even_ref, s_odd_ref)
    nq = q_ref.shape[0] // blk
    stats = ((m1_ref, l1_ref, a1_ref), (m2_ref, l2_ref, a2_ref))
    lv = lam_ref[...]
    lam = (jnp.exp(jnp.sum(lv[0:1] * lv[1:2], axis=-1, keepdims=True))
           - jnp.exp(jnp.sum(lv[2:3] * lv[3:4], axis=-1, keepdims=True)) + lam_init)

    def rows(i):
        return pl.ds(pl.multiple_of(i * blk, blk), blk)

    def init_stats():
        for m_ref, l_ref, a_ref in stats:
            m_ref[...] = jnp.full(m_ref.shape, -jnp.inf, F32)
            l_ref[...] = jnp.zeros(l_ref.shape, F32)
            a_ref[...] = jnp.zeros(a_ref.shape, F32)

    def scores(qi, j, diag, s_ref):
        q = q_ref[rows(qi), :]
        kk = k_ref[rows(j), :]
        for c in range(2):
            heads = slice(c * HEAD_DIM, (c + 1) * HEAD_DIM)
            for r0 in (range(0, blk, MXU_COLS) if diag else (0,)):
                cols = slice(r0, r0 + MXU_COLS) if diag else slice(0, blk)
                s_ref[c, r0:, cols] = lax.dot_general(
                    q[r0:, heads], kk[cols, heads], (((1,), (1,)), ((), ())), preferred_element_type=F32)

    def softmax_pv(j, s_ref, masked):
        vv = v_ref[rows(j), :]
        for c, (m_ref, l_ref, a_ref) in enumerate(stats):
            for r in range(0, blk, ATTN_ROW_BLOCK):
                rb = slice(r, r + ATTN_ROW_BLOCK)
                nc = MXU_COLS * (r // MXU_COLS + 1) if masked else blk
                s = s_ref[c, rb, :nc]
                if masked:
                    row = r + lax.broadcasted_iota(jnp.int32, s.shape, 0)
                    col = lax.broadcasted_iota(jnp.int32, s.shape, 1)
                    s = jnp.where(col <= row, s, -jnp.inf)
                m_old = m_ref[rb, :]
                m_new = jnp.maximum(m_old, jnp.max(s, axis=-1, keepdims=True))
                alpha = jnp.exp2(m_old - m_new)
                p = jnp.exp2(s - m_new)
                part = p[:, :LANES]
                for u in range(1, nc // LANES):
                    part = part + p[:, u * LANES:(u + 1) * LANES]
                l_ref[rb, :] = alpha * l_ref[rb, :] + part
                p_ref[c, rb, :nc] = p.astype(BF16)
                alpha_ref[c, rb, :] = alpha
                m_ref[rb, :] = m_new
            if masked:
                w = MXU_COLS
                parts = [jnp.dot(p_ref[c, r0:, r0:r0 + w], vv[r0:r0 + w], preferred_element_type=F32)
                         for r0 in range(0, blk, w)]
                for t in range(blk // w):
                    pv = parts[0][t * w:(t + 1) * w]
                    for u in range(1, t + 1):
                        pv = pv + parts[u][(t - u) * w:(t - u + 1) * w]
                    rows_t = slice(t * w, (t + 1) * w)
                    a_ref[rows_t, :] = alpha_ref[c, rows_t, :] * a_ref[rows_t, :] + pv
            else:
                a_ref[...] = alpha_ref[c] * a_ref[...] + jnp.dot(p_ref[c], vv, preferred_element_type=F32)

    def finalize(qi):
        l1 = jnp.sum(l1_ref[...], axis=-1, keepdims=True)
        l2 = jnp.sum(l2_ref[...], axis=-1, keepdims=True)
        o = a1_ref[...] / l1 - lam * (a2_ref[...] / l2)
        ms = jnp.mean(o * o, axis=-1, keepdims=True)
        y = o * lax.rsqrt(ms + EPS) * g_ref[...] * (1.0 - lam_init)
        o_ref[rows(qi), :] = y.astype(o_ref.dtype)

    def items(n, *work):
        for parity in range(2):
            @pl.when(n % 2 == parity)
            def _():
                for step, (next_qi, next_j, next_diag, j, masked) in enumerate(work):
                    mine = (parity + step) % 2
                    scores(next_qi, next_j, next_diag, s_bufs[1 - mine])
                    softmax_pv(j, s_bufs[mine], masked)

    init_stats()
    scores(0, 0, True, s_bufs[0])

    def q_block(qi, carry):
        base = (qi * (qi + 1)) // 2

        def full_chunk(j, c):
            items(base + j, (qi, j + 1, False, j, False))
            return c

        lax.fori_loop(0, jnp.maximum(qi - 1, 0), full_chunk, 0)

        @pl.when(qi >= 1)
        def _():
            items(base + qi - 1, (qi, qi, True, qi - 1, False))

        items(base + qi, (jnp.minimum(qi + 1, nq - 1), 0, False, qi, True))
        finalize(qi)
        init_stats()
        return carry

    lax.fori_loop(0, nq, q_block, 0)


def _diff_attention(qkv, lam_vecs, subln_g, bsz, lp, d, lam_init):
    t = qkv.shape[0]
    n_heads = d // (2 * HEAD_DIM)
    hw = 2 * HEAD_DIM
    blk = _pick_tile(lp, 768, SEQ_ALIGN)
    kern = functools.partial(_attn_kernel, blk=blk, lam_init=lam_init)
    return pl.pallas_call(
        kern,
        grid=(bsz, n_heads),
        in_specs=[pl.BlockSpec((lp, hw), lambda b, h: (b, h)),
                  pl.BlockSpec((lp, hw), lambda b, h: (b, n_heads + h)),
                  pl.BlockSpec((lp, hw), lambda b, h: (b, 2 * n_heads + h)),
                  pl.BlockSpec((4, HEAD_DIM), lambda b, h: (0, 0)),
                  pl.BlockSpec((1, hw), lambda b, h: (0, 0))],
        out_specs=pl.BlockSpec((lp, hw), lambda b, h: (b, h)),
        out_shape=jax.ShapeDtypeStruct((t, d), BF16),
        scratch_shapes=[pltpu.VMEM((blk, 1), F32), pltpu.VMEM((blk, LANES), F32), pltpu.VMEM((blk, hw), F32),
                        pltpu.VMEM((blk, 1), F32), pltpu.VMEM((blk, LANES), F32), pltpu.VMEM((blk, hw), F32),
                        pltpu.VMEM((2, blk, blk + LANES), F32), pltpu.VMEM((2, blk, blk + LANES), F32),
                        pltpu.VMEM((2, blk, blk), BF16), pltpu.VMEM((2, blk, 1), F32)],
        compiler_params=_params("parallel", "parallel"),
        name="diff_attention",
    )(qkv, qkv, qkv, lam_vecs.astype(F32), subln_g.reshape(1, hw).astype(F32))


def _col_panels(n):
    return [slice(u, u + MXU_COLS) for u in range(0, n, MXU_COLS)]


def _emit_norm_inputs(h, cols, hb_ref, ssq_ref):
    hb_ref[:, cols] = h.astype(hb_ref.dtype)
    ssq_ref[...] += jnp.sum(h * h, axis=-1, keepdims=True)


def _proj_residual_kernel(a_ref, w_ref, r_ref, o_ref, hb_ref, ssq_ref):
    @pl.when(pl.program_id(1) == 0)
    def _():
        ssq_ref[...] = jnp.zeros(ssq_ref.shape, F32)

    for cols in _col_panels(o_ref.shape[1]):
        h = r_ref[:, cols] + jnp.dot(a_ref[...], w_ref[:, cols], preferred_element_type=F32)
        o_ref[:, cols] = h
        _emit_norm_inputs(h, cols, hb_ref, ssq_ref)


def _norm_out_shapes(t, n):
    return (jax.ShapeDtypeStruct((t, n), F32), jax.ShapeDtypeStruct((t, n), BF16),
            jax.ShapeDtypeStruct((t, 1), F32))


def _proj_residual(a, w, res):
    t, k = a.shape
    n = w.shape[1]
    tm = _row_tile(t)
    tn = _pick_tile(n, 512, LANES)
    return pl.pallas_call(
        _proj_residual_kernel,
        grid=(t // tm, n // tn),
        in_specs=[pl.BlockSpec((tm, k), lambda i, j: (i, 0)),
                  pl.BlockSpec((k, tn), lambda i, j: (0, j)),
                  pl.BlockSpec((tm, tn), lambda i, j: (i, j))],
        out_specs=(pl.BlockSpec((tm, tn), lambda i, j: (i, j)),
                   pl.BlockSpec((tm, tn), lambda i, j: (i, j)),
                   pl.BlockSpec((tm, 1), lambda i, j: (i, 0))),
        out_shape=_norm_out_shapes(t, n),
        compiler_params=_params("parallel", "arbitrary"),
        name="proj_residual",
    )(a, w, res)


def _mlp_up_kernel(*refs, inv_d, side):
    a_ref, w_ref, ssq_ref = refs[:3]
    side_in = refs[3:3 + side.n_in]
    o_ref = refs[3 + side.n_in]
    side_out = refs[4 + side.n_in:]
    r2 = 1.0 / (ssq_ref[...] * inv_d + EPS)
    side.run(side_in, side_out)
    for cols in _col_panels(o_ref.shape[1]):
        u = jnp.dot(a_ref[...], w_ref[:, cols], preferred_element_type=F32)
        o_ref[:, cols] = (jnp.square(jnp.maximum(u, 0.0)) * r2).astype(o_ref.dtype)


def _mlp_up(hb, w, ssq, cast_jobs):
    t, k = hb.shape
    n = w.shape[1]
    tm = _row_tile(t)
    tn = _pick_tile(n, 512, LANES)
    nj = n // tn
    side = _SideCasts(cast_jobs, (t // tm) * nj, lambda i, j: i * nj + j)
    return pl.pallas_call(
        functools.partial(_mlp_up_kernel, inv_d=1.0 / k, side=side),
        grid=(t // tm, nj),
        in_specs=[pl.BlockSpec((tm, k), lambda i, j: (i, 0)),
                  pl.BlockSpec((k, tn), lambda i, j: (0, j)),
                  pl.BlockSpec((tm, 1), lambda i, j: (i, 0))] + side.in_specs,
        out_specs=[pl.BlockSpec((tm, tn), lambda i, j: (i, j))] + side.out_specs,
        out_shape=[jax.ShapeDtypeStruct((t, n), BF16)] + side.out_shapes,
        compiler_params=_params("arbitrary", "arbitrary"),
        name="mlp_up",
    )(hb, w, ssq, *side.operands)


def _mlp_down_kernel(a_ref, w_ref, r_ref, o_ref, *norm_refs, nk, col_axis=1):
    j = pl.program_id(col_axis)
    k = pl.program_id(col_axis + 1)

    def accumulate(base_ref, last):
        for cols in _col_panels(o_ref.shape[1]):
            h = base_ref[:, cols] + jnp.dot(a_ref[...], w_ref[:, cols], preferred_element_type=F32)
            o_ref[:, cols] = h
            if last and norm_refs:
                _emit_norm_inputs(h, cols, *norm_refs)

    if norm_refs:
        @pl.when(jnp.logical_and(j == 0, k == 0))
        def _():
            norm_refs[1][...] = jnp.zeros(norm_refs[1].shape, F32)

    if nk == 1:
        accumulate(r_ref, True)
        return

    @pl.when(k == 0)
    def _():
        accumulate(r_ref, False)

    @pl.when(jnp.logical_and(k > 0, k < nk - 1))
    def _():
        accumulate(o_ref, False)

    @pl.when(k == nk - 1)
    def _():
        accumulate(o_ref, True)


def _mlp_down(a, w, res, emit_norm):
    t, kdim = a.shape
    n = w.shape[1]
    tm = _pick_tile(t, 1024, SEQ_ALIGN if t % SEQ_ALIGN == 0 else 16)
    tn = _pick_tile(n, 1024, LANES)
    tk = _pick_tile(kdim, 4096, LANES)
    nk = kdim // tk
    tile = pl.BlockSpec((tm, tn), lambda i, j, k: (i, j))
    if emit_norm:
        out_specs = (tile, tile, pl.BlockSpec((tm, 1), lambda i, j, k: (i, 0)))
        out_shape = _norm_out_shapes(t, n)
    else:
        out_specs = tile
        out_shape = jax.ShapeDtypeStruct((t, n), F32)
    return pl.pallas_call(
        functools.partial(_mlp_down_kernel, nk=nk),
        grid=(t // tm, n // tn, nk),
        in_specs=[pl.BlockSpec((tm, tk), lambda i, j, k: (i, k)),
                  pl.BlockSpec((tk, tn), lambda i, j, k: (k, j)),
                  tile],
        out_specs=out_specs,
        out_shape=out_shape,
        compiler_params=_params("parallel", "arbitrary", "arbitrary"),
        name="mlp_down",
    )(a, w, res)


def _mlp_down_final(a, w, res, bsz, lp, row0, seq):
    kdim = a.shape[1]
    n = w.shape[1]
    tm = _pick_tile(seq, 1024, 16)
    nt = seq // tm
    tn = _pick_tile(n, 1024, LANES)
    tk = _pick_tile(kdim, 4096, LANES)
    nk = kdim // tk

    assert lp % 16 == 0 and row0 % 16 == 0 and tm % 16 == 0

    def row_start(b, m):
        return pl.multiple_of(b * lp + row0 + m * tm, 16)

    return pl.pallas_call(
        functools.partial(_mlp_down_kernel, nk=nk, col_axis=2),
        grid=(bsz, nt, n // tn, nk),
        in_specs=[pl.BlockSpec((pl.Element(tm), pl.Element(tk)), lambda b, m, j, k: (row_start(b, m), k * tk)),
                  pl.BlockSpec((tk, tn), lambda b, m, j, k: (k, j)),
                  pl.BlockSpec((pl.Element(tm), pl.Element(tn)), lambda b, m, j, k: (row_start(b, m), j * tn))],
        out_specs=pl.BlockSpec((tm, tn), lambda b, m, j, k: (b * nt + m, j)),
        out_shape=jax.ShapeDtypeStruct((bsz * seq, n), F32),
        compiler_params=_params("parallel", "parallel", "arbitrary", "arbitrary"),
        name="mlp_down_final",
    )(a, w, res)


def _glu_kernel(a_ref, wv_ref, wg_ref, bv_ref, bg_ref, r_ref, o_ref, hb_ref, ssq_ref, *, col_axis=1):
    @pl.when(pl.program_id(col_axis) == 0)
    def _():
        ssq_ref[...] = jnp.zeros(ssq_ref.shape, F32)

    for cols in _col_panels(o_ref.shape[1]):
        val = jnp.dot(a_ref[...], wv_ref[:, cols], preferred_element_type=F32) + bv_ref[:, cols]
        gate = jnp.dot(a_ref[...], wg_ref[:, cols], preferred_element_type=F32) + bg_ref[:, cols]
        h = r_ref[:, cols] + val * jax.nn.sigmoid(gate)
        o_ref[:, cols] = h
        _emit_norm_inputs(h, cols, hb_ref, ssq_ref)


def _glu_residual(a, w, bias, res):
    t, k = a.shape
    n = w.shape[1] // 2
    tm = _row_tile(t)
    tn = _pick_tile(n, 256, LANES)
    nb = n // tn
    b2 = bias.reshape(1, 2 * n).astype(F32)
    return pl.pallas_call(
        _glu_kernel,
        grid=(t // tm, nb),
        in_specs=[pl.BlockSpec((tm, k), lambda i, j: (i, 0)),
                  pl.BlockSpec((k, tn), lambda i, j: (0, j)),
                  pl.BlockSpec((k, tn), lambda i, j: (0, nb + j)),
                  pl.BlockSpec((1, tn), lambda i, j: (0, j)),
                  pl.BlockSpec((1, tn), lambda i, j: (0, nb + j)),
                  pl.BlockSpec((tm, tn), lambda i, j: (i, j))],
        out_specs=(pl.BlockSpec((tm, tn), lambda i, j: (i, j)),
                   pl.BlockSpec((tm, tn), lambda i, j: (i, j)),
                   pl.BlockSpec((tm, 1), lambda i, j: (i, 0))),
        out_shape=_norm_out_shapes(t, n),
        compiler_params=_params("parallel", "arbitrary"),
        name="glu_residual",
    )(a, w, w, b2, b2, res)


def _glu_residual_rows(a, w, bias, res, bsz, lp, row0, seq):
    k = a.shape[1]
    n = w.shape[1] // 2
    tm = _pick_tile(seq, 1024, 16)
    nt = seq // tm
    tn = _pick_tile(n, 256, LANES)
    nb = n // tn
    b2 = bias.reshape(1, 2 * n).astype(F32)
    assert lp % 16 == 0 and row0 % 16 == 0 and tm % 16 == 0

    def row_start(b, m):
        return pl.multiple_of(b * lp + row0 + m * tm, 16)

    tile = pl.BlockSpec((tm, tn), lambda b, m, j: (b * nt + m, j))
    return pl.pallas_call(
        functools.partial(_glu_kernel, col_axis=2),
        grid=(bsz, nt, nb),
        in_specs=[pl.BlockSpec((pl.Element(tm), pl.Element(k)), lambda b, m, j: (row_start(b, m), 0)),
                  pl.BlockSpec((k, tn), lambda b, m, j: (0, j)),
                  pl.BlockSpec((k, tn), lambda b, m, j: (0, nb + j)),
                  pl.BlockSpec((1, tn), lambda b, m, j: (0, j)),
                  pl.BlockSpec((1, tn), lambda b, m, j: (0, nb + j)),
                  pl.BlockSpec((pl.Element(tm), pl.Element(tn)), lambda b, m, j: (row_start(b, m), j * tn))],
        out_specs=(tile, tile, pl.BlockSpec((tm, 1), lambda b, m, j: (b * nt + m, 0))),
        out_shape=_norm_out_shapes(bsz * seq, n),
        compiler_params=_params("parallel", "parallel", "arbitrary"),
        name="glu_residual_rows",
    )(a, w, w, b2, b2, res)


def _s5_kernel(x_ref, ssq_ref, g_ref, w0_ref, cm_ref, pr_ref, pi_ref, prc_ref, pic_ref, d_ref, o_ref,
               m_scr, wst_scr, wout_scr, s_scr, xin_scr, *, n_steps, inv_d):
    q = SSM_CHUNK
    half = w0_ref.shape[2] // 2

    @pl.when(pl.program_id(1) == 0)
    def _build_weights():
        w0 = w0_ref[0]
        w0re, w0im = w0[:, :half], w0[:, half:]
        cm = cm_ref[0]
        cm_bf = cm.astype(BF16)
        row = lax.broadcasted_iota(jnp.int32, (LANES, LANES), 0)
        col = lax.broadcasted_iota(jnp.int32, (LANES, LANES), 1)
        skip = jnp.where(row == col, jnp.broadcast_to(d_ref[...], (LANES, LANES)), 0.0)
        zero_blk = jnp.zeros((LANES, LANES), BF16)
        for tau in range(q):
            ar = pr_ref[0, tau:tau + 1, :]
            ai = pi_ref[0, tau:tau + 1, :]
            w_tau = jnp.concatenate([w0re * ar - w0im * ai, w0im * ar + w0re * ai], axis=1).astype(BF16)
            t_st = q - 1 - tau
            wst_scr[t_st * LANES:(t_st + 1) * LANES, :] = w_tau
            k_tau = jnp.dot(w_tau, cm_bf, preferred_element_type=F32)
            if tau == 0:
                k_tau = k_tau + skip
            k_bf = k_tau.astype(BF16)
            for t in range(q - tau):
                m_scr[t * LANES:(t + 1) * LANES, (t + tau) * LANES:(t + tau + 1) * LANES] = k_bf
        for t in range(q):
            for t2 in range(t):
                m_scr[t * LANES:(t + 1) * LANES, t2 * LANES:(t2 + 1) * LANES] = zero_blk
        cre, cimn = cm[:half], cm[half:]
        for t in range(q):
            arc = prc_ref[0, :, t + 1:t + 2]
            aic = pic_ref[0, :, t + 1:t + 2]
            wout_scr[:half, t * LANES:(t + 1) * LANES] = (arc * cre + aic * cimn).astype(BF16)
            wout_scr[half:, t * LANES:(t + 1) * LANES] = (arc * cimn - aic * cre).astype(BF16)

    gain = g_ref[...]
    u = jnp.concatenate(
        [(x_ref[t].astype(F32) * lax.rsqrt(ssq_ref[t] * inv_d + EPS) * gain).astype(BF16) for t in range(q)],
        axis=1)
    s_scr[...] = jnp.dot(u, wst_scr[...], preferred_element_type=F32)

    shp = (SUBLANES, half)
    ar = jnp.broadcast_to(pr_ref[0, q:q + 1, :], shp)
    ai = jnp.broadcast_to(pi_ref[0, q:q + 1, :], shp)
    anr = jnp.broadcast_to(pr_ref[0, q + 1:q + 2, :], shp)
    ani = jnp.broadcast_to(pi_ref[0, q + 1:q + 2, :], shp)
    seg = lax.broadcasted_iota(jnp.int32, shp, 0)
    zeros = jnp.zeros(shp, F32)

    def shift_down(x):
        return jnp.where(seg == 0, 0.0, pltpu.roll(x, 1, 0))

    def advance(j, cr, ci):
        sj = s_scr[pl.ds(pl.multiple_of(j * SUBLANES, SUBLANES), SUBLANES), :]
        return ar * cr - ai * ci + sj[:, :half], ar * ci + ai * cr + sj[:, half:]

    er, ei = lax.fori_loop(0, n_steps, lambda j, c: advance(j, *c), (zeros, zeros))
    tr, ti = er, ei
    for _ in range(SSM_SEGMENTS - 1):
        sr, si = shift_down(tr), shift_down(ti)
        tr, ti = er + anr * sr - ani * si, ei + anr * si + ani * sr
    cr0, ci0 = shift_down(tr), shift_down(ti)

    def scan_store(j, c):
        cr, ci = c
        xin_scr[pl.ds(pl.multiple_of(j * SUBLANES, SUBLANES), SUBLANES), :] = jnp.concatenate([cr, ci], axis=1)
        return advance(j, cr, ci)

    lax.fori_loop(0, n_steps, scan_store, (cr0, ci0))

    xin = xin_scr[...].astype(BF16)
    steps_per_panel = 4
    for t0 in range(0, q, steps_per_panel):
        k_hi = (t0 + steps_per_panel) * LANES
        cols = slice(t0 * LANES, k_hi)
        y = (jnp.dot(u[:, :k_hi], m_scr[:k_hi, cols], preferred_element_type=F32)
             + jnp.dot(xin, wout_scr[:, cols], preferred_element_type=F32))
        z = jax.nn.gelu(y)
        for t in range(steps_per_panel):
            o_ref[t0 + t] = z[:, t * LANES:(t + 1) * LANES].astype(o_ref.dtype)


def _complex_pow(zr, zi, n):
    rr, ri = None, None
    br, bi = zr, zi
    while n:
        if n & 1:
            rr, ri = (br, bi) if rr is None else (rr * br - ri * bi, rr * bi + ri * br)
        n >>= 1
        if n:
            br, bi = br * br - bi * bi, 2.0 * br * bi
    return rr, ri


def _s5_tables(a_re, a_im, log_dt, b_re, b_im, c_re, c_im, n_steps):
    g, p = a_re.shape
    gpb = LANES // SSM_GROUP
    nblk = g // gpb
    dt = jnp.exp(log_dt.astype(F32))[:, None]
    ar = a_re.astype(F32)
    ai = a_im.astype(F32)
    mag = jnp.exp(dt * ar)
    ang = dt * ai
    abar_re = mag * jnp.cos(ang)
    abar_im = mag * jnp.sin(ang)
    nr = abar_re - 1.0
    ni = abar_im
    den = ar * ar + ai * ai
    f_re = (nr * ar + ni * ai) / den
    f_im = (ni * ar - nr * ai) / den
    br = b_re.astype(F32)
    bi = b_im.astype(F32)
    bb_re = f_re[..., None] * br - f_im[..., None] * bi
    bb_im = f_re[..., None] * bi + f_im[..., None] * br

    pw_r, pw_i = [jnp.ones_like(abar_re)], [jnp.zeros_like(abar_re)]
    for _ in range(SSM_CHUNK):
        pw_r.append(pw_r[-1] * abar_re - pw_i[-1] * abar_im)
        pw_i.append(pw_r[-2] * abar_im + pw_i[-1] * abar_re)
    seg_r, seg_i = _complex_pow(pw_r[-1], pw_i[-1], n_steps)
    pw_r.append(seg_r)
    pw_i.append(seg_i)
    n_rows = 24
    pr = jnp.stack(pw_r, 0).reshape(len(pw_r), nblk, gpb * p).transpose(1, 0, 2)
    pi = jnp.stack(pw_i, 0).reshape(len(pw_i), nblk, gpb * p).transpose(1, 0, 2)
    pad = ((0, 0), (0, n_rows - pr.shape[1]), (0, 0))
    pr = jnp.pad(pr, pad)
    pi = jnp.pad(pi, pad)
    prc = pr.transpose(0, 2, 1)
    pic = pi.transpose(0, 2, 1)

    eye = jnp.eye(gpb, dtype=F32)

    def in_to_state(bb):
        v = bb.reshape(nblk, gpb, p, SSM_GROUP).transpose(0, 1, 3, 2)
        e = v[:, :, :, None, :] * eye[None, :, None, :, None]
        return e.reshape(nblk, LANES, gpb * p)

    def state_to_out(c):
        v = c.astype(F32).reshape(nblk, gpb, SSM_GROUP, p).transpose(0, 1, 3, 2)
        e = v[:, :, :, None, :] * eye[None, :, None, :, None]
        return e.reshape(nblk, gpb * p, LANES)

    w0 = jnp.concatenate([in_to_state(bb_re), in_to_state(bb_im)], axis=2)
    cm = jnp.concatenate([state_to_out(c_re), -state_to_out(c_im)], axis=1)
    return w0, cm, pr, pi, prc, pic


def _s5_mixer(hb, ssq, gain, bsz, lp, a_re, a_im, log_dt, b_re, b_im, c_re, c_im, d_skip):
    t, d = hb.shape
    q, nseg = SSM_CHUNK, SSM_SEGMENTS
    assert lp % (q * nseg * 2) == 0
    n_steps = lp // (q * nseg)
    rows = n_steps * nseg

    def chunk_layout(v):
        w = v.shape[-1]
        return v.reshape(bsz, nseg, n_steps, q, w).transpose(3, 0, 2, 1, 4).reshape(q, bsz * rows, w)

    x = chunk_layout(hb)
    ssq_c = chunk_layout(ssq)

    w0, cm, pr, pi, prc, pic = _s5_tables(a_re, a_im, log_dt, b_re, b_im, c_re, c_im, n_steps)
    nblk = d // LANES
    n_state = w0.shape[2]
    kern = functools.partial(_s5_kernel, n_steps=n_steps, inv_d=1.0 / d)
    z = pl.pallas_call(
        kern,
        grid=(nblk, bsz),
        in_specs=[pl.BlockSpec((q, rows, LANES), lambda k, b: (0, b, k)),
                  pl.BlockSpec((q, rows, 1), lambda k, b: (0, b, 0)),
                  pl.BlockSpec((1, LANES), lambda k, b: (0, k)),
                  pl.BlockSpec((1, LANES, n_state), lambda k, b: (k, 0, 0)),
                  pl.BlockSpec((1, n_state, LANES), lambda k, b: (k, 0, 0)),
                  pl.BlockSpec((1,) + pr.shape[1:], lambda k, b: (k, 0, 0)),
                  pl.BlockSpec((1,) + pi.shape[1:], lambda k, b: (k, 0, 0)),
                  pl.BlockSpec((1,) + prc.shape[1:], lambda k, b: (k, 0, 0)),
                  pl.BlockSpec((1,) + pic.shape[1:], lambda k, b: (k, 0, 0)),
                  pl.BlockSpec((1, LANES), lambda k, b: (0, k))],
        out_specs=pl.BlockSpec((q, rows, LANES), lambda k, b: (0, b, k)),
        out_shape=jax.ShapeDtypeStruct((q, bsz * rows, d), BF16),
        scratch_shapes=[pltpu.VMEM((q * LANES, q * LANES), BF16),
                        pltpu.VMEM((q * LANES, n_state), BF16),
                        pltpu.VMEM((n_state, q * LANES), BF16),
                        pltpu.VMEM((rows, n_state), F32),
                        pltpu.VMEM((rows, n_state), F32)],
        compiler_params=_params("arbitrary", "arbitrary"),
        name="s5_mixer",
    )(x, ssq_c, gain.reshape(1, d).astype(F32), w0, cm, pr, pi, prc, pic, d_skip.reshape(1, d).astype(F32))
    return z.reshape(q, bsz, n_steps, nseg, d).transpose(1, 3, 2, 0, 4).reshape(t, d)


def _rope_tables(length):
    inv = ROPE_THETA ** (-jnp.arange(0, HEAD_DIM, 2, dtype=F32) / HEAD_DIM)
    ang = jnp.arange(length, dtype=F32)[:, None] * inv[None, :]
    cos, sin = jnp.cos(ang), jnp.sin(ang)
    return jnp.concatenate([cos, cos], axis=1), jnp.concatenate([-sin, sin], axis=1)


def _lambda_init(layer_idx):
    return 0.8 - 0.6 * math.exp(-0.3 * layer_idx)


def kernel(x, meta_tokens, norm_mix_g, norm_mlp_g, da_w_qkv, da_q_norm_g, da_k_norm_g, da_lambda, da_subln_g, da_w_o, ssm_a_re, ssm_a_im, ssm_log_dt, ssm_b_re, ssm_b_im, ssm_c_re, ssm_c_im, ssm_d, ssm_w_glu, ssm_b_glu, mlp_w_up, mlp_w_down):
    bsz, seq, d = x.shape
    depth = norm_mix_g.shape[0]
    n_mixers = 2
    length = N_META + seq
    lp = ((length + SEQ_ALIGN - 1) // SEQ_ALIGN) * SEQ_ALIGN
    assert meta_tokens.shape[0] == N_META and depth >= 1
    h, hn = _embed_norm(x, meta_tokens, norm_mix_g[0], lp)
    cos, sin = (jnp.tile(tab, (bsz, 1)) for tab in _rope_tables(lp))
    def up_job(i):
        return mlp_w_up, i, norm_mlp_g[i]

    def bf16_weight(name, stack, layer, gain=None):
        if name not in wb:
            src = stack[layer]
            wb[name] = (src if gain is None else src * gain.astype(F32)[:, None]).astype(BF16)
        return wb[name]

    wb = {}
    hb = ssq = None
    only_output_rows = False
    for i in range(depth):
        j = i // n_mixers
        nxt_s5 = i + 1 < depth and (i + 1) % n_mixers == 1
        if i % n_mixers == 0:
            if i > 0:
                hn = _rmsnorm(h, norm_mix_g[i])
            qkv, wb["o", j], wb["up", i] = _qkv_proj(
                hn, bf16_weight(("qkv", j), da_w_qkv, j), cos, sin, da_q_norm_g[j], da_k_norm_g[j],
                [(da_w_o, j, None), up_job(i)])
            att = _diff_attention(qkv, da_lambda[j], da_subln_g[j], bsz, lp, d, _lambda_init(i))
            h, hb, ssq = _proj_residual(att, wb["o", j], h)
        else:
            z = _s5_mixer(hb, ssq, norm_mix_g[i], bsz, lp, ssm_a_re[j], ssm_a_im[j], ssm_log_dt[j],
                          ssm_b_re[j], ssm_b_im[j], ssm_c_re[j], ssm_c_im[j], ssm_d[j])
            w_glu = bf16_weight(("glu", j), ssm_w_glu, j)
            if i == depth - 1:
                h, hb, ssq = _glu_residual_rows(z, w_glu, ssm_b_glu[j], h, bsz, lp, N_META, seq)
                only_output_rows = True
            else:
                h, hb, ssq = _glu_residual(z, w_glu, ssm_b_glu[j], h)
        jobs =[(("down", i), mlp_w_down, i, None)]
        if nxt_s5:
            jobs += [(("glu", (i + 1) // n_mixers), ssm_w_glu, (i + 1) // n_mixers, None),
                     (("up", i + 1),) + up_job(i + 1)]
        f, *casts = _mlp_up(hb, bf16_weight(("up", i), *up_job(i)), ssq, [job[1:] for job in jobs])
        wb.update({job[0]: c for job, c in zip(jobs, casts)})
        if i == depth - 1:
            if only_output_rows:
                return _mlp_down(f, wb["down", i], h, False).reshape(bsz, seq, d)
            return _mlp_down_final(f, wb["down", i], h, bsz, lp, N_META, seq).reshape(bsz, seq, d)
        if nxt_s5:
            h, hb, ssq = _mlp_down(f, wb["down", i], h, True)
        else:
            h = _mlp_down(f, wb["down", i], h, False)
```

```python
import functools
import math

import jax
import jax.numpy as jnp
from jax import lax
from jax.experimental import pallas as pl
from jax.experimental.pallas import tpu as pltpu

N_META = 16
SEQ_ALIGN = 256
HEAD_DIM = 128
LOG2E = 1.4426950408889634
ATTN_ROW_BLOCK = 64
ROPE_THETA = 10000.0
SSM_GROUP = 16
SSM_CHUNK = 16
SSM_SEGMENTS = 8
MXU_COLS = 256
LANES = 128
SUBLANES = 8
EPS = 1e-6
VMEM_LIMIT_BYTES = 56 * 1024 * 1024

F32 = jnp.float32
BF16 = jnp.bfloat16


def _pick_tile(n, target, mult):
    best = None
    for t in range(mult, min(n, target) + 1, mult):
        if n % t == 0:
            best = t
    assert best is not None, (n, target, mult)
    return best


def _row_tile(t):
    return _pick_tile(t, 1536, SEQ_ALIGN if t % SEQ_ALIGN == 0 else 16)


def _params(*sem):
    return pltpu.CompilerParams(dimension_semantics=sem, vmem_limit_bytes=VMEM_LIMIT_BYTES)


class _SideCasts:
    def __init__(self, jobs, n_steps, step_of):
        self.operands, self.in_specs, self.out_specs, self.out_shapes, self.has_gain = [], [], [], [], []
        for src, layer, gain in jobs:
            _, r, c = src.shape
            cr = next(x for x in (16 << p for p in range(24)) if r % x == 0 and r // x <= n_steps)

            def idx(*g, last=r // cr - 1):
                return (jnp.minimum(step_of(*g), last), 0)

            def src_idx(*g, layer=layer, last=r // cr - 1):
                return (layer, jnp.minimum(step_of(*g), last), 0)

            self.operands.append(src)
            self.in_specs.append(pl.BlockSpec((None, cr, c), src_idx))
            if gain is not None:
                self.operands.append(gain.reshape(r, 1).astype(F32))
                self.in_specs.append(pl.BlockSpec((cr, 1), idx))
            self.out_specs.append(pl.BlockSpec((cr, c), idx))
            self.out_shapes.append(jax.ShapeDtypeStruct((r, c), BF16))
            self.has_gain.append(gain is not None)
        self.n_in = len(self.operands)
        self.n_out = len(self.out_shapes)

    def run(self, in_refs, out_refs):
        refs = iter(in_refs)
        for has_gain, o_ref in zip(self.has_gain, out_refs):
            v = next(refs)[...]
            if has_gain:
                v = v * next(refs)[...]
            o_ref[...] = v.astype(o_ref.dtype)


def _rmsnorm_kernel(x_ref, g_ref, o_ref):
    x = x_ref[...]
    ms = jnp.mean(x * x, axis=-1, keepdims=True)
    o_ref[...] = (x * lax.rsqrt(ms + EPS) * g_ref[...]).astype(o_ref.dtype)


def _rmsnorm(x, g):
    t, d = x.shape
    tr = _pick_tile(t, 256, 16)
    return pl.pallas_call(
        _rmsnorm_kernel,
        grid=(t // tr,),
        in_specs=[pl.BlockSpec((tr, d), lambda i: (i, 0)),
                  pl.BlockSpec((1, d), lambda i: (0, 0))],
        out_specs=pl.BlockSpec((tr, d), lambda i: (i, 0)),
        out_shape=jax.ShapeDtypeStruct((t, d), BF16),
        compiler_params=_params("parallel"),
        name="rmsnorm",
    )(x, g.reshape(1, d).astype(F32))


def _embed_norm_kernel(x_ref, meta_ref, g_ref, h_ref, hn_ref, *, n_meta, seq):
    tr = h_ref.shape[0]
    i = pl.program_id(1)

    @pl.when(i == 0)
    def _():
        h_ref[:n_meta, :] = meta_ref[...]
        h_ref[n_meta:, :] = x_ref[:tr - n_meta, :]

    @pl.when(i > 0)
    def _():
        row = i * tr + lax.broadcasted_iota(jnp.int32, h_ref.shape, 0)
        h_ref[...] = jnp.where(row < n_meta + seq, x_ref[...], 0.0)

    h = h_ref[...]
    ms = jnp.mean(h * h, axis=-1, keepdims=True)
    hn_ref[...] = (h * lax.rsqrt(ms + EPS) * g_ref[...]).astype(hn_ref.dtype)


def _embed_norm(x, meta_tokens, g, lp):
    bsz, seq, d = x.shape
    n_meta = meta_tokens.shape[0]
    tr = _pick_tile(lp, 256, 16)
    assert n_meta % 16 == 0 and n_meta < tr and tr < seq
    nt = lp // tr
    tile = pl.BlockSpec((tr, d), lambda b, i: (b * nt + i, 0))

    def x_window(b, i):
        return (b, pl.multiple_of(jnp.maximum(i * tr - n_meta, 0), 16), 0)

    return pl.pallas_call(
        functools.partial(_embed_norm_kernel, n_meta=n_meta, seq=seq),
        grid=(bsz, nt),
        in_specs=[pl.BlockSpec((pl.Squeezed(), pl.Element(tr, (0, lp - n_meta - seq)), pl.Element(d)), x_window),
                  pl.BlockSpec((n_meta, d), lambda b, i: (0, 0)),
                  pl.BlockSpec((1, d), lambda b, i: (0, 0))],
        out_specs=(tile, tile),
        out_shape=(jax.ShapeDtypeStruct((bsz * lp, d), F32), jax.ShapeDtypeStruct((bsz * lp, d), BF16)),
        compiler_params=_params("parallel", "arbitrary"),
        name="embed_norm",
    )(x, meta_tokens.astype(F32), g.reshape(1, d).astype(F32))


def _qkv_kernel(*refs, n_q_blocks, q_scale, side):
    a_ref, w_ref, cos_ref, sin_ref, qg_ref, kg_ref = refs[:6]
    side_in = refs[6:6 + side.n_in]
    o_ref = refs[6 + side.n_in]
    side_out = refs[7 + side.n_in:]
    acc = jnp.dot(a_ref[...], w_ref[...], preferred_element_type=F32)
    j = pl.program_id(1)
    tn = acc.shape[1]

    def norm_rope(g, scale):
        cos = cos_ref[...]
        sin = sin_ref[...]
        for u in range(tn // HEAD_DIM):
            x = acc[:, u * HEAD_DIM:(u + 1) * HEAD_DIM]
            ms = jnp.mean(x * x, axis=-1, keepdims=True)
            y = x * lax.rsqrt(ms + EPS) * g
            y = y * cos + pltpu.roll(y, HEAD_DIM // 2, 1) * sin
            if scale != 1.0:
                y = y * scale
            o_ref[:, u * HEAD_DIM:(u + 1) * HEAD_DIM] = y.astype(o_ref.dtype)

    @pl.when(j < n_q_blocks)
    def _():
        norm_rope(qg_ref[...], q_scale)
        side.run(side_in, side_out)

    @pl.when(jnp.logical_and(j >= n_q_blocks, j < 2 * n_q_blocks))
    def _():
        norm_rope(kg_ref[...], 1.0)
        side.run(side_in, side_out)

    @pl.when(j >= 2 * n_q_blocks)
    def _():
        o_ref[...] = acc.astype(o_ref.dtype)
        side.run(side_in, side_out)


def _qkv_proj(a, w, cos, sin, q_g, k_g, cast_jobs):
    t, d = a.shape
    n = w.shape[1]
    tm = _row_tile(t)
    tn = _pick_tile(d, 512, HEAD_DIM)
    nj = n // tn
    side = _SideCasts(cast_jobs, (t // tm) * nj, lambda i, j: i * nj + j)
    kern = functools.partial(_qkv_kernel, n_q_blocks=d // tn, q_scale=HEAD_DIM ** -0.5 * LOG2E, side=side)
    return pl.pallas_call(
        kern,
        grid=(t // tm, nj),
        in_specs=[pl.BlockSpec((tm, d), lambda i, j: (i, 0)),
                  pl.BlockSpec((d, tn), lambda i, j: (0, j)),
                  pl.BlockSpec((tm, HEAD_DIM), lambda i, j: (i, 0)),
                  pl.BlockSpec((tm, HEAD_DIM), lambda i, j: (i, 0)),
                  pl.BlockSpec((1, HEAD_DIM), lambda i, j: (0, 0)),
                  pl.BlockSpec((1, HEAD_DIM), lambda i, j: (0, 0))] + side.in_specs,
        out_specs=[pl.BlockSpec((tm, tn), lambda i, j: (i, j))] + side.out_specs,
        out_shape=[jax.ShapeDtypeStruct((t, n), BF16)] + side.out_shapes,
        compiler_params=_params("arbitrary", "arbitrary"),
        name="qkv_proj",
    )(a, w, cos, sin, q_g.reshape(1, HEAD_DIM).astype(F32), k_g.reshape(1, HEAD_DIM).astype(F32),
      *side.operands)


def _attn_kernel(q_ref, k_ref, v_ref, lam_ref, g_ref, o_ref,
                 m1_ref, l1_ref, a1_ref, m2_ref, l2_ref, a2_ref, s_even_ref, s_odd_ref, p_ref, alpha_ref,
                 *, blk, lam_init):
    s_bufs = (s_even_ref, s_odd_ref)
    nq = q_ref.shape[0] // blk
    stats = ((m1_ref, l1_ref, a1_ref), (m2_ref, l2_ref, a2_ref))
    lv = lam_ref[...]
    lam = (jnp.exp(jnp.sum(lv[0:1] * lv[1:2], axis=-1, keepdims=True))
           - jnp.exp(jnp.sum(lv[2:3] * lv[3:4], axis=-1, keepdims=True)) + lam_init)

    def rows(i):
        return pl.ds(pl.multiple_of(i * blk, blk), blk)

    def init_stats():
        for m_ref, l_ref, a_ref in stats:
            m_ref[...] = jnp.full(m_ref.shape, -jnp.inf, F32)
            l_ref[...] = jnp.zeros(l_ref.shape, F32)
            a_ref[...] = jnp.zeros(a_ref.shape, F32)

    def scores(qi, j, diag, s_ref):
        q = q_ref[rows(qi), :]
        kk = k_ref[rows(j), :]
        for c in range(2):
            heads = slice(c * HEAD_DIM, (c + 1) * HEAD_DIM)
            for r0 in (range(0, blk, MXU_COLS) if diag else (0,)):
                cols = slice(r0, r0 + MXU_COLS) if diag else slice(0, blk)
                s_ref[c, r0:, cols] = lax.dot_general(
                    q[r0:, heads], kk[cols, heads], (((1,), (1,)), ((), ())), preferred_element_type=F32)

    def softmax_pv(j, s_ref, masked):
        vv = v_ref[rows(j), :]
        for c, (m_ref, l_ref, a_ref) in enumerate(stats):
            for r in range(0, blk, ATTN_ROW_BLOCK):
                rb = slice(r, r + ATTN_ROW_BLOCK)
                nc = MXU_COLS * (r // MXU_COLS + 1) if masked else blk
                s = s_ref[c, rb, :nc]
                if masked:
                    row = r + lax.broadcasted_iota(jnp.int32, s.shape, 0)
                    col = lax.broadcasted_iota(jnp.int32, s.shape, 1)
                    s = jnp.where(col <= row, s, -jnp.inf)
                m_old = m_ref[rb, :]
                m_new = jnp.maximum(m_old, jnp.max(s, axis=-1, keepdims=True))
                alpha = jnp.exp2(m_old - m_new)
                p = jnp.exp2(s - m_new)
                part = p[:, :LANES]
                for u in range(1, nc // LANES):
                    part = part + p[:, u * LANES:(u + 1) * LANES]
                l_ref[rb, :] = alpha * l_ref[rb, :] + part
                p_ref[c, rb, :nc] = p.astype(BF16)
                alpha_ref[c, rb, :] = alpha
                m_ref[rb, :] = m_new
            if masked:
                w = MXU_COLS
                parts = [jnp.dot(p_ref[c, r0:, r0:r0 + w], vv[r0:r0 + w], preferred_element_type=F32)
                         for r0 in range(0, blk, w)]
                for t in range(blk // w):
                    pv = parts[0][t * w:(t + 1) * w]
                    for u in range(1, t + 1):
                        pv = pv + parts[u][(t - u) * w:(t - u + 1) * w]
                    rows_t = slice(t * w, (t + 1) * w)
                    a_ref[rows_t, :] = alpha_ref[c, rows_t, :] * a_ref[rows_t, :] + pv
            else:
                a_ref[...] = alpha_ref[c] * a_ref[...] + jnp.dot(p_ref[c], vv, preferred_element_type=F32)

    def finalize(qi):
        l1 = jnp.sum(l1_ref[...], axis=-1, keepdims=True)
        l2 = jnp.sum(l2_ref[...], axis=-1, keepdims=True)
        o = a1_ref[...] / l1 - lam * (a2_ref[...] / l2)
        ms = jnp.mean(o * o, axis=-1, keepdims=True)
        y = o * lax.rsqrt(ms + EPS) * g_ref[...] * (1.0 - lam_init)
        o_ref[rows(qi), :] = y.astype(o_ref.dtype)

    def items(n, *work):
        for parity in range(2):
            @pl.when(n % 2 == parity)
            def _():
                for step, (next_qi, next_j, next_diag, j, masked) in enumerate(work):
                    mine = (parity + step) % 2
                    scores(next_qi, next_j, next_diag, s_bufs[1 - mine])
                    softmax_pv(j, s_bufs[mine], masked)

    init_stats()
    scores(0, 0, True, s_bufs[0])

    def q_block(qi, carry):
        base = (qi * (qi + 1)) // 2

        def full_chunk(j, c):
            items(base + j, (qi, j + 1, False, j, False))
            return c

        lax.fori_loop(0, jnp.maximum(qi - 1, 0), full_chunk, 0)

        @pl.when(qi >= 1)
        def _():
            items(base + qi - 1, (qi, qi, True, qi - 1, False))

        items(base + qi, (jnp.minimum(qi + 1, nq - 1), 0, False, qi, True))
        finalize(qi)
        for m_ref, _, _ in stats:
            m_ref[...] = jnp.full(m_ref.shape, -jnp.inf, F32)
        return carry

    lax.fori_loop(0, nq, q_block, 0)


def _diff_attention(qkv, lam_vecs, subln_g, bsz, lp, d, lam_init):
    t = qkv.shape[0]
    n_heads = d // (2 * HEAD_DIM)
    hw = 2 * HEAD_DIM
    blk = _pick_tile(lp, 768, SEQ_ALIGN)
    kern = functools.partial(_attn_kernel, blk=blk, lam_init=lam_init)
    return pl.pallas_call(
        kern,
        grid=(bsz, n_heads),
        in_specs=[pl.BlockSpec((lp, hw), lambda b, h: (b, h)),
                  pl.BlockSpec((lp, hw), lambda b, h: (b, n_heads + h)),
                  pl.BlockSpec((lp, hw), lambda b, h: (b, 2 * n_heads + h)),
                  pl.BlockSpec((4, HEAD_DIM), lambda b, h: (0, 0)),
                  pl.BlockSpec((1, hw), lambda b, h: (0, 0))],
        out_specs=pl.BlockSpec((lp, hw), lambda b, h: (b, h)),
        out_shape=jax.ShapeDtypeStruct((t, d), BF16),
        scratch_shapes=[pltpu.VMEM((blk, 1), F32), pltpu.VMEM((blk, LANES), F32), pltpu.VMEM((blk, hw), F32),
                        pltpu.VMEM((blk, 1), F32), pltpu.VMEM((blk, LANES), F32), pltpu.VMEM((blk, hw), F32),
                        pltpu.VMEM((2, blk, blk + LANES), F32), pltpu.VMEM((2, blk, blk + LANES), F32),
                        pltpu.VMEM((2, blk, blk), BF16), pltpu.VMEM((2, blk, 1), F32)],
        compiler_params=_params("parallel", "parallel"),
        name="diff_attention",
    )(qkv, qkv, qkv, lam_vecs.astype(F32), subln_g.reshape(1, hw).astype(F32))


def _col_panels(n):
    return [slice(u, u + MXU_COLS) for u in range(0, n, MXU_COLS)]


def _emit_norm_inputs(h, cols, hb_ref, ssq_ref):
    hb_ref[:, cols] = h.astype(hb_ref.dtype)
    ssq_ref[...] += jnp.sum(h * h, axis=-1, keepdims=True)


def _proj_residual_kernel(a_ref, w_ref, r_ref, o_ref, hb_ref, ssq_ref):
    @pl.when(pl.program_id(1) == 0)
    def _():
        ssq_ref[...] = jnp.zeros(ssq_ref.shape, F32)

    for cols in _col_panels(o_ref.shape[1]):
        h = r_ref[:, cols] + jnp.dot(a_ref[...], w_ref[:, cols], preferred_element_type=F32)
        o_ref[:, cols] = h
        _emit_norm_inputs(h, cols, hb_ref, ssq_ref)


def _norm_out_shapes(t, n):
    return (jax.ShapeDtypeStruct((t, n), F32), jax.ShapeDtypeStruct((t, n), BF16),
            jax.ShapeDtypeStruct((t, 1), F32))


def _proj_residual(a, w, res):
    t, k = a.shape
    n = w.shape[1]
    tm = _row_tile(t)
    tn = _pick_tile(n, 512, LANES)
    return pl.pallas_call(
        _proj_residual_kernel,
        grid=(t // tm, n // tn),
        in_specs=[pl.BlockSpec((tm, k), lambda i, j: (i, 0)),
                  pl.BlockSpec((k, tn), lambda i, j: (0, j)),
                  pl.BlockSpec((tm, tn), lambda i, j: (i, j))],
        out_specs=(pl.BlockSpec((tm, tn), lambda i, j: (i, j)),
                   pl.BlockSpec((tm, tn), lambda i, j: (i, j)),
                   pl.BlockSpec((tm, 1), lambda i, j: (i, 0))),
        out_shape=_norm_out_shapes(t, n),
        compiler_params=_params("parallel", "arbitrary"),
        name="proj_residual",
    )(a, w, res)


def _mlp_up_kernel(*refs, inv_d, side):
    a_ref, w_ref, ssq_ref = refs[:3]
    side_in = refs[3:3 + side.n_in]
    o_ref = refs[3 + side.n_in]
    side_out = refs[4 + side.n_in:]
    r2 = 1.0 / (ssq_ref[...] * inv_d + EPS)
    side.run(side_in, side_out)
    for cols in _col_panels(o_ref.shape[1]):
        u = jnp.dot(a_ref[...], w_ref[:, cols], preferred_element_type=F32)
        o_ref[:, cols] = (jnp.square(jnp.maximum(u, 0.0)) * r2).astype(o_ref.dtype)


def _mlp_up(hb, w, ssq, cast_jobs):
    t, k = hb.shape
    n = w.shape[1]
    tm = _row_tile(t)
    tn = _pick_tile(n, 512 if tm > 1024 else 1024, LANES)
    nj = n // tn
    side = _SideCasts(cast_jobs, (t // tm) * nj, lambda i, j: i * nj + j)
    return pl.pallas_call(
        functools.partial(_mlp_up_kernel, inv_d=1.0 / k, side=side),
        grid=(t // tm, nj),
        in_specs=[pl.BlockSpec((tm, k), lambda i, j: (i, 0)),
                  pl.BlockSpec((k, tn), lambda i, j: (0, j)),
                  pl.BlockSpec((tm, 1), lambda i, j: (i, 0))] + side.in_specs,
        out_specs=[pl.BlockSpec((tm, tn), lambda i, j: (i, j))] + side.out_specs,
        out_shape=[jax.ShapeDtypeStruct((t, n), BF16)] + side.out_shapes,
        compiler_params=_params("arbitrary", "arbitrary"),
        name="mlp_up",
    )(hb, w, ssq, *side.operands)


def _mlp_down_kernel(a_ref, w_ref, r_ref, o_ref, *norm_refs, nk, col_axis=1):
    j = pl.program_id(col_axis)
    k = pl.program_id(col_axis + 1)

    def accumulate(base_ref, last):
        for cols in _col_panels(o_ref.shape[1]):
            h = base_ref[:, cols] + jnp.dot(a_ref[...], w_ref[:, cols], preferred_element_type=F32)
            o_ref[:, cols] = h
            if last and norm_refs:
                _emit_norm_inputs(h, cols, *norm_refs)

    if norm_refs:
        @pl.when(jnp.logical_and(j == 0, k == 0))
        def _():
            norm_refs[1][...] = jnp.zeros(norm_refs[1].shape, F32)

    if nk == 1:
        accumulate(r_ref, True)
        return

    @pl.when(k == 0)
    def _():
        accumulate(r_ref, False)

    @pl.when(jnp.logical_and(k > 0, k < nk - 1))
    def _():
        accumulate(o_ref, False)

    @pl.when(k == nk - 1)
    def _():
        accumulate(o_ref, True)


def _mlp_down(a, w, res, emit_norm):
    t, kdim = a.shape
    n = w.shape[1]
    tm = _pick_tile(t, 1024, SEQ_ALIGN if t % SEQ_ALIGN == 0 else 16)
    tn = _pick_tile(n, 1024, LANES)
    tk = _pick_tile(kdim, 4096, LANES)
    nk = kdim // tk
    tile = pl.BlockSpec((tm, tn), lambda i, j, k: (i, j))
    if emit_norm:
        out_specs = (tile, tile, pl.BlockSpec((tm, 1), lambda i, j, k: (i, 0)))
        out_shape = _norm_out_shapes(t, n)
    else:
        out_specs = tile
        out_shape = jax.ShapeDtypeStruct((t, n), F32)
    return pl.pallas_call(
        functools.partial(_mlp_down_kernel, nk=nk),
        grid=(t // tm, n // tn, nk),
        in_specs=[pl.BlockSpec((tm, tk), lambda i, j, k: (i, k)),
                  pl.BlockSpec((tk, tn), lambda i, j, k: (k, j)),
                  tile],
        out_specs=out_specs,
        out_shape=out_shape,
        compiler_params=_params("parallel", "arbitrary", "arbitrary"),
        name="mlp_down",
    )(a, w, res)


def _mlp_down_final(a, w, res, bsz, lp, row0, seq):
    kdim = a.shape[1]
    n = w.shape[1]
    tm = _pick_tile(seq, 1024, 16)
    nt = seq // tm
    tn = _pick_tile(n, 1024, LANES)
    tk = _pick_tile(kdim, 4096, LANES)
    nk = kdim // tk

    assert lp % 16 == 0 and row0 % 16 == 0 and tm % 16 == 0

    def row_start(b, m):
        return pl.multiple_of(b * lp + row0 + m * tm, 16)

    return pl.pallas_call(
        functools.partial(_mlp_down_kernel, nk=nk, col_axis=2),
        grid=(bsz, nt, n // tn, nk),
        in_specs=[pl.BlockSpec((pl.Element(tm), pl.Element(tk)), lambda b, m, j, k: (row_start(b, m), k * tk)),
                  pl.BlockSpec((tk, tn), lambda b, m, j, k: (k, j)),
                  pl.BlockSpec((pl.Element(tm), pl.Element(tn)), lambda b, m, j, k: (row_start(b, m), j * tn))],
        out_specs=pl.BlockSpec((tm, tn), lambda b, m, j, k: (b * nt + m, j)),
        out_shape=jax.ShapeDtypeStruct((bsz * seq, n), F32),
        compiler_params=_params("parallel", "parallel", "arbitrary", "arbitrary"),
        name="mlp_down_final",
    )(a, w, res)


def _glu_kernel(a_ref, wv_ref, wg_ref, bv_ref, bg_ref, r_ref, o_ref, hb_ref, ssq_ref, *, col_axis=1):
    @pl.when(pl.program_id(col_axis) == 0)
    def _():
        ssq_ref[...] = jnp.zeros(ssq_ref.shape, F32)

    for cols in _col_panels(o_ref.shape[1]):
        val = jnp.dot(a_ref[...], wv_ref[:, cols], preferred_element_type=F32) + bv_ref[:, cols]
        gate = jnp.dot(a_ref[...], wg_ref[:, cols], preferred_element_type=F32) + bg_ref[:, cols]
        h = r_ref[:, cols] + val * jax.nn.sigmoid(gate)
        o_ref[:, cols] = h
        _emit_norm_inputs(h, cols, hb_ref, ssq_ref)


def _glu_residual(a, w, bias, res):
    t, k = a.shape
    n = w.shape[1] // 2
    tm = _row_tile(t)
    tn = _pick_tile(n, 256, LANES)
    nb = n // tn
    b2 = bias.reshape(1, 2 * n).astype(F32)
    return pl.pallas_call(
        _glu_kernel,
        grid=(t // tm, nb),
        in_specs=[pl.BlockSpec((tm, k), lambda i, j: (i, 0)),
                  pl.BlockSpec((k, tn), lambda i, j: (0, j)),
                  pl.BlockSpec((k, tn), lambda i, j: (0, nb + j)),
                  pl.BlockSpec((1, tn), lambda i, j: (0, j)),
                  pl.BlockSpec((1, tn), lambda i, j: (0, nb + j)),
                  pl.BlockSpec((tm, tn), lambda i, j: (i, j))],
        out_specs=(pl.BlockSpec((tm, tn), lambda i, j: (i, j)),
                   pl.BlockSpec((tm, tn), lambda i, j: (i, j)),
                   pl.BlockSpec((tm, 1), lambda i, j: (i, 0))),
        out_shape=_norm_out_shapes(t, n),
        compiler_params=_params("parallel", "arbitrary"),
        name="glu_residual",
    )(a, w, w, b2, b2, res)


def _glu_residual_rows(a, w, bias, res, bsz, lp, row0, seq):
    k = a.shape[1]
    n = w.shape[1] // 2
    tm = _pick_tile(seq, 1024, 16)
    nt = seq // tm
    tn = _pick_tile(n, 512, LANES)
    nb = n // tn
    b2 = bias.reshape(1, 2 * n).astype(F32)
    assert lp % 16 == 0 and row0 % 16 == 0 and tm % 16 == 0

    def row_start(b, m):
        return pl.multiple_of(b * lp + row0 + m * tm, 16)

    tile = pl.BlockSpec((tm, tn), lambda b, m, j: (b * nt + m, j))
    return pl.pallas_call(
        functools.partial(_glu_kernel, col_axis=2),
        grid=(bsz, nt, nb),
        in_specs=[pl.BlockSpec((pl.Element(tm), pl.Element(k)), lambda b, m, j: (row_start(b, m), 0)),
                  pl.BlockSpec((k, tn), lambda b, m, j: (0, j)),
                  pl.BlockSpec((k, tn), lambda b, m, j: (0, nb + j)),
                  pl.BlockSpec((1, tn), lambda b, m, j: (0, j)),
                  pl.BlockSpec((1, tn), lambda b, m, j: (0, nb + j)),
                  pl.BlockSpec((pl.Element(tm), pl.Element(tn)), lambda b, m, j: (row_start(b, m), j * tn))],
        out_specs=(tile, tile, pl.BlockSpec((tm, 1), lambda b, m, j: (b * nt + m, 0))),
        out_shape=_norm_out_shapes(bsz * seq, n),
        compiler_params=_params("parallel", "parallel", "arbitrary"),
        name="glu_residual_rows",
    )(a, w, w, b2, b2, res)


def _s5_kernel(x_ref, ssq_ref, g_ref, w0_ref, cm_ref, pr_ref, pi_ref, prc_ref, pic_ref, d_ref, o_ref,
               m_scr, wst_scr, wout_scr, s_scr, xin_scr, *, n_steps, inv_d):
    q = SSM_CHUNK
    half = w0_ref.shape[2] // 2

    @pl.when(pl.program_id(1) == 0)
    def _build_weights():
        w0 = w0_ref[0]
        w0re, w0im = w0[:, :half], w0[:, half:]
        cm = cm_ref[0]
        cm_bf = cm.astype(BF16)
        row = lax.broadcasted_iota(jnp.int32, (LANES, LANES), 0)
        col = lax.broadcasted_iota(jnp.int32, (LANES, LANES), 1)
        skip = jnp.where(row == col, jnp.broadcast_to(d_ref[...], (LANES, LANES)), 0.0)
        zero_blk = jnp.zeros((LANES, LANES), BF16)
        for tau in range(q):
            ar = pr_ref[0, tau:tau + 1, :]
            ai = pi_ref[0, tau:tau + 1, :]
            w_tau = jnp.concatenate([w0re * ar - w0im * ai, w0im * ar + w0re * ai], axis=1).astype(BF16)
            t_st = q - 1 - tau
            wst_scr[t_st * LANES:(t_st + 1) * LANES, :] = w_tau
            k_tau = jnp.dot(w_tau, cm_bf, preferred_element_type=F32)
            if tau == 0:
                k_tau = k_tau + skip
            k_bf = k_tau.astype(BF16)
            for t in range(q - tau):
                m_scr[t * LANES:(t + 1) * LANES, (t + tau) * LANES:(t + tau + 1) * LANES] = k_bf
        for t in range(q):
            for t2 in range(t):
                m_scr[t * LANES:(t + 1) * LANES, t2 * LANES:(t2 + 1) * LANES] = zero_blk
        cre, cimn = cm[:half], cm[half:]
        for t in range(q):
            arc = prc_ref[0, :, t + 1:t + 2]
            aic = pic_ref[0, :, t + 1:t + 2]
            wout_scr[:half, t * LANES:(t + 1) * LANES] = (arc * cre + aic * cimn).astype(BF16)
            wout_scr[half:, t * LANES:(t + 1) * LANES] = (arc * cimn - aic * cre).astype(BF16)

    gain = g_ref[...]
    u = jnp.concatenate(
        [(x_ref[t].astype(F32) * lax.rsqrt(ssq_ref[t] * inv_d + EPS) * gain).astype(BF16) for t in range(q)],
        axis=1)
    s_scr[...] = jnp.dot(u, wst_scr[...], preferred_element_type=F32)

    shp = (SUBLANES, half)
    ar = jnp.broadcast_to(pr_ref[0, q:q + 1, :], shp)
    ai = jnp.broadcast_to(pi_ref[0, q:q + 1, :], shp)
    anr = jnp.broadcast_to(pr_ref[0, q + 1:q + 2, :], shp)
    ani = jnp.broadcast_to(pi_ref[0, q + 1:q + 2, :], shp)
    seg = lax.broadcasted_iota(jnp.int32, shp, 0)
    zeros = jnp.zeros(shp, F32)

    def shift_down(x):
        return jnp.where(seg == 0, 0.0, pltpu.roll(x, 1, 0))

    def advance(j, cr, ci):
        sj = s_scr[pl.ds(pl.multiple_of(j * SUBLANES, SUBLANES), SUBLANES), :]
        return ar * cr - ai * ci + sj[:, :half], ar * ci + ai * cr + sj[:, half:]

    er, ei = lax.fori_loop(0, n_steps, lambda j, c: advance(j, *c), (zeros, zeros))
    tr, ti = er, ei
    for _ in range(SSM_SEGMENTS - 1):
        sr, si = shift_down(tr), shift_down(ti)
        tr, ti = er + anr * sr - ani * si, ei + anr * si + ani * sr
    cr0, ci0 = shift_down(tr), shift_down(ti)

    def scan_store(j, c):
        cr, ci = c
        xin_scr[pl.ds(pl.multiple_of(j * SUBLANES, SUBLANES), SUBLANES), :] = jnp.concatenate([cr, ci], axis=1)
        return advance(j, cr, ci)

    lax.fori_loop(0, n_steps, scan_store, (cr0, ci0))

    xin = xin_scr[...].astype(BF16)
    steps_per_panel = 4
    for t0 in range(0, q, steps_per_panel):
        k_hi = (t0 + steps_per_panel) * LANES
        cols = slice(t0 * LANES, k_hi)
        y = (jnp.dot(u[:, :k_hi], m_scr[:k_hi, cols], preferred_element_type=F32)
             + jnp.dot(xin, wout_scr[:, cols], preferred_element_type=F32))
        z = jax.nn.gelu(y)
        for t in range(steps_per_panel):
            o_ref[t0 + t] = z[:, t * LANES:(t + 1) * LANES].astype(o_ref.dtype)


def _complex_pow(zr, zi, n):
    rr, ri = None, None
    br, bi = zr, zi
    while n:
        if n & 1:
            rr, ri = (br, bi) if rr is None else (rr * br - ri * bi, rr * bi + ri * br)
        n >>= 1
        if n:
            br, bi = br * br - bi * bi, 2.0 * br * bi
    return rr, ri


def _s5_tables(a_re, a_im, log_dt, b_re, b_im, c_re, c_im, n_steps):
    g, p = a_re.shape
    gpb = LANES // SSM_GROUP
    nblk = g // gpb
    dt = jnp.exp(log_dt.astype(F32))[:, None]
    ar = a_re.astype(F32)
    ai = a_im.astype(F32)
    mag = jnp.exp(dt * ar)
    ang = dt * ai
    abar_re = mag * jnp.cos(ang)
    abar_im = mag * jnp.sin(ang)
    nr = abar_re - 1.0
    ni = abar_im
    den = ar * ar + ai * ai
    f_re = (nr * ar + ni * ai) / den
    f_im = (ni * ar - nr * ai) / den
    br = b_re.astype(F32)
    bi = b_im.astype(F32)
    bb_re = f_re[..., None] * br - f_im[..., None] * bi
    bb_im = f_re[..., None] * bi + f_im[..., None] * br

    pw_r, pw_i = [jnp.ones_like(abar_re)], [jnp.zeros_like(abar_re)]
    for _ in range(SSM_CHUNK):
        pw_r.append(pw_r[-1] * abar_re - pw_i[-1] * abar_im)
        pw_i.append(pw_r[-2] * abar_im + pw_i[-1] * abar_re)
    seg_r, seg_i = _complex_pow(pw_r[-1], pw_i[-1], n_steps)
    pw_r.append(seg_r)
    pw_i.append(seg_i)
    n_rows = 24
    pr = jnp.stack(pw_r, 0).reshape(len(pw_r), nblk, gpb * p).transpose(1, 0, 2)
    pi = jnp.stack(pw_i, 0).reshape(len(pw_i), nblk, gpb * p).transpose(1, 0, 2)
    pad = ((0, 0), (0, n_rows - pr.shape[1]), (0, 0))
    pr = jnp.pad(pr, pad)
    pi = jnp.pad(pi, pad)
    prc = pr.transpose(0, 2, 1)
    pic = pi.transpose(0, 2, 1)

    eye = jnp.eye(gpb, dtype=F32)

    def in_to_state(bb):
        v = bb.reshape(nblk, gpb, p, SSM_GROUP).transpose(0, 1, 3, 2)
        e = v[:, :, :, None, :] * eye[None, :, None, :, None]
        return e.reshape(nblk, LANES, gpb * p)

    def state_to_out(c):
        v = c.astype(F32).reshape(nblk, gpb, SSM_GROUP, p).transpose(0, 1, 3, 2)
        e = v[:, :, :, None, :] * eye[None, :, None, :, None]
        return e.reshape(nblk, gpb * p, LANES)

    w0 = jnp.concatenate([in_to_state(bb_re), in_to_state(bb_im)], axis=2)
    cm = jnp.concatenate([state_to_out(c_re), -state_to_out(c_im)], axis=1)
    return w0, cm, pr, pi, prc, pic


def _s5_mixer(hb, ssq, gain, bsz, lp, a_re, a_im, log_dt, b_re, b_im, c_re, c_im, d_skip):
    t, d = hb.shape
    q, nseg = SSM_CHUNK, SSM_SEGMENTS
    assert lp % (q * nseg * 2) == 0
    n_steps = lp // (q * nseg)
    rows = n_steps * nseg

    def chunk_layout(v):
        w = v.shape[-1]
        return v.reshape(bsz, nseg, n_steps, q, w).transpose(3, 0, 2, 1, 4).reshape(q, bsz * rows, w)

    x = chunk_layout(hb)
    ssq_c = chunk_layout(ssq)

    w0, cm, pr, pi, prc, pic = _s5_tables(a_re, a_im, log_dt, b_re, b_im, c_re, c_im, n_steps)
    nblk = d // LANES
    n_state = w0.shape[2]
    kern = functools.partial(_s5_kernel, n_steps=n_steps, inv_d=1.0 / d)
    z = pl.pallas_call(
        kern,
        grid=(nblk, bsz),
        in_specs=[pl.BlockSpec((q, rows, LANES), lambda k, b: (0, b, k)),
                  pl.BlockSpec((q, rows, 1), lambda k, b: (0, b, 0)),
                  pl.BlockSpec((1, LANES), lambda k, b: (0, k)),
                  pl.BlockSpec((1, LANES, n_state), lambda k, b: (k, 0, 0)),
                  pl.BlockSpec((1, n_state, LANES), lambda k, b: (k, 0, 0)),
                  pl.BlockSpec((1,) + pr.shape[1:], lambda k, b: (k, 0, 0)),
                  pl.BlockSpec((1,) + pi.shape[1:], lambda k, b: (k, 0, 0)),
                  pl.BlockSpec((1,) + prc.shape[1:], lambda k, b: (k, 0, 0)),
                  pl.BlockSpec((1,) + pic.shape[1:], lambda k, b: (k, 0, 0)),
                  pl.BlockSpec((1, LANES), lambda k, b: (0, k))],
        out_specs=pl.BlockSpec((q, rows, LANES), lambda k, b: (0, b, k)),
        out_shape=jax.ShapeDtypeStruct((q, bsz * rows, d), BF16),
        scratch_shapes=[pltpu.VMEM((q * LANES, q * LANES), BF16),
                        pltpu.VMEM((q * LANES, n_state), BF16),
                        pltpu.VMEM((n_state, q * LANES), BF16),
                        pltpu.VMEM((rows, n_state), F32),
                        pltpu.VMEM((rows, n_state), F32)],
        compiler_params=_params("arbitrary", "arbitrary"),
        name="s5_mixer",
    )(x, ssq_c, gain.reshape(1, d).astype(F32), w0, cm, pr, pi, prc, pic, d_skip.reshape(1, d).astype(F32))
    return z.reshape(q, bsz, n_steps, nseg, d).transpose(1, 3, 2, 0, 4).reshape(t, d)


def _rope_tables(length):
    inv = ROPE_THETA ** (-jnp.arange(0, HEAD_DIM, 2, dtype=F32) / HEAD_DIM)
    ang = jnp.arange(length, dtype=F32)[:, None] * inv[None, :]
    cos, sin = jnp.cos(ang), jnp.sin(ang)
    return jnp.concatenate([cos, cos], axis=1), jnp.concatenate([-sin, sin], axis=1)


def _lambda_init(layer_idx):
    return 0.8 - 0.6 * math.exp(-0.3 * layer_idx)


def kernel(x, meta_tokens, norm_mix_g, norm_mlp_g, da_w_qkv, da_q_norm_g, da_k_norm_g, da_lambda, da_subln_g, da_w_o, ssm_a_re, ssm_a_im, ssm_log_dt, ssm_b_re, ssm_b_im, ssm_c_re, ssm_c_im, ssm_d, ssm_w_glu, ssm_b_glu, mlp_w_up, mlp_w_down):
    bsz, seq, d = x.shape
    depth = norm_mix_g.shape[0]
    n_mixers = 2
    length = N_META + seq
    lp = ((length + SEQ_ALIGN - 1) // SEQ_ALIGN) * SEQ_ALIGN
    assert meta_tokens.shape[0] == N_META and depth >= 1
    h, hn = _embed_norm(x, meta_tokens, norm_mix_g[0], lp)
    cos, sin = (jnp.tile(tab, (bsz, 1)) for tab in _rope_tables(lp))
    def up_job(i):
        return mlp_w_up, i, norm_mlp_g[i]

    def bf16_weight(name, stack, layer, gain=None):
        if name not in wb:
            src = stack[layer]
            wb[name] = (src if gain is None else src * gain.astype(F32)[:, None]).astype(BF16)
        return wb[name]

    wb = {}
    hb = ssq = None
    only_output_rows = False
    for i in range(depth):
        j = i // n_mixers
        nxt_s5 = i + 1 < depth and (i + 1) % n_mixers == 1
        if i % n_mixers == 0:
            if i > 0:
                hn = _rmsnorm(h, norm_mix_g[i])
            qkv, wb["o", j], wb["up", i] = _qkv_proj(
                hn, bf16_weight(("qkv", j), da_w_qkv, j), cos, sin, da_q_norm_g[j], da_k_norm_g[j],
                [(da_w_o, j, None), up_job(i)])
            att = _diff_attention(qkv, da_lambda[j], da_subln_g[j], bsz, lp, d, _lambda_init(i))
            h, hb, ssq = _proj_residual(att, wb["o", j], h)
        else:
            z = _s5_mixer(hb, ssq, norm_mix_g[i], bsz, lp, ssm_a_re[j], ssm_a_im[j], ssm_log_dt[j],
                          ssm_b_re[j], ssm_b_im[j], ssm_c_re[j], ssm_c_im[j], ssm_d[j])
            w_glu = bf16_weight(("glu", j), ssm_w_glu, j)
            if i == depth - 1:
                h, hb, ssq = _glu_residual_rows(z, w_glu, ssm_b_glu[j], h, bsz, lp, N_META, seq)
                only_output_rows = True
            else:
                h, hb, ssq = _glu_residual(z, w_glu, ssm_b_glu[j], h)
        jobs =[(("down", i), mlp_w_down, i, None)]
        if nxt_s5:
            jobs += [(("glu", (i + 1) // n_mixers), ssm_w_glu, (i + 1) // n_mixers, None),
                     (("up", i + 1),) + up_job(i + 1)]
        f, *casts = _mlp_up(hb, bf16_weight(("up", i), *up_job(i)), ssq, [job[1:] for job in jobs])
        wb.update({job[0]: c for job, c in zip(jobs, casts)})
        if i == depth - 1:
            if only_output_rows:
                return _mlp_down(f, wb["down", i], h, False).reshape(bsz, seq, d)
            return _mlp_down_final(f, wb["down", i], h, bsz, lp, N_META, seq).reshape(bsz, seq, d)
        if nxt_s5:
            h, hb, ssq = _mlp_down(f, wb["down", i], h, True)
        else:
            h = _mlp_down(f, wb["down", i], h, False)
```

```python
import functools
import math

import jax
import jax.numpy as jnp
from jax import lax
from jax.experimental import pallas as pl
from jax.experimental.pallas import tpu as pltpu

N_META = 16
SEQ_ALIGN = 256
HEAD_DIM = 128
LOG2E = 1.4426950408889634
ATTN_ROW_BLOCK = 64
ROPE_THETA = 10000.0
SSM_GROUP = 16
SSM_CHUNK = 16
SSM_SEGMENTS = 8
MXU_COLS = 256
LANES = 128
SUBLANES = 8
EPS = 1e-6
VMEM_LIMIT_BYTES = 56 * 1024 * 1024

F32 = jnp.float32
BF16 = jnp.bfloat16


def _pick_tile(n, target, mult):
    best = None
    for t in range(mult, min(n, target) + 1, mult):
        if n % t == 0:
            best = t
    assert best is not None, (n, target, mult)
    return best


def _row_tile(t):
    return _pick_tile(t, 1536, SEQ_ALIGN if t % SEQ_ALIGN == 0 else 16)


def _params(*sem):
    return pltpu.CompilerParams(dimension_semantics=sem, vmem_limit_bytes=VMEM_LIMIT_BYTES)


class _SideCasts:
    def __init__(self, jobs, n_steps, step_of):
        self.operands, self.in_specs, self.out_specs, self.out_shapes, self.has_gain = [], [], [], [], []
        for src, layer, gain in jobs:
            _, r, c = src.shape
            cr = next(x for x in (16 << p for p in range(24)) if r % x == 0 and r // x <= n_steps)

            def idx(*g, last=r // cr - 1):
                return (jnp.minimum(step_of(*g), last), 0)

            def src_idx(*g, layer=layer, last=r // cr - 1):
                return (layer, jnp.minimum(step_of(*g), last), 0)

            self.operands.append(src)
            self.in_specs.append(pl.BlockSpec((None, cr, c), src_idx))
            if gain is not None:
                self.operands.append(gain.reshape(r, 1).astype(F32))
                self.in_specs.append(pl.BlockSpec((cr, 1), idx))
            self.out_specs.append(pl.BlockSpec((cr, c), idx))
            self.out_shapes.append(jax.ShapeDtypeStruct((r, c), BF16))
            self.has_gain.append(gain is not None)
        self.n_in = len(self.operands)
        self.n_out = len(self.out_shapes)

    def run(self, in_refs, out_refs):
        refs = iter(in_refs)
        for has_gain, o_ref in zip(self.has_gain, out_refs):
            v = next(refs)[...]
            if has_gain:
                v = v * next(refs)[...]
            o_ref[...] = v.astype(o_ref.dtype)


def _rmsnorm_kernel(x_ref, g_ref, o_ref):
    x = x_ref[...]
    ms = jnp.mean(x * x, axis=-1, keepdims=True)
    o_ref[...] = (x * lax.rsqrt(ms + EPS) * g_ref[...]).astype(o_ref.dtype)


def _rmsnorm(x, g):
    t, d = x.shape
    tr = _pick_tile(t, 256, 16)
    return pl.pallas_call(
        _rmsnorm_kernel,
        grid=(t // tr,),
        in_specs=[pl.BlockSpec((tr, d), lambda i: (i, 0)),
                  pl.BlockSpec((1, d), lambda i: (0, 0))],
        out_specs=pl.BlockSpec((tr, d), lambda i: (i, 0)),
        out_shape=jax.ShapeDtypeStruct((t, d), BF16),
        compiler_params=_params("parallel"),
        name="rmsnorm",
    )(x, g.reshape(1, d).astype(F32))


def _embed_norm_kernel(x_ref, meta_ref, g_ref, h_ref, hn_ref, *, n_meta, seq):
    tr = h_ref.shape[0]
    i = pl.program_id(1)

    @pl.when(i == 0)
    def _():
        h_ref[:n_meta, :] = meta_ref[...]
        h_ref[n_meta:, :] = x_ref[:tr - n_meta, :]

    @pl.when(i > 0)
    def _():
        row = i * tr + lax.broadcasted_iota(jnp.int32, h_ref.shape, 0)
        h_ref[...] = jnp.where(row < n_meta + seq, x_ref[...], 0.0)

    h = h_ref[...]
    ms = jnp.mean(h * h, axis=-1, keepdims=True)
    hn_ref[...] = (h * lax.rsqrt(ms + EPS) * g_ref[...]).astype(hn_ref.dtype)


def _embed_norm(x, meta_tokens, g, lp):
    bsz, seq, d = x.shape
    n_meta = meta_tokens.shape[0]
    tr = _pick_tile(lp, 256, 16)
    assert n_meta % 16 == 0 and n_meta < tr and tr < seq
    nt = lp // tr
    tile = pl.BlockSpec((tr, d), lambda b, i: (b * nt + i, 0))

    def x_window(b, i):
        return (b, pl.multiple_of(jnp.maximum(i * tr - n_meta, 0), 16), 0)

    return pl.pallas_call(
        functools.partial(_embed_norm_kernel, n_meta=n_meta, seq=seq),
        grid=(bsz, nt),
        in_specs=[pl.BlockSpec((pl.Squeezed(), pl.Element(tr, (0, lp - n_meta - seq)), pl.Element(d)), x_window),
                  pl.BlockSpec((n_meta, d), lambda b, i: (0, 0)),
                  pl.BlockSpec((1, d), lambda b, i: (0, 0))],
        out_specs=(tile, tile),
        out_shape=(jax.ShapeDtypeStruct((bsz * lp, d), F32), jax.ShapeDtypeStruct((bsz * lp, d), BF16)),
        compiler_params=_params("parallel", "arbitrary"),
        name="embed_norm",
    )(x, meta_tokens.astype(F32), g.reshape(1, d).astype(F32))


def _qkv_kernel(*refs, n_q_blocks, q_scale, side):
    a_ref, w_ref, cos_ref, sin_ref, qg_ref, kg_ref = refs[:6]
    side_in = refs[6:6 + side.n_in]
    o_ref = refs[6 + side.n_in]
    side_out = refs[7 + side.n_in:]
    acc = jnp.dot(a_ref[...], w_ref[...], preferred_element_type=F32)
    j = pl.program_id(1)
    tn = acc.shape[1]

    def norm_rope(g, scale):
        cos = cos_ref[...]
        sin = sin_ref[...]
        for u in range(tn // HEAD_DIM):
            x = acc[:, u * HEAD_DIM:(u + 1) * HEAD_DIM]
            ms = jnp.mean(x * x, axis=-1, keepdims=True)
            y = x * lax.rsqrt(ms + EPS) * g
            y = y * cos + pltpu.roll(y, HEAD_DIM // 2, 1) * sin
            if scale != 1.0:
                y = y * scale
            o_ref[:, u * HEAD_DIM:(u + 1) * HEAD_DIM] = y.astype(o_ref.dtype)

    @pl.when(j < n_q_blocks)
    def _():
        norm_rope(qg_ref[...], q_scale)
        side.run(side_in, side_out)

    @pl.when(jnp.logical_and(j >= n_q_blocks, j < 2 * n_q_blocks))
    def _():
        norm_rope(kg_ref[...], 1.0)
        side.run(side_in, side_out)

    @pl.when(j >= 2 * n_q_blocks)
    def _():
        o_ref[...] = acc.astype(o_ref.dtype)
        side.run(side_in, side_out)


def _qkv_proj(a, w, cos, sin, q_g, k_g, cast_jobs):
    t, d = a.shape
    n = w.shape[1]
    tm = _row_tile(t)
    tn = _pick_tile(d, 512, HEAD_DIM)
    nj = n // tn
    side = _SideCasts(cast_jobs, (t // tm) * nj, lambda i, j: i * nj + j)
    kern = functools.partial(_qkv_kernel, n_q_blocks=d // tn, q_scale=HEAD_DIM ** -0.5 * LOG2E, side=side)
    return pl.pallas_call(
        kern,
        grid=(t // tm, nj),
        in_specs=[pl.BlockSpec((tm, d), lambda i, j: (i, 0)),
                  pl.BlockSpec((d, tn), lambda i, j: (0, j)),
                  pl.BlockSpec((tm, HEAD_DIM), lambda i, j: (i, 0)),
                  pl.BlockSpec((tm, HEAD_DIM), lambda i, j: (i, 0)),
                  pl.BlockSpec((1, HEAD_DIM), lambda i, j: (0, 0)),
                  pl.BlockSpec((1, HEAD_DIM), lambda i, j: (0, 0))] + side.in_specs,
        out_specs=[pl.BlockSpec((tm, tn), lambda i, j: (i, j))] + side.out_specs,
        out_shape=[jax.ShapeDtypeStruct((t, n), BF16)] + side.out_shapes,
        compiler_params=_params("arbitrary", "arbitrary"),
        name="qkv_proj",
    )(a, w, cos, sin, q_g.reshape(1, HEAD_DIM).astype(F32), k_g.reshape(1, HEAD_DIM).astype(F32),
      *side.operands)


def _attn_kernel(q_ref, k_ref, v_ref, lam_ref, g_ref, o_ref,
                 m1_ref, l1_ref, a1_ref, m2_ref, l2_ref, a2_ref, s_even_ref, s_odd_ref, p_ref, alpha_ref,
                 *, blk, lam_init):
    s_bufs = (s_even_ref, s_odd_ref)
    nq = q_ref.shape[0] // blk
    stats = ((m1_ref, l1_ref, a1_ref), (m2_ref, l2_ref, a2_ref))
    lv = lam_ref[...]
    lam = (jnp.exp(jnp.sum(lv[0:1] * lv[1:2], axis=-1, keepdims=True))
           - jnp.exp(jnp.sum(lv[2:3] * lv[3:4], axis=-1, keepdims=True)) + lam_init)

    def rows(i):
        return pl.ds(pl.multiple_of(i * blk, blk), blk)

    def init_stats():
        for m_ref, l_ref, a_ref in stats:
            m_ref[...] = jnp.full(m_ref.shape, -jnp.inf, F32)
            l_ref[...] = jnp.zeros(l_ref.shape, F32)
            a_ref[...] = jnp.zeros(a_ref.shape, F32)

    def scores(qi, j, diag, s_ref):
        q = q_ref[rows(qi), :]
        kk = k_ref[rows(j), :]
        for c in range(2):
            heads = slice(c * HEAD_DIM, (c + 1) * HEAD_DIM)
            for r0 in (range(0, blk, MXU_COLS) if diag else (0,)):
                cols = slice(r0, r0 + MXU_COLS) if diag else slice(0, blk)
                s_ref[c, r0:, cols] = lax.dot_general(
                    q[r0:, heads], kk[cols, heads], (((1,), (1,)), ((), ())), preferred_element_type=F32)

    def softmax_pv(j, s_ref, masked):
        vv = v_ref[rows(j), :]
        for c, (m_ref, l_ref, a_ref) in enumerate(stats):
            for r in range(0, blk, ATTN_ROW_BLOCK):
                rb = slice(r, r + ATTN_ROW_BLOCK)
                nc = MXU_COLS * (r // MXU_COLS + 1) if masked else blk
                s = s_ref[c, rb, :nc]
                if masked:
                    row = r + lax.broadcasted_iota(jnp.int32, s.shape, 0)
                    col = lax.broadcasted_iota(jnp.int32, s.shape, 1)
                    s = jnp.where(col <= row, s, -jnp.inf)
                m_old = m_ref[rb, :]
                m_new = jnp.maximum(m_old, jnp.max(s, axis=-1, keepdims=True))
                alpha = jnp.exp2(m_old - m_new)
                p = jnp.exp2(s - m_new)
                part = p[:, :LANES]
                for u in range(1, nc // LANES):
                    part = part + p[:, u * LANES:(u + 1) * LANES]
                l_ref[rb, :] = alpha * l_ref[rb, :] + part
                p_ref[c, rb, :nc] = p.astype(BF16)
                alpha_ref[c, rb, :] = alpha
                m_ref[rb, :] = m_new
            if masked:
                w = MXU_COLS
                parts = [jnp.dot(p_ref[c, r0:, r0:r0 + w], vv[r0:r0 + w], preferred_element_type=F32)
                         for r0 in range(0, blk, w)]
                for t in range(blk // w):
                    pv = parts[0][t * w:(t + 1) * w]
                    for u in range(1, t + 1):
                        pv = pv + parts[u][(t - u) * w:(t - u + 1) * w]
                    rows_t = slice(t * w, (t + 1) * w)
                    a_ref[rows_t, :] = alpha_ref[c, rows_t, :] * a_ref[rows_t, :] + pv
            else:
                a_ref[...] = alpha_ref[c] * a_ref[...] + jnp.dot(p_ref[c], vv, preferred_element_type=F32)

    def finalize(qi):
        l1 = jnp.sum(l1_ref[...], axis=-1, keepdims=True)
        l2 = jnp.sum(l2_ref[...], axis=-1, keepdims=True)
        o = a1_ref[...] / l1 - lam * (a2_ref[...] / l2)
        ms = jnp.mean(o * o, axis=-1, keepdims=True)
        y = o * lax.rsqrt(ms + EPS) * g_ref[...] * (1.0 - lam_init)
        o_ref[rows(qi), :] = y.astype(o_ref.dtype)

    def items(n, *work):
        for parity in range(2):
            @pl.when(n % 2 == parity)
            def _():
                for step, (next_qi, next_j, next_diag, j, masked) in enumerate(work):
                    mine = (parity + step) % 2
                    scores(next_qi, next_j, next_diag, s_bufs[1 - mine])
                    softmax_pv(j, s_bufs[mine], masked)

    init_stats()
    scores(0, 0, True, s_bufs[0])

    def q_block(qi, carry):
        base = (qi * (qi + 1)) // 2

        def full_chunk(j, c):
            items(base + j, (qi, j + 1, False, j, False))
            return c

        lax.fori_loop(0, jnp.maximum(qi - 1, 0), full_chunk, 0)

        @pl.when(qi >= 1)
        def _():
            items(base + qi - 1, (qi, qi, True, qi - 1, False))

        items(base + qi, (jnp.minimum(qi + 1, nq - 1), 0, False, qi, True))
        finalize(qi)
        for m_ref, _, _ in stats:
            m_ref[...] = jnp.full(m_ref.shape, -jnp.inf, F32)
        return carry

    lax.fori_loop(0, nq, q_block, 0)


def _diff_attention(qkv, lam_vecs, subln_g, bsz, lp, d, lam_init):
    t = qkv.shape[0]
    n_heads = d // (2 * HEAD_DIM)
    hw = 2 * HEAD_DIM
    blk = _pick_tile(lp, 768, SEQ_ALIGN)
    kern = functools.partial(_attn_kernel, blk=blk, lam_init=lam_init)
    return pl.pallas_call(
        kern,
        grid=(bsz, n_heads),
        in_specs=[pl.BlockSpec((lp, hw), lambda b, h: (b, h)),
                  pl.BlockSpec((lp, hw), lambda b, h: (b, n_heads + h)),
                  pl.BlockSpec((lp, hw), lambda b, h: (b, 2 * n_heads + h)),
                  pl.BlockSpec((4, HEAD_DIM), lambda b, h: (0, 0)),
                  pl.BlockSpec((1, hw), lambda b, h: (0, 0))],
        out_specs=pl.BlockSpec((lp, hw), lambda b, h: (b, h)),
        out_shape=jax.ShapeDtypeStruct((t, d), BF16),
        scratch_shapes=[pltpu.VMEM((blk, 1), F32), pltpu.VMEM((blk, LANES), F32), pltpu.VMEM((blk, hw), F32),
                        pltpu.VMEM((blk, 1), F32), pltpu.VMEM((blk, LANES), F32), pltpu.VMEM((blk, hw), F32),
                        pltpu.VMEM((2, blk, blk + LANES), F32), pltpu.VMEM((2, blk, blk + LANES), F32),
                        pltpu.VMEM((2, blk, blk), BF16), pltpu.VMEM((2, blk, 1), F32)],
        compiler_params=_params("parallel", "parallel"),
        name="diff_attention",
    )(qkv, qkv, qkv, lam_vecs.astype(F32), subln_g.reshape(1, hw).astype(F32))


def _col_panels(n):
    return [slice(u, u + MXU_COLS) for u in range(0, n, MXU_COLS)]


def _emit_norm_inputs(h, cols, hb_ref, ssq_ref):
    hb_ref[:, cols] = h.astype(hb_ref.dtype)
    ssq_ref[...] += jnp.sum(h * h, axis=-1, keepdims=True)


def _proj_residual_kernel(a_ref, w_ref, r_ref, o_ref, hb_ref, ssq_ref):
    @pl.when(pl.program_id(1) == 0)
    def _():
        ssq_ref[...] = jnp.zeros(ssq_ref.shape, F32)

    for cols in _col_panels(o_ref.shape[1]):
        h = r_ref[:, cols] + jnp.dot(a_ref[...], w_ref[:, cols], preferred_element_type=F32)
        o_ref[:, cols] = h
        _emit_norm_inputs(h, cols, hb_ref, ssq_ref)


def _norm_out_shapes(t, n):
    return (jax.ShapeDtypeStruct((t, n), F32), jax.ShapeDtypeStruct((t, n), BF16),
            jax.ShapeDtypeStruct((t, 1), F32))


def _proj_residual(a, w, res):
    t, k = a.shape
    n = w.shape[1]
    tm = _pick_tile(t, 1024, SEQ_ALIGN if t % SEQ_ALIGN == 0 else 16)
    tn = _pick_tile(n, 1024, LANES)
    return pl.pallas_call(
        _proj_residual_kernel,
        grid=(t // tm, n // tn),
        in_specs=[pl.BlockSpec((tm, k), lambda i, j: (i, 0)),
                  pl.BlockSpec((k, tn), lambda i, j: (0, j)),
                  pl.BlockSpec((tm, tn), lambda i, j: (i, j))],
        out_specs=(pl.BlockSpec((tm, tn), lambda i, j: (i, j)),
                   pl.BlockSpec((tm, tn), lambda i, j: (i, j)),
                   pl.BlockSpec((tm, 1), lambda i, j: (i, 0))),
        out_shape=_norm_out_shapes(t, n),
        compiler_params=_params("parallel", "arbitrary"),
        name="proj_residual",
    )(a, w, res)


def _mlp_up_kernel(*refs, inv_d, side):
    a_ref, w_ref, ssq_ref = refs[:3]
    side_in = refs[3:3 + side.n_in]
    o_ref = refs[3 + side.n_in]
    side_out = refs[4 + side.n_in:]
    r2 = 1.0 / (ssq_ref[...] * inv_d + EPS)
    side.run(side_in, side_out)
    for cols in _col_panels(o_ref.shape[1]):
        u = jnp.dot(a_ref[...], w_ref[:, cols], preferred_element_type=F32)
        o_ref[:, cols] = (jnp.square(jnp.maximum(u, 0.0)) * r2).astype(o_ref.dtype)


def _mlp_up(hb, w, ssq, cast_jobs):
    t, k = hb.shape
    n = w.shape[1]
    tm = _pick_tile(t, 1024, SEQ_ALIGN if t % SEQ_ALIGN == 0 else 16)
    tn = _pick_tile(n, 1024, LANES)
    nj = n // tn
    side = _SideCasts(cast_jobs, (t // tm) * nj, lambda i, j: i * nj + j)
    return pl.pallas_call(
        functools.partial(_mlp_up_kernel, inv_d=1.0 / k, side=side),
        grid=(t // tm, nj),
        in_specs=[pl.BlockSpec((tm, k), lambda i, j: (i, 0)),
                  pl.BlockSpec((k, tn), lambda i, j: (0, j)),
                  pl.BlockSpec((tm, 1), lambda i, j: (i, 0))] + side.in_specs,
        out_specs=[pl.BlockSpec((tm, tn), lambda i, j: (i, j))] + side.out_specs,
        out_shape=[jax.ShapeDtypeStruct((t, n), BF16)] + side.out_shapes,
        compiler_params=_params("arbitrary", "arbitrary"),
        name="mlp_up",
    )(hb, w, ssq, *side.operands)


def _mlp_down_kernel(a_ref, w_ref, r_ref, o_ref, *norm_refs, nk, col_axis=1):
    j = pl.program_id(col_axis)
    k = pl.program_id(col_axis + 1)

    def accumulate(base_ref, last):
        for cols in _col_panels(o_ref.shape[1]):
            h = base_ref[:, cols] + jnp.dot(a_ref[...], w_ref[:, cols], preferred_element_type=F32)
            o_ref[:, cols] = h
            if last and norm_refs:
                _emit_norm_inputs(h, cols, *norm_refs)

    if norm_refs:
        @pl.when(jnp.logical_and(j == 0, k == 0))
        def _():
            norm_refs[1][...] = jnp.zeros(norm_refs[1].shape, F32)

    if nk == 1:
        accumulate(r_ref, True)
        return

    @pl.when(k == 0)
    def _():
        accumulate(r_ref, False)

    @pl.when(jnp.logical_and(k > 0, k < nk - 1))
    def _():
        accumulate(o_ref, False)

    @pl.when(k == nk - 1)
    def _():
        accumulate(o_ref, True)


def _mlp_down(a, w, res, emit_norm):
    t, kdim = a.shape
    n = w.shape[1]
    tm = _pick_tile(t, 1024, SEQ_ALIGN if t % SEQ_ALIGN == 0 else 16)
    tn = _pick_tile(n, 1024, LANES)
    tk = _pick_tile(kdim, 4096, LANES)
    nk = kdim // tk
    tile = pl.BlockSpec((tm, tn), lambda i, j, k: (i, j))
    if emit_norm:
        out_specs = (tile, tile, pl.BlockSpec((tm, 1), lambda i, j, k: (i, 0)))
        out_shape = _norm_out_shapes(t, n)
    else:
        out_specs = tile
        out_shape = jax.ShapeDtypeStruct((t, n), F32)
    return pl.pallas_call(
        functools.partial(_mlp_down_kernel, nk=nk),
        grid=(t // tm, n // tn, nk),
        in_specs=[pl.BlockSpec((tm, tk), lambda i, j, k: (i, k)),
                  pl.BlockSpec((tk, tn), lambda i, j, k: (k, j)),
                  tile],
        out_specs=out_specs,
        out_shape=out_shape,
        compiler_params=_params("parallel", "arbitrary", "arbitrary"),
        name="mlp_down",
    )(a, w, res)


def _mlp_down_final(a, w, res, bsz, lp, row0, seq):
    kdim = a.shape[1]
    n = w.shape[1]
    tm = _pick_tile(seq, 1024, 16)
    nt = seq // tm
    tn = _pick_tile(n, 1024, LANES)
    tk = _pick_tile(kdim, 4096, LANES)
    nk = kdim // tk

    assert lp % 16 == 0 and row0 % 16 == 0 and tm % 16 == 0

    def row_start(b, m):
        return pl.multiple_of(b * lp + row0 + m * tm, 16)

    return pl.pallas_call(
        functools.partial(_mlp_down_kernel, nk=nk, col_axis=2),
        grid=(bsz, nt, n // tn, nk),
        in_specs=[pl.BlockSpec((pl.Element(tm), pl.Element(tk)), lambda b, m, j, k: (row_start(b, m), k * tk)),
                  pl.BlockSpec((tk, tn), lambda b, m, j, k: (k, j)),
                  pl.BlockSpec((pl.Element(tm), pl.Element(tn)), lambda b, m, j, k: (row_start(b, m), j * tn))],
        out_specs=pl.BlockSpec((tm, tn), lambda b, m, j, k: (b * nt + m, j)),
        out_shape=jax.ShapeDtypeStruct((bsz * seq, n), F32),
        compiler_params=_params("parallel", "parallel", "arbitrary", "arbitrary"),
        name="mlp_down_final",
    )(a, w, res)


def _glu_kernel(a_ref, wv_ref, wg_ref, bv_ref, bg_ref, r_ref, o_ref, hb_ref, ssq_ref, *, col_axis=1):
    @pl.when(pl.program_id(col_axis) == 0)
    def _():
        ssq_ref[...] = jnp.zeros(ssq_ref.shape, F32)

    for cols in _col_panels(o_ref.shape[1]):
        val = jnp.dot(a_ref[...], wv_ref[:, cols], preferred_element_type=F32) + bv_ref[:, cols]
        gate = jnp.dot(a_ref[...], wg_ref[:, cols], preferred_element_type=F32) + bg_ref[:, cols]
        h = r_ref[:, cols] + val * jax.nn.sigmoid(gate)
        o_ref[:, cols] = h
        _emit_norm_inputs(h, cols, hb_ref, ssq_ref)


def _glu_residual(a, w, bias, res):
    t, k = a.shape
    n = w.shape[1] // 2
    tm = _row_tile(t)
    tn = _pick_tile(n, 256, LANES)
    nb = n // tn
    b2 = bias.reshape(1, 2 * n).astype(F32)
    return pl.pallas_call(
        _glu_kernel,
        grid=(t // tm, nb),
        in_specs=[pl.BlockSpec((tm, k), lambda i, j: (i, 0)),
                  pl.BlockSpec((k, tn), lambda i, j: (0, j)),
                  pl.BlockSpec((k, tn), lambda i, j: (0, nb + j)),
                  pl.BlockSpec((1, tn), lambda i, j: (0, j)),
                  pl.BlockSpec((1, tn), lambda i, j: (0, nb + j)),
                  pl.BlockSpec((tm, tn), lambda i, j: (i, j))],
        out_specs=(pl.BlockSpec((tm, tn), lambda i, j: (i, j)),
                   pl.BlockSpec((tm, tn), lambda i, j: (i, j)),
                   pl.BlockSpec((tm, 1), lambda i, j: (i, 0))),
        out_shape=_norm_out_shapes(t, n),
        compiler_params=_params("parallel", "arbitrary"),
        name="glu_residual",
    )(a, w, w, b2, b2, res)


def _glu_residual_rows(a, w, bias, res, bsz, lp, row0, seq):
    k = a.shape[1]
    n = w.shape[1] // 2
    tm = _pick_tile(seq, 1024, 16)
    nt = seq // tm
    tn = _pick_tile(n, 512, LANES)
    nb = n // tn
    b2 = bias.reshape(1, 2 * n).astype(F32)
    assert lp % 16 == 0 and row0 % 16 == 0 and tm % 16 == 0

    def row_start(b, m):
        return pl.multiple_of(b * lp + row0 + m * tm, 16)

    tile = pl.BlockSpec((tm, tn), lambda b, m, j: (b * nt + m, j))
    return pl.pallas_call(
        functools.partial(_glu_kernel, col_axis=2),
        grid=(bsz, nt, nb),
        in_specs=[pl.BlockSpec((pl.Element(tm), pl.Element(k)), lambda b, m, j: (row_start(b, m), 0)),
                  pl.BlockSpec((k, tn), lambda b, m, j: (0, j)),
                  pl.BlockSpec((k, tn), lambda b, m, j: (0, nb + j)),
                  pl.BlockSpec((1, tn), lambda b, m, j: (0, j)),
                  pl.BlockSpec((1, tn), lambda b, m, j: (0, nb + j)),
                  pl.BlockSpec((pl.Element(tm), pl.Element(tn)), lambda b, m, j: (row_start(b, m), j * tn))],
        out_specs=(tile, tile, pl.BlockSpec((tm, 1), lambda b, m, j: (b * nt + m, 0))),
        out_shape=_norm_out_shapes(bsz * seq, n),
        compiler_params=_params("parallel", "parallel", "arbitrary"),
        name="glu_residual_rows",
    )(a, w, w, b2, b2, res)


def _s5_kernel(x_ref, ssq_ref, g_ref, w0_ref, cm_ref, pr_ref, pi_ref, prc_ref, pic_ref, d_ref, o_ref,
               m_scr, wst_scr, wout_scr, s_scr, xin_scr, *, n_steps, inv_d):
    q = SSM_CHUNK
    half = w0_ref.shape[2] // 2

    @pl.when(pl.program_id(1) == 0)
    def _build_weights():
        w0 = w0_ref[0]
        w0re, w0im = w0[:, :half], w0[:, half:]
        cm = cm_ref[0]
        cm_bf = cm.astype(BF16)
        row = lax.broadcasted_iota(jnp.int32, (LANES, LANES), 0)
        col = lax.broadcasted_iota(jnp.int32, (LANES, LANES), 1)
        skip = jnp.where(row == col, jnp.broadcast_to(d_ref[...], (LANES, LANES)), 0.0)
        zero_blk = jnp.zeros((LANES, LANES), BF16)
        for tau in range(q):
            ar = pr_ref[0, tau:tau + 1, :]
            ai = pi_ref[0, tau:tau + 1, :]
            w_tau = jnp.concatenate([w0re * ar - w0im * ai, w0im * ar + w0re * ai], axis=1).astype(BF16)
            t_st = q - 1 - tau
            wst_scr[t_st * LANES:(t_st + 1) * LANES, :] = w_tau
            k_tau = jnp.dot(w_tau, cm_bf, preferred_element_type=F32)
            if tau == 0:
                k_tau = k_tau + skip
            k_bf = k_tau.astype(BF16)
            for t in range(q - tau):
                m_scr[t * LANES:(t + 1) * LANES, (t + tau) * LANES:(t + tau + 1) * LANES] = k_bf
        for t in range(q):
            for t2 in range(t):
                m_scr[t * LANES:(t + 1) * LANES, t2 * LANES:(t2 + 1) * LANES] = zero_blk
        cre, cimn = cm[:half], cm[half:]
        for t in range(q):
            arc = prc_ref[0, :, t + 1:t + 2]
            aic = pic_ref[0, :, t + 1:t + 2]
            wout_scr[:half, t * LANES:(t + 1) * LANES] = (arc * cre + aic * cimn).astype(BF16)
            wout_scr[half:, t * LANES:(t + 1) * LANES] = (arc * cimn - aic * cre).astype(BF16)

    gain = g_ref[...]
    u = jnp.concatenate(
        [(x_ref[t].astype(F32) * lax.rsqrt(ssq_ref[t] * inv_d + EPS) * gain).astype(BF16) for t in range(q)],
        axis=1)
    s_scr[...] = jnp.dot(u, wst_scr[...], preferred_element_type=F32)

    shp = (SUBLANES, half)
    ar = jnp.broadcast_to(pr_ref[0, q:q + 1, :], shp)
    ai = jnp.broadcast_to(pi_ref[0, q:q + 1, :], shp)
    anr = jnp.broadcast_to(pr_ref[0, q + 1:q + 2, :], shp)
    ani = jnp.broadcast_to(pi_ref[0, q + 1:q + 2, :], shp)
    seg = lax.broadcasted_iota(jnp.int32, shp, 0)
    zeros = jnp.zeros(shp, F32)

    def shift_down(x):
        return jnp.where(seg == 0, 0.0, pltpu.roll(x, 1, 0))

    def advance(j, cr, ci):
        sj = s_scr[pl.ds(pl.multiple_of(j * SUBLANES, SUBLANES), SUBLANES), :]
        return ar * cr - ai * ci + sj[:, :half], ar * ci + ai * cr + sj[:, half:]

    er, ei = lax.fori_loop(0, n_steps, lambda j, c: advance(j, *c), (zeros, zeros))
    tr, ti = er, ei
    for _ in range(SSM_SEGMENTS - 1):
        sr, si = shift_down(tr), shift_down(ti)
        tr, ti = er + anr * sr - ani * si, ei + anr * si + ani * sr
    cr0, ci0 = shift_down(tr), shift_down(ti)

    def scan_store(j, c):
        cr, ci = c
        xin_scr[pl.ds(pl.multiple_of(j * SUBLANES, SUBLANES), SUBLANES), :] = jnp.concatenate([cr, ci], axis=1)
        return advance(j, cr, ci)

    lax.fori_loop(0, n_steps, scan_store, (cr0, ci0))

    xin = xin_scr[...].astype(BF16)
    steps_per_panel = 4
    for t0 in range(0, q, steps_per_panel):
        k_hi = (t0 + steps_per_panel) * LANES
        cols = slice(t0 * LANES, k_hi)
        y = (jnp.dot(u[:, :k_hi], m_scr[:k_hi, cols], preferred_element_type=F32)
             + jnp.dot(xin, wout_scr[:, cols], preferred_element_type=F32))
        z = jax.nn.gelu(y)
        for t in range(steps_per_panel):
            o_ref[t0 + t] = z[:, t * LANES:(t + 1) * LANES].astype(o_ref.dtype)


def _complex_pow(zr, zi, n):
    rr, ri = None, None
    br, bi = zr, zi
    while n:
        if n & 1:
            rr, ri = (br, bi) if rr is None else (rr * br - ri * bi, rr * bi + ri * br)
        n >>= 1
        if n:
            br, bi = br * br - bi * bi, 2.0 * br * bi
    return rr, ri


def _s5_tables(a_re, a_im, log_dt, b_re, b_im, c_re, c_im, n_steps):
    g, p = a_re.shape
    gpb = LANES // SSM_GROUP
    nblk = g // gpb
    dt = jnp.exp(log_dt.astype(F32))[:, None]
    ar = a_re.astype(F32)
    ai = a_im.astype(F32)
    mag = jnp.exp(dt * ar)
    ang = dt * ai
    abar_re = mag * jnp.cos(ang)
    abar_im = mag * jnp.sin(ang)
    nr = abar_re - 1.0
    ni = abar_im
    den = ar * ar + ai * ai
    f_re = (nr * ar + ni * ai) / den
    f_im = (ni * ar - nr * ai) / den
    br = b_re.astype(F32)
    bi = b_im.astype(F32)
    bb_re = f_re[..., None] * br - f_im[..., None] * bi
    bb_im = f_re[..., None] * bi + f_im[..., None] * br

    pw_r, pw_i = [jnp.ones_like(abar_re)], [jnp.zeros_like(abar_re)]
    for _ in range(SSM_CHUNK):
        pw_r.append(pw_r[-1] * abar_re - pw_i[-1] * abar_im)
        pw_i.append(pw_r[-2] * abar_im + pw_i[-1] * abar_re)
    seg_r, seg_i = _complex_pow(pw_r[-1], pw_i[-1], n_steps)
    pw_r.append(seg_r)
    pw_i.append(seg_i)
    n_rows = 24
    pr = jnp.stack(pw_r, 0).reshape(len(pw_r), nblk, gpb * p).transpose(1, 0, 2)
    pi = jnp.stack(pw_i, 0).reshape(len(pw_i), nblk, gpb * p).transpose(1, 0, 2)
    pad = ((0, 0), (0, n_rows - pr.shape[1]), (0, 0))
    pr = jnp.pad(pr, pad)
    pi = jnp.pad(pi, pad)
    prc = pr.transpose(0, 2, 1)
    pic = pi.transpose(0, 2, 1)

    eye = jnp.eye(gpb, dtype=F32)

    def in_to_state(bb):
        v = bb.reshape(nblk, gpb, p, SSM_GROUP).transpose(0, 1, 3, 2)
        e = v[:, :, :, None, :] * eye[None, :, None, :, None]
        return e.reshape(nblk, LANES, gpb * p)

    def state_to_out(c):
        v = c.astype(F32).reshape(nblk, gpb, SSM_GROUP, p).transpose(0, 1, 3, 2)
        e = v[:, :, :, None, :] * eye[None, :, None, :, None]
        return e.reshape(nblk, gpb * p, LANES)

    w0 = jnp.concatenate([in_to_state(bb_re), in_to_state(bb_im)], axis=2)
    cm = jnp.concatenate([state_to_out(c_re), -state_to_out(c_im)], axis=1)
    return w0, cm, pr, pi, prc, pic


def _s5_mixer(hb, ssq, gain, bsz, lp, a_re, a_im, log_dt, b_re, b_im, c_re, c_im, d_skip):
    t, d = hb.shape
    q, nseg = SSM_CHUNK, SSM_SEGMENTS
    assert lp % (q * nseg * 2) == 0
    n_steps = lp // (q * nseg)
    rows = n_steps * nseg

    def chunk_layout(v):
        w = v.shape[-1]
        return v.reshape(bsz, nseg, n_steps, q, w).transpose(3, 0, 2, 1, 4).reshape(q, bsz * rows, w)

    x = chunk_layout(hb)
    ssq_c = chunk_layout(ssq)

    w0, cm, pr, pi, prc, pic = _s5_tables(a_re, a_im, log_dt, b_re, b_im, c_re, c_im, n_steps)
    nblk = d // LANES
    n_state = w0.shape[2]
    kern = functools.partial(_s5_kernel, n_steps=n_steps, inv_d=1.0 / d)
    z = pl.pallas_call(
        kern,
        grid=(nblk, bsz),
        in_specs=[pl.BlockSpec((q, rows, LANES), lambda k, b: (0, b, k)),
                  pl.BlockSpec((q, rows, 1), lambda k, b: (0, b, 0)),
                  pl.BlockSpec((1, LANES), lambda k, b: (0, k)),
                  pl.BlockSpec((1, LANES, n_state), lambda k, b: (k, 0, 0)),
                  pl.BlockSpec((1, n_state, LANES), lambda k, b: (k, 0, 0)),
                  pl.BlockSpec((1,) + pr.shape[1:], lambda k, b: (k, 0, 0)),
                  pl.BlockSpec((1,) + pi.shape[1:], lambda k, b: (k, 0, 0)),
                  pl.BlockSpec((1,) + prc.shape[1:], lambda k, b: (k, 0, 0)),
                  pl.BlockSpec((1,) + pic.shape[1:], lambda k, b: (k, 0, 0)),
                  pl.BlockSpec((1, LANES), lambda k, b: (0, k))],
        out_specs=pl.BlockSpec((q, rows, LANES), lambda k, b: (0, b, k)),
        out_shape=jax.ShapeDtypeStruct((q, bsz * rows, d), BF16),
        scratch_shapes=[pltpu.VMEM((q * LANES, q * LANES), BF16),
                        pltpu.VMEM((q * LANES, n_state), BF16),
                        pltpu.VMEM((n_state, q * LANES), BF16),
                        pltpu.VMEM((rows, n_state), F32),
                        pltpu.VMEM((rows, n_state), F32)],
        compiler_params=_params("arbitrary", "arbitrary"),
        name="s5_mixer",
    )(x, ssq_c, gain.reshape(1, d).astype(F32), w0, cm, pr, pi, prc, pic, d_skip.reshape(1, d).astype(F32))
    return z.reshape(q, bsz, n_steps, nseg, d).transpose(1, 3, 2, 0, 4).reshape(t, d)


def _rope_tables(length):
    inv = ROPE_THETA ** (-jnp.arange(0, HEAD_DIM, 2, dtype=F32) / HEAD_DIM)
    ang = jnp.arange(length, dtype=F32)[:, None] * inv[None, :]
    cos, sin = jnp.cos(ang), jnp.sin(ang)
    return jnp.concatenate([cos, cos], axis=1), jnp.concatenate([-sin, sin], axis=1)


def _lambda_init(layer_idx):
    return 0.8 - 0.6 * math.exp(-0.3 * layer_idx)


def kernel(x, meta_tokens, norm_mix_g, norm_mlp_g, da_w_qkv, da_q_norm_g, da_k_norm_g, da_lambda, da_subln_g, da_w_o, ssm_a_re, ssm_a_im, ssm_log_dt, ssm_b_re, ssm_b_im, ssm_c_re, ssm_c_im, ssm_d, ssm_w_glu, ssm_b_glu, mlp_w_up, mlp_w_down):
    bsz, seq, d = x.shape
    depth = norm_mix_g.shape[0]
    n_mixers = 2
    length = N_META + seq
    lp = ((length + SEQ_ALIGN - 1) // SEQ_ALIGN) * SEQ_ALIGN
    assert meta_tokens.shape[0] == N_META and depth >= 1
    h, hn = _embed_norm(x, meta_tokens, norm_mix_g[0], lp)
    cos, sin = (jnp.tile(tab, (bsz, 1)) for tab in _rope_tables(lp))
    def up_job(i):
        return mlp_w_up, i, norm_mlp_g[i]

    def bf16_weight(name, stack, layer, gain=None):
        if name not in wb:
            src = stack[layer]
            wb[name] = (src if gain is None else src * gain.astype(F32)[:, None]).astype(BF16)
        return wb[name]

    wb = {}
    hb = ssq = None
    only_output_rows = False
    for i in range(depth):
        j = i // n_mixers
        nxt_s5 = i + 1 < depth and (i + 1) % n_mixers == 1
        if i % n_mixers == 0:
            if i > 0:
                hn = _rmsnorm(h, norm_mix_g[i])
            qkv, wb["o", j], wb["up", i] = _qkv_proj(
                hn, bf16_weight(("qkv", j), da_w_qkv, j), cos, sin, da_q_norm_g[j], da_k_norm_g[j],
                [(da_w_o, j, None), up_job(i)])
            att = _diff_attention(qkv, da_lambda[j], da_subln_g[j], bsz, lp, d, _lambda_init(i))
            h, hb, ssq = _proj_residual(att, wb["o", j], h)
        else:
            z = _s5_mixer(hb, ssq, norm_mix_g[i], bsz, lp, ssm_a_re[j], ssm_a_im[j], ssm_log_dt[j],
                          ssm_b_re[j], ssm_b_im[j], ssm_c_re[j], ssm_c_im[j], ssm_d[j])
            w_glu = bf16_weight(("glu", j), ssm_w_glu, j)
            if i == depth - 1:
                h, hb, ssq = _glu_residual_rows(z, w_glu, ssm_b_glu[j], h, bsz, lp, N_META, seq)
                only_output_rows = True
            else:
                h, hb, ssq = _glu_residual(z, w_glu, ssm_b_glu[j], h)
        jobs =[(("down", i), mlp_w_down, i, None)]
        if nxt_s5:
            jobs += [(("glu", (i + 1) // n_mixers), ssm_w_glu, (i + 1) // n_mixers, None),
                     (("up", i + 1),) + up_job(i + 1)]
        f, *casts = _mlp_up(hb, bf16_weight(("up", i), *up_job(i)), ssq, [job[1:] for job in jobs])
        wb.update({job[0]: c for job, c in zip(jobs, casts)})
        if i == depth - 1:
            if only_output_rows:
                return _mlp_down(f, wb["down", i], h, False).reshape(bsz, seq, d)
            return _mlp_down_final(f, wb["down", i], h, bsz, lp, N_META, seq).reshape(bsz, seq, d)
        if nxt_s5:
            h, hb, ssq = _mlp_down(f, wb["down", i], h, True)
        else:
            h = _mlp_down(f, wb["down", i], h, False)
```

```python
import functools
import math

import jax
import jax.numpy as jnp
from jax import lax
from jax.experimental import pallas as pl
from jax.experimental.pallas import tpu as pltpu

N_META = 16
SEQ_ALIGN = 256
HEAD_DIM = 128
LOG2E = 1.4426950408889634
ATTN_ROW_BLOCK = 64
ROPE_THETA = 10000.0
SSM_GROUP = 16
SSM_CHUNK = 16
SSM_SEGMENTS = 8
MXU_COLS = 256
LANES = 128
SUBLANES = 8
EPS = 1e-6
VMEM_LIMIT_BYTES = 56 * 1024 * 1024

F32 = jnp.float32
BF16 = jnp.bfloat16


def _pick_tile(n, target, mult):
    best = None
    for t in range(mult, min(n, target) + 1, mult):
        if n % t == 0:
            best = t
    assert best is not None, (n, target, mult)
    return best


def _row_tile(t):
    return _pick_tile(t, 1536, SEQ_ALIGN if t % SEQ_ALIGN == 0 else 16)


def _params(*sem):
    return pltpu.CompilerParams(dimension_semantics=sem, vmem_limit_bytes=VMEM_LIMIT_BYTES)


class _SideCasts:
    def __init__(self, jobs, n_steps, step_of):
        self.operands, self.in_specs, self.out_specs, self.out_shapes, self.has_gain = [], [], [], [], []
        for src, layer, gain in jobs:
            _, r, c = src.shape
            cr = next(x for x in (16 << p for p in range(24)) if r % x == 0 and r // x <= n_steps)

            def idx(*g, last=r // cr - 1):
                return (jnp.minimum(step_of(*g), last), 0)

            def src_idx(*g, layer=layer, last=r // cr - 1):
                return (layer, jnp.minimum(step_of(*g), last), 0)

            self.operands.append(src)
            self.in_specs.append(pl.BlockSpec((None, cr, c), src_idx))
            if gain is not None:
                self.operands.append(gain.reshape(r, 1).astype(F32))
                self.in_specs.append(pl.BlockSpec((cr, 1), idx))
            self.out_specs.append(pl.BlockSpec((cr, c), idx))
            self.out_shapes.append(jax.ShapeDtypeStruct((r, c), BF16))
            self.has_gain.append(gain is not None)
        self.n_in = len(self.operands)
        self.n_out = len(self.out_shapes)

    def run(self, in_refs, out_refs):
        refs = iter(in_refs)
        for has_gain, o_ref in zip(self.has_gain, out_refs):
            v = next(refs)[...]
            if has_gain:
                v = v * next(refs)[...]
            o_ref[...] = v.astype(o_ref.dtype)


def _rmsnorm_kernel(x_ref, g_ref, o_ref):
    x = x_ref[...]
    ms = jnp.mean(x * x, axis=-1, keepdims=True)
    o_ref[...] = (x * lax.rsqrt(ms + EPS) * g_ref[...]).astype(o_ref.dtype)


def _rmsnorm(x, g):
    t, d = x.shape
    tr = _pick_tile(t, 256, 16)
    return pl.pallas_call(
        _rmsnorm_kernel,
        grid=(t // tr,),
        in_specs=[pl.BlockSpec((tr, d), lambda i: (i, 0)),
                  pl.BlockSpec((1, d), lambda i: (0, 0))],
        out_specs=pl.BlockSpec((tr, d), lambda i: (i, 0)),
        out_shape=jax.ShapeDtypeStruct((t, d), BF16),
        compiler_params=_params("parallel"),
        name="rmsnorm",
    )(x, g.reshape(1, d).astype(F32))


def _embed_norm_kernel(x_ref, meta_ref, g_ref, h_ref, hn_ref, *, n_meta, seq):
    tr = h_ref.shape[0]
    i = pl.program_id(1)

    @pl.when(i == 0)
    def _():
        h_ref[:n_meta, :] = meta_ref[...]
        h_ref[n_meta:, :] = x_ref[:tr - n_meta, :]

    @pl.when(i > 0)
    def _():
        row = i * tr + lax.broadcasted_iota(jnp.int32, h_ref.shape, 0)
        h_ref[...] = jnp.where(row < n_meta + seq, x_ref[...], 0.0)

    h = h_ref[...]
    ms = jnp.mean(h * h, axis=-1, keepdims=True)
    hn_ref[...] = (h * lax.rsqrt(ms + EPS) * g_ref[...]).astype(hn_ref.dtype)


def _embed_norm(x, meta_tokens, g, lp):
    bsz, seq, d = x.shape
    n_meta = meta_tokens.shape[0]
    tr = _pick_tile(lp, 256, 16)
    assert n_meta % 16 == 0 and n_meta < tr and tr < seq
    nt = lp // tr
    tile = pl.BlockSpec((tr, d), lambda b, i: (b * nt + i, 0))

    def x_window(b, i):
        return (b, pl.multiple_of(jnp.maximum(i * tr - n_meta, 0), 16), 0)

    return pl.pallas_call(
        functools.partial(_embed_norm_kernel, n_meta=n_meta, seq=seq),
        grid=(bsz, nt),
        in_specs=[pl.BlockSpec((pl.Squeezed(), pl.Element(tr, (0, lp - n_meta - seq)), pl.Element(d)), x_window),
                  pl.BlockSpec((n_meta, d), lambda b, i: (0, 0)),
                  pl.BlockSpec((1, d), lambda b, i: (0, 0))],
        out_specs=(tile, tile),
        out_shape=(jax.ShapeDtypeStruct((bsz * lp, d), F32), jax.ShapeDtypeStruct((bsz * lp, d), BF16)),
        compiler_params=_params("parallel", "arbitrary"),
        name="embed_norm",
    )(x, meta_tokens.astype(F32), g.reshape(1, d).astype(F32))


def _qkv_kernel(*refs, n_q_blocks, q_scale, side):
    a_ref, w_ref, cos_ref, sin_ref, qg_ref, kg_ref = refs[:6]
    side_in = refs[6:6 + side.n_in]
    o_ref = refs[6 + side.n_in]
    side_out = refs[7 + side.n_in:]
    acc = jnp.dot(a_ref[...], w_ref[...], preferred_element_type=F32)
    j = pl.program_id(1)
    tn = acc.shape[1]

    def norm_rope(g, scale):
        cos = cos_ref[...]
        sin = sin_ref[...]
        for u in range(tn // HEAD_DIM):
            x = acc[:, u * HEAD_DIM:(u + 1) * HEAD_DIM]
            ms = jnp.mean(x * x, axis=-1, keepdims=True)
            y = x * lax.rsqrt(ms + EPS) * g
            y = y * cos + pltpu.roll(y, HEAD_DIM // 2, 1) * sin
            if scale != 1.0:
                y = y * scale
            o_ref[:, u * HEAD_DIM:(u + 1) * HEAD_DIM] = y.astype(o_ref.dtype)

    @pl.when(j < n_q_blocks)
    def _():
        norm_rope(qg_ref[...], q_scale)
        side.run(side_in, side_out)

    @pl.when(jnp.logical_and(j >= n_q_blocks, j < 2 * n_q_blocks))
    def _():
        norm_rope(kg_ref[...], 1.0)
        side.run(side_in, side_out)

    @pl.when(j >= 2 * n_q_blocks)
    def _():
        o_ref[...] = acc.astype(o_ref.dtype)
        side.run(side_in, side_out)


def _qkv_proj(a, w, cos, sin, q_g, k_g, cast_jobs):
    t, d = a.shape
    n = w.shape[1]
    tm = _row_tile(t)
    tn = _pick_tile(d, 512, HEAD_DIM)
    nj = n // tn
    side = _SideCasts(cast_jobs, (t // tm) * nj, lambda i, j: i * nj + j)
    kern = functools.partial(_qkv_kernel, n_q_blocks=d // tn, q_scale=HEAD_DIM ** -0.5 * LOG2E, side=side)
    return pl.pallas_call(
        kern,
        grid=(t // tm, nj),
        in_specs=[pl.BlockSpec((tm, d), lambda i, j: (i, 0)),
                  pl.BlockSpec((d, tn), lambda i, j: (0, j)),
                  pl.BlockSpec((tm, HEAD_DIM), lambda i, j: (i, 0)),
                  pl.BlockSpec((tm, HEAD_DIM), lambda i, j: (i, 0)),
                  pl.BlockSpec((1, HEAD_DIM), lambda i, j: (0, 0)),
                  pl.BlockSpec((1, HEAD_DIM), lambda i, j: (0, 0))] + side.in_specs,
        out_specs=[pl.BlockSpec((tm, tn), lambda i, j: (i, j))] + side.out_specs,
        out_shape=[jax.ShapeDtypeStruct((t, n), BF16)] + side.out_shapes,
        compiler_params=_params("arbitrary", "arbitrary"),
        name="qkv_proj",
    )(a, w, cos, sin, q_g.reshape(1, HEAD_DIM).astype(F32), k_g.reshape(1, HEAD_DIM).astype(F32),
      *side.operands)


def _attn_kernel(q_ref, k_ref, v_ref, lam_ref, g_ref, o_ref,
                 m1_ref, l1_ref, a1_ref, m2_ref, l2_ref, a2_ref, s_even_ref, s_odd_ref, p_ref, alpha_ref,
                 *, blk, lam_init):
    s_bufs = (s_even_ref, s_odd_ref)
    nq = q_ref.shape[0] // blk
    stats = ((m1_ref, l1_ref, a1_ref), (m2_ref, l2_ref, a2_ref))
    lv = lam_ref[...]
    lam = (jnp.exp(jnp.sum(lv[0:1] * lv[1:2], axis=-1, keepdims=True))
           - jnp.exp(jnp.sum(lv[2:3] * lv[3:4], axis=-1, keepdims=True)) + lam_init)

    def rows(i):
        return pl.ds(pl.multiple_of(i * blk, blk), blk)

    def init_stats():
        for m_ref, l_ref, a_ref in stats:
            m_ref[...] = jnp.full(m_ref.shape, -jnp.inf, F32)
            l_ref[...] = jnp.zeros(l_ref.shape, F32)
            a_ref[...] = jnp.zeros(a_ref.shape, F32)

    def scores(qi, j, diag, s_ref):
        q = q_ref[rows(qi), :]
        kk = k_ref[rows(j), :]
        for c in range(2):
            heads = slice(c * HEAD_DIM, (c + 1) * HEAD_DIM)
            for r0 in (range(0, blk, MXU_COLS) if diag else (0,)):
                cols = slice(r0, r0 + MXU_COLS) if diag else slice(0, blk)
                s_ref[c, r0:, cols] = lax.dot_general(
                    q[r0:, heads], kk[cols, heads], (((1,), (1,)), ((), ())), preferred_element_type=F32)

    def softmax_pv(j, s_ref, masked):
        vv = v_ref[rows(j), :]
        for c, (m_ref, l_ref, a_ref) in enumerate(stats):
            for r in range(0, blk, ATTN_ROW_BLOCK):
                rb = slice(r, r + ATTN_ROW_BLOCK)
                nc = MXU_COLS * (r // MXU_COLS + 1) if masked else blk
                s = s_ref[c, rb, :nc]
                if masked:
                    row = r + lax.broadcasted_iota(jnp.int32, s.shape, 0)
                    col = lax.broadcasted_iota(jnp.int32, s.shape, 1)
                    s = jnp.where(col <= row, s, -jnp.inf)
                m_old = m_ref[rb, :]
                m_new = jnp.maximum(m_old, jnp.max(s, axis=-1, keepdims=True))
                alpha = jnp.exp2(m_old - m_new)
                p = jnp.exp2(s - m_new)
                part = p[:, :LANES]
                for u in range(1, nc // LANES):
                    part = part + p[:, u * LANES:(u + 1) * LANES]
                l_ref[rb, :] = alpha * l_ref[rb, :] + part
                p_ref[c, rb, :nc] = p.astype(BF16)
                alpha_ref[c, rb, :] = alpha
                m_ref[rb, :] = m_new
            if masked:
                w = MXU_COLS
                parts = [jnp.dot(p_ref[c, r0:, r0:r0 + w], vv[r0:r0 + w], preferred_element_type=F32)
                         for r0 in range(0, blk, w)]
                for t in range(blk // w):
                    pv = parts[0][t * w:(t + 1) * w]
                    for u in range(1, t + 1):
                        pv = pv + parts[u][(t - u) * w:(t - u + 1) * w]
                    rows_t = slice(t * w, (t + 1) * w)
                    a_ref[rows_t, :] = alpha_ref[c, rows_t, :] * a_ref[rows_t, :] + pv
            else:
                a_ref[...] = alpha_ref[c] * a_ref[...] + jnp.dot(p_ref[c], vv, preferred_element_type=F32)

    def finalize(qi):
        l1 = jnp.sum(l1_ref[...], axis=-1, keepdims=True)
        l2 = jnp.sum(l2_ref[...], axis=-1, keepdims=True)
        o = a1_ref[...] / l1 - lam * (a2_ref[...] / l2)
        ms = jnp.mean(o * o, axis=-1, keepdims=True)
        y = o * lax.rsqrt(ms + EPS) * g_ref[...] * (1.0 - lam_init)
        o_ref[rows(qi), :] = y.astype(o_ref.dtype)

    def items(n, *work):
        for parity in range(2):
            @pl.when(n % 2 == parity)
            def _():
                for step, (next_qi, next_j, next_diag, j, masked) in enumerate(work):
                    mine = (parity + step) % 2
                    scores(next_qi, next_j, next_diag, s_bufs[1 - mine])
                    softmax_pv(j, s_bufs[mine], masked)

    init_stats()
    scores(0, 0, True, s_bufs[0])

    def q_block(qi, carry):
        base = (qi * (qi + 1)) // 2

        def full_chunk(j, c):
            items(base + j, (qi, j + 1, False, j, False))
            return c

        lax.fori_loop(0, jnp.maximum(qi - 1, 0), full_chunk, 0)

        @pl.when(qi >= 1)
        def _():
            items(base + qi - 1, (qi, qi, True, qi - 1, False))

        items(base + qi, (jnp.minimum(qi + 1, nq - 1), 0, False, qi, True))
        finalize(qi)
        for m_ref, _, _ in stats:
            m_ref[...] = jnp.full(m_ref.shape, -jnp.inf, F32)
        return carry

    lax.fori_loop(0, nq, q_block, 0)


def _diff_attention(qkv, lam_vecs, subln_g, bsz, lp, d, lam_init):
    t = qkv.shape[0]
    n_heads = d // (2 * HEAD_DIM)
    hw = 2 * HEAD_DIM
    blk = _pick_tile(lp, 768, SEQ_ALIGN)
    kern = functools.partial(_attn_kernel, blk=blk, lam_init=lam_init)
    return pl.pallas_call(
        kern,
        grid=(bsz, n_heads),
        in_specs=[pl.BlockSpec((lp, hw), lambda b, h: (b, h)),
                  pl.BlockSpec((lp, hw), lambda b, h: (b, n_heads + h)),
                  pl.BlockSpec((lp, hw), lambda b, h: (b, 2 * n_heads + h)),
                  pl.BlockSpec((4, HEAD_DIM), lambda b, h: (0, 0)),
                  pl.BlockSpec((1, hw), lambda b, h: (0, 0))],
        out_specs=pl.BlockSpec((lp, hw), lambda b, h: (b, h)),
        out_shape=jax.ShapeDtypeStruct((t, d), BF16),
        scratch_shapes=[pltpu.VMEM((blk, 1), F32), pltpu.VMEM((blk, LANES), F32), pltpu.VMEM((blk, hw), F32),
                        pltpu.VMEM((blk, 1), F32), pltpu.VMEM((blk, LANES), F32), pltpu.VMEM((blk, hw), F32),
                        pltpu.VMEM((2, blk, blk + LANES), F32), pltpu.VMEM((2, blk, blk + LANES), F32),
                        pltpu.VMEM((2, blk, blk), BF16), pltpu.VMEM((2, blk, 1), F32)],
        compiler_params=_params("parallel", "parallel"),
        name="diff_attention",
    )(qkv, qkv, qkv, lam_vecs.astype(F32), subln_g.reshape(1, hw).astype(F32))


def _col_panels(n):
    return [slice(u, u + MXU_COLS) for u in range(0, n, MXU_COLS)]


def _emit_norm_inputs(h, cols, hb_ref, ssq_ref):
    hb_ref[:, cols] = h.astype(hb_ref.dtype)
    ssq_ref[...] += jnp.sum(h * h, axis=-1, keepdims=True)


def _proj_residual_kernel(a_ref, w_ref, r_ref, o_ref, hb_ref, ssq_ref):
    @pl.when(pl.program_id(1) == 0)
    def _():
        ssq_ref[...] = jnp.zeros(ssq_ref.shape, F32)

    for cols in _col_panels(o_ref.shape[1]):
        h = r_ref[:, cols] + jnp.dot(a_ref[...], w_ref[:, cols], preferred_element_type=F32)
        o_ref[:, cols] = h
        _emit_norm_inputs(h, cols, hb_ref, ssq_ref)


def _norm_out_shapes(t, n):
    return (jax.ShapeDtypeStruct((t, n), F32), jax.ShapeDtypeStruct((t, n), BF16),
            jax.ShapeDtypeStruct((t, 1), F32))


def _proj_residual(a, w, res):
    t, k = a.shape
    n = w.shape[1]
    tm = _pick_tile(t, 1024, SEQ_ALIGN if t % SEQ_ALIGN == 0 else 16)
    tn = _pick_tile(n, 1024, LANES)
    return pl.pallas_call(
        _proj_residual_kernel,
        grid=(t // tm, n // tn),
        in_specs=[pl.BlockSpec((tm, k), lambda i, j: (i, 0)),
                  pl.BlockSpec((k, tn), lambda i, j: (0, j)),
                  pl.BlockSpec((tm, tn), lambda i, j: (i, j))],
        out_specs=(pl.BlockSpec((tm, tn), lambda i, j: (i, j)),
                   pl.BlockSpec((tm, tn), lambda i, j: (i, j)),
                   pl.BlockSpec((tm, 1), lambda i, j: (i, 0))),
        out_shape=_norm_out_shapes(t, n),
        compiler_params=_params("parallel", "arbitrary"),
        name="proj_residual",
    )(a, w, res)


def _mlp_up_kernel(*refs, inv_d, side):
    a_ref, w_ref, ssq_ref = refs[:3]
    side_in = refs[3:3 + side.n_in]
    o_ref = refs[3 + side.n_in]
    side_out = refs[4 + side.n_in:]
    r2 = 1.0 / (ssq_ref[...] * inv_d + EPS)
    side.run(side_in, side_out)
    for cols in _col_panels(o_ref.shape[1]):
        u = jnp.dot(a_ref[...], w_ref[:, cols], preferred_element_type=F32)
        o_ref[:, cols] = (jnp.square(jnp.maximum(u, 0.0)) * r2).astype(o_ref.dtype)


def _mlp_up(hb, w, ssq, cast_jobs):
    t, k = hb.shape
    n = w.shape[1]
    tm = _pick_tile(t, 1056, 16)
    tn = _pick_tile(n, 1024, LANES)
    nj = n // tn
    side = _SideCasts(cast_jobs, (t // tm) * nj, lambda i, j: i * nj + j)
    return pl.pallas_call(
        functools.partial(_mlp_up_kernel, inv_d=1.0 / k, side=side),
        grid=(t // tm, nj),
        in_specs=[pl.BlockSpec((tm, k), lambda i, j: (i, 0)),
                  pl.BlockSpec((k, tn), lambda i, j: (0, j)),
                  pl.BlockSpec((tm, 1), lambda i, j: (i, 0))] + side.in_specs,
        out_specs=[pl.BlockSpec((tm, tn), lambda i, j: (i, j))] + side.out_specs,
        out_shape=[jax.ShapeDtypeStruct((t, n), BF16)] + side.out_shapes,
        compiler_params=_params("arbitrary", "arbitrary"),
        name="mlp_up",
    )(hb, w, ssq, *side.operands)


def _mlp_down_kernel(a_ref, w_ref, r_ref, o_ref, *norm_refs, nk, col_axis=1):
    j = pl.program_id(col_axis)
    k = pl.program_id(col_axis + 1)

    def accumulate(base_ref, last):
        for cols in _col_panels(o_ref.shape[1]):
            h = base_ref[:, cols] + jnp.dot(a_ref[...], w_ref[:, cols], preferred_element_type=F32)
            o_ref[:, cols] = h
            if last and norm_refs:
                _emit_norm_inputs(h, cols, *norm_refs)

    if norm_refs:
        @pl.when(jnp.logical_and(j == 0, k == 0))
        def _():
            norm_refs[1][...] = jnp.zeros(norm_refs[1].shape, F32)

    if nk == 1:
        accumulate(r_ref, True)
        return

    @pl.when(k == 0)
    def _():
        accumulate(r_ref, False)

    @pl.when(jnp.logical_and(k > 0, k < nk - 1))
    def _():
        accumulate(o_ref, False)

    @pl.when(k == nk - 1)
    def _():
        accumulate(o_ref, True)


def _mlp_down(a, w, res, emit_norm):
    t, kdim = a.shape
    n = w.shape[1]
    tm = _pick_tile(t, 1024, SEQ_ALIGN if t % SEQ_ALIGN == 0 else 16)
    tn = _pick_tile(n, 1024, LANES)
    tk = _pick_tile(kdim, 4096, LANES)
    nk = kdim // tk
    tile = pl.BlockSpec((tm, tn), lambda i, j, k: (i, j))
    if emit_norm:
        out_specs = (tile, tile, pl.BlockSpec((tm, 1), lambda i, j, k: (i, 0)))
        out_shape = _norm_out_shapes(t, n)
    else:
        out_specs = tile
        out_shape = jax.ShapeDtypeStruct((t, n), F32)
    return pl.pallas_call(
        functools.partial(_mlp_down_kernel, nk=nk),
        grid=(t // tm, n // tn, nk),
        in_specs=[pl.BlockSpec((tm, tk), lambda i, j, k: (i, k)),
                  pl.BlockSpec((tk, tn), lambda i, j, k: (k, j)),
                  tile],
        out_specs=out_specs,
        out_shape=out_shape,
        compiler_params=_params("parallel", "arbitrary", "arbitrary"),
        name="mlp_down",
    )(a, w, res)


def _mlp_down_final(a, w, res, bsz, lp, row0, seq):
    kdim = a.shape[1]
    n = w.shape[1]
    tm = _pick_tile(seq, 1024, 16)
    nt = seq // tm
    tn = _pick_tile(n, 1024, LANES)
    tk = _pick_tile(kdim, 4096, LANES)
    nk = kdim // tk

    assert lp % 16 == 0 and row0 % 16 == 0 and tm % 16 == 0

    def row_start(b, m):
        return pl.multiple_of(b * lp + row0 + m * tm, 16)

    return pl.pallas_call(
        functools.partial(_mlp_down_kernel, nk=nk, col_axis=2),
        grid=(bsz, nt, n // tn, nk),
        in_specs=[pl.BlockSpec((pl.Element(tm), pl.Element(tk)), lambda b, m, j, k: (row_start(b, m), k * tk)),
                  pl.BlockSpec((tk, tn), lambda b, m, j, k: (k, j)),
                  pl.BlockSpec((pl.Element(tm), pl.Element(tn)), lambda b, m, j, k: (row_start(b, m), j * tn))],
        out_specs=pl.BlockSpec((tm, tn), lambda b, m, j, k: (b * nt + m, j)),
        out_shape=jax.ShapeDtypeStruct((bsz * seq, n), F32),
        compiler_params=_params("parallel", "parallel", "arbitrary", "arbitrary"),
        name="mlp_down_final",
    )(a, w, res)


def _glu_kernel(a_ref, wv_ref, wg_ref, bv_ref, bg_ref, r_ref, o_ref, hb_ref, ssq_ref, *, col_axis=1):
    @pl.when(pl.program_id(col_axis) == 0)
    def _():
        ssq_ref[...] = jnp.zeros(ssq_ref.shape, F32)

    for cols in _col_panels(o_ref.shape[1]):
        val = jnp.dot(a_ref[...], wv_ref[:, cols], preferred_element_type=F32) + bv_ref[:, cols]
        gate = jnp.dot(a_ref[...], wg_ref[:, cols], preferred_element_type=F32) + bg_ref[:, cols]
        h = r_ref[:, cols] + val * jax.nn.sigmoid(gate)
        o_ref[:, cols] = h
        _emit_norm_inputs(h, cols, hb_ref, ssq_ref)


def _glu_residual(a, w, bias, res):
    t, k = a.shape
    n = w.shape[1] // 2
    tm = _row_tile(t)
    tn = _pick_tile(n, 256, LANES)
    nb = n // tn
    b2 = bias.reshape(1, 2 * n).astype(F32)
    return pl.pallas_call(
        _glu_kernel,
        grid=(t // tm, nb),
        in_specs=[pl.BlockSpec((tm, k), lambda i, j: (i, 0)),
                  pl.BlockSpec((k, tn), lambda i, j: (0, j)),
                  pl.BlockSpec((k, tn), lambda i, j: (0, nb + j)),
                  pl.BlockSpec((1, tn), lambda i, j: (0, j)),
                  pl.BlockSpec((1, tn), lambda i, j: (0, nb + j)),
                  pl.BlockSpec((tm, tn), lambda i, j: (i, j))],
        out_specs=(pl.BlockSpec((tm, tn), lambda i, j: (i, j)),
                   pl.BlockSpec((tm, tn), lambda i, j: (i, j)),
                   pl.BlockSpec((tm, 1), lambda i, j: (i, 0))),
        out_shape=_norm_out_shapes(t, n),
        compiler_params=_params("parallel", "arbitrary"),
        name="glu_residual",
    )(a, w, w, b2, b2, res)


def _glu_residual_rows(a, w, bias, res, bsz, lp, row0, seq):
    k = a.shape[1]
    n = w.shape[1] // 2
    tm = _pick_tile(seq, 1024, 16)
    nt = seq // tm
    tn = _pick_tile(n, 512, LANES)
    nb = n // tn
    b2 = bias.reshape(1, 2 * n).astype(F32)
    assert lp % 16 == 0 and row0 % 16 == 0 and tm % 16 == 0

    def row_start(b, m):
        return pl.multiple_of(b * lp + row0 + m * tm, 16)

    tile = pl.BlockSpec((tm, tn), lambda b, m, j: (b * nt + m, j))
    return pl.pallas_call(
        functools.partial(_glu_kernel, col_axis=2),
        grid=(bsz, nt, nb),
        in_specs=[pl.BlockSpec((pl.Element(tm), pl.Element(k)), lambda b, m, j: (row_start(b, m), 0)),
                  pl.BlockSpec((k, tn), lambda b, m, j: (0, j)),
                  pl.BlockSpec((k, tn), lambda b, m, j: (0, nb + j)),
                  pl.BlockSpec((1, tn), lambda b, m, j: (0, j)),
                  pl.BlockSpec((1, tn), lambda b, m, j: (0, nb + j)),
                  pl.BlockSpec((pl.Element(tm), pl.Element(tn)), lambda b, m, j: (row_start(b, m), j * tn))],
        out_specs=(tile, tile, pl.BlockSpec((tm, 1), lambda b, m, j: (b * nt + m, 0))),
        out_shape=_norm_out_shapes(bsz * seq, n),
        compiler_params=_params("parallel", "parallel", "arbitrary"),
        name="glu_residual_rows",
    )(a, w, w, b2, b2, res)


def _s5_kernel(x_ref, ssq_ref, g_ref, w0_ref, cm_ref, pr_ref, pi_ref, prc_ref, pic_ref, d_ref, o_ref,
               m_scr, wst_scr, wout_scr, s_scr, xin_scr, *, n_steps, inv_d):
    q = SSM_CHUNK
    half = w0_ref.shape[2] // 2

    @pl.when(pl.program_id(1) == 0)
    def _build_weights():
        w0 = w0_ref[0]
        w0re, w0im = w0[:, :half], w0[:, half:]
        cm = cm_ref[0]
        cm_bf = cm.astype(BF16)
        row = lax.broadcasted_iota(jnp.int32, (LANES, LANES), 0)
        col = lax.broadcasted_iota(jnp.int32, (LANES, LANES), 1)
        skip = jnp.where(row == col, jnp.broadcast_to(d_ref[...], (LANES, LANES)), 0.0)
        zero_blk = jnp.zeros((LANES, LANES), BF16)
        for tau in range(q):
            ar = pr_ref[0, tau:tau + 1, :]
            ai = pi_ref[0, tau:tau + 1, :]
            w_tau = jnp.concatenate([w0re * ar - w0im * ai, w0im * ar + w0re * ai], axis=1).astype(BF16)
            t_st = q - 1 - tau
            wst_scr[t_st * LANES:(t_st + 1) * LANES, :] = w_tau
            k_tau = jnp.dot(w_tau, cm_bf, preferred_element_type=F32)
            if tau == 0:
                k_tau = k_tau + skip
            k_bf = k_tau.astype(BF16)
            for t in range(q - tau):
                m_scr[t * LANES:(t + 1) * LANES, (t + tau) * LANES:(t + tau + 1) * LANES] = k_bf
        for t in range(q):
            for t2 in range(t):
                m_scr[t * LANES:(t + 1) * LANES, t2 * LANES:(t2 + 1) * LANES] = zero_blk
        cre, cimn = cm[:half], cm[half:]
        for t in range(q):
            arc = prc_ref[0, :, t + 1:t + 2]
            aic = pic_ref[0, :, t + 1:t + 2]
            wout_scr[:half, t * LANES:(t + 1) * LANES] = (arc * cre + aic * cimn).astype(BF16)
            wout_scr[half:, t * LANES:(t + 1) * LANES] = (arc * cimn - aic * cre).astype(BF16)

    gain = g_ref[...]
    u = jnp.concatenate(
        [(x_ref[t].astype(F32) * lax.rsqrt(ssq_ref[t] * inv_d + EPS) * gain).astype(BF16) for t in range(q)],
        axis=1)
    s_scr[...] = jnp.dot(u, wst_scr[...], preferred_element_type=F32)

    shp = (SUBLANES, half)
    ar = jnp.broadcast_to(pr_ref[0, q:q + 1, :], shp)
    ai = jnp.broadcast_to(pi_ref[0, q:q + 1, :], shp)
    anr = jnp.broadcast_to(pr_ref[0, q + 1:q + 2, :], shp)
    ani = jnp.broadcast_to(pi_ref[0, q + 1:q + 2, :], shp)
    seg = lax.broadcasted_iota(jnp.int32, shp, 0)
    zeros = jnp.zeros(shp, F32)

    def shift_down(x):
        return jnp.where(seg == 0, 0.0, pltpu.roll(x, 1, 0))

    def advance(j, cr, ci):
        sj = s_scr[pl.ds(pl.multiple_of(j * SUBLANES, SUBLANES), SUBLANES), :]
        return ar * cr - ai * ci + sj[:, :half], ar * ci + ai * cr + sj[:, half:]

    er, ei = lax.fori_loop(0, n_steps, lambda j, c: advance(j, *c), (zeros, zeros))
    tr, ti = er, ei
    for _ in range(SSM_SEGMENTS - 1):
        sr, si = shift_down(tr), shift_down(ti)
        tr, ti = er + anr * sr - ani * si, ei + anr * si + ani * sr
    cr0, ci0 = shift_down(tr), shift_down(ti)

    def scan_store(j, c):
        cr, ci = c
        xin_scr[pl.ds(pl.multiple_of(j * SUBLANES, SUBLANES), SUBLANES), :] = jnp.concatenate([cr, ci], axis=1)
        return advance(j, cr, ci)

    lax.fori_loop(0, n_steps, scan_store, (cr0, ci0))

    xin = xin_scr[...].astype(BF16)
    steps_per_panel = 4
    for t0 in range(0, q, steps_per_panel):
        k_hi = (t0 + steps_per_panel) * LANES
        cols = slice(t0 * LANES, k_hi)
        y = (jnp.dot(u[:, :k_hi], m_scr[:k_hi, cols], preferred_element_type=F32)
             + jnp.dot(xin, wout_scr[:, cols], preferred_element_type=F32))
        z = jax.nn.gelu(y)
        for t in range(steps_per_panel):
            o_ref[t0 + t] = z[:, t * LANES:(t + 1) * LANES].astype(o_ref.dtype)


def _complex_pow(zr, zi, n):
    rr, ri = None, None
    br, bi = zr, zi
    while n:
        if n & 1:
            rr, ri = (br, bi) if rr is None else (rr * br - ri * bi, rr * bi + ri * br)
        n >>= 1
        if n:
            br, bi = br * br - bi * bi, 2.0 * br * bi
    return rr, ri


def _s5_tables(a_re, a_im, log_dt, b_re, b_im, c_re, c_im, n_steps):
    g, p = a_re.shape
    gpb = LANES // SSM_GROUP
    nblk = g // gpb
    dt = jnp.exp(log_dt.astype(F32))[:, None]
    ar = a_re.astype(F32)
    ai = a_im.astype(F32)
    mag = jnp.exp(dt * ar)
    ang = dt * ai
    abar_re = mag * jnp.cos(ang)
    abar_im = mag * jnp.sin(ang)
    nr = abar_re - 1.0
    ni = abar_im
    den = ar * ar + ai * ai
    f_re = (nr * ar + ni * ai) / den
    f_im = (ni * ar - nr * ai) / den
    br = b_re.astype(F32)
    bi = b_im.astype(F32)
    bb_re = f_re[..., None] * br - f_im[..., None] * bi
    bb_im = f_re[..., None] * bi + f_im[..., None] * br

    pw_r, pw_i = [jnp.ones_like(abar_re)], [jnp.zeros_like(abar_re)]
    for _ in range(SSM_CHUNK):
        pw_r.append(pw_r[-1] * abar_re - pw_i[-1] * abar_im)
        pw_i.append(pw_r[-2] * abar_im + pw_i[-1] * abar_re)
    seg_r, seg_i = _complex_pow(pw_r[-1], pw_i[-1], n_steps)
    pw_r.append(seg_r)
    pw_i.append(seg_i)
    n_rows = 24
    pr = jnp.stack(pw_r, 0).reshape(len(pw_r), nblk, gpb * p).transpose(1, 0, 2)
    pi = jnp.stack(pw_i, 0).reshape(len(pw_i), nblk, gpb * p).transpose(1, 0, 2)
    pad = ((0, 0), (0, n_rows - pr.shape[1]), (0, 0))
    pr = jnp.pad(pr, pad)
    pi = jnp.pad(pi, pad)
    prc = pr.transpose(0, 2, 1)
    pic = pi.transpose(0, 2, 1)

    eye = jnp.eye(gpb, dtype=F32)

    def in_to_state(bb):
        v = bb.reshape(nblk, gpb, p, SSM_GROUP).transpose(0, 1, 3, 2)
        e = v[:, :, :, None, :] * eye[None, :, None, :, None]
        return e.reshape(nblk, LANES, gpb * p)

    def state_to_out(c):
        v = c.astype(F32).reshape(nblk, gpb, SSM_GROUP, p).transpose(0, 1, 3, 2)
        e = v[:, :, :, None, :] * eye[None, :, None, :, None]
        return e.reshape(nblk, gpb * p, LANES)

    w0 = jnp.concatenate([in_to_state(bb_re), in_to_state(bb_im)], axis=2)
    cm = jnp.concatenate([state_to_out(c_re), -state_to_out(c_im)], axis=1)
    return w0, cm, pr, pi, prc, pic


def _s5_mixer(hb, ssq, gain, bsz, lp, a_re, a_im, log_dt, b_re, b_im, c_re, c_im, d_skip):
    t, d = hb.shape
    q, nseg = SSM_CHUNK, SSM_SEGMENTS
    assert lp % (q * nseg * 2) == 0
    n_steps = lp // (q * nseg)
    rows = n_steps * nseg

    def chunk_layout(v):
        w = v.shape[-1]
        return v.reshape(bsz, nseg, n_steps, q, w).transpose(3, 0, 2, 1, 4).reshape(q, bsz * rows, w)

    x = chunk_layout(hb)
    ssq_c = chunk_layout(ssq)

    w0, cm, pr, pi, prc, pic = _s5_tables(a_re, a_im, log_dt, b_re, b_im, c_re, c_im, n_steps)
    nblk = d // LANES
    n_state = w0.shape[2]
    kern = functools.partial(_s5_kernel, n_steps=n_steps, inv_d=1.0 / d)
    z = pl.pallas_call(
        kern,
        grid=(nblk, bsz),
        in_specs=[pl.BlockSpec((q, rows, LANES), lambda k, b: (0, b, k)),
                  pl.BlockSpec((q, rows, 1), lambda k, b: (0, b, 0)),
                  pl.BlockSpec((1, LANES), lambda k, b: (0, k)),
                  pl.BlockSpec((1, LANES, n_state), lambda k, b: (k, 0, 0)),
                  pl.BlockSpec((1, n_state, LANES), lambda k, b: (k, 0, 0)),
                  pl.BlockSpec((1,) + pr.shape[1:], lambda k, b: (k, 0, 0)),
                  pl.BlockSpec((1,) + pi.shape[1:], lambda k, b: (k, 0, 0)),
                  pl.BlockSpec((1,) + prc.shape[1:], lambda k, b: (k, 0, 0)),
                  pl.BlockSpec((1,) + pic.shape[1:], lambda k, b: (k, 0, 0)),
                  pl.BlockSpec((1, LANES), lambda k, b: (0, k))],
        out_specs=pl.BlockSpec((q, rows, LANES), lambda k, b: (0, b, k)),
        out_shape=jax.ShapeDtypeStruct((q, bsz * rows, d), BF16),
        scratch_shapes=[pltpu.VMEM((q * LANES, q * LANES), BF16),
                        pltpu.VMEM((q * LANES, n_state), BF16),
                        pltpu.VMEM((n_state, q * LANES), BF16),
                        pltpu.VMEM((rows, n_state), F32),
                        pltpu.VMEM((rows, n_state), F32)],
        compiler_params=_params("arbitrary", "arbitrary"),
        name="s5_mixer",
    )(x, ssq_c, gain.reshape(1, d).astype(F32), w0, cm, pr, pi, prc, pic, d_skip.reshape(1, d).astype(F32))
    return z.reshape(q, bsz, n_steps, nseg, d).transpose(1, 3, 2, 0, 4).reshape(t, d)


def _rope_tables(length):
    inv = ROPE_THETA ** (-jnp.arange(0, HEAD_DIM, 2, dtype=F32) / HEAD_DIM)
    ang = jnp.arange(length, dtype=F32)[:, None] * inv[None, :]
    cos, sin = jnp.cos(ang), jnp.sin(ang)
    return jnp.concatenate([cos, cos], axis=1), jnp.concatenate([-sin, sin], axis=1)


def _lambda_init(layer_idx):
    return 0.8 - 0.6 * math.exp(-0.3 * layer_idx)


def kernel(x, meta_tokens, norm_mix_g, norm_mlp_g, da_w_qkv, da_q_norm_g, da_k_norm_g, da_lambda, da_subln_g, da_w_o, ssm_a_re, ssm_a_im, ssm_log_dt, ssm_b_re, ssm_b_im, ssm_c_re, ssm_c_im, ssm_d, ssm_w_glu, ssm_b_glu, mlp_w_up, mlp_w_down):
    bsz, seq, d = x.shape
    depth = norm_mix_g.shape[0]
    n_mixers = 2
    length = N_META + seq
    lp = ((length + SEQ_ALIGN - 1) // SEQ_ALIGN) * SEQ_ALIGN
    assert meta_tokens.shape[0] == N_META and depth >= 1
    h, hn = _embed_norm(x, meta_tokens, norm_mix_g[0], lp)
    cos, sin = (jnp.tile(tab, (bsz, 1)) for tab in _rope_tables(lp))
    def up_job(i):
        return mlp_w_up, i, norm_mlp_g[i]

    def bf16_weight(name, stack, layer, gain=None):
        if name not in wb:
            src = stack[layer]
            wb[name] = (src if gain is None else src * gain.astype(F32)[:, None]).astype(BF16)
        return wb[name]

    wb = {}
    hb = ssq = None
    only_output_rows = False
    for i in range(depth):
        j = i // n_mixers
        nxt_s5 = i + 1 < depth and (i + 1) % n_mixers == 1
        if i % n_mixers == 0:
            if i > 0:
                hn = _rmsnorm(h, norm_mix_g[i])
            qkv, wb["o", j], wb["up", i] = _qkv_proj(
                hn, bf16_weight(("qkv", j), da_w_qkv, j), cos, sin, da_q_norm_g[j], da_k_norm_g[j],
                [(da_w_o, j, None), up_job(i)])
            att = _diff_attention(qkv, da_lambda[j], da_subln_g[j], bsz, lp, d, _lambda_init(i))
            h, hb, ssq = _proj_residual(att, wb["o", j], h)
        else:
            z = _s5_mixer(hb, ssq, norm_mix_g[i], bsz, lp, ssm_a_re[j], ssm_a_im[j], ssm_log_dt[j],
                          ssm_b_re[j], ssm_b_im[j], ssm_c_re[j], ssm_c_im[j], ssm_d[j])
            w_glu = bf16_weight(("glu", j), ssm_w_glu, j)
            if i == depth - 1:
                h, hb, ssq = _glu_residual_rows(z, w_glu, ssm_b_glu[j], h, bsz, lp, N_META, seq)
                only_output_rows = True
            else:
                h, hb, ssq = _glu_residual(z, w_glu, ssm_b_glu[j], h)
        jobs =[(("down", i), mlp_w_down, i, None)]
        if nxt_s5:
            jobs += [(("glu", (i + 1) // n_mixers), ssm_w_glu, (i + 1) // n_mixers, None),
                     (("up", i + 1),) + up_job(i + 1)]
        f, *casts = _mlp_up(hb, bf16_weight(("up", i), *up_job(i)), ssq, [job[1:] for job in jobs])
        wb.update({job[0]: c for job, c in zip(jobs, casts)})
        if i == depth - 1:
            if only_output_rows:
                return _mlp_down(f, wb["down", i], h, False).reshape(bsz, seq, d)
            return _mlp_down_final(f, wb["down", i], h, bsz, lp, N_META, seq).reshape(bsz, seq, d)
        if nxt_s5:
            h, hb, ssq = _mlp_down(f, wb["down", i], h, True)
        else:
            h = _mlp_down(f, wb["down", i], h, False)
```
